```python
import math
import jax, jax.numpy as jnp
from jax import lax
import numpy as np

D_MODEL = 1024
BATCH = 8
SEQ = 2048
DEPTH = 2

HEAD_DIM = 64
D_MIX = D_MODEL
MLSTM_DIM = D_MIX // 4
CONV_DIM = D_MIX // 4
ATTN_DIM = D_MIX // 2
MLSTM_HEADS = MLSTM_DIM // HEAD_DIM
ATTN_HEADS = ATTN_DIM // HEAD_DIM
MLSTM_CHUNK = 64
QK_CONV = 4
CONF_KERNEL = 31
MOBA_BLOCK = 256
MOBA_TOPK = 3
MOBA_Q_CHUNK = 64
REL_BUCKETS = 32
REL_MAX_DIST = 128
D_FF = 2752
FFN_CONV = 3
EPS = 1e-6
NEG = -1e30
P_IN = 4 * MLSTM_DIM + 2 * MLSTM_HEADS + 2 * CONV_DIM + 3 * ATTN_DIM

kernel_name = "hybrid_mlstm_conformer_moba_block"


def _split_points():
    sizes = [MLSTM_DIM] * 4 + [MLSTM_HEADS] * 2 + [CONV_DIM] * 2 + [ATTN_DIM] * 3
    return [int(c) for c in np.cumsum(sizes)[:-1]]


def rmsnorm(x, g):
    xf = x.astype(jnp.float32)
    y = xf * lax.rsqrt(jnp.mean(xf * xf, axis=-1, keepdims=True) + EPS)
    return (y * g.astype(jnp.float32)).astype(x.dtype)


def layernorm(x, g, b):
    xf = x.astype(jnp.float32)
    mu = jnp.mean(xf, axis=-1, keepdims=True)
    var = jnp.mean(jnp.square(xf - mu), axis=-1, keepdims=True)
    y = (xf - mu) * lax.rsqrt(var + EPS)
    return (y * g.astype(jnp.float32) + b.astype(jnp.float32)).astype(x.dtype)


def causal_dwconv(x, w, b):
    width, ch = w.shape
    y = lax.conv_general_dilated(
        x, w[:, None, :], window_strides=(1,), padding=[(width - 1, 0)],
        dimension_numbers=("NWC", "WIO", "NWC"), feature_group_count=ch)
    return y + b


def split_heads(t, n_heads):
    bn, s, _ = t.shape
    return t.reshape(bn, s, n_heads, -1).transpose(0, 2, 1, 3)


def merge_heads(t):
    bn, h, s, d = t.shape
    return t.transpose(0, 2, 1, 3).reshape(bn, s, h * d)


def mlstm_chunkwise(q, k, v, i_pre, f_pre):
    bn, nh, s, d = q.shape
    L = MLSTM_CHUNK
    nc = s // L
    q = q * (d ** -0.5)
    logf = jax.nn.log_sigmoid(f_pre)

    def to_chunks(t):
        return jnp.moveaxis(t.reshape(bn, nh, nc, L, *t.shape[3:]), 2, 0)

    causal = jnp.tril(jnp.ones((L, L), dtype=bool))

    def step(carry, inp):
        C, n, m = carry
        qc, kc, vc, ic, lfc = inp
        b = jnp.cumsum(lfc, axis=-1)
        b_tot = b[..., -1]
        dmat = jnp.where(causal, b[..., :, None] - b[..., None, :] + ic[..., None, :], NEG)
        inter = b + m[..., None]
        m_t = jnp.maximum(inter, jnp.max(dmat, axis=-1))
        w = jnp.exp(dmat - m_t[..., None])
        a_inter = jnp.exp(inter - m_t)
        sc = jnp.einsum("bhtd,bhsd->bhts", qc, kc) * w
        num = (a_inter[..., None] * jnp.einsum("bhvk,bhtk->bhtv", C, qc)
               + jnp.einsum("bhts,bhsv->bhtv", sc, vc))
        den = a_inter * jnp.einsum("bhk,bhtk->bht", n, qc) + jnp.sum(sc, axis=-1)
        h = num / jnp.maximum(jnp.abs(den), jnp.exp(-m_t))[..., None]
        g = b_tot[..., None] - b + ic
        m_new = jnp.maximum(b_tot + m, jnp.max(g, axis=-1))
        wk = jnp.exp(g - m_new[..., None])
        decay = jnp.exp(b_tot + m - m_new)
        C_new = decay[..., None, None] * C + jnp.einsum("bhs,bhsv,bhsk->bhvk", wk, vc, kc)
        n_new = decay[..., None] * n + jnp.einsum("bhs,bhsk->bhk", wk, kc)
        return (C_new, n_new, m_new), h

    init = (jnp.zeros((bn, nh, d, d), jnp.float32),
            jnp.zeros((bn, nh, d), jnp.float32),
            jnp.full((bn, nh), NEG, jnp.float32))
    _, hs = lax.scan(step, init, (to_chunks(q), to_chunks(k), to_chunks(v),
                                  to_chunks(i_pre), to_chunks(logf)))
    return jnp.moveaxis(hs, 0, 2).reshape(bn, nh, s, d)


def t5_bucket(dist):
    n = jnp.maximum(dist, 0)
    max_exact = REL_BUCKETS // 2
    nf = jnp.maximum(n, 1).astype(jnp.float32)
    large = max_exact + (jnp.log(nf / max_exact) / math.log(REL_MAX_DIST / max_exact)
                         * (REL_BUCKETS - max_exact)).astype(jnp.int32)
    large = jnp.minimum(large, REL_BUCKETS - 1)
    return jnp.where(n < max_exact, n, large)


def moba_attention(q, k, v, rel_bias):
    bn, nh, s, d = q.shape
    blk = MOBA_BLOCK
    qc_len = MOBA_Q_CHUNK
    nb = -(-s // blk)
    pad = nb * blk - s
    kp = jnp.pad(k, ((0, 0), (0, 0), (0, pad), (0, 0)))
    vp = jnp.pad(v, ((0, 0), (0, 0), (0, pad), (0, 0)))
    kb = kp.reshape(bn, nh, nb, blk, d)
    vb = vp.reshape(bn, nh, nb, blk, d)
    kmean = jnp.mean(kb, axis=3)
    k_sel = min(MOBA_TOPK, nb)
    scale = d ** -0.5
    bias_t = rel_bias.T
    b_idx = jnp.arange(bn)[:, None, None, None]
    h_idx = jnp.arange(nh)[None, :, None, None]
    h_idx5 = jnp.arange(nh)[None, :, None, None, None]
    nq = s // qc_len

    def one_chunk(c):
        q0 = c * qc_len
        qc = lax.dynamic_slice_in_dim(q, q0, qc_len, axis=2)
        t = q0 + jnp.arange(qc_len)
        own = q0 // blk
        gate = jnp.einsum("bhqd,bhnd->bhqn", qc, kmean).astype(jnp.float32)
        gate = jnp.where(jnp.arange(nb) < own, gate, NEG)
        _, idx = lax.top_k(gate, k_sel)
        valid = jnp.arange(k_sel) < own
        kg = kb[b_idx, h_idx, idx]
        vg = vb[b_idx, h_idx, idx]
        s_pos = idx[..., None] * blk + jnp.arange(blk)
        dist = t[None, None, :, None, None] - s_pos
        s_past = (jnp.einsum("bhqd,bhqjkd->bhqjk", qc, kg).astype(jnp.float32) * scale
                  + bias_t[h_idx5, t5_bucket(dist)].astype(jnp.float32))
        s_past = jnp.where(valid[:, None], s_past, NEG)
        ko = lax.dynamic_slice_in_dim(kp, own * blk, blk, axis=2)
        vo = lax.dynamic_slice_in_dim(vp, own * blk, blk, axis=2)
        dist_o = t[:, None] - (own * blk + jnp.arange(blk))[None, :]
        s_own = (jnp.einsum("bhqd,bhkd->bhqk", qc, ko).astype(jnp.float32) * scale
                 + bias_t[:, t5_bucket(dist_o)][None].astype(jnp.float32))
        s_own = jnp.where(dist_o >= 0, s_own, NEG)
        logits = jnp.concatenate([s_past.reshape(bn, nh, qc_len, k_sel * blk), s_own], axis=-1)
        p = jax.nn.softmax(logits, axis=-1).astype(v.dtype)
        p_past = p[..., :k_sel * blk].reshape(bn, nh, qc_len, k_sel, blk)
        p_own = p[..., k_sel * blk:]
        return (jnp.einsum("bhqjk,bhqjkd->bhqd", p_past, vg)
                + jnp.einsum("bhqk,bhkd->bhqd", p_own, vo))

    outs = lax.map(one_chunk, jnp.arange(nq))
    return jnp.moveaxis(outs, 0, 2).reshape(bn, nh, s, d)


def hybrid_layer(x, norm1_g, w_in, mlstm_qk_conv_w, mlstm_qk_conv_b, mlstm_ig_b,
                 mlstm_fg_b, mlstm_head_g, conf_dw_w, conf_dw_b, conf_ln_g, conf_ln_b,
                 rel_bias, w_out, norm2_g, ffn_w_up, ffn_conv_w, ffn_conv_b, ffn_w_down):
    f32 = jnp.float32
    h = rmsnorm(x, norm1_g)
    z = h @ w_in
    (m_q, m_k, m_v, m_o, m_i, m_f, c_a, c_g, a_q, a_k, a_v) = jnp.split(z, _split_points(), axis=-1)

    qk = jax.nn.silu(causal_dwconv(jnp.concatenate([m_q, m_k], axis=-1),
                                   mlstm_qk_conv_w, mlstm_qk_conv_b))
    m_q, m_k = jnp.split(qk, 2, axis=-1)
    hm = mlstm_chunkwise(split_heads(m_q, MLSTM_HEADS).astype(f32),
                         split_heads(m_k, MLSTM_HEADS).astype(f32),
                         split_heads(m_v, MLSTM_HEADS).astype(f32),
                         (m_i + mlstm_ig_b).astype(f32).transpose(0, 2, 1),
                         (m_f + mlstm_fg_b).astype(f32).transpose(0, 2, 1))
    mu = jnp.mean(hm, axis=-1, keepdims=True)
    var = jnp.mean(jnp.square(hm - mu), axis=-1, keepdims=True)
    hm = (hm - mu) * lax.rsqrt(var + EPS)
    y_m = (merge_heads(hm) * mlstm_head_g.astype(f32)).astype(x.dtype) * jax.nn.sigmoid(m_o)

    u = c_a * jax.nn.sigmoid(c_g)
    u = causal_dwconv(u, conf_dw_w, conf_dw_b)
    y_c = jax.nn.silu(layernorm(u, conf_ln_g, conf_ln_b))

    y_a = merge_heads(moba_attention(split_heads(a_q, ATTN_HEADS),
                                     split_heads(a_k, ATTN_HEADS),
                                     split_heads(a_v, ATTN_HEADS), rel_bias))

    x = x + jnp.concatenate([y_m, y_c, y_a], axis=-1) @ w_out

    u = causal_dwconv(rmsnorm(x, norm2_g) @ ffn_w_up, ffn_conv_w, ffn_conv_b)
    g, val = jnp.split(u, 2, axis=-1)
    return x + (jax.nn.silu(g) * val) @ ffn_w_down


def setup_inputs(seed: int = 0) -> dict:
    key = jax.random.key(seed)
    ks = jax.random.split(key, 24)

    def nrm(k, shape, s):
        return jax.random.normal(k, shape, jnp.float32) * s

    fg_base = jnp.linspace(3.0, 6.0, MLSTM_HEADS, dtype=jnp.float32)
    return {
        "x": nrm(ks[0], (BATCH, SEQ, D_MODEL), 1.0),
        "norm1_g": 1.0 + nrm(ks[1], (DEPTH, D_MODEL), 0.05),
        "w_in": nrm(ks[2], (DEPTH, D_MODEL, P_IN), D_MODEL ** -0.5),
        "mlstm_qk_conv_w": nrm(ks[3], (DEPTH, QK_CONV, 2 * MLSTM_DIM), QK_CONV ** -0.5),
        "mlstm_qk_conv_b": nrm(ks[4], (DEPTH, 2 * MLSTM_DIM), 0.02),
        "mlstm_ig_b": nrm(ks[5], (DEPTH, MLSTM_HEADS), 0.1),
        "mlstm_fg_b": fg_base[None, :] + nrm(ks[6], (DEPTH, MLSTM_HEADS), 0.1),
        "mlstm_head_g": 1.0 + nrm(ks[7], (DEPTH, MLSTM_DIM), 0.05),
        "conf_dw_w": nrm(ks[8], (DEPTH, CONF_KERNEL, CONV_DIM), CONF_KERNEL ** -0.5),
        "conf_dw_b": nrm(ks[9], (DEPTH, CONV_DIM), 0.02),
        "conf_ln_g": 1.0 + nrm(ks[10], (DEPTH, CONV_DIM), 0.05),
        "conf_ln_b": nrm(ks[11], (DEPTH, CONV_DIM), 0.02),
        "rel_bias": nrm(ks[12], (REL_BUCKETS, ATTN_HEADS), 0.2),
        "w_out": nrm(ks[13], (DEPTH, D_MIX, D_MODEL), D_MIX ** -0.5),
        "norm2_g": 1.0 + nrm(ks[14], (DEPTH, D_MODEL), 0.05),
        "ffn_w_up": nrm(ks[15], (DEPTH, D_MODEL, 2 * D_FF), D_MODEL ** -0.5),
        "ffn_conv_w": nrm(ks[16], (DEPTH, FFN_CONV, 2 * D_FF), FFN_CONV ** -0.5),
        "ffn_conv_b": nrm(ks[17], (DEPTH, 2 * D_FF), 0.02),
        "ffn_w_down": nrm(ks[18], (DEPTH, D_FF, D_MODEL), D_FF ** -0.5),
        "final_g": 1.0 + nrm(ks[19], (D_MODEL,), 0.05),
    }


def reference(x, norm1_g, w_in, mlstm_qk_conv_w, mlstm_qk_conv_b, mlstm_ig_b, mlstm_fg_b,
              mlstm_head_g, conf_dw_w, conf_dw_b, conf_ln_g, conf_ln_b, rel_bias, w_out,
              norm2_g, ffn_w_up, ffn_conv_w, ffn_conv_b, ffn_w_down, final_g):
    for l in range(DEPTH):
        x = hybrid_layer(x, norm1_g[l], w_in[l], mlstm_qk_conv_w[l], mlstm_qk_conv_b[l],
                         mlstm_ig_b[l], mlstm_fg_b[l], mlstm_head_g[l], conf_dw_w[l],
                         conf_dw_b[l], conf_ln_g[l], conf_ln_b[l], rel_bias, w_out[l],
                         norm2_g[l], ffn_w_up[l], ffn_conv_w[l], ffn_conv_b[l], ffn_w_down[l])
    return rmsnorm(x, final_g)
```

```python
import functools
import math

import numpy as np
import jax
import jax.numpy as jnp
from jax import lax
from jax.experimental import pallas as pl
from jax.experimental.pallas import tpu as pltpu

F32 = jnp.float32
BF16 = jnp.bfloat16
HIGHEST = lax.Precision.HIGHEST

D_MODEL = 1024
DEPTH = 2
HEAD_DIM = 64
MLSTM_DIM = 256
CONV_DIM = 256
ATTN_DIM = 512
MLSTM_HEADS = MLSTM_DIM // HEAD_DIM
ATTN_HEADS = ATTN_DIM // HEAD_DIM
QK_CONV = 4
CONF_KERNEL = 31
MOBA_BLOCK = 256
MOBA_TOPK = 3
REL_BUCKETS = 32
REL_MAX_DIST = 128
D_FF = 2752
FFN_CONV = 3
EPS = 1e-6
NEG = -1e30

LANES = 128
SUBLANES = 8
VMEM_LIMIT = 56 * 1024 * 1024

GATE_PAD = LANES
ZM_W = 4 * MLSTM_DIM + GATE_PAD
ZC_W = 2 * CONV_DIM
ZA_W = 3 * ATTN_DIM
P_W = ZM_W + ZC_W + ZA_W
D_FFP = -(-D_FF // 256) * 256
FFN_CHUNK = 256
N_FFN_CHUNKS = D_FFP // FFN_CHUNK

TM_IN = 512
TM_FFN = 512
MLSTM_L = 256
ATT_TQ = 128
CONF_ROWS = 64
CONF_PAD = 32
MAX_BLOCKS = SUBLANES
PEN_BIG = 2.0 ** 100

_NT = (((1,), (1,)), ((), ()))


def _t5_saturation_distance():
    n = np.arange(1, 4 * MOBA_BLOCK, dtype=np.float32)
    max_exact = REL_BUCKETS // 2
    large = max_exact + (np.log(n / max_exact) / math.log(REL_MAX_DIST / max_exact)
                         * (REL_BUCKETS - max_exact)).astype(np.int32)
    bucket = np.where(n < max_exact, n.astype(np.int32), np.minimum(large, REL_BUCKETS - 1))
    not_last = np.nonzero(bucket != REL_BUCKETS - 1)[0]
    return int(n[not_last[-1]]) + 1


assert _t5_saturation_distance() <= MOBA_BLOCK + 1


def _sigmoid(x):
    return 1.0 / (1.0 + jnp.exp(-x))


def _log_sigmoid(x):
    return jnp.minimum(x, 0.0) - jnp.log1p(jnp.exp(-jnp.abs(x)))


def _rms(xf, g):
    return xf * lax.rsqrt(jnp.mean(xf * xf, axis=-1, keepdims=True) + EPS) * g


def _dot(a, b, **kw):
    return jnp.dot(a, b, preferred_element_type=F32, **kw)


def _const_spec(shape):
    nd = len(shape)
    return pl.BlockSpec(shape, lambda *_: (0,) * nd)


def _in_proj_kernel(x_ref, g_ref, w_ref, zm_ref, zc_ref, za_ref):
    h = _rms(x_ref[...], g_ref[...]).astype(BF16)
    zm_ref[...] = _dot(h, w_ref[:, 0:ZM_W])
    zc_ref[...] = _dot(h, w_ref[:, ZM_W:ZM_W + ZC_W])
    za_ref[...] = _dot(h, w_ref[:, ZM_W + ZC_W:P_W]).astype(BF16)


def _in_proj(x2d, g, w_r):
    n = x2d.shape[0]
    return pl.pallas_call(
        _in_proj_kernel,
        out_shape=(jax.ShapeDtypeStruct((n, ZM_W), F32),
                   jax.ShapeDtypeStruct((n, ZC_W), F32),
                   jax.ShapeDtypeStruct((n, ZA_W), BF16)),
        grid=(n // TM_IN,),
        in_specs=[pl.BlockSpec((TM_IN, D_MODEL), lambda i: (i, 0)),
                  _const_spec((1, D_MODEL)),
                  _const_spec((D_MODEL, P_W))],
        out_specs=(pl.BlockSpec((TM_IN, ZM_W), lambda i: (i, 0)),
                   pl.BlockSpec((TM_IN, ZC_W), lambda i: (i, 0)),
                   pl.BlockSpec((TM_IN, ZA_W), lambda i: (i, 0))),
        compiler_params=pltpu.CompilerParams(
            dimension_semantics=("arbitrary",), vmem_limit_bytes=VMEM_LIMIT),
        name="in_proj",
    )(x2d, g, w_r)


def _mlstm_kernel(zm_ref, cw_ref, cb_ref, gb_ref, hg_ref, o_ref, g_ref, gt_ref, *, seq):
    L = MLSTM_L
    n_chunks = seq // L
    scale = HEAD_DIM ** -0.5
    lane = lax.broadcasted_iota(jnp.int32, (1, LANES), 1)
    head0 = lane < HEAD_DIM

    gates = zm_ref[:, 4 * MLSTM_DIM:ZM_W] + gb_ref[...]
    is_f = (lane >= MLSTM_HEADS) & (lane < 2 * MLSTM_HEADS)
    gates = jnp.where(is_f, _log_sigmoid(gates), gates)
    g_ref[...] = gates
    gt_ref[...] = gates.T[0:SUBLANES, :]

    ti = lax.broadcasted_iota(jnp.int32, (L, L), 0)
    si = lax.broadcasted_iota(jnp.int32, (L, L), 1)
    causal = si <= ti
    tri = causal.astype(F32)
    tri_t = (ti <= si).astype(F32)
    srow_head = lax.broadcasted_iota(jnp.int32, (LANES, 2 * LANES), 0) // HEAD_DIM
    scol = lax.broadcasted_iota(jnp.int32, (LANES, 2 * LANES), 1)
    state_mask = ((scol < LANES) & (scol // HEAD_DIM == srow_head)) | (scol == LANES + srow_head)
    ones_col = jnp.where(lane == 0, 1.0, 0.0).astype(BF16)
    cb = cb_ref[...]
    hg = hg_ref[...]

    def chunk(c, carry):
        states, m_prev = carry
        r0 = pl.multiple_of(c * L, L)
        gc = g_ref[pl.ds(r0, L), :]
        gtc = gt_ref[:, pl.ds(r0, L)]
        bcols = _dot(tri, gc, precision=HIGHEST)
        brows = _dot(gtc, tri_t, precision=HIGHEST)

        xcur = zm_ref[pl.ds(r0, L), 0:2 * MLSTM_DIM]
        pr = pl.multiple_of(jnp.maximum(r0 - SUBLANES, 0), SUBLANES)
        xprev = jnp.where(c > 0, zm_ref[pl.ds(pr, SUBLANES), 0:2 * MLSTM_DIM], 0.0)
        xcat = jnp.concatenate([xprev, xcur], axis=0)
        y = cb
        for j in range(QK_CONV):
            off = SUBLANES - (QK_CONV - 1) + j
            y = y + cw_ref[j:j + 1, :] * xcat[off:off + L]
        qk = y * _sigmoid(y)

        new_states = []
        new_m = []
        for p in range(2):
            lo = p * LANES
            qp = (qk[:, lo:lo + LANES] * scale).astype(BF16)
            kpf = qk[:, MLSTM_DIM + lo:MLSTM_DIM + lo + LANES]
            kpb = kpf.astype(BF16)
            vp = zm_ref[pl.ds(r0, L), 2 * MLSTM_DIM + lo:2 * MLSTM_DIM + lo + LANES].astype(BF16)
            vaug = jnp.concatenate([vp, jnp.broadcast_to(ones_col, (L, LANES))], axis=1)
            inter = _dot(qp, states[p].astype(BF16))
            hvals, wks, decays = [], [], []
            for hh in range(2):
                h = 2 * p + hh
                mp = m_prev[h]
                bcol = bcols[:, MLSTM_HEADS + h:MLSTM_HEADS + h + 1]
                icol = gc[:, h:h + 1]
                brow = brows[MLSTM_HEADS + h:MLSTM_HEADS + h + 1, :]
                irow = gtc[h:h + 1, :]
                dmat = jnp.where(causal, bcol - brow + irow, NEG)
                m_t = jnp.maximum(bcol + mp, jnp.max(dmat, axis=1, keepdims=True))
                w = jnp.exp(dmat - m_t)
                qh = jnp.where(head0 if hh == 0 else ~head0, qp, jnp.zeros_like(qp))
                s = lax.dot_general(qh, kpb, _NT, preferred_element_type=F32)
                r = _dot((s * w).astype(BF16), vaug)
                a = jnp.exp(bcol + mp - m_t)
                num = a * inter[:, 0:LANES] + r[:, 0:LANES]
                den = a * inter[:, LANES + hh:LANES + hh + 1] + r[:, LANES:LANES + 1]
                hvals.append(num / jnp.maximum(jnp.abs(den), jnp.exp(-m_t)))
                btot = bcol[L - 1:L, :]
                gk = btot - bcol + icol
                m_new = jnp.maximum(btot + mp, jnp.max(gk, axis=0, keepdims=True))
                wks.append(jnp.exp(gk - m_new))
                decays.append(jnp.exp(btot + mp - m_new))
                new_m.append(m_new)
            kw = kpf * jnp.where(head0, wks[0], wks[1])
            upd = _dot(kw.T.astype(BF16), vaug)
            decay_rows = jnp.where(srow_head[:, 0:1] == 0, decays[0], decays[1])
            new_states.append(decay_rows * states[p] + jnp.where(state_mask, upd, 0.0))

            hv = jnp.where(head0, hvals[0], hvals[1])
            inv = 1.0 / HEAD_DIM
            mu0 = jnp.sum(jnp.where(head0, hv, 0.0), axis=1, keepdims=True) * inv
            mu1 = jnp.sum(jnp.where(head0, 0.0, hv), axis=1, keepdims=True) * inv
            d = hv - jnp.where(head0, mu0, mu1)
            dd = d * d
            v0 = jnp.sum(jnp.where(head0, dd, 0.0), axis=1, keepdims=True) * inv
            v1 = jnp.sum(jnp.where(head0, 0.0, dd), axis=1, keepdims=True) * inv
            hn = d * lax.rsqrt(jnp.where(head0, v0, v1) + EPS)
            og = zm_ref[pl.ds(r0, L), 3 * MLSTM_DIM + lo:3 * MLSTM_DIM + lo + LANES]
            o_ref[pl.ds(r0, L), lo:lo + LANES] = (hn * hg[:, lo:lo + LANES] * _sigmoid(og)).astype(o_ref.dtype)
        return tuple(new_states), tuple(new_m)

    init = (tuple(jnp.zeros((LANES, 2 * LANES), F32) for _ in range(2)),
            tuple(jnp.full((1, 1), NEG, F32) for _ in range(MLSTM_HEADS)))
    lax.fori_loop(0, n_chunks, chunk, init)


def _mlstm(zm, cw, cb, gate_bias, head_g, batch, seq):
    return pl.pallas_call(
        functools.partial(_mlstm_kernel, seq=seq),
        out_shape=jax.ShapeDtypeStruct((batch * seq, MLSTM_DIM), BF16),
        grid=(batch,),
        in_specs=[pl.BlockSpec((seq, ZM_W), lambda b: (b, 0)),
                  _const_spec((QK_CONV, 2 * MLSTM_DIM)),
                  _const_spec((1, 2 * MLSTM_DIM)),
                  _const_spec((1, GATE_PAD)),
                  _const_spec((1, MLSTM_DIM))],
        out_specs=pl.BlockSpec((seq, MLSTM_DIM), lambda b: (b, 0)),
        scratch_shapes=[pltpu.VMEM((seq, GATE_PAD), F32),
                        pltpu.VMEM((SUBLANES, seq), F32)],
        compiler_params=pltpu.CompilerParams(
            dimension_semantics=("arbitrary",), vmem_limit_bytes=VMEM_LIMIT),
        name="mlstm",
    )(zm, cw, cb, gate_bias, head_g)


def _conformer_kernel(zc_ref, w_ref, b_ref, lg_ref, lb_ref, o_ref, up_ref, *, seq):
    up_ref[0:CONF_PAD, :] = jnp.zeros((CONF_PAD, CONV_DIM), F32)
    up_ref[CONF_PAD:CONF_PAD + seq, :] = zc_ref[:, 0:CONV_DIM] * _sigmoid(zc_ref[:, CONV_DIM:ZC_W])
    bias = b_ref[...]
    lg = lg_ref[...]
    lb = lb_ref[...]

    def tile(c, carry):
        r0 = pl.multiple_of(c * CONF_ROWS, CONF_ROWS)
        win = up_ref[pl.ds(r0, CONF_ROWS + CONF_PAD), :]
        acc = jnp.broadcast_to(bias, (CONF_ROWS, CONV_DIM))
        for j in range(CONF_KERNEL):
            off = CONF_PAD - (CONF_KERNEL - 1) + j
            acc = acc + w_ref[j:j + 1, :] * win[off:off + CONF_ROWS]
        mu = jnp.mean(acc, axis=-1, keepdims=True)
        d = acc - mu
        var = jnp.mean(d * d, axis=-1, keepdims=True)
        y = d * lax.rsqrt(var + EPS) * lg + lb
        o_ref[pl.ds(r0, CONF_ROWS), :] = (y * _sigmoid(y)).astype(o_ref.dtype)
        return carry

    lax.fori_loop(0, seq // CONF_ROWS, tile, 0)


def _conformer(zc, w, b, ln_g, ln_b, batch, seq):
    return pl.pallas_call(
        functools.partial(_conformer_kernel, seq=seq),
        out_shape=jax.ShapeDtypeStruct((batch * seq, CONV_DIM), BF16),
        grid=(batch,),
        in_specs=[pl.BlockSpec((seq, ZC_W), lambda b: (b, 0)),
                  _const_spec((CONF_KERNEL, CONV_DIM)),
                  _const_spec((1, CONV_DIM)),
                  _const_spec((1, CONV_DIM)),
                  _const_spec((1, CONV_DIM))],
        out_specs=pl.BlockSpec((seq, CONV_DIM), lambda b: (b, 0)),
        scratch_shapes=[pltpu.VMEM((CONF_PAD + seq, CONV_DIM), F32)],
        compiler_params=pltpu.CompilerParams(
            dimension_semantics=("arbitrary",), vmem_limit_bytes=VMEM_LIMIT),
        name="conformer",
    )(zc, w, b, ln_g, ln_b)


def _bias_tiles_kernel(rb_ref, o_ref):
    h = pl.program_id(0)
    i = lax.broadcasted_iota(jnp.int32, (MOBA_BLOCK, MOBA_BLOCK), 0)
    j = lax.broadcasted_iota(jnp.int32, (MOBA_BLOCK, MOBA_BLOCK), 1)
    max_exact = REL_BUCKETS // 2
    for t in range(2):
        dist = i - j + t * MOBA_BLOCK
        n = jnp.maximum(dist, 0)
        nf = jnp.maximum(n, 1).astype(F32)
        large = max_exact + (jnp.log(nf / max_exact) / math.log(REL_MAX_DIST / max_exact)
                             * (REL_BUCKETS - max_exact)).astype(jnp.int32)
        large = jnp.minimum(large, REL_BUCKETS - 1)
        bucket = jnp.where(n < max_exact, n, large)
        bias = jnp.zeros((MOBA_BLOCK, MOBA_BLOCK), F32)
        for bk in range(REL_BUCKETS):
            bias = jnp.where(bucket == bk, rb_ref[bk, h], bias)
        if t == 0:
            bias = jnp.where(dist >= 0, bias, NEG)
        o_ref[0, t] = bias


def _bias_tiles(rel_bias):
    return pl.pallas_call(
        _bias_tiles_kernel,
        out_shape=jax.ShapeDtypeStruct((ATTN_HEADS, 2, MOBA_BLOCK, MOBA_BLOCK), F32),
        grid=(ATTN_HEADS,),
        in_specs=[pl.BlockSpec(memory_space=pltpu.SMEM)],
        out_specs=pl.BlockSpec((1, 2, MOBA_BLOCK, MOBA_BLOCK), lambda h: (h, 0, 0, 0)),
        compiler_params=pltpu.CompilerParams(dimension_semantics=("arbitrary",)),
        name="bias_tiles",
    )(rel_bias)


def _moba_kernel(far_ref, q_ref, k_ref, v_ref, d_ref, o_ref, qa_ref, ka_ref, *, seq):
    hp = pl.program_id(0)
    blk = MOBA_BLOCK
    nb = seq // blk
    scale = HEAD_DIM ** -0.5
    lane = lax.broadcasted_iota(jnp.int32, (1, LANES), 1)
    head0 = lane < HEAD_DIM

    kf = k_ref[...].astype(F32)
    qf = q_ref[...].astype(F32)
    means = [jnp.mean(kf[n * blk:(n + 1) * blk], axis=0, keepdims=True) for n in range(nb)]
    if nb < MAX_BLOCKS:
        means.append(jnp.zeros((MAX_BLOCKS - nb, LANES), F32))
    kmean = jnp.concatenate(means, axis=0)
    blk_i = lax.broadcasted_iota(jnp.int32, (MAX_BLOCKS, seq), 0)
    own_i = lax.broadcasted_iota(jnp.int32, (MAX_BLOCKS, seq), 1) // blk
    row_blk = lax.broadcasted_iota(jnp.int32, (seq, LANES), 0) // blk
    lane_full = lax.broadcasted_iota(jnp.int32, (seq, LANES), 1)

    for hh in range(2):
        mine = head0 if hh == 0 else ~head0
        gate = lax.dot_general(jnp.where(mine, kmean, 0.0), qf, _NT,
                               precision=HIGHEST, preferred_element_type=F32)
        cand = blk_i < own_i
        g = jnp.where(cand, gate, NEG)
        rank = jnp.zeros((MAX_BLOCKS, seq), jnp.int32)
        for m in range(MAX_BLOCKS):
            gm = g[m:m + 1, :]
            beats = (gm > g) | ((gm == g) & (blk_i > m))
            rank = rank + beats.astype(jnp.int32)
        keep = (cand & (rank < MOBA_TOPK)) | (blk_i >= own_i)
        pen = jnp.where(keep, 0.0, -PEN_BIG)
        pen_lane0 = (1 - hh) * HEAD_DIM
        parts = [pen]
        if pen_lane0:
            parts.insert(0, jnp.zeros((pen_lane0, seq), F32))
        tail_rows = LANES - MAX_BLOCKS - pen_lane0
        if tail_rows:
            parts.append(jnp.zeros((tail_rows, seq), F32))
        pen_t = jnp.concatenate(parts, axis=0).T
        qa_ref[hh] = jnp.where(mine, qf * scale, pen_t).astype(BF16)
        onehot = (lane_full == pen_lane0 + row_blk).astype(F32)
        ka_ref[hh] = jnp.where(mine, kf, onehot).astype(BF16)

    tq = ATT_TQ
    per_blk = blk // tq

    def qtile(qi, carry):
        r0 = pl.multiple_of(qi * tq, tq)
        own = qi // per_blk
        sub0 = pl.multiple_of((qi % per_blk) * tq, tq)
        k0 = pl.multiple_of(own * blk, blk)
        outs = []
        for hh in range(2):
            q = qa_ref[hh, pl.ds(r0, tq), :]
            far_bias = far_ref[2 * hp + hh]

            def scores(n0, hh=hh, q=q):
                return lax.dot_general(q, ka_ref[hh, pl.ds(n0, blk), :], _NT,
                                       preferred_element_type=F32)

            def update(s, m, l, acc, n0):
                m_new = jnp.maximum(m, jnp.max(s, axis=1, keepdims=True))
                alpha = jnp.exp(m - m_new)
                p = jnp.exp(s - m_new)
                l = alpha * l + jnp.sum(p, axis=1, keepdims=True)
                acc = alpha * acc + _dot(p.astype(BF16), v_ref[pl.ds(n0, blk), :])
                return m_new, l, acc

            s = scores(k0) + d_ref[hh, 0, pl.ds(sub0, tq), :]
            m = jnp.max(s, axis=1, keepdims=True)
            p = jnp.exp(s - m)
            l = jnp.sum(p, axis=1, keepdims=True)
            acc = _dot(p.astype(BF16), v_ref[pl.ds(k0, blk), :])

            def near(args, hh=hh, scores=scores, update=update):
                n0 = pl.multiple_of(k0 - blk, blk)
                s1 = scores(n0) + d_ref[hh, 1, pl.ds(sub0, tq), :]
                return update(s1, *args, n0)

            m, l, acc = lax.cond(own >= 1, near, lambda args: args, (m, l, acc))

            def far(n, c, scores=scores, update=update, far_bias=far_bias):
                n0 = pl.multiple_of(n * blk, blk)
                return update(scores(n0) + far_bias, *c, n0)

            m, l, acc = lax.fori_loop(0, own - 1, far, (m, l, acc))
            outs.append(acc / l)
        o_ref[pl.ds(r0, tq), :] = jnp.where(head0, outs[0], outs[1]).astype(o_ref.dtype)
        return carry

    lax.fori_loop(0, seq // tq, qtile, 0)


def _moba(za, bias_tiles, far_bias, batch, seq):
    n_pairs = ATTN_HEADS // 2
    assert seq % MOBA_BLOCK == 0 and seq // MOBA_BLOCK <= MAX_BLOCKS
    return pl.pallas_call(
        functools.partial(_moba_kernel, seq=seq),
        out_shape=jax.ShapeDtypeStruct((batch * seq, ATTN_DIM), BF16),
        grid=(n_pairs, batch),
        in_specs=[pl.BlockSpec(memory_space=pltpu.SMEM),
                  pl.BlockSpec((seq, LANES), lambda hp, b: (b, hp)),
                  pl.BlockSpec((seq, LANES), lambda hp, b: (b, n_pairs + hp)),
                  pl.BlockSpec((seq, LANES), lambda hp, b: (b, 2 * n_pairs + hp)),
                  pl.BlockSpec((2, 2, MOBA_BLOCK, MOBA_BLOCK), lambda hp, b: (hp, 0, 0, 0))],
        out_specs=pl.BlockSpec((seq, LANES), lambda hp, b: (b, hp)),
        scratch_shapes=[pltpu.VMEM((2, seq, LANES), BF16),
                        pltpu.VMEM((2, seq, LANES), BF16)],
        compiler_params=pltpu.CompilerParams(
            dimension_semantics=("arbitrary", "arbitrary"), vmem_limit_bytes=VMEM_LIMIT),
        name="moba",
    )(far_bias, za, za, za, bias_tiles)


def _out_ffn_kernel(x_ref, ym_ref, yc_ref, ya_ref, wo_ref, g2_ref, wup_ref, cw_ref, cb_ref,
                    wdn_ref, fg_ref, o_ref, hn_ref, acc_ref, ubuf_ref, tail_ref,
                    *, tiles_per_seq, final_norm):
    tm = TM_FFN
    first = (pl.program_id(0) % tiles_per_seq) == 0
    y = jnp.concatenate([ym_ref[...], yc_ref[...], ya_ref[...]], axis=1)
    x1 = x_ref[...] + _dot(y, wo_ref[...])
    hn_ref[...] = _rms(x1, g2_ref[...]).astype(BF16)
    acc_ref[...] = x1

    for c in range(N_FFN_CHUNKS):
        conv = []
        for part in range(2):
            c0 = part * D_FFP + c * FFN_CHUNK
            u = _dot(hn_ref[...], wup_ref[:, c0:c0 + FFN_CHUNK])
            ubuf_ref[part, 0:SUBLANES, :] = jnp.where(first, 0.0, tail_ref[2 * c + part])
            ubuf_ref[part, SUBLANES:SUBLANES + tm, :] = u
            tail_ref[2 * c + part] = u[tm - SUBLANES:tm, :]
            acc = cb_ref[:, c0:c0 + FFN_CHUNK] + cw_ref[FFN_CONV - 1:FFN_CONV, c0:c0 + FFN_CHUNK] * u
            for j in range(FFN_CONV - 1):
                off = SUBLANES - (FFN_CONV - 1) + j
                acc = acc + cw_ref[j:j + 1, c0:c0 + FFN_CHUNK] * ubuf_ref[part, off:off + tm, :]
            conv.append(acc)
        act = (conv[0] * _sigmoid(conv[0]) * conv[1]).astype(BF16)
        acc_ref[...] += _dot(act, wdn_ref[c * FFN_CHUNK:(c + 1) * FFN_CHUNK, :])

    out = acc_ref[...]
    if final_norm:
        out = _rms(out, fg_ref[...])
    o_ref[...] = out


def _out_ffn(x2d, ym, yc, ya, wo, g2, wup, cw, cb, wdn, fg, seq, final_norm):
    n = x2d.shape[0]
    tiles_per_seq = seq // TM_FFN
    resident = functools.partial(pl.BlockSpec, pipeline_mode=pl.Buffered(1))

    def const(shape):
        return resident(shape, lambda i: (0,) * len(shape))

    return pl.pallas_call(
        functools.partial(_out_ffn_kernel, tiles_per_seq=tiles_per_seq, final_norm=final_norm),
        out_shape=jax.ShapeDtypeStruct((n, D_MODEL), F32),
        grid=(n // TM_FFN,),
        in_specs=[pl.BlockSpec((TM_FFN, D_MODEL), lambda i: (i, 0)),
                  pl.BlockSpec((TM_FFN, MLSTM_DIM), lambda i: (i, 0)),
                  pl.BlockSpec((TM_FFN, CONV_DIM), lambda i: (i, 0)),
                  pl.BlockSpec((TM_FFN, ATTN_DIM), lambda i: (i, 0)),
                  const((D_MODEL, D_MODEL)),
                  const((1, D_MODEL)),
                  const((D_MODEL, 2 * D_FFP)),
                  const((FFN_CONV, 2 * D_FFP)),
                  const((1, 2 * D_FFP)),
                  const((D_FFP, D_MODEL)),
                  const((1, D_MODEL))],
        out_specs=pl.BlockSpec((TM_FFN, D_MODEL), lambda i: (i, 0)),
        scratch_shapes=[pltpu.VMEM((TM_FFN, D_MODEL), BF16),
                        pltpu.VMEM((TM_FFN, D_MODEL), F32),
                        pltpu.VMEM((2, SUBLANES + TM_FFN, FFN_CHUNK), F32),
                        pltpu.VMEM((2 * N_FFN_CHUNKS, SUBLANES, FFN_CHUNK), F32)],
        compiler_params=pltpu.CompilerParams(
            dimension_semantics=("arbitrary",), vmem_limit_bytes=VMEM_LIMIT),
        name="out_ffn",
    )(x2d, ym, yc, ya, wo, g2, wup, cw, cb, wdn, fg)


def _pad_cols(a, width):
    return jnp.pad(a, ((0, 0), (0, width - a.shape[1])))


def _prep_w_in(w_in):
    m_end = 4 * MLSTM_DIM
    g_end = m_end + 2 * MLSTM_HEADS
    c_end = g_end + 2 * CONV_DIM
    return jnp.concatenate([w_in[:, :m_end], _pad_cols(w_in[:, m_end:g_end], GATE_PAD),
                            w_in[:, g_end:c_end], w_in[:, c_end:]], axis=1).astype(BF16)


def _prep_ffn_cols(a):
    return jnp.concatenate([_pad_cols(a[:, :D_FF], D_FFP), _pad_cols(a[:, D_FF:], D_FFP)], axis=1)


def _layer(x2d, batch, seq, bias_tiles, far_bias, norm1_g, w_in, qk_w, qk_b, ig_b, fg_b, head_g,
           dw_w, dw_b, ln_g, ln_b, w_out, norm2_g, w_up, f_w, f_b, w_down, final_g, final_norm):
    zm, zc, za = _in_proj(x2d, norm1_g[None, :], _prep_w_in(w_in))
    gate_bias = _pad_cols(jnp.concatenate([ig_b, fg_b])[None, :], GATE_PAD)
    ym = _mlstm(zm, qk_w, qk_b[None, :], gate_bias, head_g[None, :], batch, seq)
    yc = _conformer(zc, dw_w, dw_b[None, :], ln_g[None, :], ln_b[None, :], batch, seq)
    ya = _moba(za, bias_tiles, far_bias, batch, seq)
    return _out_ffn(x2d, ym, yc, ya, w_out.astype(BF16), norm2_g[None, :],
                    _prep_ffn_cols(w_up).astype(BF16), _prep_ffn_cols(f_w),
                    _prep_ffn_cols(f_b[None, :]),
                    jnp.pad(w_down, ((0, D_FFP - D_FF), (0, 0))).astype(BF16),
                    final_g[None, :], seq, final_norm)


def kernel(x, norm1_g, w_in, mlstm_qk_conv_w, mlstm_qk_conv_b, mlstm_ig_b, mlstm_fg_b, mlstm_head_g, conf_dw_w, conf_dw_b, conf_ln_g, conf_ln_b, rel_bias, w_out, norm2_g, ffn_w_up, ffn_conv_w, ffn_conv_b, ffn_w_down, final_g):
    batch, seq, _ = x.shape
    bias_tiles = _bias_tiles(rel_bias)
    far_bias = rel_bias[REL_BUCKETS - 1]
    x2d = x.reshape(batch * seq, D_MODEL)
    for l in range(DEPTH):
        x2d = _layer(x2d, batch, seq, bias_tiles, far_bias, norm1_g[l], w_in[l],
                     mlstm_qk_conv_w[l], mlstm_qk_conv_b[l], mlstm_ig_b[l], mlstm_fg_b[l],
                     mlstm_head_g[l], conf_dw_w[l], conf_dw_b[l], conf_ln_g[l], conf_ln_b[l],
                     w_out[l], norm2_g[l], ffn_w_up[l], ffn_conv_w[l], ffn_conv_b[l],
                     ffn_w_down[l], final_g, l == DEPTH - 1)
    return x2d.reshape(batch, seq, D_MODEL)
```

```python
import functools
import math

import numpy as np
import jax
import jax.numpy as jnp
from jax import lax
from jax.experimental import pallas as pl
from jax.experimental.pallas import tpu as pltpu

F32 = jnp.float32
BF16 = jnp.bfloat16
HIGHEST = lax.Precision.HIGHEST

D_MODEL = 1024
DEPTH = 2
HEAD_DIM = 64
MLSTM_DIM = 256
CONV_DIM = 256
ATTN_DIM = 512
MLSTM_HEADS = MLSTM_DIM // HEAD_DIM
ATTN_HEADS = ATTN_DIM // HEAD_DIM
QK_CONV = 4
CONF_KERNEL = 31
MOBA_BLOCK = 256
MOBA_TOPK = 3
REL_BUCKETS = 32
REL_MAX_DIST = 128
D_FF = 2752
FFN_CONV = 3
EPS = 1e-6
NEG = -1e30

LANES = 128
SUBLANES = 8
VMEM_LIMIT = 56 * 1024 * 1024

GATE_PAD = LANES
ZM_W = 4 * MLSTM_DIM + GATE_PAD
ZC_W = 2 * CONV_DIM
ZA_W = 3 * ATTN_DIM
P_W = ZM_W + ZC_W + ZA_W
D_FFP = -(-D_FF // 256) * 256
FFN_CHUNK = 256
N_FFN_CHUNKS = D_FFP // FFN_CHUNK

TM_IN = 512
TM_FFN = 512
MLSTM_L = 256
CONF_ROWS = 64
CONF_PAD = 32
MAX_BLOCKS = SUBLANES
PEN_BIG = 2.0 ** 100

_NT = (((1,), (1,)), ((), ()))


def _t5_saturation_distance():
    n = np.arange(1, 4 * MOBA_BLOCK, dtype=np.float32)
    max_exact = REL_BUCKETS // 2
    large = max_exact + (np.log(n / max_exact) / math.log(REL_MAX_DIST / max_exact)
                         * (REL_BUCKETS - max_exact)).astype(np.int32)
    bucket = np.where(n < max_exact, n.astype(np.int32), np.minimum(large, REL_BUCKETS - 1))
    not_last = np.nonzero(bucket != REL_BUCKETS - 1)[0]
    return int(n[not_last[-1]]) + 1


assert _t5_saturation_distance() <= MOBA_BLOCK + 1


def _sigmoid(x):
    return 1.0 / (1.0 + jnp.exp(-x))


def _log_sigmoid(x):
    return jnp.minimum(x, 0.0) - jnp.log1p(jnp.exp(-jnp.abs(x)))


def _rms(xf, g):
    return xf * lax.rsqrt(jnp.mean(xf * xf, axis=-1, keepdims=True) + EPS) * g


def _dot(a, b, **kw):
    return jnp.dot(a, b, preferred_element_type=F32, **kw)


def _const_spec(shape):
    nd = len(shape)
    return pl.BlockSpec(shape, lambda *_: (0,) * nd)


def _in_proj_kernel(x_ref, g_ref, w_ref, zm_ref, zc_ref, za_ref):
    h = _rms(x_ref[...], g_ref[...]).astype(BF16)
    zm_ref[...] = _dot(h, w_ref[:, 0:ZM_W])
    zc_ref[...] = _dot(h, w_ref[:, ZM_W:ZM_W + ZC_W])
    za_ref[...] = _dot(h, w_ref[:, ZM_W + ZC_W:P_W]).astype(BF16)


def _in_proj(x2d, g, w_r):
    n = x2d.shape[0]
    return pl.pallas_call(
        _in_proj_kernel,
        out_shape=(jax.ShapeDtypeStruct((n, ZM_W), F32),
                   jax.ShapeDtypeStruct((n, ZC_W), F32),
                   jax.ShapeDtypeStruct((n, ZA_W), BF16)),
        grid=(n // TM_IN,),
        in_specs=[pl.BlockSpec((TM_IN, D_MODEL), lambda i: (i, 0)),
                  _const_spec((1, D_MODEL)),
                  _const_spec((D_MODEL, P_W))],
        out_specs=(pl.BlockSpec((TM_IN, ZM_W), lambda i: (i, 0)),
                   pl.BlockSpec((TM_IN, ZC_W), lambda i: (i, 0)),
                   pl.BlockSpec((TM_IN, ZA_W), lambda i: (i, 0))),
        compiler_params=pltpu.CompilerParams(
            dimension_semantics=("arbitrary",), vmem_limit_bytes=VMEM_LIMIT),
        name="in_proj",
    )(x2d, g, w_r)


def _mlstm_kernel(zm_ref, cw_ref, cb_ref, gb_ref, hg_ref, o_ref, g_ref, gt_ref, *, seq):
    L = MLSTM_L
    n_chunks = seq // L
    scale = HEAD_DIM ** -0.5
    lane = lax.broadcasted_iota(jnp.int32, (1, LANES), 1)
    head0 = lane < HEAD_DIM

    gates = zm_ref[:, 4 * MLSTM_DIM:ZM_W] + gb_ref[...]
    is_f = (lane >= MLSTM_HEADS) & (lane < 2 * MLSTM_HEADS)
    gates = jnp.where(is_f, _log_sigmoid(gates), gates)
    g_ref[...] = gates
    gt_ref[...] = gates.T[0:SUBLANES, :]

    ti = lax.broadcasted_iota(jnp.int32, (L, L), 0)
    si = lax.broadcasted_iota(jnp.int32, (L, L), 1)
    causal = si <= ti
    tri = causal.astype(F32)
    tri_t = (ti <= si).astype(F32)
    srow_head = lax.broadcasted_iota(jnp.int32, (LANES, 2 * LANES), 0) // HEAD_DIM
    scol = lax.broadcasted_iota(jnp.int32, (LANES, 2 * LANES), 1)
    state_mask = ((scol < LANES) & (scol // HEAD_DIM == srow_head)) | (scol == LANES + srow_head)
    ones_col = jnp.where(lane == 0, 1.0, 0.0).astype(BF16)
    cb = cb_ref[...]
    hg = hg_ref[...]

    def chunk(c, carry):
        states, m_prev = carry
        r0 = pl.multiple_of(c * L, L)
        gc = g_ref[pl.ds(r0, L), :]
        gtc = gt_ref[:, pl.ds(r0, L)]
        bcols = _dot(tri, gc, precision=HIGHEST)
        brows = _dot(gtc, tri_t, precision=HIGHEST)

        xcur = zm_ref[pl.ds(r0, L), 0:2 * MLSTM_DIM]
        pr = pl.multiple_of(jnp.maximum(r0 - SUBLANES, 0), SUBLANES)
        xprev = jnp.where(c > 0, zm_ref[pl.ds(pr, SUBLANES), 0:2 * MLSTM_DIM], 0.0)
        xcat = jnp.concatenate([xprev, xcur], axis=0)
        y = cb
        for j in range(QK_CONV):
            off = SUBLANES - (QK_CONV - 1) + j
            y = y + cw_ref[j:j + 1, :] * xcat[off:off + L]
        qk = y * _sigmoid(y)

        new_states = []
        new_m = []
        for p in range(2):
            lo = p * LANES
            qp = (qk[:, lo:lo + LANES] * scale).astype(BF16)
            kpf = qk[:, MLSTM_DIM + lo:MLSTM_DIM + lo + LANES]
            kpb = kpf.astype(BF16)
            vp = zm_ref[pl.ds(r0, L), 2 * MLSTM_DIM + lo:2 * MLSTM_DIM + lo + LANES].astype(BF16)
            vaug = jnp.concatenate([vp, jnp.broadcast_to(ones_col, (L, LANES))], axis=1)
            inter = _dot(qp, states[p].astype(BF16))
            hvals, wks, decays = [], [], []
            for hh in range(2):
                h = 2 * p + hh
                mp = m_prev[h]
                bcol = bcols[:, MLSTM_HEADS + h:MLSTM_HEADS + h + 1]
                icol = gc[:, h:h + 1]
                brow = brows[MLSTM_HEADS + h:MLSTM_HEADS + h + 1, :]
                irow = gtc[h:h + 1, :]
                dmat = jnp.where(causal, bcol - brow + irow, NEG)
                m_t = jnp.maximum(bcol + mp, jnp.max(dmat, axis=1, keepdims=True))
                w = jnp.exp(dmat - m_t)
                qh = jnp.where(head0 if hh == 0 else ~head0, qp, jnp.zeros_like(qp))
                s = lax.dot_general(qh, kpb, _NT, preferred_element_type=F32)
                r = _dot((s * w).astype(BF16), vaug)
                a = jnp.exp(bcol + mp - m_t)
                num = a * inter[:, 0:LANES] + r[:, 0:LANES]
                den = a * inter[:, LANES + hh:LANES + hh + 1] + r[:, LANES:LANES + 1]
                hvals.append(num / jnp.maximum(jnp.abs(den), jnp.exp(-m_t)))
                btot = bcol[L - 1:L, :]
                gk = btot - bcol + icol
                m_new = jnp.maximum(btot + mp, jnp.max(gk, axis=0, keepdims=True))
                wks.append(jnp.exp(gk - m_new))
                decays.append(jnp.exp(btot + mp - m_new))
                new_m.append(m_new)
            kw = kpf * jnp.where(head0, wks[0], wks[1])
            upd = _dot(kw.T.astype(BF16), vaug)
            decay_rows = jnp.where(srow_head[:, 0:1] == 0, decays[0], decays[1])
            new_states.append(decay_rows * states[p] + jnp.where(state_mask, upd, 0.0))

            hv = jnp.where(head0, hvals[0], hvals[1])
            inv = 1.0 / HEAD_DIM
            mu0 = jnp.sum(jnp.where(head0, hv, 0.0), axis=1, keepdims=True) * inv
            mu1 = jnp.sum(jnp.where(head0, 0.0, hv), axis=1, keepdims=True) * inv
            d = hv - jnp.where(head0, mu0, mu1)
            dd = d * d
            v0 = jnp.sum(jnp.where(head0, dd, 0.0), axis=1, keepdims=True) * inv
            v1 = jnp.sum(jnp.where(head0, 0.0, dd), axis=1, keepdims=True) * inv
            hn = d * lax.rsqrt(jnp.where(head0, v0, v1) + EPS)
            og = zm_ref[pl.ds(r0, L), 3 * MLSTM_DIM + lo:3 * MLSTM_DIM + lo + LANES]
            o_ref[pl.ds(r0, L), lo:lo + LANES] = (hn * hg[:, lo:lo + LANES] * _sigmoid(og)).astype(o_ref.dtype)
        return tuple(new_states), tuple(new_m)

    init = (tuple(jnp.zeros((LANES, 2 * LANES), F32) for _ in range(2)),
            tuple(jnp.full((1, 1), NEG, F32) for _ in range(MLSTM_HEADS)))
    lax.fori_loop(0, n_chunks, chunk, init)


def _mlstm(zm, cw, cb, gate_bias, head_g, batch, seq):
    return pl.pallas_call(
        functools.partial(_mlstm_kernel, seq=seq),
        out_shape=jax.ShapeDtypeStruct((batch * seq, MLSTM_DIM), BF16),
        grid=(batch,),
        in_specs=[pl.BlockSpec((seq, ZM_W), lambda b: (b, 0)),
                  _const_spec((QK_CONV, 2 * MLSTM_DIM)),
                  _const_spec((1, 2 * MLSTM_DIM)),
                  _const_spec((1, GATE_PAD)),
                  _const_spec((1, MLSTM_DIM))],
        out_specs=pl.BlockSpec((seq, MLSTM_DIM), lambda b: (b, 0)),
        scratch_shapes=[pltpu.VMEM((seq, GATE_PAD), F32),
                        pltpu.VMEM((SUBLANES, seq), F32)],
        compiler_params=pltpu.CompilerParams(
            dimension_semantics=("arbitrary",), vmem_limit_bytes=VMEM_LIMIT),
        name="mlstm",
    )(zm, cw, cb, gate_bias, head_g)


def _conformer_kernel(zc_ref, w_ref, b_ref, lg_ref, lb_ref, o_ref, up_ref, *, seq):
    up_ref[0:CONF_PAD, :] = jnp.zeros((CONF_PAD, CONV_DIM), F32)
    up_ref[CONF_PAD:CONF_PAD + seq, :] = zc_ref[:, 0:CONV_DIM] * _sigmoid(zc_ref[:, CONV_DIM:ZC_W])
    bias = b_ref[...]
    lg = lg_ref[...]
    lb = lb_ref[...]

    def tile(c, carry):
        r0 = pl.multiple_of(c * CONF_ROWS, CONF_ROWS)
        win = up_ref[pl.ds(r0, CONF_ROWS + CONF_PAD), :]
        acc = jnp.broadcast_to(bias, (CONF_ROWS, CONV_DIM))
        for j in range(CONF_KERNEL):
            off = CONF_PAD - (CONF_KERNEL - 1) + j
            acc = acc + w_ref[j:j + 1, :] * win[off:off + CONF_ROWS]
        mu = jnp.mean(acc, axis=-1, keepdims=True)
        d = acc - mu
        var = jnp.mean(d * d, axis=-1, keepdims=True)
        y = d * lax.rsqrt(var + EPS) * lg + lb
        o_ref[pl.ds(r0, CONF_ROWS), :] = (y * _sigmoid(y)).astype(o_ref.dtype)
        return carry

    lax.fori_loop(0, seq // CONF_ROWS, tile, 0)


def _conformer(zc, w, b, ln_g, ln_b, batch, seq):
    return pl.pallas_call(
        functools.partial(_conformer_kernel, seq=seq),
        out_shape=jax.ShapeDtypeStruct((batch * seq, CONV_DIM), BF16),
        grid=(batch,),
        in_specs=[pl.BlockSpec((seq, ZC_W), lambda b: (b, 0)),
                  _const_spec((CONF_KERNEL, CONV_DIM)),
                  _const_spec((1, CONV_DIM)),
                  _const_spec((1, CONV_DIM)),
                  _const_spec((1, CONV_DIM))],
        out_specs=pl.BlockSpec((seq, CONV_DIM), lambda b: (b, 0)),
        scratch_shapes=[pltpu.VMEM((CONF_PAD + seq, CONV_DIM), F32)],
        compiler_params=pltpu.CompilerParams(
            dimension_semantics=("arbitrary",), vmem_limit_bytes=VMEM_LIMIT),
        name="conformer",
    )(zc, w, b, ln_g, ln_b)


def _bias_tiles_kernel(rb_ref, o_ref):
    h = pl.program_id(0)
    i = lax.broadcasted_iota(jnp.int32, (MOBA_BLOCK, MOBA_BLOCK), 0)
    j = lax.broadcasted_iota(jnp.int32, (MOBA_BLOCK, MOBA_BLOCK), 1)
    max_exact = REL_BUCKETS // 2
    for t in range(2):
        dist = i - j + t * MOBA_BLOCK
        n = jnp.maximum(dist, 0)
        nf = jnp.maximum(n, 1).astype(F32)
        large = max_exact + (jnp.log(nf / max_exact) / math.log(REL_MAX_DIST / max_exact)
                             * (REL_BUCKETS - max_exact)).astype(jnp.int32)
        large = jnp.minimum(large, REL_BUCKETS - 1)
        bucket = jnp.where(n < max_exact, n, large)
        bias = jnp.zeros((MOBA_BLOCK, MOBA_BLOCK), F32)
        for bk in range(REL_BUCKETS):
            bias = jnp.where(bucket == bk, rb_ref[bk, h], bias)
        if t == 0:
            bias = jnp.where(dist >= 0, bias, NEG)
        o_ref[0, t] = bias


def _bias_tiles(rel_bias):
    return pl.pallas_call(
        _bias_tiles_kernel,
        out_shape=jax.ShapeDtypeStruct((ATTN_HEADS, 2, MOBA_BLOCK, MOBA_BLOCK), F32),
        grid=(ATTN_HEADS,),
        in_specs=[pl.BlockSpec(memory_space=pltpu.SMEM)],
        out_specs=pl.BlockSpec((1, 2, MOBA_BLOCK, MOBA_BLOCK), lambda h: (h, 0, 0, 0)),
        compiler_params=pltpu.CompilerParams(dimension_semantics=("arbitrary",)),
        name="bias_tiles",
    )(rel_bias)


def _moba_kernel(far_ref, q_ref, k_ref, v_ref, d_ref, o_ref, qa_ref, ka_ref, s_ref, *, seq):
    hp = pl.program_id(0)
    blk = MOBA_BLOCK
    nb = seq // blk
    scale = HEAD_DIM ** -0.5
    lane = lax.broadcasted_iota(jnp.int32, (1, LANES), 1)
    head0 = lane < HEAD_DIM

    kf = k_ref[...].astype(F32)
    qf = q_ref[...].astype(F32)
    means = [jnp.mean(kf[n * blk:(n + 1) * blk], axis=0, keepdims=True) for n in range(nb)]
    if nb < MAX_BLOCKS:
        means.append(jnp.zeros((MAX_BLOCKS - nb, LANES), F32))
    kmean = jnp.concatenate(means, axis=0)
    blk_i = lax.broadcasted_iota(jnp.int32, (MAX_BLOCKS, seq), 0)
    own_i = lax.broadcasted_iota(jnp.int32, (MAX_BLOCKS, seq), 1) // blk
    row_blk = lax.broadcasted_iota(jnp.int32, (seq, LANES), 0) // blk
    lane_full = lax.broadcasted_iota(jnp.int32, (seq, LANES), 1)

    for hh in range(2):
        mine = head0 if hh == 0 else ~head0
        gate = lax.dot_general(jnp.where(mine, kmean, 0.0), qf, _NT,
                               precision=HIGHEST, preferred_element_type=F32)
        cand = blk_i < own_i
        g = jnp.where(cand, gate, NEG)
        rank = jnp.zeros((MAX_BLOCKS, seq), jnp.int32)
        for m in range(MAX_BLOCKS):
            gm = g[m:m + 1, :]
            beats = (gm > g) | ((gm == g) & (blk_i > m))
            rank = rank + beats.astype(jnp.int32)
        keep = (cand & (rank < MOBA_TOPK)) | (blk_i >= own_i)
        pen = jnp.where(keep, 0.0, -PEN_BIG)
        pen_lane0 = (1 - hh) * HEAD_DIM
        parts = [pen]
        if pen_lane0:
            parts.insert(0, jnp.zeros((pen_lane0, seq), F32))
        tail_rows = LANES - MAX_BLOCKS - pen_lane0
        if tail_rows:
            parts.append(jnp.zeros((tail_rows, seq), F32))
        pen_t = jnp.concatenate(parts, axis=0).T
        qa_ref[hh] = jnp.where(mine, qf * scale, pen_t).astype(BF16)
        onehot = (lane_full == pen_lane0 + row_blk).astype(F32)
        ka_ref[hh] = jnp.where(mine, kf, onehot).astype(BF16)

    half = blk // 2
    for own in range(nb):
        r0 = own * blk
        outs = []
        for hh in range(2):
            q = qa_ref[hh, r0:r0 + blk, :]
            far_bias = far_ref[2 * hp + hh]
            mx = None
            for n in range(own + 1):
                s = lax.dot_general(q, ka_ref[hh, n * blk:(n + 1) * blk, :], _NT,
                                    preferred_element_type=F32)
                if n == own:
                    s = s + d_ref[hh, 0]
                elif n == own - 1:
                    s = s + d_ref[hh, 1]
                else:
                    s = s + far_bias
                s_ref[hh, n] = s
                folded = jnp.maximum(s[:, :half], s[:, half:])
                mx = folded if mx is None else jnp.maximum(mx, folded)
            m = jnp.max(mx, axis=1, keepdims=True)
            lsum = None
            acc = None
            for n in range(own + 1):
                p = jnp.exp(s_ref[hh, n] - m)
                folded = p[:, :half] + p[:, half:]
                lsum = folded if lsum is None else lsum + folded
                pv = _dot(p.astype(BF16), v_ref[n * blk:(n + 1) * blk, :])
                acc = pv if acc is None else acc + pv
            outs.append(acc / jnp.sum(lsum, axis=1, keepdims=True))
        o_ref[r0:r0 + blk, :] = jnp.where(head0, outs[0], outs[1]).astype(o_ref.dtype)


def _moba(za, bias_tiles, far_bias, batch, seq):
    n_pairs = ATTN_HEADS // 2
    assert seq % MOBA_BLOCK == 0 and seq // MOBA_BLOCK <= MAX_BLOCKS
    return pl.pallas_call(
        functools.partial(_moba_kernel, seq=seq),
        out_shape=jax.ShapeDtypeStruct((batch * seq, ATTN_DIM), BF16),
        grid=(n_pairs, batch),
        in_specs=[pl.BlockSpec(memory_space=pltpu.SMEM),
                  pl.BlockSpec((seq, LANES), lambda hp, b: (b, hp)),
                  pl.BlockSpec((seq, LANES), lambda hp, b: (b, n_pairs + hp)),
                  pl.BlockSpec((seq, LANES), lambda hp, b: (b, 2 * n_pairs + hp)),
                  pl.BlockSpec((2, 2, MOBA_BLOCK, MOBA_BLOCK), lambda hp, b: (hp, 0, 0, 0))],
        out_specs=pl.BlockSpec((seq, LANES), lambda hp, b: (b, hp)),
        scratch_shapes=[pltpu.VMEM((2, seq, LANES), BF16),
                        pltpu.VMEM((2, seq, LANES), BF16),
                        pltpu.VMEM((2, seq // MOBA_BLOCK, MOBA_BLOCK, MOBA_BLOCK), F32)],
        compiler_params=pltpu.CompilerParams(
            dimension_semantics=("arbitrary", "arbitrary"), vmem_limit_bytes=VMEM_LIMIT),
        name="moba",
    )(far_bias, za, za, za, bias_tiles)


def _out_ffn_kernel(x_ref, ym_ref, yc_ref, ya_ref, wo_ref, g2_ref, wup_ref, cw_ref, cb_ref,
                    wdn_ref, fg_ref, o_ref, hn_ref, acc_ref, ubuf_ref, tail_ref,
                    *, tiles_per_seq, final_norm):
    tm = TM_FFN
    first = (pl.program_id(0) % tiles_per_seq) == 0
    y = jnp.concatenate([ym_ref[...], yc_ref[...], ya_ref[...]], axis=1)
    x1 = x_ref[...] + _dot(y, wo_ref[...])
    hn_ref[...] = _rms(x1, g2_ref[...]).astype(BF16)
    acc_ref[...] = x1

    for c in range(N_FFN_CHUNKS):
        conv = []
        for part in range(2):
            c0 = part * D_FFP + c * FFN_CHUNK
            u = _dot(hn_ref[...], wup_ref[:, c0:c0 + FFN_CHUNK])
            ubuf_ref[part, 0:SUBLANES, :] = jnp.where(first, 0.0, tail_ref[2 * c + part])
            ubuf_ref[part, SUBLANES:SUBLANES + tm, :] = u
            tail_ref[2 * c + part] = u[tm - SUBLANES:tm, :]
            acc = cb_ref[:, c0:c0 + FFN_CHUNK] + cw_ref[FFN_CONV - 1:FFN_CONV, c0:c0 + FFN_CHUNK] * u
            for j in range(FFN_CONV - 1):
                off = SUBLANES - (FFN_CONV - 1) + j
                acc = acc + cw_ref[j:j + 1, c0:c0 + FFN_CHUNK] * ubuf_ref[part, off:off + tm, :]
            conv.append(acc)
        act = (conv[0] * _sigmoid(conv[0]) * conv[1]).astype(BF16)
        acc_ref[...] += _dot(act, wdn_ref[c * FFN_CHUNK:(c + 1) * FFN_CHUNK, :])

    out = acc_ref[...]
    if final_norm:
        out = _rms(out, fg_ref[...])
    o_ref[...] = out


def _out_ffn(x2d, ym, yc, ya, wo, g2, wup, cw, cb, wdn, fg, seq, final_norm):
    n = x2d.shape[0]
    tiles_per_seq = seq // TM_FFN
    resident = functools.partial(pl.BlockSpec, pipeline_mode=pl.Buffered(1))

    def const(shape):
        return resident(shape, lambda i: (0,) * len(shape))

    return pl.pallas_call(
        functools.partial(_out_ffn_kernel, tiles_per_seq=tiles_per_seq, final_norm=final_norm),
        out_shape=jax.ShapeDtypeStruct((n, D_MODEL), F32),
        grid=(n // TM_FFN,),
        in_specs=[pl.BlockSpec((TM_FFN, D_MODEL), lambda i: (i, 0)),
                  pl.BlockSpec((TM_FFN, MLSTM_DIM), lambda i: (i, 0)),
                  pl.BlockSpec((TM_FFN, CONV_DIM), lambda i: (i, 0)),
                  pl.BlockSpec((TM_FFN, ATTN_DIM), lambda i: (i, 0)),
                  const((D_MODEL, D_MODEL)),
                  const((1, D_MODEL)),
                  const((D_MODEL, 2 * D_FFP)),
                  const((FFN_CONV, 2 * D_FFP)),
                  const((1, 2 * D_FFP)),
                  const((D_FFP, D_MODEL)),
                  const((1, D_MODEL))],
        out_specs=pl.BlockSpec((TM_FFN, D_MODEL), lambda i: (i, 0)),
        scratch_shapes=[pltpu.VMEM((TM_FFN, D_MODEL), BF16),
                        pltpu.VMEM((TM_FFN, D_MODEL), F32),
                        pltpu.VMEM((2, SUBLANES + TM_FFN, FFN_CHUNK), F32),
                        pltpu.VMEM((2 * N_FFN_CHUNKS, SUBLANES, FFN_CHUNK), F32)],
        compiler_params=pltpu.CompilerParams(
            dimension_semantics=("arbitrary",), vmem_limit_bytes=VMEM_LIMIT),
        name="out_ffn",
    )(x2d, ym, yc, ya, wo, g2, wup, cw, cb, wdn, fg)


def _pad_cols(a, width):
    return jnp.pad(a, ((0, 0), (0, width - a.shape[1])))


def _prep_w_in(w_in):
    m_end = 4 * MLSTM_DIM
    g_end = m_end + 2 * MLSTM_HEADS
    c_end = g_end + 2 * CONV_DIM
    return jnp.concatenate([w_in[:, :m_end], _pad_cols(w_in[:, m_end:g_end], GATE_PAD),
                            w_in[:, g_end:c_end], w_in[:, c_end:]], axis=1).astype(BF16)


def _prep_ffn_cols(a):
    return jnp.concatenate([_pad_cols(a[:, :D_FF], D_FFP), _pad_cols(a[:, D_FF:], D_FFP)], axis=1)


def _layer(x2d, batch, seq, bias_tiles, far_bias, norm1_g, w_in, qk_w, qk_b, ig_b, fg_b, head_g,
           dw_w, dw_b, ln_g, ln_b, w_out, norm2_g, w_up, f_w, f_b, w_down, final_g, final_norm):
    zm, zc, za = _in_proj(x2d, norm1_g[None, :], _prep_w_in(w_in))
    gate_bias = _pad_cols(jnp.concatenate([ig_b, fg_b])[None, :], GATE_PAD)
    ym = _mlstm(zm, qk_w, qk_b[None, :], gate_bias, head_g[None, :], batch, seq)
    yc = _conformer(zc, dw_w, dw_b[None, :], ln_g[None, :], ln_b[None, :], batch, seq)
    ya = _moba(za, bias_tiles, far_bias, batch, seq)
    return _out_ffn(x2d, ym, yc, ya, w_out.astype(BF16), norm2_g[None, :],
                    _prep_ffn_cols(w_up).astype(BF16), _prep_ffn_cols(f_w),
                    _prep_ffn_cols(f_b[None, :]),
                    jnp.pad(w_down, ((0, D_FFP - D_FF), (0, 0))).astype(BF16),
                    final_g[None, :], seq, final_norm)


def kernel(x, norm1_g, w_in, mlstm_qk_conv_w, mlstm_qk_conv_b, mlstm_ig_b, mlstm_fg_b, mlstm_head_g, conf_dw_w, conf_dw_b, conf_ln_g, conf_ln_b, rel_bias, w_out, norm2_g, ffn_w_up, ffn_conv_w, ffn_conv_b, ffn_w_down, final_g):
    batch, seq, _ = x.shape
    bias_tiles = _bias_tiles(rel_bias)
    far_bias = rel_bias[REL_BUCKETS - 1]
    x2d = x.reshape(batch * seq, D_MODEL)
    for l in range(DEPTH):
        x2d = _layer(x2d, batch, seq, bias_tiles, far_bias, norm1_g[l], w_in[l],
                     mlstm_qk_conv_w[l], mlstm_qk_conv_b[l], mlstm_ig_b[l], mlstm_fg_b[l],
                     mlstm_head_g[l], conf_dw_w[l], conf_dw_b[l], conf_ln_g[l], conf_ln_b[l],
                     w_out[l], norm2_g[l], ffn_w_up[l], ffn_conv_w[l], ffn_conv_b[l],
                     ffn_w_down[l], final_g, l == DEPTH - 1)
    return x2d.reshape(batch, seq, D_MODEL)
```

```python
import functools
import math

import numpy as np
import jax
import jax.numpy as jnp
from jax import lax
from jax.experimental import pallas as pl
from jax.experimental.pallas import tpu as pltpu

F32 = jnp.float32
BF16 = jnp.bfloat16
HIGHEST = lax.Precision.HIGHEST

D_MODEL = 1024
DEPTH = 2
HEAD_DIM = 64
MLSTM_DIM = 256
CONV_DIM = 256
ATTN_DIM = 512
MLSTM_HEADS = MLSTM_DIM // HEAD_DIM
ATTN_HEADS = ATTN_DIM // HEAD_DIM
QK_CONV = 4
CONF_KERNEL = 31
MOBA_BLOCK = 256
MOBA_TOPK = 3
REL_BUCKETS = 32
REL_MAX_DIST = 128
D_FF = 2752
FFN_CONV = 3
EPS = 1e-6
NEG = -1e30

LANES = 128
SUBLANES = 8
VMEM_LIMIT = 56 * 1024 * 1024

GATE_PAD = LANES
ZM_W = 4 * MLSTM_DIM + GATE_PAD
ZC_W = 2 * CONV_DIM
ZA_W = 3 * ATTN_DIM
P_W = ZM_W + ZC_W + ZA_W
D_FFP = -(-D_FF // 256) * 256
FFN_CHUNK = 256
N_FFN_CHUNKS = D_FFP // FFN_CHUNK
FFN_DOWN_GROUP = 4

TM_IN = 512
TM_FFN = 512
MLSTM_L = 256
CONF_ROWS = 64
CONF_PAD = 32
MAX_BLOCKS = SUBLANES
PEN_BIG = 2.0 ** 100

_NT = (((1,), (1,)), ((), ()))


def _t5_saturation_distance():
    n = np.arange(1, 4 * MOBA_BLOCK, dtype=np.float32)
    max_exact = REL_BUCKETS // 2
    large = max_exact + (np.log(n / max_exact) / math.log(REL_MAX_DIST / max_exact)
                         * (REL_BUCKETS - max_exact)).astype(np.int32)
    bucket = np.where(n < max_exact, n.astype(np.int32), np.minimum(large, REL_BUCKETS - 1))
    not_last = np.nonzero(bucket != REL_BUCKETS - 1)[0]
    return int(n[not_last[-1]]) + 1


assert _t5_saturation_distance() <= MOBA_BLOCK + 1


def _sigmoid(x):
    return 1.0 / (1.0 + jnp.exp(-x))


def _log_sigmoid(x):
    return jnp.minimum(x, 0.0) - jnp.log1p(jnp.exp(-jnp.abs(x)))


def _rms(xf, g):
    return xf * lax.rsqrt(jnp.mean(xf * xf, axis=-1, keepdims=True) + EPS) * g


def _dot(a, b, **kw):
    return jnp.dot(a, b, preferred_element_type=F32, **kw)


def _const_spec(shape):
    nd = len(shape)
    return pl.BlockSpec(shape, lambda *_: (0,) * nd)


def _in_proj_kernel(x_ref, g_ref, w_ref, zm_ref, zc_ref, za_ref):
    h = _rms(x_ref[...], g_ref[...]).astype(BF16)
    zm_ref[...] = _dot(h, w_ref[:, 0:ZM_W])
    zc_ref[...] = _dot(h, w_ref[:, ZM_W:ZM_W + ZC_W])
    za_ref[...] = _dot(h, w_ref[:, ZM_W + ZC_W:P_W]).astype(BF16)


def _in_proj(x2d, g, w_r):
    n = x2d.shape[0]
    return pl.pallas_call(
        _in_proj_kernel,
        out_shape=(jax.ShapeDtypeStruct((n, ZM_W), F32),
                   jax.ShapeDtypeStruct((n, ZC_W), F32),
                   jax.ShapeDtypeStruct((n, ZA_W), BF16)),
        grid=(n // TM_IN,),
        in_specs=[pl.BlockSpec((TM_IN, D_MODEL), lambda i: (i, 0)),
                  _const_spec((1, D_MODEL)),
                  _const_spec((D_MODEL, P_W))],
        out_specs=(pl.BlockSpec((TM_IN, ZM_W), lambda i: (i, 0)),
                   pl.BlockSpec((TM_IN, ZC_W), lambda i: (i, 0)),
                   pl.BlockSpec((TM_IN, ZA_W), lambda i: (i, 0))),
        compiler_params=pltpu.CompilerParams(
            dimension_semantics=("arbitrary",), vmem_limit_bytes=VMEM_LIMIT),
        name="in_proj",
    )(x2d, g, w_r)


def _mlstm_kernel(zm_ref, cw_ref, cb_ref, gb_ref, hg_ref, o_ref, g_ref, gt_ref, *, seq):
    L = MLSTM_L
    n_chunks = seq // L
    scale = HEAD_DIM ** -0.5
    lane = lax.broadcasted_iota(jnp.int32, (1, LANES), 1)
    head0 = lane < HEAD_DIM

    gates = zm_ref[:, 4 * MLSTM_DIM:ZM_W] + gb_ref[...]
    is_f = (lane >= MLSTM_HEADS) & (lane < 2 * MLSTM_HEADS)
    gates = jnp.where(is_f, _log_sigmoid(gates), gates)
    g_ref[...] = gates
    gt_ref[...] = gates.T[0:SUBLANES, :]

    ti = lax.broadcasted_iota(jnp.int32, (L, L), 0)
    si = lax.broadcasted_iota(jnp.int32, (L, L), 1)
    causal = si <= ti
    tri = causal.astype(F32)
    tri_t = (ti <= si).astype(F32)
    srow_head = lax.broadcasted_iota(jnp.int32, (LANES, 2 * LANES), 0) // HEAD_DIM
    scol = lax.broadcasted_iota(jnp.int32, (LANES, 2 * LANES), 1)
    state_mask = ((scol < LANES) & (scol // HEAD_DIM == srow_head)) | (scol == LANES + srow_head)
    ones_col = jnp.where(lane == 0, 1.0, 0.0).astype(BF16)
    cb = cb_ref[...]
    hg = hg_ref[...]

    def chunk(c, carry):
        states, m_prev = carry
        r0 = pl.multiple_of(c * L, L)
        gc = g_ref[pl.ds(r0, L), :]
        gtc = gt_ref[:, pl.ds(r0, L)]
        bcols = _dot(tri, gc, precision=HIGHEST)
        brows = _dot(gtc, tri_t, precision=HIGHEST)

        xcur = zm_ref[pl.ds(r0, L), 0:2 * MLSTM_DIM]
        pr = pl.multiple_of(jnp.maximum(r0 - SUBLANES, 0), SUBLANES)
        xprev = jnp.where(c > 0, zm_ref[pl.ds(pr, SUBLANES), 0:2 * MLSTM_DIM], 0.0)
        xcat = jnp.concatenate([xprev, xcur], axis=0)
        y = cb
        for j in range(QK_CONV):
            off = SUBLANES - (QK_CONV - 1) + j
            y = y + cw_ref[j:j + 1, :] * xcat[off:off + L]
        qk = y * _sigmoid(y)

        new_states = []
        new_m = []
        for p in range(2):
            lo = p * LANES
            qp = (qk[:, lo:lo + LANES] * scale).astype(BF16)
            kpf = qk[:, MLSTM_DIM + lo:MLSTM_DIM + lo + LANES]
            kpb = kpf.astype(BF16)
            vp = zm_ref[pl.ds(r0, L), 2 * MLSTM_DIM + lo:2 * MLSTM_DIM + lo + LANES].astype(BF16)
            vaug = jnp.concatenate([vp, jnp.broadcast_to(ones_col, (L, LANES))], axis=1)
            inter = _dot(qp, states[p].astype(BF16))
            hvals, wks, decays = [], [], []
            for hh in range(2):
                h = 2 * p + hh
                mp = m_prev[h]
                bcol = bcols[:, MLSTM_HEADS + h:MLSTM_HEADS + h + 1]
                icol = gc[:, h:h + 1]
                brow = brows[MLSTM_HEADS + h:MLSTM_HEADS + h + 1, :]
                irow = gtc[h:h + 1, :]
                dmat = jnp.where(causal, bcol - brow + irow, NEG)
                m_t = jnp.maximum(bcol + mp, jnp.max(dmat, axis=1, keepdims=True))
                w = jnp.exp(dmat - m_t)
                qh = jnp.where(head0 if hh == 0 else ~head0, qp, jnp.zeros_like(qp))
                s = lax.dot_general(qh, kpb, _NT, preferred_element_type=F32)
                r = _dot((s * w).astype(BF16), vaug)
                a = jnp.exp(bcol + mp - m_t)
                num = a * inter[:, 0:LANES] + r[:, 0:LANES]
                den = a * inter[:, LANES + hh:LANES + hh + 1] + r[:, LANES:LANES + 1]
                hvals.append(num / jnp.maximum(jnp.abs(den), jnp.exp(-m_t)))
                btot = bcol[L - 1:L, :]
                gk = btot - bcol + icol
                m_new = jnp.maximum(btot + mp, jnp.max(gk, axis=0, keepdims=True))
                wks.append(jnp.exp(gk - m_new))
                decays.append(jnp.exp(btot + mp - m_new))
                new_m.append(m_new)
            kw = kpf * jnp.where(head0, wks[0], wks[1])
            upd = _dot(kw.T.astype(BF16), vaug)
            decay_rows = jnp.where(srow_head[:, 0:1] == 0, decays[0], decays[1])
            new_states.append(decay_rows * states[p] + jnp.where(state_mask, upd, 0.0))

            hv = jnp.where(head0, hvals[0], hvals[1])
            inv = 1.0 / HEAD_DIM
            mu0 = jnp.sum(jnp.where(head0, hv, 0.0), axis=1, keepdims=True) * inv
            mu1 = jnp.sum(jnp.where(head0, 0.0, hv), axis=1, keepdims=True) * inv
            d = hv - jnp.where(head0, mu0, mu1)
            dd = d * d
            v0 = jnp.sum(jnp.where(head0, dd, 0.0), axis=1, keepdims=True) * inv
            v1 = jnp.sum(jnp.where(head0, 0.0, dd), axis=1, keepdims=True) * inv
            hn = d * lax.rsqrt(jnp.where(head0, v0, v1) + EPS)
            og = zm_ref[pl.ds(r0, L), 3 * MLSTM_DIM + lo:3 * MLSTM_DIM + lo + LANES]
            o_ref[pl.ds(r0, L), lo:lo + LANES] = (hn * hg[:, lo:lo + LANES] * _sigmoid(og)).astype(o_ref.dtype)
        return tuple(new_states), tuple(new_m)

    init = (tuple(jnp.zeros((LANES, 2 * LANES), F32) for _ in range(2)),
            tuple(jnp.full((1, 1), NEG, F32) for _ in range(MLSTM_HEADS)))
    lax.fori_loop(0, n_chunks, chunk, init)


def _mlstm(zm, cw, cb, gate_bias, head_g, batch, seq):
    return pl.pallas_call(
        functools.partial(_mlstm_kernel, seq=seq),
        out_shape=jax.ShapeDtypeStruct((batch * seq, MLSTM_DIM), BF16),
        grid=(batch,),
        in_specs=[pl.BlockSpec((seq, ZM_W), lambda b: (b, 0)),
                  _const_spec((QK_CONV, 2 * MLSTM_DIM)),
                  _const_spec((1, 2 * MLSTM_DIM)),
                  _const_spec((1, GATE_PAD)),
                  _const_spec((1, MLSTM_DIM))],
        out_specs=pl.BlockSpec((seq, MLSTM_DIM), lambda b: (b, 0)),
        scratch_shapes=[pltpu.VMEM((seq, GATE_PAD), F32),
                        pltpu.VMEM((SUBLANES, seq), F32)],
        compiler_params=pltpu.CompilerParams(
            dimension_semantics=("arbitrary",), vmem_limit_bytes=VMEM_LIMIT),
        name="mlstm",
    )(zm, cw, cb, gate_bias, head_g)


def _conformer_kernel(zc_ref, w_ref, b_ref, lg_ref, lb_ref, o_ref, up_ref, *, seq):
    up_ref[0:CONF_PAD, :] = jnp.zeros((CONF_PAD, CONV_DIM), F32)
    up_ref[CONF_PAD:CONF_PAD + seq, :] = zc_ref[:, 0:CONV_DIM] * _sigmoid(zc_ref[:, CONV_DIM:ZC_W])
    bias = b_ref[...]
    lg = lg_ref[...]
    lb = lb_ref[...]

    def tile(c, carry):
        r0 = pl.multiple_of(c * CONF_ROWS, CONF_ROWS)
        win = up_ref[pl.ds(r0, CONF_ROWS + CONF_PAD), :]
        acc = jnp.broadcast_to(bias, (CONF_ROWS, CONV_DIM))
        for j in range(CONF_KERNEL):
            off = CONF_PAD - (CONF_KERNEL - 1) + j
            acc = acc + w_ref[j:j + 1, :] * win[off:off + CONF_ROWS]
        mu = jnp.mean(acc, axis=-1, keepdims=True)
        d = acc - mu
        var = jnp.mean(d * d, axis=-1, keepdims=True)
        y = d * lax.rsqrt(var + EPS) * lg + lb
        o_ref[pl.ds(r0, CONF_ROWS), :] = (y * _sigmoid(y)).astype(o_ref.dtype)
        return carry

    lax.fori_loop(0, seq // CONF_ROWS, tile, 0)


def _conformer(zc, w, b, ln_g, ln_b, batch, seq):
    return pl.pallas_call(
        functools.partial(_conformer_kernel, seq=seq),
        out_shape=jax.ShapeDtypeStruct((batch * seq, CONV_DIM), BF16),
        grid=(batch,),
        in_specs=[pl.BlockSpec((seq, ZC_W), lambda b: (b, 0)),
                  _const_spec((CONF_KERNEL, CONV_DIM)),
                  _const_spec((1, CONV_DIM)),
                  _const_spec((1, CONV_DIM)),
                  _const_spec((1, CONV_DIM))],
        out_specs=pl.BlockSpec((seq, CONV_DIM), lambda b: (b, 0)),
        scratch_shapes=[pltpu.VMEM((CONF_PAD + seq, CONV_DIM), F32)],
        compiler_params=pltpu.CompilerParams(
            dimension_semantics=("arbitrary",), vmem_limit_bytes=VMEM_LIMIT),
        name="conformer",
    )(zc, w, b, ln_g, ln_b)


def _bias_tiles_kernel(rb_ref, o_ref):
    h = pl.program_id(0)
    i = lax.broadcasted_iota(jnp.int32, (MOBA_BLOCK, MOBA_BLOCK), 0)
    j = lax.broadcasted_iota(jnp.int32, (MOBA_BLOCK, MOBA_BLOCK), 1)
    max_exact = REL_BUCKETS // 2
    for t in range(2):
        dist = i - j + t * MOBA_BLOCK
        n = jnp.maximum(dist, 0)
        nf = jnp.maximum(n, 1).astype(F32)
        large = max_exact + (jnp.log(nf / max_exact) / math.log(REL_MAX_DIST / max_exact)
                             * (REL_BUCKETS - max_exact)).astype(jnp.int32)
        large = jnp.minimum(large, REL_BUCKETS - 1)
        bucket = jnp.where(n < max_exact, n, large)
        bias = jnp.zeros((MOBA_BLOCK, MOBA_BLOCK), F32)
        for bk in range(REL_BUCKETS):
            bias = jnp.where(bucket == bk, rb_ref[bk, h], bias)
        if t == 0:
            bias = jnp.where(dist >= 0, bias, NEG)
        o_ref[0, t] = bias


def _bias_tiles(rel_bias):
    return pl.pallas_call(
        _bias_tiles_kernel,
        out_shape=jax.ShapeDtypeStruct((ATTN_HEADS, 2, MOBA_BLOCK, MOBA_BLOCK), F32),
        grid=(ATTN_HEADS,),
        in_specs=[pl.BlockSpec(memory_space=pltpu.SMEM)],
        out_specs=pl.BlockSpec((1, 2, MOBA_BLOCK, MOBA_BLOCK), lambda h: (h, 0, 0, 0)),
        compiler_params=pltpu.CompilerParams(dimension_semantics=("arbitrary",)),
        name="bias_tiles",
    )(rel_bias)


def _moba_kernel(far_ref, q_ref, k_ref, v_ref, d_ref, o_ref, qa_ref, ka_ref, s_ref, *, seq):
    hp = pl.program_id(0)
    blk = MOBA_BLOCK
    nb = seq // blk
    scale = HEAD_DIM ** -0.5
    lane = lax.broadcasted_iota(jnp.int32, (1, LANES), 1)
    head0 = lane < HEAD_DIM

    kf = k_ref[...].astype(F32)
    qf = q_ref[...].astype(F32)
    means = [jnp.mean(kf[n * blk:(n + 1) * blk], axis=0, keepdims=True) for n in range(nb)]
    if nb < MAX_BLOCKS:
        means.append(jnp.zeros((MAX_BLOCKS - nb, LANES), F32))
    kmean = jnp.concatenate(means, axis=0)
    blk_i = lax.broadcasted_iota(jnp.int32, (MAX_BLOCKS, seq), 0)
    own_i = lax.broadcasted_iota(jnp.int32, (MAX_BLOCKS, seq), 1) // blk
    row_blk = lax.broadcasted_iota(jnp.int32, (seq, LANES), 0) // blk
    lane_full = lax.broadcasted_iota(jnp.int32, (seq, LANES), 1)

    for hh in range(2):
        mine = head0 if hh == 0 else ~head0
        gate = lax.dot_general(jnp.where(mine, kmean, 0.0), qf, _NT,
                               precision=HIGHEST, preferred_element_type=F32)
        cand = blk_i < own_i
        g = jnp.where(cand, gate, NEG)
        rank = jnp.zeros((MAX_BLOCKS, seq), jnp.int32)
        for m in range(MAX_BLOCKS):
            gm = g[m:m + 1, :]
            beats = (gm > g) | ((gm == g) & (blk_i > m))
            rank = rank + beats.astype(jnp.int32)
        keep = (cand & (rank < MOBA_TOPK)) | (blk_i >= own_i)
        pen = jnp.where(keep, 0.0, -PEN_BIG)
        pen_lane0 = (1 - hh) * HEAD_DIM
        parts = [pen]
        if pen_lane0:
            parts.insert(0, jnp.zeros((pen_lane0, seq), F32))
        tail_rows = LANES - MAX_BLOCKS - pen_lane0
        if tail_rows:
            parts.append(jnp.zeros((tail_rows, seq), F32))
        pen_t = jnp.concatenate(parts, axis=0).T
        qa_ref[hh] = jnp.where(mine, qf * scale, pen_t).astype(BF16)
        onehot = (lane_full == pen_lane0 + row_blk).astype(F32)
        ka_ref[hh] = jnp.where(mine, kf, onehot).astype(BF16)

    half = blk // 2
    for own in range(nb):
        r0 = own * blk
        outs = []
        for hh in range(2):
            q = qa_ref[hh, r0:r0 + blk, :]
            far_bias = far_ref[2 * hp + hh]
            mx = None
            for n in range(own + 1):
                s = lax.dot_general(q, ka_ref[hh, n * blk:(n + 1) * blk, :], _NT,
                                    preferred_element_type=F32)
                if n == own:
                    s = s + d_ref[hh, 0]
                elif n == own - 1:
                    s = s + d_ref[hh, 1]
                else:
                    s = s + far_bias
                s_ref[hh, n] = s
                folded = jnp.maximum(s[:, :half], s[:, half:])
                mx = folded if mx is None else jnp.maximum(mx, folded)
            m = jnp.max(mx, axis=1, keepdims=True)
            lsum = None
            acc = None
            for n in range(own + 1):
                p = jnp.exp(s_ref[hh, n] - m)
                folded = p[:, :half] + p[:, half:]
                lsum = folded if lsum is None else lsum + folded
                pv = _dot(p.astype(BF16), v_ref[n * blk:(n + 1) * blk, :])
                acc = pv if acc is None else acc + pv
            outs.append(acc / jnp.sum(lsum, axis=1, keepdims=True))
        o_ref[r0:r0 + blk, :] = jnp.where(head0, outs[0], outs[1]).astype(o_ref.dtype)


def _moba(za, bias_tiles, far_bias, batch, seq):
    n_pairs = ATTN_HEADS // 2
    assert seq % MOBA_BLOCK == 0 and seq // MOBA_BLOCK <= MAX_BLOCKS
    return pl.pallas_call(
        functools.partial(_moba_kernel, seq=seq),
        out_shape=jax.ShapeDtypeStruct((batch * seq, ATTN_DIM), BF16),
        grid=(n_pairs, batch),
        in_specs=[pl.BlockSpec(memory_space=pltpu.SMEM),
                  pl.BlockSpec((seq, LANES), lambda hp, b: (b, hp)),
                  pl.BlockSpec((seq, LANES), lambda hp, b: (b, n_pairs + hp)),
                  pl.BlockSpec((seq, LANES), lambda hp, b: (b, 2 * n_pairs + hp)),
                  pl.BlockSpec((2, 2, MOBA_BLOCK, MOBA_BLOCK), lambda hp, b: (hp, 0, 0, 0))],
        out_specs=pl.BlockSpec((seq, LANES), lambda hp, b: (b, hp)),
        scratch_shapes=[pltpu.VMEM((2, seq, LANES), BF16),
                        pltpu.VMEM((2, seq, LANES), BF16),
                        pltpu.VMEM((2, seq // MOBA_BLOCK, MOBA_BLOCK, MOBA_BLOCK), F32)],
        compiler_params=pltpu.CompilerParams(
            dimension_semantics=("arbitrary", "arbitrary"), vmem_limit_bytes=VMEM_LIMIT),
        name="moba",
    )(far_bias, za, za, za, bias_tiles)


def _out_ffn_kernel(x_ref, ym_ref, yc_ref, ya_ref, wo_ref, g2_ref, wup_ref, cw_ref, cb_ref,
                    wdn_ref, fg_ref, o_ref, hn_ref, acc_ref, act_ref, tail_ref,
                    *, tiles_per_seq, final_norm):
    tm = TM_FFN
    first = (pl.program_id(0) % tiles_per_seq) == 0
    y = jnp.concatenate([ym_ref[...], yc_ref[...], ya_ref[...]], axis=1)
    x1 = x_ref[...] + _dot(y, wo_ref[...])
    hn_ref[...] = _rms(x1, g2_ref[...]).astype(BF16)
    acc_ref[...] = x1
    row8 = lax.broadcasted_iota(jnp.int32, (SUBLANES, FFN_CHUNK), 0)

    for c in range(N_FFN_CHUNKS):
        conv = []
        for part in range(2):
            c0 = part * D_FFP + c * FFN_CHUNK
            u = _dot(hn_ref[...], wup_ref[:, c0:c0 + FFN_CHUNK])
            tail = jnp.where(first, 0.0, tail_ref[2 * c + part])
            tail_ref[2 * c + part] = u[tm - SUBLANES:tm, :]
            acc = cb_ref[:, c0:c0 + FFN_CHUNK] + cw_ref[FFN_CONV - 1:FFN_CONV, c0:c0 + FFN_CHUNK] * u
            for d in range(1, FFN_CONV):
                shifted = pltpu.roll(u, d, axis=0)
                top = jnp.where(row8 < d, pltpu.roll(tail, d, axis=0), shifted[0:SUBLANES])
                shifted = jnp.concatenate([top, shifted[SUBLANES:]], axis=0)
                j = FFN_CONV - 1 - d
                acc = acc + cw_ref[j:j + 1, c0:c0 + FFN_CHUNK] * shifted
            conv.append(acc)
        act_ref[:, c * FFN_CHUNK:(c + 1) * FFN_CHUNK] = (
            conv[0] * _sigmoid(conv[0]) * conv[1]).astype(BF16)
        if (c + 1) % FFN_DOWN_GROUP == 0 or c == N_FFN_CHUNKS - 1:
            k0 = (c // FFN_DOWN_GROUP) * FFN_DOWN_GROUP * FFN_CHUNK
            k1 = (c + 1) * FFN_CHUNK
            acc_ref[...] += _dot(act_ref[:, k0:k1], wdn_ref[k0:k1, :])

    out = acc_ref[...]
    if final_norm:
        out = _rms(out, fg_ref[...])
    o_ref[...] = out


def _out_ffn(x2d, ym, yc, ya, wo, g2, wup, cw, cb, wdn, fg, seq, final_norm):
    n = x2d.shape[0]
    tiles_per_seq = seq // TM_FFN
    resident = functools.partial(pl.BlockSpec, pipeline_mode=pl.Buffered(1))

    def const(shape):
        return resident(shape, lambda i: (0,) * len(shape))

    return pl.pallas_call(
        functools.partial(_out_ffn_kernel, tiles_per_seq=tiles_per_seq, final_norm=final_norm),
        out_shape=jax.ShapeDtypeStruct((n, D_MODEL), F32),
        grid=(n // TM_FFN,),
        in_specs=[pl.BlockSpec((TM_FFN, D_MODEL), lambda i: (i, 0)),
                  pl.BlockSpec((TM_FFN, MLSTM_DIM), lambda i: (i, 0)),
                  pl.BlockSpec((TM_FFN, CONV_DIM), lambda i: (i, 0)),
                  pl.BlockSpec((TM_FFN, ATTN_DIM), lambda i: (i, 0)),
                  const((D_MODEL, D_MODEL)),
                  const((1, D_MODEL)),
                  const((D_MODEL, 2 * D_FFP)),
                  const((FFN_CONV, 2 * D_FFP)),
                  const((1, 2 * D_FFP)),
                  const((D_FFP, D_MODEL)),
                  const((1, D_MODEL))],
        out_specs=pl.BlockSpec((TM_FFN, D_MODEL), lambda i: (i, 0)),
        scratch_shapes=[pltpu.VMEM((TM_FFN, D_MODEL), BF16),
                        pltpu.VMEM((TM_FFN, D_MODEL), F32),
                        pltpu.VMEM((TM_FFN, D_FFP), BF16),
                        pltpu.VMEM((2 * N_FFN_CHUNKS, SUBLANES, FFN_CHUNK), F32)],
        compiler_params=pltpu.CompilerParams(
            dimension_semantics=("arbitrary",), vmem_limit_bytes=VMEM_LIMIT),
        name="out_ffn",
    )(x2d, ym, yc, ya, wo, g2, wup, cw, cb, wdn, fg)


def _pad_cols(a, width):
    return jnp.pad(a, ((0, 0), (0, width - a.shape[1])))


def _prep_w_in(w_in):
    m_end = 4 * MLSTM_DIM
    g_end = m_end + 2 * MLSTM_HEADS
    c_end = g_end + 2 * CONV_DIM
    return jnp.concatenate([w_in[:, :m_end], _pad_cols(w_in[:, m_end:g_end], GATE_PAD),
                            w_in[:, g_end:c_end], w_in[:, c_end:]], axis=1).astype(BF16)


def _prep_ffn_cols(a):
    return jnp.concatenate([_pad_cols(a[:, :D_FF], D_FFP), _pad_cols(a[:, D_FF:], D_FFP)], axis=1)


def _layer(x2d, batch, seq, bias_tiles, far_bias, norm1_g, w_in, qk_w, qk_b, ig_b, fg_b, head_g,
           dw_w, dw_b, ln_g, ln_b, w_out, norm2_g, w_up, f_w, f_b, w_down, final_g, final_norm):
    zm, zc, za = _in_proj(x2d, norm1_g[None, :], _prep_w_in(w_in))
    gate_bias = _pad_cols(jnp.concatenate([ig_b, fg_b])[None, :], GATE_PAD)
    ym = _mlstm(zm, qk_w, qk_b[None, :], gate_bias, head_g[None, :], batch, seq)
    yc = _conformer(zc, dw_w, dw_b[None, :], ln_g[None, :], ln_b[None, :], batch, seq)
    ya = _moba(za, bias_tiles, far_bias, batch, seq)
    return _out_ffn(x2d, ym, yc, ya, w_out.astype(BF16), norm2_g[None, :],
                    _prep_ffn_cols(w_up).astype(BF16), _prep_ffn_cols(f_w),
                    _prep_ffn_cols(f_b[None, :]),
                    jnp.pad(w_down, ((0, D_FFP - D_FF), (0, 0))).astype(BF16),
                    final_g[None, :], seq, final_norm)


def kernel(x, norm1_g, w_in, mlstm_qk_conv_w, mlstm_qk_conv_b, mlstm_ig_b, mlstm_fg_b, mlstm_head_g, conf_dw_w, conf_dw_b, conf_ln_g, conf_ln_b, rel_bias, w_out, norm2_g, ffn_w_up, ffn_conv_w, ffn_conv_b, ffn_w_down, final_g):
    batch, seq, _ = x.shape
    bias_tiles = _bias_tiles(rel_bias)
    far_bias = rel_bias[REL_BUCKETS - 1]
    x2d = x.reshape(batch * seq, D_MODEL)
    for l in range(DEPTH):
        x2d = _layer(x2d, batch, seq, bias_tiles, far_bias, norm1_g[l], w_in[l],
                     mlstm_qk_conv_w[l], mlstm_qk_conv_b[l], mlstm_ig_b[l], mlstm_fg_b[l],
                     mlstm_head_g[l], conf_dw_w[l], conf_dw_b[l], conf_ln_g[l], conf_ln_b[l],
                     w_out[l], norm2_g[l], ffn_w_up[l], ffn_conv_w[l], ffn_conv_b[l],
                     ffn_w_down[l], final_g, l == DEPTH - 1)
    return x2d.reshape(batch, seq, D_MODEL)
```

```python
import functools
import math

import numpy as np
import jax
import jax.numpy as jnp
from jax import lax
from jax.experimental import pallas as pl
from jax.experimental.pallas import tpu as pltpu

F32 = jnp.float32
BF16 = jnp.bfloat16
HIGHEST = lax.Precision.HIGHEST

D_MODEL = 1024
DEPTH = 2
HEAD_DIM = 64
MLSTM_DIM = 256
CONV_DIM = 256
ATTN_DIM = 512
MLSTM_HEADS = MLSTM_DIM // HEAD_DIM
ATTN_HEADS = ATTN_DIM // HEAD_DIM
QK_CONV = 4
CONF_KERNEL = 31
MOBA_BLOCK = 256
MOBA_TOPK = 3
REL_BUCKETS = 32
REL_MAX_DIST = 128
D_FF = 2752
FFN_CONV = 3
EPS = 1e-6
NEG = -1e30

LANES = 128
SUBLANES = 8
VMEM_LIMIT = 56 * 1024 * 1024

GATE_PAD = LANES
ZM_W = 4 * MLSTM_DIM + GATE_PAD
ZC_W = 2 * CONV_DIM
ZA_W = 3 * ATTN_DIM
P_W = ZM_W + ZC_W + ZA_W
D_FFP = -(-D_FF // 256) * 256
FFN_CHUNK = 256
N_FFN_CHUNKS = D_FFP // FFN_CHUNK
FFN_DOWN_GROUP = 4

TM_IN = 512
TM_FFN = 512
MLSTM_L = 256
CONF_ROWS = 64
CONF_PAD = 32
MAX_BLOCKS = SUBLANES
PEN_BIG = 2.0 ** 100

_NT = (((1,), (1,)), ((), ()))


def _t5_saturation_distance():
    n = np.arange(1, 4 * MOBA_BLOCK, dtype=np.float32)
    max_exact = REL_BUCKETS // 2
    large = max_exact + (np.log(n / max_exact) / math.log(REL_MAX_DIST / max_exact)
                         * (REL_BUCKETS - max_exact)).astype(np.int32)
    bucket = np.where(n < max_exact, n.astype(np.int32), np.minimum(large, REL_BUCKETS - 1))
    not_last = np.nonzero(bucket != REL_BUCKETS - 1)[0]
    return int(n[not_last[-1]]) + 1


assert _t5_saturation_distance() <= MOBA_BLOCK + 1


def _sigmoid(x):
    return 1.0 / (1.0 + jnp.exp(-x))


def _log_sigmoid(x):
    return jnp.minimum(x, 0.0) - jnp.log1p(jnp.exp(-jnp.abs(x)))


def _rms(xf, g):
    return xf * lax.rsqrt(jnp.mean(xf * xf, axis=-1, keepdims=True) + EPS) * g


def _dot(a, b, **kw):
    return jnp.dot(a, b, preferred_element_type=F32, **kw)


def _const_spec(shape):
    nd = len(shape)
    return pl.BlockSpec(shape, lambda *_: (0,) * nd)


def _layer_spec(shape, layer, **kw):
    nd = len(shape)
    return pl.BlockSpec((None,) + tuple(shape), lambda *_: (layer,) + (0,) * nd, **kw)


def _in_proj_kernel(x_ref, g_ref, w_ref, zm_ref, zc_ref, za_ref):
    h = _rms(x_ref[...], g_ref[...]).astype(BF16)
    zm_ref[...] = _dot(h, w_ref[:, 0:ZM_W])
    zc_ref[...] = _dot(h, w_ref[:, ZM_W:ZM_W + ZC_W])
    za_ref[...] = _dot(h, w_ref[:, ZM_W + ZC_W:P_W]).astype(BF16)


def _in_proj(x2d, g, w_r, layer):
    n = x2d.shape[0]
    return pl.pallas_call(
        _in_proj_kernel,
        out_shape=(jax.ShapeDtypeStruct((n, ZM_W), F32),
                   jax.ShapeDtypeStruct((n, ZC_W), F32),
                   jax.ShapeDtypeStruct((n, ZA_W), BF16)),
        grid=(n // TM_IN,),
        in_specs=[pl.BlockSpec((TM_IN, D_MODEL), lambda i: (i, 0)),
                  _layer_spec((1, D_MODEL), layer),
                  _layer_spec((D_MODEL, P_W), layer)],
        out_specs=(pl.BlockSpec((TM_IN, ZM_W), lambda i: (i, 0)),
                   pl.BlockSpec((TM_IN, ZC_W), lambda i: (i, 0)),
                   pl.BlockSpec((TM_IN, ZA_W), lambda i: (i, 0))),
        compiler_params=pltpu.CompilerParams(
            dimension_semantics=("arbitrary",), vmem_limit_bytes=VMEM_LIMIT),
        name="in_proj",
    )(x2d, g, w_r)


def _mlstm_kernel(zm_ref, cw_ref, cb_ref, gb_ref, hg_ref, o_ref, g_ref, gt_ref, *, seq):
    L = MLSTM_L
    n_chunks = seq // L
    scale = HEAD_DIM ** -0.5
    lane = lax.broadcasted_iota(jnp.int32, (1, LANES), 1)
    head0 = lane < HEAD_DIM

    gates = zm_ref[:, 4 * MLSTM_DIM:ZM_W] + gb_ref[...]
    is_f = (lane >= MLSTM_HEADS) & (lane < 2 * MLSTM_HEADS)
    gates = jnp.where(is_f, _log_sigmoid(gates), gates)
    g_ref[...] = gates
    gt_ref[...] = gates.T[0:SUBLANES, :]

    ti = lax.broadcasted_iota(jnp.int32, (L, L), 0)
    si = lax.broadcasted_iota(jnp.int32, (L, L), 1)
    causal = si <= ti
    tri = causal.astype(F32)
    tri_t = (ti <= si).astype(F32)
    srow_head = lax.broadcasted_iota(jnp.int32, (LANES, 2 * LANES), 0) // HEAD_DIM
    scol = lax.broadcasted_iota(jnp.int32, (LANES, 2 * LANES), 1)
    state_mask = ((scol < LANES) & (scol // HEAD_DIM == srow_head)) | (scol == LANES + srow_head)
    ones_col = jnp.where(lane == 0, 1.0, 0.0).astype(BF16)
    cb = cb_ref[...]
    hg = hg_ref[...]

    def chunk(c, carry):
        states, m_prev = carry
        r0 = pl.multiple_of(c * L, L)
        gc = g_ref[pl.ds(r0, L), :]
        gtc = gt_ref[:, pl.ds(r0, L)]
        bcols = _dot(tri, gc, precision=HIGHEST)
        brows = _dot(gtc, tri_t, precision=HIGHEST)

        xcur = zm_ref[pl.ds(r0, L), 0:2 * MLSTM_DIM]
        pr = pl.multiple_of(jnp.maximum(r0 - SUBLANES, 0), SUBLANES)
        xprev = jnp.where(c > 0, zm_ref[pl.ds(pr, SUBLANES), 0:2 * MLSTM_DIM], 0.0)
        xcat = jnp.concatenate([xprev, xcur], axis=0)
        y = cb
        for j in range(QK_CONV):
            off = SUBLANES - (QK_CONV - 1) + j
            y = y + cw_ref[j:j + 1, :] * xcat[off:off + L]
        qk = y * _sigmoid(y)

        new_states = []
        new_m = []
        for p in range(2):
            lo = p * LANES
            qp = (qk[:, lo:lo + LANES] * scale).astype(BF16)
            kpf = qk[:, MLSTM_DIM + lo:MLSTM_DIM + lo + LANES]
            kpb = kpf.astype(BF16)
            vp = zm_ref[pl.ds(r0, L), 2 * MLSTM_DIM + lo:2 * MLSTM_DIM + lo + LANES].astype(BF16)
            vaug = jnp.concatenate([vp, jnp.broadcast_to(ones_col, (L, LANES))], axis=1)
            inter = _dot(qp, states[p].astype(BF16))
            hvals, wks, decays = [], [], []
            for hh in range(2):
                h = 2 * p + hh
                mp = m_prev[h]
                bcol = bcols[:, MLSTM_HEADS + h:MLSTM_HEADS + h + 1]
                icol = gc[:, h:h + 1]
                brow = brows[MLSTM_HEADS + h:MLSTM_HEADS + h + 1, :]
                irow = gtc[h:h + 1, :]
                dmat = jnp.where(causal, bcol - brow + irow, NEG)
                m_t = jnp.maximum(bcol + mp, jnp.max(dmat, axis=1, keepdims=True))
                w = jnp.exp(dmat - m_t)
                qh = jnp.where(head0 if hh == 0 else ~head0, qp, jnp.zeros_like(qp))
                s = lax.dot_general(qh, kpb, _NT, preferred_element_type=F32)
                r = _dot((s * w).astype(BF16), vaug)
                a = jnp.exp(bcol + mp - m_t)
                num = a * inter[:, 0:LANES] + r[:, 0:LANES]
                den = a * inter[:, LANES + hh:LANES + hh + 1] + r[:, LANES:LANES + 1]
                hvals.append(num / jnp.maximum(jnp.abs(den), jnp.exp(-m_t)))
                btot = bcol[L - 1:L, :]
                gk = btot - bcol + icol
                m_new = jnp.maximum(btot + mp, jnp.max(gk, axis=0, keepdims=True))
                wks.append(jnp.exp(gk - m_new))
                decays.append(jnp.exp(btot + mp - m_new))
                new_m.append(m_new)
            kw = kpf * jnp.where(head0, wks[0], wks[1])
            upd = _dot(kw.T.astype(BF16), vaug)
            decay_rows = jnp.where(srow_head[:, 0:1] == 0, decays[0], decays[1])
            new_states.append(decay_rows * states[p] + jnp.where(state_mask, upd, 0.0))

            hv = jnp.where(head0, hvals[0], hvals[1])
            inv = 1.0 / HEAD_DIM
            mu0 = jnp.sum(jnp.where(head0, hv, 0.0), axis=1, keepdims=True) * inv
            mu1 = jnp.sum(jnp.where(head0, 0.0, hv), axis=1, keepdims=True) * inv
            d = hv - jnp.where(head0, mu0, mu1)
            dd = d * d
            v0 = jnp.sum(jnp.where(head0, dd, 0.0), axis=1, keepdims=True) * inv
            v1 = jnp.sum(jnp.where(head0, 0.0, dd), axis=1, keepdims=True) * inv
            hn = d * lax.rsqrt(jnp.where(head0, v0, v1) + EPS)
            og = zm_ref[pl.ds(r0, L), 3 * MLSTM_DIM + lo:3 * MLSTM_DIM + lo + LANES]
            o_ref[pl.ds(r0, L), lo:lo + LANES] = (hn * hg[:, lo:lo + LANES] * _sigmoid(og)).astype(o_ref.dtype)
        return tuple(new_states), tuple(new_m)

    init = (tuple(jnp.zeros((LANES, 2 * LANES), F32) for _ in range(2)),
            tuple(jnp.full((1, 1), NEG, F32) for _ in range(MLSTM_HEADS)))
    lax.fori_loop(0, n_chunks, chunk, init)


def _mlstm(zm, cw, cb, gate_bias, head_g, layer, batch, seq):
    return pl.pallas_call(
        functools.partial(_mlstm_kernel, seq=seq),
        out_shape=jax.ShapeDtypeStruct((batch * seq, MLSTM_DIM), BF16),
        grid=(batch,),
        in_specs=[pl.BlockSpec((seq, ZM_W), lambda b: (b, 0)),
                  _layer_spec((QK_CONV, 2 * MLSTM_DIM), layer),
                  _layer_spec((1, 2 * MLSTM_DIM), layer),
                  _layer_spec((1, GATE_PAD), layer),
                  _layer_spec((1, MLSTM_DIM), layer)],
        out_specs=pl.BlockSpec((seq, MLSTM_DIM), lambda b: (b, 0)),
        scratch_shapes=[pltpu.VMEM((seq, GATE_PAD), F32),
                        pltpu.VMEM((SUBLANES, seq), F32)],
        compiler_params=pltpu.CompilerParams(
            dimension_semantics=("arbitrary",), vmem_limit_bytes=VMEM_LIMIT),
        name="mlstm",
    )(zm, cw, cb, gate_bias, head_g)


def _conformer_kernel(zc_ref, w_ref, b_ref, lg_ref, lb_ref, o_ref, up_ref, *, seq):
    up_ref[0:CONF_PAD, :] = jnp.zeros((CONF_PAD, CONV_DIM), F32)
    up_ref[CONF_PAD:CONF_PAD + seq, :] = zc_ref[:, 0:CONV_DIM] * _sigmoid(zc_ref[:, CONV_DIM:ZC_W])
    bias = b_ref[...]
    lg = lg_ref[...]
    lb = lb_ref[...]

    def tile(c, carry):
        r0 = pl.multiple_of(c * CONF_ROWS, CONF_ROWS)
        win_rows = CONF_ROWS + CONF_PAD
        win = up_ref[pl.ds(r0, win_rows), :]
        acc = jnp.broadcast_to(bias, (CONF_ROWS, CONV_DIM))
        for r in range(SUBLANES):
            rolled = win if r == 0 else pltpu.roll(win, win_rows - r, axis=0)
            for j in range(CONF_KERNEL):
                off = CONF_PAD - (CONF_KERNEL - 1) + j
                if off % SUBLANES == r:
                    base = off - r
                    acc = acc + w_ref[j:j + 1, :] * rolled[base:base + CONF_ROWS]
        mu = jnp.mean(acc, axis=-1, keepdims=True)
        d = acc - mu
        var = jnp.mean(d * d, axis=-1, keepdims=True)
        y = d * lax.rsqrt(var + EPS) * lg + lb
        o_ref[pl.ds(r0, CONF_ROWS), :] = (y * _sigmoid(y)).astype(o_ref.dtype)
        return carry

    lax.fori_loop(0, seq // CONF_ROWS, tile, 0)


def _conformer(zc, w, b, ln_g, ln_b, layer, batch, seq):
    return pl.pallas_call(
        functools.partial(_conformer_kernel, seq=seq),
        out_shape=jax.ShapeDtypeStruct((batch * seq, CONV_DIM), BF16),
        grid=(batch,),
        in_specs=[pl.BlockSpec((seq, ZC_W), lambda b: (b, 0)),
                  _layer_spec((CONF_KERNEL, CONV_DIM), layer),
                  _layer_spec((1, CONV_DIM), layer),
                  _layer_spec((1, CONV_DIM), layer),
                  _layer_spec((1, CONV_DIM), layer)],
        out_specs=pl.BlockSpec((seq, CONV_DIM), lambda b: (b, 0)),
        scratch_shapes=[pltpu.VMEM((CONF_PAD + seq, CONV_DIM), F32)],
        compiler_params=pltpu.CompilerParams(
            dimension_semantics=("arbitrary",), vmem_limit_bytes=VMEM_LIMIT),
        name="conformer",
    )(zc, w, b, ln_g, ln_b)


def _bias_tiles_kernel(rb_ref, o_ref):
    h = pl.program_id(0)
    i = lax.broadcasted_iota(jnp.int32, (MOBA_BLOCK, MOBA_BLOCK), 0)
    j = lax.broadcasted_iota(jnp.int32, (MOBA_BLOCK, MOBA_BLOCK), 1)
    max_exact = REL_BUCKETS // 2
    for t in range(2):
        dist = i - j + t * MOBA_BLOCK
        n = jnp.maximum(dist, 0)
        nf = jnp.maximum(n, 1).astype(F32)
        large = max_exact + (jnp.log(nf / max_exact) / math.log(REL_MAX_DIST / max_exact)
                             * (REL_BUCKETS - max_exact)).astype(jnp.int32)
        large = jnp.minimum(large, REL_BUCKETS - 1)
        bucket = jnp.where(n < max_exact, n, large)
        bias = jnp.zeros((MOBA_BLOCK, MOBA_BLOCK), F32)
        for bk in range(REL_BUCKETS):
            bias = jnp.where(bucket == bk, rb_ref[bk, h], bias)
        if t == 0:
            bias = jnp.where(dist >= 0, bias, NEG)
        o_ref[0, t] = bias


def _bias_tiles(rel_bias):
    return pl.pallas_call(
        _bias_tiles_kernel,
        out_shape=jax.ShapeDtypeStruct((ATTN_HEADS, 2, MOBA_BLOCK, MOBA_BLOCK), F32),
        grid=(ATTN_HEADS,),
        in_specs=[pl.BlockSpec(memory_space=pltpu.SMEM)],
        out_specs=pl.BlockSpec((1, 2, MOBA_BLOCK, MOBA_BLOCK), lambda h: (h, 0, 0, 0)),
        compiler_params=pltpu.CompilerParams(dimension_semantics=("arbitrary",)),
        name="bias_tiles",
    )(rel_bias)


def _moba_kernel(far_ref, q_ref, k_ref, v_ref, d_ref, o_ref, qa_ref, ka_ref, s_ref, *, seq):
    hp = pl.program_id(0)
    blk = MOBA_BLOCK
    nb = seq // blk
    scale = HEAD_DIM ** -0.5
    lane = lax.broadcasted_iota(jnp.int32, (1, LANES), 1)
    head0 = lane < HEAD_DIM

    kf = k_ref[...].astype(F32)
    qf = q_ref[...].astype(F32)
    means = [jnp.mean(kf[n * blk:(n + 1) * blk], axis=0, keepdims=True) for n in range(nb)]
    if nb < MAX_BLOCKS:
        means.append(jnp.zeros((MAX_BLOCKS - nb, LANES), F32))
    kmean = jnp.concatenate(means, axis=0)
    blk_i = lax.broadcasted_iota(jnp.int32, (MAX_BLOCKS, seq), 0)
    own_i = lax.broadcasted_iota(jnp.int32, (MAX_BLOCKS, seq), 1) // blk
    row_blk = lax.broadcasted_iota(jnp.int32, (seq, LANES), 0) // blk
    lane_full = lax.broadcasted_iota(jnp.int32, (seq, LANES), 1)

    for hh in range(2):
        mine = head0 if hh == 0 else ~head0
        gate = lax.dot_general(jnp.where(mine, kmean, 0.0), qf, _NT,
                               precision=HIGHEST, preferred_element_type=F32)
        cand = blk_i < own_i
        g = jnp.where(cand, gate, NEG)
        rank = jnp.zeros((MAX_BLOCKS, seq), jnp.int32)
        for m in range(MAX_BLOCKS):
            gm = g[m:m + 1, :]
            beats = (gm > g) | ((gm == g) & (blk_i > m))
            rank = rank + beats.astype(jnp.int32)
        keep = (cand & (rank < MOBA_TOPK)) | (blk_i >= own_i)
        pen = jnp.where(keep, 0.0, -PEN_BIG)
        pen_lane0 = (1 - hh) * HEAD_DIM
        parts = [pen]
        if pen_lane0:
            parts.insert(0, jnp.zeros((pen_lane0, seq), F32))
        tail_rows = LANES - MAX_BLOCKS - pen_lane0
        if tail_rows:
            parts.append(jnp.zeros((tail_rows, seq), F32))
        pen_t = jnp.concatenate(parts, axis=0).T
        qa_ref[hh] = jnp.where(mine, qf * scale, pen_t).astype(BF16)
        onehot = (lane_full == pen_lane0 + row_blk).astype(F32)
        ka_ref[hh] = jnp.where(mine, kf, onehot).astype(BF16)

    half = blk // 2
    for own in range(nb):
        r0 = own * blk
        outs = []
        for hh in range(2):
            q = qa_ref[hh, r0:r0 + blk, :]
            far_bias = far_ref[2 * hp + hh]
            mx = None
            for n in range(own + 1):
                s = lax.dot_general(q, ka_ref[hh, n * blk:(n + 1) * blk, :], _NT,
                                    preferred_element_type=F32)
                if n == own:
                    s = s + d_ref[hh, 0]
                elif n == own - 1:
                    s = s + d_ref[hh, 1]
                else:
                    s = s + far_bias
                s_ref[hh, n] = s
                folded = jnp.maximum(s[:, :half], s[:, half:])
                mx = folded if mx is None else jnp.maximum(mx, folded)
            m = jnp.max(mx, axis=1, keepdims=True)
            lsum = None
            acc = None
            for n in range(own + 1):
                p = jnp.exp(s_ref[hh, n] - m)
                folded = p[:, :half] + p[:, half:]
                lsum = folded if lsum is None else lsum + folded
                pv = _dot(p.astype(BF16), v_ref[n * blk:(n + 1) * blk, :])
                acc = pv if acc is None else acc + pv
            outs.append(acc / jnp.sum(lsum, axis=1, keepdims=True))
        o_ref[r0:r0 + blk, :] = jnp.where(head0, outs[0], outs[1]).astype(o_ref.dtype)


def _moba(za, bias_tiles, far_bias, batch, seq):
    n_pairs = ATTN_HEADS // 2
    assert seq % MOBA_BLOCK == 0 and seq // MOBA_BLOCK <= MAX_BLOCKS
    return pl.pallas_call(
        functools.partial(_moba_kernel, seq=seq),
        out_shape=jax.ShapeDtypeStruct((batch * seq, ATTN_DIM), BF16),
        grid=(n_pairs, batch),
        in_specs=[pl.BlockSpec(memory_space=pltpu.SMEM),
                  pl.BlockSpec((seq, LANES), lambda hp, b: (b, hp)),
                  pl.BlockSpec((seq, LANES), lambda hp, b: (b, n_pairs + hp)),
                  pl.BlockSpec((seq, LANES), lambda hp, b: (b, 2 * n_pairs + hp)),
                  pl.BlockSpec((2, 2, MOBA_BLOCK, MOBA_BLOCK), lambda hp, b: (hp, 0, 0, 0))],
        out_specs=pl.BlockSpec((seq, LANES), lambda hp, b: (b, hp)),
        scratch_shapes=[pltpu.VMEM((2, seq, LANES), BF16),
                        pltpu.VMEM((2, seq, LANES), BF16),
                        pltpu.VMEM((2, seq // MOBA_BLOCK, MOBA_BLOCK, MOBA_BLOCK), F32)],
        compiler_params=pltpu.CompilerParams(
            dimension_semantics=("arbitrary", "arbitrary"), vmem_limit_bytes=VMEM_LIMIT),
        name="moba",
    )(far_bias, za, za, za, bias_tiles)


def _out_ffn_kernel(x_ref, ym_ref, yc_ref, ya_ref, wo_ref, g2_ref, wup_ref, cw_ref, cb_ref,
                    wdn_ref, fg_ref, o_ref, hn_ref, acc_ref, act_ref, tail_ref,
                    *, tiles_per_seq, final_norm):
    tm = TM_FFN
    first = (pl.program_id(0) % tiles_per_seq) == 0
    y = jnp.concatenate([ym_ref[...], yc_ref[...], ya_ref[...]], axis=1)
    x1 = x_ref[...] + _dot(y, wo_ref[...])
    hn_ref[...] = _rms(x1, g2_ref[...]).astype(BF16)
    acc_ref[...] = x1
    row8 = lax.broadcasted_iota(jnp.int32, (SUBLANES, FFN_CHUNK), 0)

    for c in range(N_FFN_CHUNKS):
        conv = []
        for part in range(2):
            c0 = part * D_FFP + c * FFN_CHUNK
            u = _dot(hn_ref[...], wup_ref[:, c0:c0 + FFN_CHUNK])
            tail = jnp.where(first, 0.0, tail_ref[2 * c + part])
            tail_ref[2 * c + part] = u[tm - SUBLANES:tm, :]
            acc = cb_ref[:, c0:c0 + FFN_CHUNK] + cw_ref[FFN_CONV - 1:FFN_CONV, c0:c0 + FFN_CHUNK] * u
            for d in range(1, FFN_CONV):
                shifted = pltpu.roll(u, d, axis=0)
                top = jnp.where(row8 < d, pltpu.roll(tail, d, axis=0), shifted[0:SUBLANES])
                shifted = jnp.concatenate([top, shifted[SUBLANES:]], axis=0)
                j = FFN_CONV - 1 - d
                acc = acc + cw_ref[j:j + 1, c0:c0 + FFN_CHUNK] * shifted
            conv.append(acc)
        act_ref[:, c * FFN_CHUNK:(c + 1) * FFN_CHUNK] = (
            conv[0] * _sigmoid(conv[0]) * conv[1]).astype(BF16)
        if (c + 1) % FFN_DOWN_GROUP == 0 or c == N_FFN_CHUNKS - 1:
            k0 = (c // FFN_DOWN_GROUP) * FFN_DOWN_GROUP * FFN_CHUNK
            k1 = (c + 1) * FFN_CHUNK
            acc_ref[...] += _dot(act_ref[:, k0:k1], wdn_ref[k0:k1, :])

    out = acc_ref[...]
    if final_norm:
        out = _rms(out, fg_ref[...])
    o_ref[...] = out


def _out_ffn(x2d, ym, yc, ya, wo, g2, wup, cw, cb, wdn, fg, layer, seq, final_norm):
    n = x2d.shape[0]
    tiles_per_seq = seq // TM_FFN

    def const(shape):
        return _layer_spec(shape, layer, pipeline_mode=pl.Buffered(1))

    return pl.pallas_call(
        functools.partial(_out_ffn_kernel, tiles_per_seq=tiles_per_seq, final_norm=final_norm),
        out_shape=jax.ShapeDtypeStruct((n, D_MODEL), F32),
        grid=(n // TM_FFN,),
        in_specs=[pl.BlockSpec((TM_FFN, D_MODEL), lambda i: (i, 0)),
                  pl.BlockSpec((TM_FFN, MLSTM_DIM), lambda i: (i, 0)),
                  pl.BlockSpec((TM_FFN, CONV_DIM), lambda i: (i, 0)),
                  pl.BlockSpec((TM_FFN, ATTN_DIM), lambda i: (i, 0)),
                  const((D_MODEL, D_MODEL)),
                  const((1, D_MODEL)),
                  const((D_MODEL, 2 * D_FFP)),
                  const((FFN_CONV, 2 * D_FFP)),
                  const((1, 2 * D_FFP)),
                  const((D_FFP, D_MODEL)),
                  _const_spec((1, D_MODEL))],
        out_specs=pl.BlockSpec((TM_FFN, D_MODEL), lambda i: (i, 0)),
        scratch_shapes=[pltpu.VMEM((TM_FFN, D_MODEL), BF16),
                        pltpu.VMEM((TM_FFN, D_MODEL), F32),
                        pltpu.VMEM((TM_FFN, D_FFP), BF16),
                        pltpu.VMEM((2 * N_FFN_CHUNKS, SUBLANES, FFN_CHUNK), F32)],
        compiler_params=pltpu.CompilerParams(
            dimension_semantics=("arbitrary",), vmem_limit_bytes=VMEM_LIMIT),
        name="out_ffn",
    )(x2d, ym, yc, ya, wo, g2, wup, cw, cb, wdn, fg)


def _pad_last(a, width):
    return jnp.pad(a, [(0, 0)] * (a.ndim - 1) + [(0, width - a.shape[-1])])


def _row(a):
    return a[:, None, :]


def _prep_w_in(w_in):
    m_end = 4 * MLSTM_DIM
    g_end = m_end + 2 * MLSTM_HEADS
    return jnp.concatenate([w_in[..., :m_end], _pad_last(w_in[..., m_end:g_end], GATE_PAD),
                            w_in[..., g_end:]], axis=-1).astype(BF16)


def _prep_ffn_cols(a):
    return jnp.concatenate([_pad_last(a[..., :D_FF], D_FFP), _pad_last(a[..., D_FF:], D_FFP)],
                           axis=-1)


def _prep_w_down(w_down):
    return jnp.pad(w_down, ((0, 0), (0, D_FFP - D_FF), (0, 0))).astype(BF16)


def _prep_gate_bias(ig_b, fg_b):
    return _row(_pad_last(jnp.concatenate([ig_b, fg_b], axis=-1), GATE_PAD))


def kernel(x, norm1_g, w_in, mlstm_qk_conv_w, mlstm_qk_conv_b, mlstm_ig_b, mlstm_fg_b, mlstm_head_g, conf_dw_w, conf_dw_b, conf_ln_g, conf_ln_b, rel_bias, w_out, norm2_g, ffn_w_up, ffn_conv_w, ffn_conv_b, ffn_w_down, final_g):
    batch, seq, _ = x.shape
    bias_tiles = _bias_tiles(rel_bias)
    far_bias = rel_bias[REL_BUCKETS - 1]
    w_in_r = _prep_w_in(w_in)
    gate_bias = _prep_gate_bias(mlstm_ig_b, mlstm_fg_b)
    w_out_b = w_out.astype(BF16)
    w_up_r = _prep_ffn_cols(ffn_w_up).astype(BF16)
    f_w = _prep_ffn_cols(ffn_conv_w)
    f_b = _row(_prep_ffn_cols(ffn_conv_b))
    w_down_r = _prep_w_down(ffn_w_down)
    x2d = x.reshape(batch * seq, D_MODEL)
    for l in range(DEPTH):
        zm, zc, za = _in_proj(x2d, _row(norm1_g), w_in_r, l)
        ym = _mlstm(zm, mlstm_qk_conv_w, _row(mlstm_qk_conv_b), gate_bias, _row(mlstm_head_g),
                    l, batch, seq)
        yc = _conformer(zc, conf_dw_w, _row(conf_dw_b), _row(conf_ln_g), _row(conf_ln_b),
                        l, batch, seq)
        ya = _moba(za, bias_tiles, far_bias, batch, seq)
        x2d = _out_ffn(x2d, ym, yc, ya, w_out_b, _row(norm2_g), w_up_r, f_w, f_b, w_down_r,
                       final_g[None, :], l, seq, l == DEPTH - 1)
    return x2d.reshape(batch, seq, D_MODEL)
```

```python
import functools
import math

import numpy as np
import jax
import jax.numpy as jnp
from jax import lax
from jax.experimental import pallas as pl
from jax.experimental.pallas import tpu as pltpu

F32 = jnp.float32
BF16 = jnp.bfloat16
HIGHEST = lax.Precision.HIGHEST

D_MODEL = 1024
DEPTH = 2
HEAD_DIM = 64
MLSTM_DIM = 256
CONV_DIM = 256
ATTN_DIM = 512
MLSTM_HEADS = MLSTM_DIM // HEAD_DIM
ATTN_HEADS = ATTN_DIM // HEAD_DIM
QK_CONV = 4
CONF_KERNEL = 31
MOBA_BLOCK = 256
MOBA_TOPK = 3
REL_BUCKETS = 32
REL_MAX_DIST = 128
D_FF = 2752
FFN_CONV = 3
EPS = 1e-6
NEG = -1e30

LANES = 128
SUBLANES = 8
VMEM_LIMIT = 56 * 1024 * 1024

GATE_PAD = LANES
ZM_W = 4 * MLSTM_DIM + GATE_PAD
ZC_W = 2 * CONV_DIM
ZA_W = 3 * ATTN_DIM
P_W = ZM_W + ZC_W + ZA_W
D_FFP = -(-D_FF // 256) * 256
FFN_CHUNK = 256
N_FFN_CHUNKS = D_FFP // FFN_CHUNK
FFN_DOWN_GROUP = 4

TM_IN = 512
TM_FFN = 512
MLSTM_L = 256
MLSTM_STATE_ROWS = 2 * HEAD_DIM + 16
CONF_ROWS = 64
CONF_PAD = 32
MAX_BLOCKS = SUBLANES
PEN_BIG = 2.0 ** 100

_NT = (((1,), (1,)), ((), ()))


def _t5_saturation_distance():
    n = np.arange(1, 4 * MOBA_BLOCK, dtype=np.float32)
    max_exact = REL_BUCKETS // 2
    large = max_exact + (np.log(n / max_exact) / math.log(REL_MAX_DIST / max_exact)
                         * (REL_BUCKETS - max_exact)).astype(np.int32)
    bucket = np.where(n < max_exact, n.astype(np.int32), np.minimum(large, REL_BUCKETS - 1))
    not_last = np.nonzero(bucket != REL_BUCKETS - 1)[0]
    return int(n[not_last[-1]]) + 1


assert _t5_saturation_distance() <= MOBA_BLOCK + 1


def _sigmoid(x):
    return 1.0 / (1.0 + jnp.exp(-x))


def _log_sigmoid(x):
    return jnp.minimum(x, 0.0) - jnp.log1p(jnp.exp(-jnp.abs(x)))


def _rms(xf, g):
    return xf * lax.rsqrt(jnp.mean(xf * xf, axis=-1, keepdims=True) + EPS) * g


def _dot(a, b, **kw):
    return jnp.dot(a, b, preferred_element_type=F32, **kw)


def _const_spec(shape):
    nd = len(shape)
    return pl.BlockSpec(shape, lambda *_: (0,) * nd)


def _layer_spec(shape, layer, **kw):
    nd = len(shape)
    return pl.BlockSpec((None,) + tuple(shape), lambda *_: (layer,) + (0,) * nd, **kw)


def _in_proj_kernel(x_ref, g_ref, w_ref, zm_ref, zc_ref, za_ref):
    h = _rms(x_ref[...], g_ref[...]).astype(BF16)
    zm_ref[...] = _dot(h, w_ref[:, 0:ZM_W])
    zc_ref[...] = _dot(h, w_ref[:, ZM_W:ZM_W + ZC_W])
    za_ref[...] = _dot(h, w_ref[:, ZM_W + ZC_W:P_W]).astype(BF16)


def _in_proj(x2d, g, w_r, layer):
    n = x2d.shape[0]
    return pl.pallas_call(
        _in_proj_kernel,
        out_shape=(jax.ShapeDtypeStruct((n, ZM_W), F32),
                   jax.ShapeDtypeStruct((n, ZC_W), F32),
                   jax.ShapeDtypeStruct((n, ZA_W), BF16)),
        grid=(n // TM_IN,),
        in_specs=[pl.BlockSpec((TM_IN, D_MODEL), lambda i: (i, 0)),
                  _layer_spec((1, D_MODEL), layer),
                  _layer_spec((D_MODEL, P_W), layer)],
        out_specs=(pl.BlockSpec((TM_IN, ZM_W), lambda i: (i, 0)),
                   pl.BlockSpec((TM_IN, ZC_W), lambda i: (i, 0)),
                   pl.BlockSpec((TM_IN, ZA_W), lambda i: (i, 0))),
        compiler_params=pltpu.CompilerParams(
            dimension_semantics=("arbitrary",), vmem_limit_bytes=VMEM_LIMIT),
        name="in_proj",
    )(x2d, g, w_r)


def _mlstm_kernel(zm_ref, cw_ref, cb_ref, gb_ref, hg_ref, o_ref, g_ref, gt_ref, *, seq):
    L = MLSTM_L
    n_chunks = seq // L
    nh = MLSTM_HEADS
    scale = HEAD_DIM ** -0.5
    lane = lax.broadcasted_iota(jnp.int32, (1, LANES), 1)
    head0 = lane < HEAD_DIM

    gates = zm_ref[:, 4 * MLSTM_DIM:ZM_W] + gb_ref[...]
    is_f = (lane >= nh) & (lane < 2 * nh)
    gates = jnp.where(is_f, _log_sigmoid(gates), gates)
    g_ref[...] = gates
    gt_ref[...] = gates.T[0:SUBLANES, :]

    ri = lax.broadcasted_iota(jnp.int32, (L, L), 0)
    ci = lax.broadcasted_iota(jnp.int32, (L, L), 1)
    visible = ri <= ci
    tri = (ci <= ri).astype(F32)
    tri_t = visible.astype(F32)
    srow = lax.broadcasted_iota(jnp.int32, (MLSTM_STATE_ROWS, LANES), 0)
    scol = lax.broadcasted_iota(jnp.int32, (MLSTM_STATE_ROWS, LANES), 1)
    srow_head = jnp.where(srow < LANES, srow // HEAD_DIM, srow - LANES)
    state_mask = (scol // HEAD_DIM) == srow_head
    row_is_head0 = srow_head[:, 0:1] == 0
    vrow_is_head0 = lax.broadcasted_iota(jnp.int32, (LANES, 1), 0) < HEAD_DIM
    extra_row = lax.broadcasted_iota(jnp.int32, (MLSTM_STATE_ROWS - LANES, 1), 0)
    ones_rows = jnp.ones((MLSTM_STATE_ROWS - LANES, L), BF16)
    cb = cb_ref[...]
    hg = hg_ref[...]

    def chunk(c, carry):
        states, m_prev = carry
        r0 = pl.multiple_of(c * L, L)
        gc = g_ref[pl.ds(r0, L), :]
        gtc = gt_ref[:, pl.ds(r0, L)]
        bcols = _dot(tri, gc, precision=HIGHEST)
        brows = _dot(gtc, tri_t, precision=HIGHEST)
        e_rows = gtc[0:nh] - brows[nh:2 * nh]

        xcur = zm_ref[pl.ds(r0, L), 0:2 * MLSTM_DIM]
        pr = pl.multiple_of(jnp.maximum(r0 - SUBLANES, 0), SUBLANES)
        xprev = jnp.where(c > 0, zm_ref[pl.ds(pr, SUBLANES), 0:2 * MLSTM_DIM], 0.0)
        xcat = jnp.concatenate([xprev, xcur], axis=0)
        y = cb
        for j in range(QK_CONV):
            off = SUBLANES - (QK_CONV - 1) + j
            y = y + cw_ref[j:j + 1, :] * xcat[off:off + L]
        qk = y * _sigmoid(y)

        new_states = []
        new_m = []
        for p in range(2):
            lo = p * LANES
            q_t = (qk[:, lo:lo + LANES] * scale).T.astype(BF16)
            k_b = qk[:, MLSTM_DIM + lo:MLSTM_DIM + lo + LANES].astype(BF16)
            v_t = zm_ref[pl.ds(r0, L), 2 * MLSTM_DIM + lo:2 * MLSTM_DIM + lo + LANES].T
            v_tb = v_t.astype(BF16)
            inter = _dot(states[p].astype(BF16), q_t)
            hn_t, wks, decays = [], [], []
            for hh in range(2):
                h = 2 * p + hh
                mp = m_prev[h]
                e_col = gc[:, h:h + 1] - bcols[:, nh + h:nh + h + 1]
                e_row = e_rows[h:h + 1]
                b_row = brows[nh + h:nh + h + 1]
                em = jnp.where(visible, e_col, NEG)
                g = jnp.maximum(mp, jnp.max(em, axis=0, keepdims=True))
                w_t = jnp.exp(em - g)
                k_h = jnp.where(head0 if hh == 0 else ~head0, k_b, jnp.zeros_like(k_b))
                s_t = _dot(k_h, q_t)
                lhs = jnp.concatenate(
                    [v_tb[hh * HEAD_DIM:(hh + 1) * HEAD_DIM], ones_rows], axis=0)
                r = _dot(lhs, (s_t * w_t).astype(BF16))
                a = jnp.exp(mp - g)
                num = a * inter[hh * HEAD_DIM:(hh + 1) * HEAD_DIM] + r[0:HEAD_DIM]
                den = a * inter[LANES + hh:LANES + hh + 1] + r[HEAD_DIM:HEAD_DIM + 1]
                hv = num * (1.0 / jnp.maximum(jnp.abs(den), jnp.exp(-(b_row + g))))
                mu = jnp.mean(hv, axis=0, keepdims=True)
                d = hv - mu
                var = jnp.mean(d * d, axis=0, keepdims=True)
                hn_t.append(d * lax.rsqrt(var + EPS))
                g_last = jnp.maximum(mp, jnp.max(e_row, axis=1, keepdims=True))
                wks.append(jnp.exp(e_row - g_last))
                decays.append(jnp.exp(mp - g_last))
                new_m.append(b_row[:, L - 1:L] + g_last)
            vw = v_t * jnp.where(vrow_is_head0, wks[0], wks[1])
            extra = jnp.where(extra_row == 0, wks[0], jnp.where(extra_row == 1, wks[1], 0.0))
            upd = _dot(jnp.concatenate([vw, extra], axis=0).astype(BF16), k_b)
            decay_rows = jnp.where(row_is_head0, decays[0], decays[1])
            new_states.append(decay_rows * states[p] + jnp.where(state_mask, upd, 0.0))

            hn = jnp.concatenate(hn_t, axis=0).T
            og = zm_ref[pl.ds(r0, L), 3 * MLSTM_DIM + lo:3 * MLSTM_DIM + lo + LANES]
            o_ref[pl.ds(r0, L), lo:lo + LANES] = (hn * hg[:, lo:lo + LANES] * _sigmoid(og)).astype(o_ref.dtype)
        return tuple(new_states), tuple(new_m)

    init = (tuple(jnp.zeros((MLSTM_STATE_ROWS, LANES), F32) for _ in range(2)),
            tuple(jnp.full((1, 1), NEG, F32) for _ in range(nh)))
    lax.fori_loop(0, n_chunks, chunk, init)


def _mlstm(zm, cw, cb, gate_bias, head_g, layer, batch, seq):
    return pl.pallas_call(
        functools.partial(_mlstm_kernel, seq=seq),
        out_shape=jax.ShapeDtypeStruct((batch * seq, MLSTM_DIM), BF16),
        grid=(batch,),
        in_specs=[pl.BlockSpec((seq, ZM_W), lambda b: (b, 0)),
                  _layer_spec((QK_CONV, 2 * MLSTM_DIM), layer),
                  _layer_spec((1, 2 * MLSTM_DIM), layer),
                  _layer_spec((1, GATE_PAD), layer),
                  _layer_spec((1, MLSTM_DIM), layer)],
        out_specs=pl.BlockSpec((seq, MLSTM_DIM), lambda b: (b, 0)),
        scratch_shapes=[pltpu.VMEM((seq, GATE_PAD), F32),
                        pltpu.VMEM((SUBLANES, seq), F32)],
        compiler_params=pltpu.CompilerParams(
            dimension_semantics=("arbitrary",), vmem_limit_bytes=VMEM_LIMIT),
        name="mlstm",
    )(zm, cw, cb, gate_bias, head_g)


def _conformer_kernel(zc_ref, w_ref, b_ref, lg_ref, lb_ref, o_ref, up_ref, *, seq):
    up_ref[0:CONF_PAD, :] = jnp.zeros((CONF_PAD, CONV_DIM), F32)
    up_ref[CONF_PAD:CONF_PAD + seq, :] = zc_ref[:, 0:CONV_DIM] * _sigmoid(zc_ref[:, CONV_DIM:ZC_W])
    bias = b_ref[...]
    lg = lg_ref[...]
    lb = lb_ref[...]

    def tile(c, carry):
        r0 = pl.multiple_of(c * CONF_ROWS, CONF_ROWS)
        win_rows = CONF_ROWS + CONF_PAD
        win = up_ref[pl.ds(r0, win_rows), :]
        acc = jnp.broadcast_to(bias, (CONF_ROWS, CONV_DIM))
        for r in range(SUBLANES):
            rolled = win if r == 0 else pltpu.roll(win, win_rows - r, axis=0)
            for j in range(CONF_KERNEL):
                off = CONF_PAD - (CONF_KERNEL - 1) + j
                if off % SUBLANES == r:
                    base = off - r
                    acc = acc + w_ref[j:j + 1, :] * rolled[base:base + CONF_ROWS]
        mu = jnp.mean(acc, axis=-1, keepdims=True)
        d = acc - mu
        var = jnp.mean(d * d, axis=-1, keepdims=True)
        y = d * lax.rsqrt(var + EPS) * lg + lb
        o_ref[pl.ds(r0, CONF_ROWS), :] = (y * _sigmoid(y)).astype(o_ref.dtype)
        return carry

    lax.fori_loop(0, seq // CONF_ROWS, tile, 0)


def _conformer(zc, w, b, ln_g, ln_b, layer, batch, seq):
    return pl.pallas_call(
        functools.partial(_conformer_kernel, seq=seq),
        out_shape=jax.ShapeDtypeStruct((batch * seq, CONV_DIM), BF16),
        grid=(batch,),
        in_specs=[pl.BlockSpec((seq, ZC_W), lambda b: (b, 0)),
                  _layer_spec((CONF_KERNEL, CONV_DIM), layer),
                  _layer_spec((1, CONV_DIM), layer),
                  _layer_spec((1, CONV_DIM), layer),
                  _layer_spec((1, CONV_DIM), layer)],
        out_specs=pl.BlockSpec((seq, CONV_DIM), lambda b: (b, 0)),
        scratch_shapes=[pltpu.VMEM((CONF_PAD + seq, CONV_DIM), F32)],
        compiler_params=pltpu.CompilerParams(
            dimension_semantics=("arbitrary",), vmem_limit_bytes=VMEM_LIMIT),
        name="conformer",
    )(zc, w, b, ln_g, ln_b)


def _bias_tiles_kernel(rb_ref, o_ref):
    h = pl.program_id(0)
    i = lax.broadcasted_iota(jnp.int32, (MOBA_BLOCK, MOBA_BLOCK), 0)
    j = lax.broadcasted_iota(jnp.int32, (MOBA_BLOCK, MOBA_BLOCK), 1)
    max_exact = REL_BUCKETS // 2
    for t in range(2):
        dist = i - j + t * MOBA_BLOCK
        n = jnp.maximum(dist, 0)
        nf = jnp.maximum(n, 1).astype(F32)
        large = max_exact + (jnp.log(nf / max_exact) / math.log(REL_MAX_DIST / max_exact)
                             * (REL_BUCKETS - max_exact)).astype(jnp.int32)
        large = jnp.minimum(large, REL_BUCKETS - 1)
        bucket = jnp.where(n < max_exact, n, large)
        bias = jnp.zeros((MOBA_BLOCK, MOBA_BLOCK), F32)
        for bk in range(REL_BUCKETS):
            bias = jnp.where(bucket == bk, rb_ref[bk, h], bias)
        if t == 0:
            bias = jnp.where(dist >= 0, bias, NEG)
        o_ref[0, t] = bias


def _bias_tiles(rel_bias):
    return pl.pallas_call(
        _bias_tiles_kernel,
        out_shape=jax.ShapeDtypeStruct((ATTN_HEADS, 2, MOBA_BLOCK, MOBA_BLOCK), F32),
        grid=(ATTN_HEADS,),
        in_specs=[pl.BlockSpec(memory_space=pltpu.SMEM)],
        out_specs=pl.BlockSpec((1, 2, MOBA_BLOCK, MOBA_BLOCK), lambda h: (h, 0, 0, 0)),
        compiler_params=pltpu.CompilerParams(dimension_semantics=("arbitrary",)),
        name="bias_tiles",
    )(rel_bias)


def _moba_kernel(far_ref, q_ref, k_ref, v_ref, d_ref, o_ref, qa_ref, ka_ref, s_ref, *, seq):
    hp = pl.program_id(0)
    blk = MOBA_BLOCK
    nb = seq // blk
    scale = HEAD_DIM ** -0.5
    lane = lax.broadcasted_iota(jnp.int32, (1, LANES), 1)
    head0 = lane < HEAD_DIM

    kf = k_ref[...].astype(F32)
    qf = q_ref[...].astype(F32)
    means = [jnp.mean(kf[n * blk:(n + 1) * blk], axis=0, keepdims=True) for n in range(nb)]
    if nb < MAX_BLOCKS:
        means.append(jnp.zeros((MAX_BLOCKS - nb, LANES), F32))
    kmean = jnp.concatenate(means, axis=0)
    blk_i = lax.broadcasted_iota(jnp.int32, (MAX_BLOCKS, seq), 0)
    own_i = lax.broadcasted_iota(jnp.int32, (MAX_BLOCKS, seq), 1) // blk
    row_blk = lax.broadcasted_iota(jnp.int32, (seq, LANES), 0) // blk
    lane_full = lax.broadcasted_iota(jnp.int32, (seq, LANES), 1)

    for hh in range(2):
        mine = head0 if hh == 0 else ~head0
        gate = lax.dot_general(jnp.where(mine, kmean, 0.0), qf, _NT,
                               precision=HIGHEST, preferred_element_type=F32)
        cand = blk_i < own_i
        g = jnp.where(cand, gate, NEG)
        rank = jnp.zeros((MAX_BLOCKS, seq), jnp.int32)
        for m in range(MAX_BLOCKS):
            gm = g[m:m + 1, :]
            beats = (gm > g) | ((gm == g) & (blk_i > m))
            rank = rank + beats.astype(jnp.int32)
        keep = (cand & (rank < MOBA_TOPK)) | (blk_i >= own_i)
        pen = jnp.where(keep, 0.0, -PEN_BIG)
        pen_lane0 = (1 - hh) * HEAD_DIM
        parts = [pen]
        if pen_lane0:
            parts.insert(0, jnp.zeros((pen_lane0, seq), F32))
        tail_rows = LANES - MAX_BLOCKS - pen_lane0
        if tail_rows:
            parts.append(jnp.zeros((tail_rows, seq), F32))
        pen_t = jnp.concatenate(parts, axis=0).T
        qa_ref[hh] = jnp.where(mine, qf * scale, pen_t).astype(BF16)
        onehot = (lane_full == pen_lane0 + row_blk).astype(F32)
        ka_ref[hh] = jnp.where(mine, kf, onehot).astype(BF16)

    half = blk // 2
    for own in range(nb):
        r0 = own * blk
        outs = []
        for hh in range(2):
            q = qa_ref[hh, r0:r0 + blk, :]
            far_bias = far_ref[2 * hp + hh]
            mx = None
            for n in range(own + 1):
                s = lax.dot_general(q, ka_ref[hh, n * blk:(n + 1) * blk, :], _NT,
                                    preferred_element_type=F32)
                if n == own:
                    s = s + d_ref[hh, 0]
                elif n == own - 1:
                    s = s + d_ref[hh, 1]
                else:
                    s = s + far_bias
                s_ref[hh, n] = s
                folded = jnp.maximum(s[:, :half], s[:, half:])
                mx = folded if mx is None else jnp.maximum(mx, folded)
            m = jnp.max(mx, axis=1, keepdims=True)
            lsum = None
            acc = None
            for n in range(own + 1):
                p = jnp.exp(s_ref[hh, n] - m)
                folded = p[:, :half] + p[:, half:]
                lsum = folded if lsum is None else lsum + folded
                pv = _dot(p.astype(BF16), v_ref[n * blk:(n + 1) * blk, :])
                acc = pv if acc is None else acc + pv
            outs.append(acc / jnp.sum(lsum, axis=1, keepdims=True))
        o_ref[r0:r0 + blk, :] = jnp.where(head0, outs[0], outs[1]).astype(o_ref.dtype)


def _moba(za, bias_tiles, far_bias, batch, seq):
    n_pairs = ATTN_HEADS // 2
    assert seq % MOBA_BLOCK == 0 and seq // MOBA_BLOCK <= MAX_BLOCKS
    return pl.pallas_call(
        functools.partial(_moba_kernel, seq=seq),
        out_shape=jax.ShapeDtypeStruct((batch * seq, ATTN_DIM), BF16),
        grid=(n_pairs, batch),
        in_specs=[pl.BlockSpec(memory_space=pltpu.SMEM),
                  pl.BlockSpec((seq, LANES), lambda hp, b: (b, hp)),
                  pl.BlockSpec((seq, LANES), lambda hp, b: (b, n_pairs + hp)),
                  pl.BlockSpec((seq, LANES), lambda hp, b: (b, 2 * n_pairs + hp)),
                  pl.BlockSpec((2, 2, MOBA_BLOCK, MOBA_BLOCK), lambda hp, b: (hp, 0, 0, 0))],
        out_specs=pl.BlockSpec((seq, LANES), lambda hp, b: (b, hp)),
        scratch_shapes=[pltpu.VMEM((2, seq, LANES), BF16),
                        pltpu.VMEM((2, seq, LANES), BF16),
                        pltpu.VMEM((2, seq // MOBA_BLOCK, MOBA_BLOCK, MOBA_BLOCK), F32)],
        compiler_params=pltpu.CompilerParams(
            dimension_semantics=("arbitrary", "arbitrary"), vmem_limit_bytes=VMEM_LIMIT),
        name="moba",
    )(far_bias, za, za, za, bias_tiles)


def _out_ffn_kernel(x_ref, ym_ref, yc_ref, ya_ref, wo_ref, g2_ref, wup_ref, cw_ref, cb_ref,
                    wdn_ref, fg_ref, o_ref, hn_ref, acc_ref, act_ref, tail_ref,
                    *, tiles_per_seq, final_norm):
    tm = TM_FFN
    first = (pl.program_id(0) % tiles_per_seq) == 0
    y = jnp.concatenate([ym_ref[...], yc_ref[...], ya_ref[...]], axis=1)
    x1 = x_ref[...] + _dot(y, wo_ref[...])
    hn_ref[...] = _rms(x1, g2_ref[...]).astype(BF16)
    acc_ref[...] = x1
    row8 = lax.broadcasted_iota(jnp.int32, (SUBLANES, FFN_CHUNK), 0)

    for c in range(N_FFN_CHUNKS):
        conv = []
        for part in range(2):
            c0 = part * D_FFP + c * FFN_CHUNK
            u = _dot(hn_ref[...], wup_ref[:, c0:c0 + FFN_CHUNK])
            tail = jnp.where(first, 0.0, tail_ref[2 * c + part])
            tail_ref[2 * c + part] = u[tm - SUBLANES:tm, :]
            acc = cb_ref[:, c0:c0 + FFN_CHUNK] + cw_ref[FFN_CONV - 1:FFN_CONV, c0:c0 + FFN_CHUNK] * u
            for d in range(1, FFN_CONV):
                shifted = pltpu.roll(u, d, axis=0)
                top = jnp.where(row8 < d, pltpu.roll(tail, d, axis=0), shifted[0:SUBLANES])
                shifted = jnp.concatenate([top, shifted[SUBLANES:]], axis=0)
                j = FFN_CONV - 1 - d
                acc = acc + cw_ref[j:j + 1, c0:c0 + FFN_CHUNK] * shifted
            conv.append(acc)
        act_ref[:, c * FFN_CHUNK:(c + 1) * FFN_CHUNK] = (
            conv[0] * _sigmoid(conv[0]) * conv[1]).astype(BF16)
        if (c + 1) % FFN_DOWN_GROUP == 0 or c == N_FFN_CHUNKS - 1:
            k0 = (c // FFN_DOWN_GROUP) * FFN_DOWN_GROUP * FFN_CHUNK
            k1 = (c + 1) * FFN_CHUNK
            acc_ref[...] += _dot(act_ref[:, k0:k1], wdn_ref[k0:k1, :])

    out = acc_ref[...]
    if final_norm:
        out = _rms(out, fg_ref[...])
    o_ref[...] = out


def _out_ffn(x2d, ym, yc, ya, wo, g2, wup, cw, cb, wdn, fg, layer, seq, final_norm):
    n = x2d.shape[0]
    tiles_per_seq = seq // TM_FFN

    def const(shape):
        return _layer_spec(shape, layer, pipeline_mode=pl.Buffered(1))

    return pl.pallas_call(
        functools.partial(_out_ffn_kernel, tiles_per_seq=tiles_per_seq, final_norm=final_norm),
        out_shape=jax.ShapeDtypeStruct((n, D_MODEL), F32),
        grid=(n // TM_FFN,),
        in_specs=[pl.BlockSpec((TM_FFN, D_MODEL), lambda i: (i, 0)),
                  pl.BlockSpec((TM_FFN, MLSTM_DIM), lambda i: (i, 0)),
                  pl.BlockSpec((TM_FFN, CONV_DIM), lambda i: (i, 0)),
                  pl.BlockSpec((TM_FFN, ATTN_DIM), lambda i: (i, 0)),
                  const((D_MODEL, D_MODEL)),
                  const((1, D_MODEL)),
                  const((D_MODEL, 2 * D_FFP)),
                  const((FFN_CONV, 2 * D_FFP)),
                  const((1, 2 * D_FFP)),
                  const((D_FFP, D_MODEL)),
                  _const_spec((1, D_MODEL))],
        out_specs=pl.BlockSpec((TM_FFN, D_MODEL), lambda i: (i, 0)),
        scratch_shapes=[pltpu.VMEM((TM_FFN, D_MODEL), BF16),
                        pltpu.VMEM((TM_FFN, D_MODEL), F32),
                        pltpu.VMEM((TM_FFN, D_FFP), BF16),
                        pltpu.VMEM((2 * N_FFN_CHUNKS, SUBLANES, FFN_CHUNK), F32)],
        compiler_params=pltpu.CompilerParams(
            dimension_semantics=("arbitrary",), vmem_limit_bytes=VMEM_LIMIT),
        name="out_ffn",
    )(x2d, ym, yc, ya, wo, g2, wup, cw, cb, wdn, fg)


def _pad_last(a, width):
    return jnp.pad(a, [(0, 0)] * (a.ndim - 1) + [(0, width - a.shape[-1])])


def _row(a):
    return a[:, None, :]


def _prep_w_in(w_in):
    m_end = 4 * MLSTM_DIM
    g_end = m_end + 2 * MLSTM_HEADS
    return jnp.concatenate([w_in[..., :m_end], _pad_last(w_in[..., m_end:g_end], GATE_PAD),
                            w_in[..., g_end:]], axis=-1).astype(BF16)


def _prep_ffn_cols(a):
    return jnp.concatenate([_pad_last(a[..., :D_FF], D_FFP), _pad_last(a[..., D_FF:], D_FFP)],
                           axis=-1)


def _prep_w_down(w_down):
    return jnp.pad(w_down, ((0, 0), (0, D_FFP - D_FF), (0, 0))).astype(BF16)


def _prep_gate_bias(ig_b, fg_b):
    return _row(_pad_last(jnp.concatenate([ig_b, fg_b], axis=-1), GATE_PAD))


def kernel(x, norm1_g, w_in, mlstm_qk_conv_w, mlstm_qk_conv_b, mlstm_ig_b, mlstm_fg_b, mlstm_head_g, conf_dw_w, conf_dw_b, conf_ln_g, conf_ln_b, rel_bias, w_out, norm2_g, ffn_w_up, ffn_conv_w, ffn_conv_b, ffn_w_down, final_g):
    batch, seq, _ = x.shape
    bias_tiles = _bias_tiles(rel_bias)
    far_bias = rel_bias[REL_BUCKETS - 1]
    w_in_r = _prep_w_in(w_in)
    gate_bias = _prep_gate_bias(mlstm_ig_b, mlstm_fg_b)
    w_out_b = w_out.astype(BF16)
    w_up_r = _prep_ffn_cols(ffn_w_up).astype(BF16)
    f_w = _prep_ffn_cols(ffn_conv_w)
    f_b = _row(_prep_ffn_cols(ffn_conv_b))
    w_down_r = _prep_w_down(ffn_w_down)
    x2d = x.reshape(batch * seq, D_MODEL)
    for l in range(DEPTH):
        zm, zc, za = _in_proj(x2d, _row(norm1_g), w_in_r, l)
        ym = _mlstm(zm, mlstm_qk_conv_w, _row(mlstm_qk_conv_b), gate_bias, _row(mlstm_head_g),
                    l, batch, seq)
        yc = _conformer(zc, conf_dw_w, _row(conf_dw_b), _row(conf_ln_g), _row(conf_ln_b),
                        l, batch, seq)
        ya = _moba(za, bias_tiles, far_bias, batch, seq)
        x2d = _out_ffn(x2d, ym, yc, ya, w_out_b, _row(norm2_g), w_up_r, f_w, f_b, w_down_r,
                       final_g[None, :], l, seq, l == DEPTH - 1)
    return x2d.reshape(batch, seq, D_MODEL)
```

```python
import functools
import math

import numpy as np
import jax
import jax.numpy as jnp
from jax import lax
from jax.experimental import pallas as pl
from jax.experimental.pallas import tpu as pltpu

F32 = jnp.float32
BF16 = jnp.bfloat16
HIGHEST = lax.Precision.HIGHEST

D_MODEL = 1024
DEPTH = 2
HEAD_DIM = 64
MLSTM_DIM = 256
CONV_DIM = 256
ATTN_DIM = 512
MLSTM_HEADS = MLSTM_DIM // HEAD_DIM
ATTN_HEADS = ATTN_DIM // HEAD_DIM
QK_CONV = 4
CONF_KERNEL = 31
MOBA_BLOCK = 256
MOBA_TOPK = 3
REL_BUCKETS = 32
REL_MAX_DIST = 128
D_FF = 2752
FFN_CONV = 3
EPS = 1e-6
NEG = -1e30

LANES = 128
SUBLANES = 8
VMEM_LIMIT = 56 * 1024 * 1024

GATE_PAD = LANES
ZM_W = 4 * MLSTM_DIM + GATE_PAD
ZC_W = 2 * CONV_DIM
ZA_W = 3 * ATTN_DIM
P_W = ZM_W + ZC_W + ZA_W
D_FFP = -(-D_FF // 256) * 256
FFN_CHUNK = 256
N_FFN_CHUNKS = D_FFP // FFN_CHUNK
FFN_DOWN_GROUP = 4

TM_IN = 512
TM_FFN = 512
MLSTM_L = 256
MLSTM_STATE_ROWS = 2 * HEAD_DIM + 16
CONF_ROWS = 64
CONF_PAD = 32
MAX_BLOCKS = SUBLANES
PEN_BIG = 2.0 ** 100

_NT = (((1,), (1,)), ((), ()))


def _t5_saturation_distance():
    n = np.arange(1, 4 * MOBA_BLOCK, dtype=np.float32)
    max_exact = REL_BUCKETS // 2
    large = max_exact + (np.log(n / max_exact) / math.log(REL_MAX_DIST / max_exact)
                         * (REL_BUCKETS - max_exact)).astype(np.int32)
    bucket = np.where(n < max_exact, n.astype(np.int32), np.minimum(large, REL_BUCKETS - 1))
    not_last = np.nonzero(bucket != REL_BUCKETS - 1)[0]
    return int(n[not_last[-1]]) + 1


assert _t5_saturation_distance() <= MOBA_BLOCK + 1


def _sigmoid(x):
    return 1.0 / (1.0 + jnp.exp(-x))


def _log_sigmoid(x):
    return jnp.minimum(x, 0.0) - jnp.log1p(jnp.exp(-jnp.abs(x)))


def _rms(xf, g):
    return xf * lax.rsqrt(jnp.mean(xf * xf, axis=-1, keepdims=True) + EPS) * g


def _dot(a, b, **kw):
    return jnp.dot(a, b, preferred_element_type=F32, **kw)


def _const_spec(shape):
    nd = len(shape)
    return pl.BlockSpec(shape, lambda *_: (0,) * nd)


def _layer_spec(shape, layer, **kw):
    nd = len(shape)
    return pl.BlockSpec((None,) + tuple(shape), lambda *_: (layer,) + (0,) * nd, **kw)


def _in_proj_kernel(x_ref, g_ref, w_ref, zm_ref, zc_ref, za_ref):
    h = _rms(x_ref[...], g_ref[...]).astype(BF16)
    zm_ref[...] = _dot(h, w_ref[:, 0:ZM_W])
    zc_ref[...] = _dot(h, w_ref[:, ZM_W:ZM_W + ZC_W])
    za_ref[...] = _dot(h, w_ref[:, ZM_W + ZC_W:P_W]).astype(BF16)


def _in_proj(x2d, g, w_r, layer):
    n = x2d.shape[0]
    return pl.pallas_call(
        _in_proj_kernel,
        out_shape=(jax.ShapeDtypeStruct((n, ZM_W), F32),
                   jax.ShapeDtypeStruct((n, ZC_W), F32),
                   jax.ShapeDtypeStruct((n, ZA_W), BF16)),
        grid=(n // TM_IN,),
        in_specs=[pl.BlockSpec((TM_IN, D_MODEL), lambda i: (i, 0)),
                  _layer_spec((1, D_MODEL), layer),
                  _layer_spec((D_MODEL, P_W), layer)],
        out_specs=(pl.BlockSpec((TM_IN, ZM_W), lambda i: (i, 0)),
                   pl.BlockSpec((TM_IN, ZC_W), lambda i: (i, 0)),
                   pl.BlockSpec((TM_IN, ZA_W), lambda i: (i, 0))),
        compiler_params=pltpu.CompilerParams(
            dimension_semantics=("arbitrary",), vmem_limit_bytes=VMEM_LIMIT),
        name="in_proj",
    )(x2d, g, w_r)


def _mlstm_kernel(zm_ref, cw_ref, cb_ref, gb_ref, hg_ref, o_ref, g_ref, gt_ref, *, seq):
    L = MLSTM_L
    n_chunks = seq // L
    nh = MLSTM_HEADS
    scale = HEAD_DIM ** -0.5
    lane = lax.broadcasted_iota(jnp.int32, (1, LANES), 1)
    head0 = lane < HEAD_DIM

    gates = zm_ref[:, 4 * MLSTM_DIM:ZM_W] + gb_ref[...]
    is_f = (lane >= nh) & (lane < 2 * nh)
    gates = jnp.where(is_f, _log_sigmoid(gates), gates)
    g_ref[...] = gates
    gt_ref[...] = gates.T[0:SUBLANES, :]

    ri = lax.broadcasted_iota(jnp.int32, (L, L), 0)
    ci = lax.broadcasted_iota(jnp.int32, (L, L), 1)
    visible = ri <= ci
    tri = (ci <= ri).astype(F32)
    tri_t = visible.astype(F32)
    srow = lax.broadcasted_iota(jnp.int32, (MLSTM_STATE_ROWS, LANES), 0)
    scol = lax.broadcasted_iota(jnp.int32, (MLSTM_STATE_ROWS, LANES), 1)
    srow_head = jnp.where(srow < LANES, srow // HEAD_DIM, srow - LANES)
    state_mask = (scol // HEAD_DIM) == srow_head
    row_is_head0 = srow_head[:, 0:1] == 0
    vrow_is_head0 = lax.broadcasted_iota(jnp.int32, (LANES, 1), 0) < HEAD_DIM
    extra_row = lax.broadcasted_iota(jnp.int32, (MLSTM_STATE_ROWS - LANES, 1), 0)
    ones_rows = jnp.ones((MLSTM_STATE_ROWS - LANES, L), BF16)
    cb = cb_ref[...]
    hg = hg_ref[...]

    def chunk(c, carry):
        states, m_prev = carry
        r0 = pl.multiple_of(c * L, L)
        gc = g_ref[pl.ds(r0, L), :]
        gtc = gt_ref[:, pl.ds(r0, L)]
        bcols = _dot(tri, gc, precision=HIGHEST)
        brows = _dot(gtc, tri_t, precision=HIGHEST)
        e_rows = gtc[0:nh] - brows[nh:2 * nh]

        xcur = zm_ref[pl.ds(r0, L), 0:2 * MLSTM_DIM]
        pr = pl.multiple_of(jnp.maximum(r0 - SUBLANES, 0), SUBLANES)
        xprev = jnp.where(c > 0, zm_ref[pl.ds(pr, SUBLANES), 0:2 * MLSTM_DIM], 0.0)
        xcat = jnp.concatenate([xprev, xcur], axis=0)
        y = cb
        for j in range(QK_CONV):
            off = SUBLANES - (QK_CONV - 1) + j
            y = y + cw_ref[j:j + 1, :] * xcat[off:off + L]
        qk = y * _sigmoid(y)

        new_states = []
        new_m = []
        for p in range(2):
            lo = p * LANES
            q_t = (qk[:, lo:lo + LANES] * scale).T.astype(BF16)
            k_b = qk[:, MLSTM_DIM + lo:MLSTM_DIM + lo + LANES].astype(BF16)
            v_t = zm_ref[pl.ds(r0, L), 2 * MLSTM_DIM + lo:2 * MLSTM_DIM + lo + LANES].T
            v_tb = v_t.astype(BF16)
            inter = _dot(states[p].astype(BF16), q_t)
            hn_t, wks, decays = [], [], []
            for hh in range(2):
                h = 2 * p + hh
                mp = m_prev[h]
                e_col = gc[:, h:h + 1] - bcols[:, nh + h:nh + h + 1]
                e_row = e_rows[h:h + 1]
                b_row = brows[nh + h:nh + h + 1]
                em = jnp.where(visible, e_col, NEG)
                g = jnp.maximum(mp, jnp.max(em, axis=0, keepdims=True))
                w_t = jnp.exp(em - g)
                k_h = jnp.where(head0 if hh == 0 else ~head0, k_b, jnp.zeros_like(k_b))
                s_t = _dot(k_h, q_t)
                lhs = jnp.concatenate(
                    [v_tb[hh * HEAD_DIM:(hh + 1) * HEAD_DIM], ones_rows], axis=0)
                r = _dot(lhs, (s_t * w_t).astype(BF16))
                a = jnp.exp(mp - g)
                num = a * inter[hh * HEAD_DIM:(hh + 1) * HEAD_DIM] + r[0:HEAD_DIM]
                den = a * inter[LANES + hh:LANES + hh + 1] + r[HEAD_DIM:HEAD_DIM + 1]
                hv = num * (1.0 / jnp.maximum(jnp.abs(den), jnp.exp(-(b_row + g))))
                mu = jnp.mean(hv, axis=0, keepdims=True)
                d = hv - mu
                var = jnp.mean(d * d, axis=0, keepdims=True)
                hn_t.append(d * lax.rsqrt(var + EPS))
                g_last = jnp.maximum(mp, jnp.max(e_row, axis=1, keepdims=True))
                wks.append(jnp.exp(e_row - g_last))
                decays.append(jnp.exp(mp - g_last))
                new_m.append(b_row[:, L - 1:L] + g_last)
            vw = v_t * jnp.where(vrow_is_head0, wks[0], wks[1])
            extra = jnp.where(extra_row == 0, wks[0], jnp.where(extra_row == 1, wks[1], 0.0))
            upd = _dot(jnp.concatenate([vw, extra], axis=0).astype(BF16), k_b)
            decay_rows = jnp.where(row_is_head0, decays[0], decays[1])
            new_states.append(decay_rows * states[p] + jnp.where(state_mask, upd, 0.0))

            hn = jnp.concatenate(hn_t, axis=0).T
            og = zm_ref[pl.ds(r0, L), 3 * MLSTM_DIM + lo:3 * MLSTM_DIM + lo + LANES]
            o_ref[pl.ds(r0, L), lo:lo + LANES] = (hn * hg[:, lo:lo + LANES] * _sigmoid(og)).astype(o_ref.dtype)
        return tuple(new_states), tuple(new_m)

    init = (tuple(jnp.zeros((MLSTM_STATE_ROWS, LANES), F32) for _ in range(2)),
            tuple(jnp.full((1, 1), NEG, F32) for _ in range(nh)))
    lax.fori_loop(0, n_chunks, chunk, init)


def _mlstm(zm, cw, cb, gate_bias, head_g, layer, batch, seq):
    return pl.pallas_call(
        functools.partial(_mlstm_kernel, seq=seq),
        out_shape=jax.ShapeDtypeStruct((batch * seq, MLSTM_DIM), BF16),
        grid=(batch,),
        in_specs=[pl.BlockSpec((seq, ZM_W), lambda b: (b, 0)),
                  _layer_spec((QK_CONV, 2 * MLSTM_DIM), layer),
                  _layer_spec((1, 2 * MLSTM_DIM), layer),
                  _layer_spec((1, GATE_PAD), layer),
                  _layer_spec((1, MLSTM_DIM), layer)],
        out_specs=pl.BlockSpec((seq, MLSTM_DIM), lambda b: (b, 0)),
        scratch_shapes=[pltpu.VMEM((seq, GATE_PAD), F32),
                        pltpu.VMEM((SUBLANES, seq), F32)],
        compiler_params=pltpu.CompilerParams(
            dimension_semantics=("arbitrary",), vmem_limit_bytes=VMEM_LIMIT),
        name="mlstm",
    )(zm, cw, cb, gate_bias, head_g)


def _conformer_kernel(zc_ref, w_ref, b_ref, lg_ref, lb_ref, o_ref, up_ref, *, seq):
    up_ref[0:CONF_PAD, :] = jnp.zeros((CONF_PAD, CONV_DIM), F32)
    up_ref[CONF_PAD:CONF_PAD + seq, :] = zc_ref[:, 0:CONV_DIM] * _sigmoid(zc_ref[:, CONV_DIM:ZC_W])
    bias = b_ref[...]
    lg = lg_ref[...]
    lb = lb_ref[...]

    def tile(c, carry):
        r0 = pl.multiple_of(c * CONF_ROWS, CONF_ROWS)
        win_rows = CONF_ROWS + CONF_PAD
        win = up_ref[pl.ds(r0, win_rows), :]
        acc = jnp.broadcast_to(bias, (CONF_ROWS, CONV_DIM))
        for r in range(SUBLANES):
            rolled = win if r == 0 else pltpu.roll(win, win_rows - r, axis=0)
            for j in range(CONF_KERNEL):
                off = CONF_PAD - (CONF_KERNEL - 1) + j
                if off % SUBLANES == r:
                    base = off - r
                    acc = acc + w_ref[j:j + 1, :] * rolled[base:base + CONF_ROWS]
        mu = jnp.mean(acc, axis=-1, keepdims=True)
        d = acc - mu
        var = jnp.mean(d * d, axis=-1, keepdims=True)
        y = d * lax.rsqrt(var + EPS) * lg + lb
        o_ref[pl.ds(r0, CONF_ROWS), :] = (y * _sigmoid(y)).astype(o_ref.dtype)
        return carry

    lax.fori_loop(0, seq // CONF_ROWS, tile, 0)


def _conformer(zc, w, b, ln_g, ln_b, layer, batch, seq):
    return pl.pallas_call(
        functools.partial(_conformer_kernel, seq=seq),
        out_shape=jax.ShapeDtypeStruct((batch * seq, CONV_DIM), BF16),
        grid=(batch,),
        in_specs=[pl.BlockSpec((seq, ZC_W), lambda b: (b, 0)),
                  _layer_spec((CONF_KERNEL, CONV_DIM), layer),
                  _layer_spec((1, CONV_DIM), layer),
                  _layer_spec((1, CONV_DIM), layer),
                  _layer_spec((1, CONV_DIM), layer)],
        out_specs=pl.BlockSpec((seq, CONV_DIM), lambda b: (b, 0)),
        scratch_shapes=[pltpu.VMEM((CONF_PAD + seq, CONV_DIM), F32)],
        compiler_params=pltpu.CompilerParams(
            dimension_semantics=("arbitrary",), vmem_limit_bytes=VMEM_LIMIT),
        name="conformer",
    )(zc, w, b, ln_g, ln_b)


def _bias_tiles_kernel(rb_ref, o_ref):
    h = pl.program_id(0)
    i = lax.broadcasted_iota(jnp.int32, (MOBA_BLOCK, MOBA_BLOCK), 0)
    j = lax.broadcasted_iota(jnp.int32, (MOBA_BLOCK, MOBA_BLOCK), 1)
    max_exact = REL_BUCKETS // 2
    for t in range(2):
        dist = i - j + t * MOBA_BLOCK
        n = jnp.maximum(dist, 0)
        nf = jnp.maximum(n, 1).astype(F32)
        large = max_exact + (jnp.log(nf / max_exact) / math.log(REL_MAX_DIST / max_exact)
                             * (REL_BUCKETS - max_exact)).astype(jnp.int32)
        large = jnp.minimum(large, REL_BUCKETS - 1)
        bucket = jnp.where(n < max_exact, n, large)
        bias = jnp.zeros((MOBA_BLOCK, MOBA_BLOCK), F32)
        for bk in range(REL_BUCKETS):
            bias = jnp.where(bucket == bk, rb_ref[bk, h], bias)
        if t == 0:
            bias = jnp.where(dist >= 0, bias, NEG)
        o_ref[0, t] = bias


def _bias_tiles(rel_bias):
    return pl.pallas_call(
        _bias_tiles_kernel,
        out_shape=jax.ShapeDtypeStruct((ATTN_HEADS, 2, MOBA_BLOCK, MOBA_BLOCK), F32),
        grid=(ATTN_HEADS,),
        in_specs=[pl.BlockSpec(memory_space=pltpu.SMEM)],
        out_specs=pl.BlockSpec((1, 2, MOBA_BLOCK, MOBA_BLOCK), lambda h: (h, 0, 0, 0)),
        compiler_params=pltpu.CompilerParams(dimension_semantics=("arbitrary",)),
        name="bias_tiles",
    )(rel_bias)


def _moba_kernel(far_ref, q_ref, k_ref, v_ref, d_ref, o_ref, qa_ref, ka_ref, s_ref, va_ref,
                 p_ref, *, seq):
    hp = pl.program_id(0)
    blk = MOBA_BLOCK
    nb = seq // blk
    scale = HEAD_DIM ** -0.5
    lane = lax.broadcasted_iota(jnp.int32, (1, LANES), 1)
    head0 = lane < HEAD_DIM

    blk_i = lax.broadcasted_iota(jnp.int32, (MAX_BLOCKS, seq), 0)
    own_i = lax.broadcasted_iota(jnp.int32, (MAX_BLOCKS, seq), 1) // blk
    row_blk = lax.broadcasted_iota(jnp.int32, (seq, LANES), 0) // blk
    lane_full = lax.broadcasted_iota(jnp.int32, (seq, LANES), 1)

    avg = jnp.where(blk_i == own_i, 1.0 / blk, 0.0).astype(BF16)
    kmean = _dot(avg, k_ref[...])
    km_hi = kmean.astype(BF16).astype(F32)
    km_mid = (kmean - km_hi).astype(BF16).astype(F32)
    km_lo = kmean - km_hi - km_mid
    gate_lhs = jnp.concatenate(
        [jnp.where(head0 if hh == 0 else ~head0, part, 0.0)
         for hh in range(2) for part in (km_hi, km_mid, km_lo)], axis=0).astype(BF16)
    gates = lax.dot_general(gate_lhs, q_ref[...], _NT, preferred_element_type=F32)
    sel_row = lax.broadcasted_iota(jnp.int32, (2 * MAX_BLOCKS, LANES), 0)
    sel_lane = lax.broadcasted_iota(jnp.int32, (2 * MAX_BLOCKS, LANES), 1)

    for hh in range(2):
        mine = head0 if hh == 0 else ~head0
        g0 = 3 * MAX_BLOCKS * hh
        gate = (gates[g0:g0 + MAX_BLOCKS] + gates[g0 + MAX_BLOCKS:g0 + 2 * MAX_BLOCKS]
                + gates[g0 + 2 * MAX_BLOCKS:g0 + 3 * MAX_BLOCKS])
        cand = blk_i < own_i
        g = jnp.where(cand, gate, NEG)
        rank = jnp.zeros((MAX_BLOCKS, seq), jnp.int32)
        for m in range(MAX_BLOCKS):
            gm = g[m:m + 1, :]
            beats = (gm > g) | ((gm == g) & (blk_i > m))
            rank = rank + beats.astype(jnp.int32)
        keep = (cand & (rank < MOBA_TOPK)) | (blk_i >= own_i)
        pen = jnp.where(keep, 0.0, -PEN_BIG)
        pen_lane0 = (1 - hh) * HEAD_DIM
        pen16 = jnp.concatenate([pen, jnp.zeros_like(pen)], axis=0).astype(BF16)
        place = ((sel_row < MAX_BLOCKS) & (sel_lane == pen_lane0 + sel_row)).astype(BF16)
        pen_t = lax.dot_general(pen16, place, (((0,), (0,)), ((), ())),
                                preferred_element_type=F32)
        q_scaled = (q_ref[...].astype(F32) * scale).astype(BF16)
        qa_ref[hh] = jnp.where(mine, q_scaled, pen_t.astype(BF16))
        onehot = (lane_full == pen_lane0 + row_blk).astype(BF16)
        ka_ref[hh] = jnp.where(mine, k_ref[...], onehot)

    va_ref[:, 0:LANES] = v_ref[...]
    va_ref[:, LANES:2 * LANES] = jnp.ones((seq, LANES), BF16)

    half = blk // 2

    def pass1(own, hh):
        q = qa_ref[hh, own * blk:(own + 1) * blk, :]
        far_bias = far_ref[2 * hp + hh]
        mx = {}
        for n in range(own + 1):
            s = lax.dot_general(q, ka_ref[hh, n * blk:(n + 1) * blk, :], _NT,
                                preferred_element_type=F32)
            is_far = n < own - 1
            if n == own:
                s = s + d_ref[hh, 0]
            elif n == own - 1:
                s = s + d_ref[hh, 1]
            s_ref[hh, n] = s
            folded = jnp.maximum(s[:, :half], s[:, half:])
            mx[is_far] = jnp.maximum(mx[is_far], folded) if is_far in mx else folded
        mx_all = jnp.maximum(mx[False], mx[True] + far_bias) if True in mx else mx[False]
        m = jnp.max(mx_all, axis=1, keepdims=True)
        return m, m - far_bias

    def pass2(own, hh, m, m_far):
        for n in range(own + 1):
            p = jnp.exp(s_ref[hh, n] - (m_far if n < own - 1 else m))
            p_ref[hh, :, n * blk:(n + 1) * blk] = p.astype(BF16)
        keys = (own + 1) * blk
        res = _dot(p_ref[hh, :, 0:keys], va_ref[0:keys, :])
        return res[:, 0:LANES] / res[:, LANES:2 * LANES]

    items = [(own, hh) for own in range(nb) for hh in range(2)]
    stats = pass1(*items[0])
    outs = []
    for i, (own, hh) in enumerate(items):
        next_stats = pass1(*items[i + 1]) if i + 1 < len(items) else None
        outs.append(pass2(own, hh, *stats))
        stats = next_stats
        if hh == 1:
            o_ref[own * blk:(own + 1) * blk, :] = jnp.where(head0, *outs).astype(o_ref.dtype)
            outs = []


def _moba(za, bias_tiles, far_bias, batch, seq):
    n_pairs = ATTN_HEADS // 2
    assert seq % MOBA_BLOCK == 0 and seq // MOBA_BLOCK <= MAX_BLOCKS
    return pl.pallas_call(
        functools.partial(_moba_kernel, seq=seq),
        out_shape=jax.ShapeDtypeStruct((batch * seq, ATTN_DIM), BF16),
        grid=(n_pairs, batch),
        in_specs=[pl.BlockSpec(memory_space=pltpu.SMEM),
                  pl.BlockSpec((seq, LANES), lambda hp, b: (b, hp)),
                  pl.BlockSpec((seq, LANES), lambda hp, b: (b, n_pairs + hp)),
                  pl.BlockSpec((seq, LANES), lambda hp, b: (b, 2 * n_pairs + hp)),
                  pl.BlockSpec((2, 2, MOBA_BLOCK, MOBA_BLOCK), lambda hp, b: (hp, 0, 0, 0))],
        out_specs=pl.BlockSpec((seq, LANES), lambda hp, b: (b, hp)),
        scratch_shapes=[pltpu.VMEM((2, seq, LANES), BF16),
                        pltpu.VMEM((2, seq, LANES), BF16),
                        pltpu.VMEM((2, seq // MOBA_BLOCK, MOBA_BLOCK, MOBA_BLOCK), F32),
                        pltpu.VMEM((seq, 2 * LANES), BF16),
                        pltpu.VMEM((2, MOBA_BLOCK, seq), BF16)],
        compiler_params=pltpu.CompilerParams(
            dimension_semantics=("arbitrary", "arbitrary"), vmem_limit_bytes=VMEM_LIMIT),
        name="moba",
    )(far_bias, za, za, za, bias_tiles)


def _out_ffn_kernel(x_ref, ym_ref, yc_ref, ya_ref, wo_ref, g2_ref, wup_ref, cw_ref, cb_ref,
                    wdn_ref, fg_ref, o_ref, hn_ref, acc_ref, act_ref, tail_ref,
                    *, tiles_per_seq, final_norm):
    tm = TM_FFN
    first = (pl.program_id(0) % tiles_per_seq) == 0
    y = jnp.concatenate([ym_ref[...], yc_ref[...], ya_ref[...]], axis=1)
    x1 = x_ref[...] + _dot(y, wo_ref[...])
    hn_ref[...] = _rms(x1, g2_ref[...]).astype(BF16)
    acc_ref[...] = x1
    row8 = lax.broadcasted_iota(jnp.int32, (SUBLANES, FFN_CHUNK), 0)

    for c in range(N_FFN_CHUNKS):
        conv = []
        for part in range(2):
            c0 = part * D_FFP + c * FFN_CHUNK
            u = _dot(hn_ref[...], wup_ref[:, c0:c0 + FFN_CHUNK])
            tail = jnp.where(first, 0.0, tail_ref[2 * c + part])
            tail_ref[2 * c + part] = u[tm - SUBLANES:tm, :]
            acc = cb_ref[:, c0:c0 + FFN_CHUNK] + cw_ref[FFN_CONV - 1:FFN_CONV, c0:c0 + FFN_CHUNK] * u
            for d in range(1, FFN_CONV):
                shifted = pltpu.roll(u, d, axis=0)
                top = jnp.where(row8 < d, pltpu.roll(tail, d, axis=0), shifted[0:SUBLANES])
                shifted = jnp.concatenate([top, shifted[SUBLANES:]], axis=0)
                j = FFN_CONV - 1 - d
                acc = acc + cw_ref[j:j + 1, c0:c0 + FFN_CHUNK] * shifted
            conv.append(acc)
        act_ref[:, c * FFN_CHUNK:(c + 1) * FFN_CHUNK] = (
            conv[0] * _sigmoid(conv[0]) * conv[1]).astype(BF16)
        if (c + 1) % FFN_DOWN_GROUP == 0 or c == N_FFN_CHUNKS - 1:
            k0 = (c // FFN_DOWN_GROUP) * FFN_DOWN_GROUP * FFN_CHUNK
            k1 = (c + 1) * FFN_CHUNK
            acc_ref[...] += _dot(act_ref[:, k0:k1], wdn_ref[k0:k1, :])

    out = acc_ref[...]
    if final_norm:
        out = _rms(out, fg_ref[...])
    o_ref[...] = out


def _out_ffn(x2d, ym, yc, ya, wo, g2, wup, cw, cb, wdn, fg, layer, seq, final_norm):
    n = x2d.shape[0]
    tiles_per_seq = seq // TM_FFN

    def const(shape):
        return _layer_spec(shape, layer, pipeline_mode=pl.Buffered(1))

    return pl.pallas_call(
        functools.partial(_out_ffn_kernel, tiles_per_seq=tiles_per_seq, final_norm=final_norm),
        out_shape=jax.ShapeDtypeStruct((n, D_MODEL), F32),
        grid=(n // TM_FFN,),
        in_specs=[pl.BlockSpec((TM_FFN, D_MODEL), lambda i: (i, 0)),
                  pl.BlockSpec((TM_FFN, MLSTM_DIM), lambda i: (i, 0)),
                  pl.BlockSpec((TM_FFN, CONV_DIM), lambda i: (i, 0)),
                  pl.BlockSpec((TM_FFN, ATTN_DIM), lambda i: (i, 0)),
                  const((D_MODEL, D_MODEL)),
                  const((1, D_MODEL)),
                  const((D_MODEL, 2 * D_FFP)),
                  const((FFN_CONV, 2 * D_FFP)),
                  const((1, 2 * D_FFP)),
                  const((D_FFP, D_MODEL)),
                  _const_spec((1, D_MODEL))],
        out_specs=pl.BlockSpec((TM_FFN, D_MODEL), lambda i: (i, 0)),
        scratch_shapes=[pltpu.VMEM((TM_FFN, D_MODEL), BF16),
                        pltpu.VMEM((TM_FFN, D_MODEL), F32),
                        pltpu.VMEM((TM_FFN, D_FFP), BF16),
                        pltpu.VMEM((2 * N_FFN_CHUNKS, SUBLANES, FFN_CHUNK), F32)],
        compiler_params=pltpu.CompilerParams(
            dimension_semantics=("arbitrary",), vmem_limit_bytes=VMEM_LIMIT),
        name="out_ffn",
    )(x2d, ym, yc, ya, wo, g2, wup, cw, cb, wdn, fg)


def _pad_last(a, width):
    return jnp.pad(a, [(0, 0)] * (a.ndim - 1) + [(0, width - a.shape[-1])])


def _row(a):
    return a[:, None, :]


def _prep_w_in(w_in):
    m_end = 4 * MLSTM_DIM
    g_end = m_end + 2 * MLSTM_HEADS
    return jnp.concatenate([w_in[..., :m_end], _pad_last(w_in[..., m_end:g_end], GATE_PAD),
                            w_in[..., g_end:]], axis=-1).astype(BF16)


def _prep_ffn_cols(a):
    return jnp.concatenate([_pad_last(a[..., :D_FF], D_FFP), _pad_last(a[..., D_FF:], D_FFP)],
                           axis=-1)


def _prep_w_down(w_down):
    return jnp.pad(w_down, ((0, 0), (0, D_FFP - D_FF), (0, 0))).astype(BF16)


def _prep_gate_bias(ig_b, fg_b):
    return _row(_pad_last(jnp.concatenate([ig_b, fg_b], axis=-1), GATE_PAD))


def kernel(x, norm1_g, w_in, mlstm_qk_conv_w, mlstm_qk_conv_b, mlstm_ig_b, mlstm_fg_b, mlstm_head_g, conf_dw_w, conf_dw_b, conf_ln_g, conf_ln_b, rel_bias, w_out, norm2_g, ffn_w_up, ffn_conv_w, ffn_conv_b, ffn_w_down, final_g):
    batch, seq, _ = x.shape
    bias_tiles = _bias_tiles(rel_bias)
    far_bias = rel_bias[REL_BUCKETS - 1]
    w_in_r = _prep_w_in(w_in)
    gate_bias = _prep_gate_bias(mlstm_ig_b, mlstm_fg_b)
    w_out_b = w_out.astype(BF16)
    w_up_r = _prep_ffn_cols(ffn_w_up).astype(BF16)
    f_w = _prep_ffn_cols(ffn_conv_w)
    f_b = _row(_prep_ffn_cols(ffn_conv_b))
    w_down_r = _prep_w_down(ffn_w_down)
    x2d = x.reshape(batch * seq, D_MODEL)
    for l in range(DEPTH):
        zm, zc, za = _in_proj(x2d, _row(norm1_g), w_in_r, l)
        ym = _mlstm(zm, mlstm_qk_conv_w, _row(mlstm_qk_conv_b), gate_bias, _row(mlstm_head_g),
                    l, batch, seq)
        yc = _conformer(zc, conf_dw_w, _row(conf_dw_b), _row(conf_ln_g), _row(conf_ln_b),
                        l, batch, seq)
        ya = _moba(za, bias_tiles, far_bias, batch, seq)
        x2d = _out_ffn(x2d, ym, yc, ya, w_out_b, _row(norm2_g), w_up_r, f_w, f_b, w_down_r,
                       final_g[None, :], l, seq, l == DEPTH - 1)
    return x2d.reshape(batch, seq, D_MODEL)
```

```python
import functools
import math

import numpy as np
import jax
import jax.numpy as jnp
from jax import lax
from jax.experimental import pallas as pl
from jax.experimental.pallas import tpu as pltpu

F32 = jnp.float32
BF16 = jnp.bfloat16
HIGHEST = lax.Precision.HIGHEST

D_MODEL = 1024
DEPTH = 2
HEAD_DIM = 64
MLSTM_DIM = 256
CONV_DIM = 256
ATTN_DIM = 512
MLSTM_HEADS = MLSTM_DIM // HEAD_DIM
ATTN_HEADS = ATTN_DIM // HEAD_DIM
QK_CONV = 4
CONF_KERNEL = 31
MOBA_BLOCK = 256
MOBA_TOPK = 3
REL_BUCKETS = 32
REL_MAX_DIST = 128
D_FF = 2752
FFN_CONV = 3
EPS = 1e-6
NEG = -1e30

LANES = 128
SUBLANES = 8
VMEM_LIMIT = 56 * 1024 * 1024

GATE_PAD = LANES
ZM_W = 4 * MLSTM_DIM + GATE_PAD
ZC_W = 2 * CONV_DIM
ZA_W = 3 * ATTN_DIM
P_W = ZM_W + ZC_W + ZA_W
D_FFP = -(-D_FF // 256) * 256
FFN_CHUNK = 256
N_FFN_CHUNKS = D_FFP // FFN_CHUNK
FFN_DOWN_ENDS = (4, 8, N_FFN_CHUNKS - 1, N_FFN_CHUNKS)

TM_IN = 512
TM_FFN = 512
MLSTM_L = 256
MLSTM_STATE_ROWS = 2 * HEAD_DIM + 16
CONF_ROWS = 64
CONF_PAD = 32
MAX_BLOCKS = SUBLANES
PEN_BIG = 2.0 ** 100

_NT = (((1,), (1,)), ((), ()))


def _t5_saturation_distance():
    n = np.arange(1, 4 * MOBA_BLOCK, dtype=np.float32)
    max_exact = REL_BUCKETS // 2
    large = max_exact + (np.log(n / max_exact) / math.log(REL_MAX_DIST / max_exact)
                         * (REL_BUCKETS - max_exact)).astype(np.int32)
    bucket = np.where(n < max_exact, n.astype(np.int32), np.minimum(large, REL_BUCKETS - 1))
    not_last = np.nonzero(bucket != REL_BUCKETS - 1)[0]
    return int(n[not_last[-1]]) + 1


assert _t5_saturation_distance() <= MOBA_BLOCK + 1


def _sigmoid(x):
    return 1.0 / (1.0 + jnp.exp(-x))


def _log_sigmoid(x):
    return jnp.minimum(x, 0.0) - jnp.log1p(jnp.exp(-jnp.abs(x)))


def _rms(xf, g):
    return xf * lax.rsqrt(jnp.mean(xf * xf, axis=-1, keepdims=True) + EPS) * g


def _dot(a, b, **kw):
    return jnp.dot(a, b, preferred_element_type=F32, **kw)


def _const_spec(shape):
    nd = len(shape)
    return pl.BlockSpec(shape, lambda *_: (0,) * nd)


def _layer_spec(shape, layer, **kw):
    nd = len(shape)
    return pl.BlockSpec((None,) + tuple(shape), lambda *_: (layer,) + (0,) * nd, **kw)


def _conformer_tile(win, w_ref, bias, ln_g, ln_b):
    win_rows = CONF_ROWS + CONF_PAD
    acc = jnp.broadcast_to(bias, (CONF_ROWS, CONV_DIM))
    for r in range(SUBLANES):
        rolled = win if r == 0 else pltpu.roll(win, win_rows - r, axis=0)
        for j in range(CONF_KERNEL):
            off = CONF_PAD - (CONF_KERNEL - 1) + j
            if off % SUBLANES == r:
                base = off - r
                acc = acc + w_ref[j:j + 1, :] * rolled[base:base + CONF_ROWS]
    mu = jnp.mean(acc, axis=-1, keepdims=True)
    d = acc - mu
    var = jnp.mean(d * d, axis=-1, keepdims=True)
    y = d * lax.rsqrt(var + EPS) * ln_g + ln_b
    return y * _sigmoid(y)


def _in_proj_kernel(x_ref, g_ref, w_ref, cw_ref, cb_ref, lg_ref, lb_ref, zm_ref, za_ref, yc_ref,
                    up_ref, *, tiles_per_seq):
    tm = TM_IN
    first = (pl.program_id(0) % tiles_per_seq) == 0
    h = _rms(x_ref[...], g_ref[...]).astype(BF16)
    zc = _dot(h, w_ref[:, ZM_W:ZM_W + ZC_W])
    up_ref[0:CONF_PAD, :] = jnp.where(first, 0.0, up_ref[tm:tm + CONF_PAD, :])
    up_ref[CONF_PAD:CONF_PAD + tm, :] = zc[:, 0:CONV_DIM] * _sigmoid(zc[:, CONV_DIM:ZC_W])
    zm_ref[...] = _dot(h, w_ref[:, 0:ZM_W])
    za_ref[...] = _dot(h, w_ref[:, ZM_W + ZC_W:P_W]).astype(BF16)
    bias, ln_g, ln_b = cb_ref[...], lg_ref[...], lb_ref[...]
    for t in range(tm // CONF_ROWS):
        r0 = t * CONF_ROWS
        win = up_ref[r0:r0 + CONF_ROWS + CONF_PAD, :]
        yc_ref[r0:r0 + CONF_ROWS, :] = _conformer_tile(win, cw_ref, bias, ln_g, ln_b).astype(yc_ref.dtype)


def _in_proj(x2d, g, w_r, conf_w, conf_b, conf_ln_g, conf_ln_b, layer, seq):
    n = x2d.shape[0]
    return pl.pallas_call(
        functools.partial(_in_proj_kernel, tiles_per_seq=seq // TM_IN),
        out_shape=(jax.ShapeDtypeStruct((n, ZM_W), F32),
                   jax.ShapeDtypeStruct((n, ZA_W), BF16),
                   jax.ShapeDtypeStruct((n, CONV_DIM), BF16)),
        grid=(n // TM_IN,),
        in_specs=[pl.BlockSpec((TM_IN, D_MODEL), lambda i: (i, 0)),
                  _layer_spec((1, D_MODEL), layer),
                  _layer_spec((D_MODEL, P_W), layer),
                  _layer_spec((CONF_KERNEL, CONV_DIM), layer),
                  _layer_spec((1, CONV_DIM), layer),
                  _layer_spec((1, CONV_DIM), layer),
                  _layer_spec((1, CONV_DIM), layer)],
        out_specs=(pl.BlockSpec((TM_IN, ZM_W), lambda i: (i, 0)),
                   pl.BlockSpec((TM_IN, ZA_W), lambda i: (i, 0)),
                   pl.BlockSpec((TM_IN, CONV_DIM), lambda i: (i, 0))),
        scratch_shapes=[pltpu.VMEM((CONF_PAD + TM_IN, CONV_DIM), F32)],
        compiler_params=pltpu.CompilerParams(
            dimension_semantics=("arbitrary",), vmem_limit_bytes=VMEM_LIMIT),
        name="in_proj",
    )(x2d, g, w_r, conf_w, conf_b, conf_ln_g, conf_ln_b)


def _mlstm_kernel(zm_ref, cw_ref, cb_ref, gb_ref, hg_ref, o_ref, g_ref, gt_ref, *, seq):
    L = MLSTM_L
    n_chunks = seq // L
    nh = MLSTM_HEADS
    scale = HEAD_DIM ** -0.5
    lane = lax.broadcasted_iota(jnp.int32, (1, LANES), 1)
    head0 = lane < HEAD_DIM

    gates = zm_ref[:, 4 * MLSTM_DIM:ZM_W] + gb_ref[...]
    is_f = (lane >= nh) & (lane < 2 * nh)
    gates = jnp.where(is_f, _log_sigmoid(gates), gates)
    g_ref[...] = gates
    gt_ref[...] = gates.T[0:SUBLANES, :]

    ri = lax.broadcasted_iota(jnp.int32, (L, L), 0)
    ci = lax.broadcasted_iota(jnp.int32, (L, L), 1)
    visible = ri <= ci
    tri = (ci <= ri).astype(F32)
    tri_t = visible.astype(F32)
    srow = lax.broadcasted_iota(jnp.int32, (MLSTM_STATE_ROWS, LANES), 0)
    scol = lax.broadcasted_iota(jnp.int32, (MLSTM_STATE_ROWS, LANES), 1)
    srow_head = jnp.where(srow < LANES, srow // HEAD_DIM, srow - LANES)
    state_mask = (scol // HEAD_DIM) == srow_head
    row_is_head0 = srow_head[:, 0:1] == 0
    vrow_is_head0 = lax.broadcasted_iota(jnp.int32, (LANES, 1), 0) < HEAD_DIM
    extra_row = lax.broadcasted_iota(jnp.int32, (MLSTM_STATE_ROWS - LANES, 1), 0)
    ones_rows = jnp.ones((MLSTM_STATE_ROWS - LANES, L), BF16)
    cb = cb_ref[...]
    hg = hg_ref[...]

    def chunk(c, carry):
        states, m_prev = carry
        r0 = pl.multiple_of(c * L, L)
        gc = g_ref[pl.ds(r0, L), :]
        gtc = gt_ref[:, pl.ds(r0, L)]
        bcols = _dot(tri, gc, precision=HIGHEST)
        brows = _dot(gtc, tri_t, precision=HIGHEST)
        e_rows = gtc[0:nh] - brows[nh:2 * nh]

        xcur = zm_ref[pl.ds(r0, L), 0:2 * MLSTM_DIM]
        pr = pl.multiple_of(jnp.maximum(r0 - SUBLANES, 0), SUBLANES)
        xprev = jnp.where(c > 0, zm_ref[pl.ds(pr, SUBLANES), 0:2 * MLSTM_DIM], 0.0)
        xcat = jnp.concatenate([xprev, xcur], axis=0)
        y = cb
        for j in range(QK_CONV):
            off = SUBLANES - (QK_CONV - 1) + j
            y = y + cw_ref[j:j + 1, :] * xcat[off:off + L]
        qk = y * _sigmoid(y)

        new_states = []
        new_m = []
        for p in range(2):
            lo = p * LANES
            q_t = (qk[:, lo:lo + LANES] * scale).T.astype(BF16)
            k_b = qk[:, MLSTM_DIM + lo:MLSTM_DIM + lo + LANES].astype(BF16)
            v_t = zm_ref[pl.ds(r0, L), 2 * MLSTM_DIM + lo:2 * MLSTM_DIM + lo + LANES].T
            v_tb = v_t.astype(BF16)
            inter = _dot(states[p].astype(BF16), q_t)
            hn_t, wks, decays = [], [], []
            for hh in range(2):
                h = 2 * p + hh
                mp = m_prev[h]
                e_col = gc[:, h:h + 1] - bcols[:, nh + h:nh + h + 1]
                e_row = e_rows[h:h + 1]
                b_row = brows[nh + h:nh + h + 1]
                em = jnp.where(visible, e_col, NEG)
                g = jnp.maximum(mp, jnp.max(em, axis=0, keepdims=True))
                w_t = jnp.exp(em - g)
                k_h = jnp.where(head0 if hh == 0 else ~head0, k_b, jnp.zeros_like(k_b))
                s_t = _dot(k_h, q_t)
                lhs = jnp.concatenate(
                    [v_tb[hh * HEAD_DIM:(hh + 1) * HEAD_DIM], ones_rows], axis=0)
                r = _dot(lhs, (s_t * w_t).astype(BF16))
                a = jnp.exp(mp - g)
                num = a * inter[hh * HEAD_DIM:(hh + 1) * HEAD_DIM] + r[0:HEAD_DIM]
                den = a * inter[LANES + hh:LANES + hh + 1] + r[HEAD_DIM:HEAD_DIM + 1]
                hv = num * (1.0 / jnp.maximum(jnp.abs(den), jnp.exp(-(b_row + g))))
                mu = jnp.mean(hv, axis=0, keepdims=True)
                d = hv - mu
                var = jnp.mean(d * d, axis=0, keepdims=True)
                hn_t.append(d * lax.rsqrt(var + EPS))
                g_last = jnp.maximum(mp, jnp.max(e_row, axis=1, keepdims=True))
                wks.append(jnp.exp(e_row - g_last))
                decays.append(jnp.exp(mp - g_last))
                new_m.append(b_row[:, L - 1:L] + g_last)
            vw = v_t * jnp.where(vrow_is_head0, wks[0], wks[1])
            extra = jnp.where(extra_row == 0, wks[0], jnp.where(extra_row == 1, wks[1], 0.0))
            upd = _dot(jnp.concatenate([vw, extra], axis=0).astype(BF16), k_b)
            decay_rows = jnp.where(row_is_head0, decays[0], decays[1])
            new_states.append(decay_rows * states[p] + jnp.where(state_mask, upd, 0.0))

            hn = jnp.concatenate(hn_t, axis=0).T
            og = zm_ref[pl.ds(r0, L), 3 * MLSTM_DIM + lo:3 * MLSTM_DIM + lo + LANES]
            o_ref[pl.ds(r0, L), lo:lo + LANES] = (hn * hg[:, lo:lo + LANES] * _sigmoid(og)).astype(o_ref.dtype)
        return tuple(new_states), tuple(new_m)

    init = (tuple(jnp.zeros((MLSTM_STATE_ROWS, LANES), F32) for _ in range(2)),
            tuple(jnp.full((1, 1), NEG, F32) for _ in range(nh)))
    lax.fori_loop(0, n_chunks, chunk, init)


def _mlstm(zm, cw, cb, gate_bias, head_g, layer, batch, seq):
    return pl.pallas_call(
        functools.partial(_mlstm_kernel, seq=seq),
        out_shape=jax.ShapeDtypeStruct((batch * seq, MLSTM_DIM), BF16),
        grid=(batch,),
        in_specs=[pl.BlockSpec((seq, ZM_W), lambda b: (b, 0)),
                  _layer_spec((QK_CONV, 2 * MLSTM_DIM), layer),
                  _layer_spec((1, 2 * MLSTM_DIM), layer),
                  _layer_spec((1, GATE_PAD), layer),
                  _layer_spec((1, MLSTM_DIM), layer)],
        out_specs=pl.BlockSpec((seq, MLSTM_DIM), lambda b: (b, 0)),
        scratch_shapes=[pltpu.VMEM((seq, GATE_PAD), F32),
                        pltpu.VMEM((SUBLANES, seq), F32)],
        compiler_params=pltpu.CompilerParams(
            dimension_semantics=("arbitrary",), vmem_limit_bytes=VMEM_LIMIT),
        name="mlstm",
    )(zm, cw, cb, gate_bias, head_g)


def _bias_tiles_kernel(rb_ref, o_ref):
    h = pl.program_id(0)
    i = lax.broadcasted_iota(jnp.int32, (MOBA_BLOCK, MOBA_BLOCK), 0)
    j = lax.broadcasted_iota(jnp.int32, (MOBA_BLOCK, MOBA_BLOCK), 1)
    max_exact = REL_BUCKETS // 2
    for t in range(2):
        dist = i - j + t * MOBA_BLOCK
        n = jnp.maximum(dist, 0)
        nf = jnp.maximum(n, 1).astype(F32)
        large = max_exact + (jnp.log(nf / max_exact) / math.log(REL_MAX_DIST / max_exact)
                             * (REL_BUCKETS - max_exact)).astype(jnp.int32)
        large = jnp.minimum(large, REL_BUCKETS - 1)
        bucket = jnp.where(n < max_exact, n, large)
        bias = jnp.zeros((MOBA_BLOCK, MOBA_BLOCK), F32)
        for bk in range(REL_BUCKETS):
            bias = jnp.where(bucket == bk, rb_ref[bk, h], bias)
        if t == 0:
            bias = jnp.where(dist >= 0, bias, NEG)
        o_ref[0, t] = bias


def _bias_tiles(rel_bias):
    return pl.pallas_call(
        _bias_tiles_kernel,
        out_shape=jax.ShapeDtypeStruct((ATTN_HEADS, 2, MOBA_BLOCK, MOBA_BLOCK), F32),
        grid=(ATTN_HEADS,),
        in_specs=[pl.BlockSpec(memory_space=pltpu.SMEM)],
        out_specs=pl.BlockSpec((1, 2, MOBA_BLOCK, MOBA_BLOCK), lambda h: (h, 0, 0, 0)),
        compiler_params=pltpu.CompilerParams(dimension_semantics=("arbitrary",)),
        name="bias_tiles",
    )(rel_bias)


def _moba_kernel(far_ref, q_ref, k_ref, v_ref, d_ref, o_ref, qa_ref, ka_ref, s_ref, va_ref,
                 p_ref, *, seq):
    hp = pl.program_id(0)
    blk = MOBA_BLOCK
    nb = seq // blk
    scale = HEAD_DIM ** -0.5
    lane = lax.broadcasted_iota(jnp.int32, (1, LANES), 1)
    head0 = lane < HEAD_DIM

    blk_i = lax.broadcasted_iota(jnp.int32, (MAX_BLOCKS, seq), 0)
    own_i = lax.broadcasted_iota(jnp.int32, (MAX_BLOCKS, seq), 1) // blk
    row_blk = lax.broadcasted_iota(jnp.int32, (seq, LANES), 0) // blk
    lane_full = lax.broadcasted_iota(jnp.int32, (seq, LANES), 1)

    avg = jnp.where(blk_i == own_i, 1.0 / blk, 0.0).astype(BF16)
    kmean = _dot(avg, k_ref[...])
    km_hi = kmean.astype(BF16).astype(F32)
    km_mid = (kmean - km_hi).astype(BF16).astype(F32)
    km_lo = kmean - km_hi - km_mid
    gate_lhs = jnp.concatenate(
        [jnp.where(head0 if hh == 0 else ~head0, part, 0.0)
         for hh in range(2) for part in (km_hi, km_mid, km_lo)], axis=0).astype(BF16)
    gates = lax.dot_general(gate_lhs, q_ref[...], _NT, preferred_element_type=F32)
    sel_row = lax.broadcasted_iota(jnp.int32, (2 * MAX_BLOCKS, LANES), 0)
    sel_lane = lax.broadcasted_iota(jnp.int32, (2 * MAX_BLOCKS, LANES), 1)

    for hh in range(2):
        mine = head0 if hh == 0 else ~head0
        g0 = 3 * MAX_BLOCKS * hh
        gate = (gates[g0:g0 + MAX_BLOCKS] + gates[g0 + MAX_BLOCKS:g0 + 2 * MAX_BLOCKS]
                + gates[g0 + 2 * MAX_BLOCKS:g0 + 3 * MAX_BLOCKS])
        cand = blk_i < own_i
        g = jnp.where(cand, gate, NEG)
        rank = jnp.zeros((MAX_BLOCKS, seq), jnp.int32)
        for m in range(MAX_BLOCKS):
            gm = g[m:m + 1, :]
            beats = (gm > g) | ((gm == g) & (blk_i > m))
            rank = rank + beats.astype(jnp.int32)
        keep = (cand & (rank < MOBA_TOPK)) | (blk_i >= own_i)
        pen = jnp.where(keep, 0.0, -PEN_BIG)
        pen_lane0 = (1 - hh) * HEAD_DIM
        pen16 = jnp.concatenate([pen, jnp.zeros_like(pen)], axis=0).astype(BF16)
        place = ((sel_row < MAX_BLOCKS) & (sel_lane == pen_lane0 + sel_row)).astype(BF16)
        pen_t = lax.dot_general(pen16, place, (((0,), (0,)), ((), ())),
                                preferred_element_type=F32)
        q_scaled = (q_ref[...].astype(F32) * scale).astype(BF16)
        qa_ref[hh] = jnp.where(mine, q_scaled, pen_t.astype(BF16))
        onehot = (lane_full == pen_lane0 + row_blk).astype(BF16)
        ka_ref[hh] = jnp.where(mine, k_ref[...], onehot)

    va_ref[:, 0:LANES] = v_ref[...]
    va_ref[:, LANES:2 * LANES] = jnp.ones((seq, LANES), BF16)

    half = blk // 2

    def pass1(own, hh):
        q = qa_ref[hh, own * blk:(own + 1) * blk, :]
        far_bias = far_ref[2 * hp + hh]
        mx = {}
        for n in range(own + 1):
            s = lax.dot_general(q, ka_ref[hh, n * blk:(n + 1) * blk, :], _NT,
                                preferred_element_type=F32)
            is_far = n < own - 1
            if n == own:
                s = s + d_ref[hh, 0]
            elif n == own - 1:
                s = s + d_ref[hh, 1]
            s_ref[hh, n] = s
            folded = jnp.maximum(s[:, :half], s[:, half:])
            mx[is_far] = jnp.maximum(mx[is_far], folded) if is_far in mx else folded
        mx_all = jnp.maximum(mx[False], mx[True] + far_bias) if True in mx else mx[False]
        m = jnp.max(mx_all, axis=1, keepdims=True)
        return m, m - far_bias

    def pass2(own, hh, m, m_far):
        for n in range(own + 1):
            p = jnp.exp(s_ref[hh, n] - (m_far if n < own - 1 else m))
            p_ref[hh, :, n * blk:(n + 1) * blk] = p.astype(BF16)
        keys = (own + 1) * blk
        res = _dot(p_ref[hh, :, 0:keys], va_ref[0:keys, :])
        return res[:, 0:LANES] / res[:, LANES:2 * LANES]

    items = [(own, hh) for own in range(nb) for hh in range(2)]
    stats = pass1(*items[0])
    outs = []
    for i, (own, hh) in enumerate(items):
        next_stats = pass1(*items[i + 1]) if i + 1 < len(items) else None
        outs.append(pass2(own, hh, *stats))
        stats = next_stats
        if hh == 1:
            o_ref[own * blk:(own + 1) * blk, :] = jnp.where(head0, *outs).astype(o_ref.dtype)
            outs = []


def _moba(za, bias_tiles, far_bias, batch, seq):
    n_pairs = ATTN_HEADS // 2
    assert seq % MOBA_BLOCK == 0 and seq // MOBA_BLOCK <= MAX_BLOCKS
    return pl.pallas_call(
        functools.partial(_moba_kernel, seq=seq),
        out_shape=jax.ShapeDtypeStruct((batch * seq, ATTN_DIM), BF16),
        grid=(n_pairs, batch),
        in_specs=[pl.BlockSpec(memory_space=pltpu.SMEM),
                  pl.BlockSpec((seq, LANES), lambda hp, b: (b, hp)),
                  pl.BlockSpec((seq, LANES), lambda hp, b: (b, n_pairs + hp)),
                  pl.BlockSpec((seq, LANES), lambda hp, b: (b, 2 * n_pairs + hp)),
                  pl.BlockSpec((2, 2, MOBA_BLOCK, MOBA_BLOCK), lambda hp, b: (hp, 0, 0, 0))],
        out_specs=pl.BlockSpec((seq, LANES), lambda hp, b: (b, hp)),
        scratch_shapes=[pltpu.VMEM((2, seq, LANES), BF16),
                        pltpu.VMEM((2, seq, LANES), BF16),
                        pltpu.VMEM((2, seq // MOBA_BLOCK, MOBA_BLOCK, MOBA_BLOCK), F32),
                        pltpu.VMEM((seq, 2 * LANES), BF16),
                        pltpu.VMEM((2, MOBA_BLOCK, seq), BF16)],
        compiler_params=pltpu.CompilerParams(
            dimension_semantics=("arbitrary", "arbitrary"), vmem_limit_bytes=VMEM_LIMIT),
        name="moba",
    )(far_bias, za, za, za, bias_tiles)


def _out_ffn_kernel(x_ref, ym_ref, yc_ref, ya_ref, wo_ref, g2_ref, wup_ref, cw_ref, cb_ref,
                    wdn_ref, fg_ref, o_ref, hn_ref, acc_ref, act_ref, tail_ref,
                    *, tiles_per_seq, final_norm):
    tm = TM_FFN
    first = (pl.program_id(0) % tiles_per_seq) == 0
    y = jnp.concatenate([ym_ref[...], yc_ref[...], ya_ref[...]], axis=1)
    x1 = x_ref[...] + _dot(y, wo_ref[...])
    hn_ref[...] = _rms(x1, g2_ref[...]).astype(BF16)
    acc_ref[...] = x1
    row8 = lax.broadcasted_iota(jnp.int32, (SUBLANES, FFN_CHUNK), 0)

    def up_proj(c):
        return [_dot(hn_ref[...], wup_ref[:, part * D_FFP + c * FFN_CHUNK:
                                          part * D_FFP + (c + 1) * FFN_CHUNK]) for part in range(2)]

    u_next = up_proj(0)
    for c in range(N_FFN_CHUNKS):
        u_pair = u_next
        if c + 1 < N_FFN_CHUNKS:
            u_next = up_proj(c + 1)
        conv = []
        for part in range(2):
            c0 = part * D_FFP + c * FFN_CHUNK
            u = u_pair[part]
            tail = jnp.where(first, 0.0, tail_ref[2 * c + part])
            tail_ref[2 * c + part] = u[tm - SUBLANES:tm, :]
            acc = cb_ref[:, c0:c0 + FFN_CHUNK] + cw_ref[FFN_CONV - 1:FFN_CONV, c0:c0 + FFN_CHUNK] * u
            for d in range(1, FFN_CONV):
                shifted = pltpu.roll(u, d, axis=0)
                top = jnp.where(row8 < d, pltpu.roll(tail, d, axis=0), shifted[0:SUBLANES])
                shifted = jnp.concatenate([top, shifted[SUBLANES:]], axis=0)
                j = FFN_CONV - 1 - d
                acc = acc + cw_ref[j:j + 1, c0:c0 + FFN_CHUNK] * shifted
            conv.append(acc)
        act_ref[:, c * FFN_CHUNK:(c + 1) * FFN_CHUNK] = (
            conv[0] * _sigmoid(conv[0]) * conv[1]).astype(BF16)
        if c + 1 in FFN_DOWN_ENDS:
            g = FFN_DOWN_ENDS.index(c + 1)
            k0 = (FFN_DOWN_ENDS[g - 1] if g else 0) * FFN_CHUNK
            k1 = (c + 1) * FFN_CHUNK
            acc_ref[...] += _dot(act_ref[:, k0:k1], wdn_ref[k0:k1, :])

    out = acc_ref[...]
    if final_norm:
        out = _rms(out, fg_ref[...])
    o_ref[...] = out


def _out_ffn(x2d, ym, yc, ya, wo, g2, wup, cw, cb, wdn, fg, layer, seq, final_norm):
    n = x2d.shape[0]
    tiles_per_seq = seq // TM_FFN

    def const(shape):
        return _layer_spec(shape, layer, pipeline_mode=pl.Buffered(1))

    return pl.pallas_call(
        functools.partial(_out_ffn_kernel, tiles_per_seq=tiles_per_seq, final_norm=final_norm),
        out_shape=jax.ShapeDtypeStruct((n, D_MODEL), F32),
        grid=(n // TM_FFN,),
        in_specs=[pl.BlockSpec((TM_FFN, D_MODEL), lambda i: (i, 0)),
                  pl.BlockSpec((TM_FFN, MLSTM_DIM), lambda i: (i, 0)),
                  pl.BlockSpec((TM_FFN, CONV_DIM), lambda i: (i, 0)),
                  pl.BlockSpec((TM_FFN, ATTN_DIM), lambda i: (i, 0)),
                  const((D_MODEL, D_MODEL)),
                  const((1, D_MODEL)),
                  const((D_MODEL, 2 * D_FFP)),
                  const((FFN_CONV, 2 * D_FFP)),
                  const((1, 2 * D_FFP)),
                  const((D_FFP, D_MODEL)),
                  _const_spec((1, D_MODEL))],
        out_specs=pl.BlockSpec((TM_FFN, D_MODEL), lambda i: (i, 0)),
        scratch_shapes=[pltpu.VMEM((TM_FFN, D_MODEL), BF16),
                        pltpu.VMEM((TM_FFN, D_MODEL), F32),
                        pltpu.VMEM((TM_FFN, D_FFP), BF16),
                        pltpu.VMEM((2 * N_FFN_CHUNKS, SUBLANES, FFN_CHUNK), F32)],
        compiler_params=pltpu.CompilerParams(
            dimension_semantics=("arbitrary",), vmem_limit_bytes=VMEM_LIMIT),
        name="out_ffn",
    )(x2d, ym, yc, ya, wo, g2, wup, cw, cb, wdn, fg)


def _pad_last(a, width):
    return jnp.pad(a, [(0, 0)] * (a.ndim - 1) + [(0, width - a.shape[-1])])


def _row(a):
    return a[:, None, :]


def _prep_w_in(w_in):
    m_end = 4 * MLSTM_DIM
    g_end = m_end + 2 * MLSTM_HEADS
    return jnp.concatenate([w_in[..., :m_end], _pad_last(w_in[..., m_end:g_end], GATE_PAD),
                            w_in[..., g_end:]], axis=-1).astype(BF16)


def _prep_ffn_cols(a):
    return jnp.concatenate([_pad_last(a[..., :D_FF], D_FFP), _pad_last(a[..., D_FF:], D_FFP)],
                           axis=-1)


def _prep_w_down(w_down):
    return jnp.pad(w_down, ((0, 0), (0, D_FFP - D_FF), (0, 0))).astype(BF16)


def _prep_gate_bias(ig_b, fg_b):
    return _row(_pad_last(jnp.concatenate([ig_b, fg_b], axis=-1), GATE_PAD))


def kernel(x, norm1_g, w_in, mlstm_qk_conv_w, mlstm_qk_conv_b, mlstm_ig_b, mlstm_fg_b, mlstm_head_g, conf_dw_w, conf_dw_b, conf_ln_g, conf_ln_b, rel_bias, w_out, norm2_g, ffn_w_up, ffn_conv_w, ffn_conv_b, ffn_w_down, final_g):
    batch, seq, _ = x.shape
    bias_tiles = _bias_tiles(rel_bias)
    far_bias = rel_bias[REL_BUCKETS - 1]
    w_in_r = _prep_w_in(w_in)
    gate_bias = _prep_gate_bias(mlstm_ig_b, mlstm_fg_b)
    w_out_b = w_out.astype(BF16)
    w_up_r = _prep_ffn_cols(ffn_w_up).astype(BF16)
    f_w = _prep_ffn_cols(ffn_conv_w)
    f_b = _row(_prep_ffn_cols(ffn_conv_b))
    w_down_r = _prep_w_down(ffn_w_down)
    x2d = x.reshape(batch * seq, D_MODEL)
    for l in range(DEPTH):
        zm, za, yc = _in_proj(x2d, _row(norm1_g), w_in_r, conf_dw_w, _row(conf_dw_b),
                              _row(conf_ln_g), _row(conf_ln_b), l, seq)
        ym = _mlstm(zm, mlstm_qk_conv_w, _row(mlstm_qk_conv_b), gate_bias, _row(mlstm_head_g),
                    l, batch, seq)
        ya = _moba(za, bias_tiles, far_bias, batch, seq)
        x2d = _out_ffn(x2d, ym, yc, ya, w_out_b, _row(norm2_g), w_up_r, f_w, f_b, w_down_r,
                       final_g[None, :], l, seq, l == DEPTH - 1)
    return x2d.reshape(batch, seq, D_MODEL)
```

```python
import functools
import math

import numpy as np
import jax
import jax.numpy as jnp
from jax import lax
from jax.experimental import pallas as pl
from jax.experimental.pallas import tpu as pltpu

F32 = jnp.float32
BF16 = jnp.bfloat16
HIGHEST = lax.Precision.HIGHEST

D_MODEL = 1024
DEPTH = 2
HEAD_DIM = 64
MLSTM_DIM = 256
CONV_DIM = 256
ATTN_DIM = 512
MLSTM_HEADS = MLSTM_DIM // HEAD_DIM
ATTN_HEADS = ATTN_DIM // HEAD_DIM
QK_CONV = 4
CONF_KERNEL = 31
MOBA_BLOCK = 256
MOBA_TOPK = 3
REL_BUCKETS = 32
REL_MAX_DIST = 128
D_FF = 2752
FFN_CONV = 3
EPS = 1e-6
NEG = -1e30

LANES = 128
SUBLANES = 8
VMEM_LIMIT = 56 * 1024 * 1024

GATE_PAD = LANES
ZM_W = 4 * MLSTM_DIM + GATE_PAD
ZC_W = 2 * CONV_DIM
ZA_W = 3 * ATTN_DIM
P_W = ZM_W + ZC_W + ZA_W
D_FFP = -(-D_FF // 256) * 256
FFN_CHUNK = 256
N_FFN_CHUNKS = D_FFP // FFN_CHUNK
FFN_DOWN_ENDS = (4, 8, N_FFN_CHUNKS - 1, N_FFN_CHUNKS)

TM_IN = 512
TM_FFN = 512
MLSTM_L = 256
MLSTM_STATE_ROWS = 2 * HEAD_DIM + 16
CONF_ROWS = 64
CONF_PAD = 32
MAX_BLOCKS = SUBLANES
PEN_BIG = 2.0 ** 100

_NT = (((1,), (1,)), ((), ()))


def _t5_saturation_distance():
    n = np.arange(1, 4 * MOBA_BLOCK, dtype=np.float32)
    max_exact = REL_BUCKETS // 2
    large = max_exact + (np.log(n / max_exact) / math.log(REL_MAX_DIST / max_exact)
                         * (REL_BUCKETS - max_exact)).astype(np.int32)
    bucket = np.where(n < max_exact, n.astype(np.int32), np.minimum(large, REL_BUCKETS - 1))
    not_last = np.nonzero(bucket != REL_BUCKETS - 1)[0]
    return int(n[not_last[-1]]) + 1


assert _t5_saturation_distance() <= MOBA_BLOCK + 1


def _sigmoid(x):
    return 1.0 / (1.0 + jnp.exp(-x))


def _log_sigmoid(x):
    return jnp.minimum(x, 0.0) - jnp.log1p(jnp.exp(-jnp.abs(x)))


def _rms(xf, g):
    return xf * lax.rsqrt(jnp.mean(xf * xf, axis=-1, keepdims=True) + EPS) * g


def _dot(a, b, **kw):
    return jnp.dot(a, b, preferred_element_type=F32, **kw)


def _const_spec(shape):
    nd = len(shape)
    return pl.BlockSpec(shape, lambda *_: (0,) * nd)


def _layer_spec(shape, layer, **kw):
    nd = len(shape)
    return pl.BlockSpec((None,) + tuple(shape), lambda *_: (layer,) + (0,) * nd, **kw)


def _conformer_tile(win, w_ref, bias, ln_g, ln_b):
    win_rows = CONF_ROWS + CONF_PAD
    acc = jnp.broadcast_to(bias, (CONF_ROWS, CONV_DIM))
    for r in range(SUBLANES):
        rolled = win if r == 0 else pltpu.roll(win, win_rows - r, axis=0)
        for j in range(CONF_KERNEL):
            off = CONF_PAD - (CONF_KERNEL - 1) + j
            if off % SUBLANES == r:
                base = off - r
                acc = acc + w_ref[j:j + 1, :] * rolled[base:base + CONF_ROWS]
    mu = jnp.mean(acc, axis=-1, keepdims=True)
    d = acc - mu
    var = jnp.mean(d * d, axis=-1, keepdims=True)
    y = d * lax.rsqrt(var + EPS) * ln_g + ln_b
    return y * _sigmoid(y)


def _in_proj_kernel(x_ref, g_ref, w_ref, cw_ref, cb_ref, lg_ref, lb_ref, zm_ref, za_ref, yc_ref,
                    up_ref, *, tiles_per_seq):
    tm = TM_IN
    first = (pl.program_id(0) % tiles_per_seq) == 0
    h = _rms(x_ref[...], g_ref[...]).astype(BF16)
    zc = _dot(h, w_ref[:, ZM_W:ZM_W + ZC_W])
    up_ref[0:CONF_PAD, :] = jnp.where(first, 0.0, up_ref[tm:tm + CONF_PAD, :])
    up_ref[CONF_PAD:CONF_PAD + tm, :] = zc[:, 0:CONV_DIM] * _sigmoid(zc[:, CONV_DIM:ZC_W])
    zm_ref[...] = _dot(h, w_ref[:, 0:ZM_W])
    za_ref[...] = _dot(h, w_ref[:, ZM_W + ZC_W:P_W]).astype(BF16)
    bias, ln_g, ln_b = cb_ref[...], lg_ref[...], lb_ref[...]
    for t in range(tm // CONF_ROWS):
        r0 = t * CONF_ROWS
        win = up_ref[r0:r0 + CONF_ROWS + CONF_PAD, :]
        yc_ref[r0:r0 + CONF_ROWS, :] = _conformer_tile(win, cw_ref, bias, ln_g, ln_b).astype(yc_ref.dtype)


def _in_proj(x2d, g, w_r, conf_w, conf_b, conf_ln_g, conf_ln_b, layer, seq):
    n = x2d.shape[0]
    return pl.pallas_call(
        functools.partial(_in_proj_kernel, tiles_per_seq=seq // TM_IN),
        out_shape=(jax.ShapeDtypeStruct((n, ZM_W), F32),
                   jax.ShapeDtypeStruct((n, ZA_W), BF16),
                   jax.ShapeDtypeStruct((n, CONV_DIM), BF16)),
        grid=(n // TM_IN,),
        in_specs=[pl.BlockSpec((TM_IN, D_MODEL), lambda i: (i, 0)),
                  _layer_spec((1, D_MODEL), layer),
                  _layer_spec((D_MODEL, P_W), layer),
                  _layer_spec((CONF_KERNEL, CONV_DIM), layer),
                  _layer_spec((1, CONV_DIM), layer),
                  _layer_spec((1, CONV_DIM), layer),
                  _layer_spec((1, CONV_DIM), layer)],
        out_specs=(pl.BlockSpec((TM_IN, ZM_W), lambda i: (i, 0)),
                   pl.BlockSpec((TM_IN, ZA_W), lambda i: (i, 0)),
                   pl.BlockSpec((TM_IN, CONV_DIM), lambda i: (i, 0))),
        scratch_shapes=[pltpu.VMEM((CONF_PAD + TM_IN, CONV_DIM), F32)],
        compiler_params=pltpu.CompilerParams(
            dimension_semantics=("arbitrary",), vmem_limit_bytes=VMEM_LIMIT),
        name="in_proj",
    )(x2d, g, w_r, conf_w, conf_b, conf_ln_g, conf_ln_b)


def _mlstm_kernel(zm_ref, cw_ref, cb_ref, gb_ref, hg_ref, o_ref, g_ref, gt_ref, *, seq):
    L = MLSTM_L
    n_chunks = seq // L
    nh = MLSTM_HEADS
    scale = HEAD_DIM ** -0.5
    lane = lax.broadcasted_iota(jnp.int32, (1, LANES), 1)
    head0 = lane < HEAD_DIM

    gates = zm_ref[:, 4 * MLSTM_DIM:ZM_W] + gb_ref[...]
    is_f = (lane >= nh) & (lane < 2 * nh)
    gates = jnp.where(is_f, _log_sigmoid(gates), gates)
    g_ref[...] = gates
    gt_ref[...] = gates.T[0:SUBLANES, :]

    ri = lax.broadcasted_iota(jnp.int32, (L, L), 0)
    ci = lax.broadcasted_iota(jnp.int32, (L, L), 1)
    visible = ri <= ci
    tri = (ci <= ri).astype(F32)
    tri_t = visible.astype(F32)
    srow = lax.broadcasted_iota(jnp.int32, (MLSTM_STATE_ROWS, LANES), 0)
    scol = lax.broadcasted_iota(jnp.int32, (MLSTM_STATE_ROWS, LANES), 1)
    srow_head = jnp.where(srow < LANES, srow // HEAD_DIM, srow - LANES)
    state_mask = (scol // HEAD_DIM) == srow_head
    row_is_head0 = srow_head[:, 0:1] == 0
    vrow_is_head0 = lax.broadcasted_iota(jnp.int32, (LANES, 1), 0) < HEAD_DIM
    extra_row = lax.broadcasted_iota(jnp.int32, (MLSTM_STATE_ROWS - LANES, 1), 0)
    ones_rows = jnp.ones((MLSTM_STATE_ROWS - LANES, L), BF16)
    cb = cb_ref[...]
    hg = hg_ref[...]

    def chunk(c, carry):
        states, m_prev = carry
        r0 = pl.multiple_of(c * L, L)
        gc = g_ref[pl.ds(r0, L), :]
        gtc = gt_ref[:, pl.ds(r0, L)]
        bcols = _dot(tri, gc, precision=HIGHEST)
        brows = _dot(gtc, tri_t, precision=HIGHEST)
        e_rows = gtc[0:nh] - brows[nh:2 * nh]

        xcur = zm_ref[pl.ds(r0, L), 0:2 * MLSTM_DIM]
        pr = pl.multiple_of(jnp.maximum(r0 - SUBLANES, 0), SUBLANES)
        xprev = jnp.where(c > 0, zm_ref[pl.ds(pr, SUBLANES), 0:2 * MLSTM_DIM], 0.0)
        xcat = jnp.concatenate([xprev, xcur], axis=0)
        y = cb
        for j in range(QK_CONV):
            off = SUBLANES - (QK_CONV - 1) + j
            y = y + cw_ref[j:j + 1, :] * xcat[off:off + L]
        qk = y * _sigmoid(y)

        new_states = []
        new_m = []
        for p in range(2):
            lo = p * LANES
            q_t = (qk[:, lo:lo + LANES] * scale).T.astype(BF16)
            k_b = qk[:, MLSTM_DIM + lo:MLSTM_DIM + lo + LANES].astype(BF16)
            v_t = zm_ref[pl.ds(r0, L), 2 * MLSTM_DIM + lo:2 * MLSTM_DIM + lo + LANES].T
            v_tb = v_t.astype(BF16)
            inter = _dot(states[p].astype(BF16), q_t)
            hn_t, wks, decays = [], [], []
            for hh in range(2):
                h = 2 * p + hh
                mp = m_prev[h]
                e_col = gc[:, h:h + 1] - bcols[:, nh + h:nh + h + 1]
                e_row = e_rows[h:h + 1]
                b_row = brows[nh + h:nh + h + 1]
                em = jnp.where(visible, e_col, NEG)
                g = jnp.maximum(mp, jnp.max(em, axis=0, keepdims=True))
                w_t = jnp.exp(em - g)
                k_h = jnp.where(head0 if hh == 0 else ~head0, k_b, jnp.zeros_like(k_b))
                s_t = _dot(k_h, q_t)
                lhs = jnp.concatenate(
                    [v_tb[hh * HEAD_DIM:(hh + 1) * HEAD_DIM], ones_rows], axis=0)
                r = _dot(lhs, (s_t * w_t).astype(BF16))
                a = jnp.exp(mp - g)
                num = a * inter[hh * HEAD_DIM:(hh + 1) * HEAD_DIM] + r[0:HEAD_DIM]
                den = a * inter[LANES + hh:LANES + hh + 1] + r[HEAD_DIM:HEAD_DIM + 1]
                hv = num * (1.0 / jnp.maximum(jnp.abs(den), jnp.exp(-(b_row + g))))
                mu = jnp.mean(hv, axis=0, keepdims=True)
                d = hv - mu
                var = jnp.mean(d * d, axis=0, keepdims=True)
                hn_t.append(d * lax.rsqrt(var + EPS))
                g_last = jnp.maximum(mp, jnp.max(e_row, axis=1, keepdims=True))
                wks.append(jnp.exp(e_row - g_last))
                decays.append(jnp.exp(mp - g_last))
                new_m.append(b_row[:, L - 1:L] + g_last)
            vw = v_t * jnp.where(vrow_is_head0, wks[0], wks[1])
            extra = jnp.where(extra_row == 0, wks[0], jnp.where(extra_row == 1, wks[1], 0.0))
            upd = _dot(jnp.concatenate([vw, extra], axis=0).astype(BF16), k_b)
            decay_rows = jnp.where(row_is_head0, decays[0], decays[1])
            new_states.append(decay_rows * states[p] + jnp.where(state_mask, upd, 0.0))

            hn = jnp.concatenate(hn_t, axis=0).T
            og = zm_ref[pl.ds(r0, L), 3 * MLSTM_DIM + lo:3 * MLSTM_DIM + lo + LANES]
            o_ref[pl.ds(r0, L), lo:lo + LANES] = (hn * hg[:, lo:lo + LANES] * _sigmoid(og)).astype(o_ref.dtype)
        return tuple(new_states), tuple(new_m)

    init = (tuple(jnp.zeros((MLSTM_STATE_ROWS, LANES), F32) for _ in range(2)),
            tuple(jnp.full((1, 1), NEG, F32) for _ in range(nh)))
    lax.fori_loop(0, n_chunks, chunk, init, unroll=True)


def _mlstm(zm, cw, cb, gate_bias, head_g, layer, batch, seq):
    return pl.pallas_call(
        functools.partial(_mlstm_kernel, seq=seq),
        out_shape=jax.ShapeDtypeStruct((batch * seq, MLSTM_DIM), BF16),
        grid=(batch,),
        in_specs=[pl.BlockSpec((seq, ZM_W), lambda b: (b, 0)),
                  _layer_spec((QK_CONV, 2 * MLSTM_DIM), layer),
                  _layer_spec((1, 2 * MLSTM_DIM), layer),
                  _layer_spec((1, GATE_PAD), layer),
                  _layer_spec((1, MLSTM_DIM), layer)],
        out_specs=pl.BlockSpec((seq, MLSTM_DIM), lambda b: (b, 0)),
        scratch_shapes=[pltpu.VMEM((seq, GATE_PAD), F32),
                        pltpu.VMEM((SUBLANES, seq), F32)],
        compiler_params=pltpu.CompilerParams(
            dimension_semantics=("arbitrary",), vmem_limit_bytes=VMEM_LIMIT),
        name="mlstm",
    )(zm, cw, cb, gate_bias, head_g)


def _bias_tiles_kernel(rb_ref, o_ref):
    h = pl.program_id(0)
    i = lax.broadcasted_iota(jnp.int32, (MOBA_BLOCK, MOBA_BLOCK), 0)
    j = lax.broadcasted_iota(jnp.int32, (MOBA_BLOCK, MOBA_BLOCK), 1)
    max_exact = REL_BUCKETS // 2
    for t in range(2):
        dist = i - j + t * MOBA_BLOCK
        n = jnp.maximum(dist, 0)
        nf = jnp.maximum(n, 1).astype(F32)
        large = max_exact + (jnp.log(nf / max_exact) / math.log(REL_MAX_DIST / max_exact)
                             * (REL_BUCKETS - max_exact)).astype(jnp.int32)
        large = jnp.minimum(large, REL_BUCKETS - 1)
        bucket = jnp.where(n < max_exact, n, large)
        bias = jnp.zeros((MOBA_BLOCK, MOBA_BLOCK), F32)
        for bk in range(REL_BUCKETS):
            bias = jnp.where(bucket == bk, rb_ref[bk, h], bias)
        if t == 0:
            bias = jnp.where(dist >= 0, bias, NEG)
        o_ref[0, t] = bias


def _bias_tiles(rel_bias):
    return pl.pallas_call(
        _bias_tiles_kernel,
        out_shape=jax.ShapeDtypeStruct((ATTN_HEADS, 2, MOBA_BLOCK, MOBA_BLOCK), F32),
        grid=(ATTN_HEADS,),
        in_specs=[pl.BlockSpec(memory_space=pltpu.SMEM)],
        out_specs=pl.BlockSpec((1, 2, MOBA_BLOCK, MOBA_BLOCK), lambda h: (h, 0, 0, 0)),
        compiler_params=pltpu.CompilerParams(dimension_semantics=("arbitrary",)),
        name="bias_tiles",
    )(rel_bias)


def _moba_kernel(far_ref, q_ref, k_ref, v_ref, d_ref, o_ref, qa_ref, ka_ref, s_ref, va_ref,
                 p_ref, *, seq):
    hp = pl.program_id(0)
    blk = MOBA_BLOCK
    nb = seq // blk
    scale = HEAD_DIM ** -0.5
    lane = lax.broadcasted_iota(jnp.int32, (1, LANES), 1)
    head0 = lane < HEAD_DIM

    blk_i = lax.broadcasted_iota(jnp.int32, (MAX_BLOCKS, seq), 0)
    own_i = lax.broadcasted_iota(jnp.int32, (MAX_BLOCKS, seq), 1) // blk
    row_blk = lax.broadcasted_iota(jnp.int32, (seq, LANES), 0) // blk
    lane_full = lax.broadcasted_iota(jnp.int32, (seq, LANES), 1)

    avg = jnp.where(blk_i == own_i, 1.0 / blk, 0.0).astype(BF16)
    kmean = _dot(avg, k_ref[...])
    km_hi = kmean.astype(BF16).astype(F32)
    km_mid = (kmean - km_hi).astype(BF16).astype(F32)
    km_lo = kmean - km_hi - km_mid
    gate_lhs = jnp.concatenate(
        [jnp.where(head0 if hh == 0 else ~head0, part, 0.0)
         for hh in range(2) for part in (km_hi, km_mid, km_lo)], axis=0).astype(BF16)
    gates = lax.dot_general(gate_lhs, q_ref[...], _NT, preferred_element_type=F32)
    sel_row = lax.broadcasted_iota(jnp.int32, (2 * MAX_BLOCKS, LANES), 0)
    sel_lane = lax.broadcasted_iota(jnp.int32, (2 * MAX_BLOCKS, LANES), 1)

    for hh in range(2):
        mine = head0 if hh == 0 else ~head0
        g0 = 3 * MAX_BLOCKS * hh
        gate = (gates[g0:g0 + MAX_BLOCKS] + gates[g0 + MAX_BLOCKS:g0 + 2 * MAX_BLOCKS]
                + gates[g0 + 2 * MAX_BLOCKS:g0 + 3 * MAX_BLOCKS])
        cand = blk_i < own_i
        g = jnp.where(cand, gate, NEG)
        rank = jnp.zeros((MAX_BLOCKS, seq), jnp.int32)
        for m in range(MAX_BLOCKS):
            gm = g[m:m + 1, :]
            beats = (gm > g) | ((gm == g) & (blk_i > m))
            rank = rank + beats.astype(jnp.int32)
        keep = (cand & (rank < MOBA_TOPK)) | (blk_i >= own_i)
        pen = jnp.where(keep, 0.0, -PEN_BIG)
        pen_lane0 = (1 - hh) * HEAD_DIM
        pen16 = jnp.concatenate([pen, jnp.zeros_like(pen)], axis=0).astype(BF16)
        place = ((sel_row < MAX_BLOCKS) & (sel_lane == pen_lane0 + sel_row)).astype(BF16)
        pen_t = lax.dot_general(pen16, place, (((0,), (0,)), ((), ())),
                                preferred_element_type=F32)
        q_scaled = (q_ref[...].astype(F32) * scale).astype(BF16)
        qa_ref[hh] = jnp.where(mine, q_scaled, pen_t.astype(BF16))
        onehot = (lane_full == pen_lane0 + row_blk).astype(BF16)
        ka_ref[hh] = jnp.where(mine, k_ref[...], onehot)

    va_ref[:, 0:LANES] = v_ref[...]
    va_ref[:, LANES:2 * LANES] = jnp.ones((seq, LANES), BF16)

    half = blk // 2

    def pass1(own, hh):
        q = qa_ref[hh, own * blk:(own + 1) * blk, :]
        far_bias = far_ref[2 * hp + hh]
        mx = {}
        for n in range(own + 1):
            s = lax.dot_general(q, ka_ref[hh, n * blk:(n + 1) * blk, :], _NT,
                                preferred_element_type=F32)
            is_far = n < own - 1
            if n == own:
                s = s + d_ref[hh, 0]
            elif n == own - 1:
                s = s + d_ref[hh, 1]
            s_ref[hh, n] = s
            folded = jnp.maximum(s[:, :half], s[:, half:])
            mx[is_far] = jnp.maximum(mx[is_far], folded) if is_far in mx else folded
        mx_all = jnp.maximum(mx[False], mx[True] + far_bias) if True in mx else mx[False]
        m = jnp.max(mx_all, axis=1, keepdims=True)
        return m, m - far_bias

    def pass2(own, hh, m, m_far):
        for n in range(own + 1):
            p = jnp.exp(s_ref[hh, n] - (m_far if n < own - 1 else m))
            p_ref[hh, :, n * blk:(n + 1) * blk] = p.astype(BF16)
        keys = (own + 1) * blk
        res = _dot(p_ref[hh, :, 0:keys], va_ref[0:keys, :])
        return res[:, 0:LANES] / res[:, LANES:2 * LANES]

    items = [(own, hh) for own in range(nb) for hh in range(2)]
    stats = pass1(*items[0])
    outs = []
    for i, (own, hh) in enumerate(items):
        next_stats = pass1(*items[i + 1]) if i + 1 < len(items) else None
        outs.append(pass2(own, hh, *stats))
        stats = next_stats
        if hh == 1:
            o_ref[own * blk:(own + 1) * blk, :] = jnp.where(head0, *outs).astype(o_ref.dtype)
            outs = []


def _moba(za, bias_tiles, far_bias, batch, seq):
    n_pairs = ATTN_HEADS // 2
    assert seq % MOBA_BLOCK == 0 and seq // MOBA_BLOCK <= MAX_BLOCKS
    return pl.pallas_call(
        functools.partial(_moba_kernel, seq=seq),
        out_shape=jax.ShapeDtypeStruct((batch * seq, ATTN_DIM), BF16),
        grid=(n_pairs, batch),
        in_specs=[pl.BlockSpec(memory_space=pltpu.SMEM),
                  pl.BlockSpec((seq, LANES), lambda hp, b: (b, hp)),
                  pl.BlockSpec((seq, LANES), lambda hp, b: (b, n_pairs + hp)),
                  pl.BlockSpec((seq, LANES), lambda hp, b: (b, 2 * n_pairs + hp)),
                  pl.BlockSpec((2, 2, MOBA_BLOCK, MOBA_BLOCK), lambda hp, b: (hp, 0, 0, 0))],
        out_specs=pl.BlockSpec((seq, LANES), lambda hp, b: (b, hp)),
        scratch_shapes=[pltpu.VMEM((2, seq, LANES), BF16),
                        pltpu.VMEM((2, seq, LANES), BF16),
                        pltpu.VMEM((2, seq // MOBA_BLOCK, MOBA_BLOCK, MOBA_BLOCK), F32),
                        pltpu.VMEM((seq, 2 * LANES), BF16),
                        pltpu.VMEM((2, MOBA_BLOCK, seq), BF16)],
        compiler_params=pltpu.CompilerParams(
            dimension_semantics=("arbitrary", "arbitrary"), vmem_limit_bytes=VMEM_LIMIT),
        name="moba",
    )(far_bias, za, za, za, bias_tiles)


def _out_ffn_kernel(x_ref, ym_ref, yc_ref, ya_ref, wo_ref, g2_ref, wup_ref, cw_ref, cb_ref,
                    wdn_ref, fg_ref, o_ref, hn_ref, acc_ref, act_ref, tail_ref,
                    *, tiles_per_seq, final_norm):
    tm = TM_FFN
    first = (pl.program_id(0) % tiles_per_seq) == 0
    y = jnp.concatenate([ym_ref[...], yc_ref[...], ya_ref[...]], axis=1)
    x1 = x_ref[...] + _dot(y, wo_ref[...])
    hn_ref[...] = _rms(x1, g2_ref[...]).astype(BF16)
    acc_ref[...] = x1
    row8 = lax.broadcasted_iota(jnp.int32, (SUBLANES, FFN_CHUNK), 0)

    def up_proj(c):
        return [_dot(hn_ref[...], wup_ref[:, part * D_FFP + c * FFN_CHUNK:
                                          part * D_FFP + (c + 1) * FFN_CHUNK]) for part in range(2)]

    u_next = up_proj(0)
    for c in range(N_FFN_CHUNKS):
        u_pair = u_next
        if c + 1 < N_FFN_CHUNKS:
            u_next = up_proj(c + 1)
        conv = []
        for part in range(2):
            c0 = part * D_FFP + c * FFN_CHUNK
            u = u_pair[part]
            tail = jnp.where(first, 0.0, tail_ref[2 * c + part])
            tail_ref[2 * c + part] = u[tm - SUBLANES:tm, :]
            acc = cb_ref[:, c0:c0 + FFN_CHUNK] + cw_ref[FFN_CONV - 1:FFN_CONV, c0:c0 + FFN_CHUNK] * u
            for d in range(1, FFN_CONV):
                shifted = pltpu.roll(u, d, axis=0)
                top = jnp.where(row8 < d, pltpu.roll(tail, d, axis=0), shifted[0:SUBLANES])
                shifted = jnp.concatenate([top, shifted[SUBLANES:]], axis=0)
                j = FFN_CONV - 1 - d
                acc = acc + cw_ref[j:j + 1, c0:c0 + FFN_CHUNK] * shifted
            conv.append(acc)
        act_ref[:, c * FFN_CHUNK:(c + 1) * FFN_CHUNK] = (
            conv[0] * _sigmoid(conv[0]) * conv[1]).astype(BF16)
        if c + 1 in FFN_DOWN_ENDS:
            g = FFN_DOWN_ENDS.index(c + 1)
            k0 = (FFN_DOWN_ENDS[g - 1] if g else 0) * FFN_CHUNK
            k1 = (c + 1) * FFN_CHUNK
            acc_ref[...] += _dot(act_ref[:, k0:k1], wdn_ref[k0:k1, :])

    out = acc_ref[...]
    if final_norm:
        out = _rms(out, fg_ref[...])
    o_ref[...] = out


def _out_ffn(x2d, ym, yc, ya, wo, g2, wup, cw, cb, wdn, fg, layer, seq, final_norm):
    n = x2d.shape[0]
    tiles_per_seq = seq // TM_FFN

    def const(shape):
        return _layer_spec(shape, layer, pipeline_mode=pl.Buffered(1))

    return pl.pallas_call(
        functools.partial(_out_ffn_kernel, tiles_per_seq=tiles_per_seq, final_norm=final_norm),
        out_shape=jax.ShapeDtypeStruct((n, D_MODEL), F32),
        grid=(n // TM_FFN,),
        in_specs=[pl.BlockSpec((TM_FFN, D_MODEL), lambda i: (i, 0)),
                  pl.BlockSpec((TM_FFN, MLSTM_DIM), lambda i: (i, 0)),
                  pl.BlockSpec((TM_FFN, CONV_DIM), lambda i: (i, 0)),
                  pl.BlockSpec((TM_FFN, ATTN_DIM), lambda i: (i, 0)),
                  const((D_MODEL, D_MODEL)),
                  const((1, D_MODEL)),
                  const((D_MODEL, 2 * D_FFP)),
                  const((FFN_CONV, 2 * D_FFP)),
                  const((1, 2 * D_FFP)),
                  const((D_FFP, D_MODEL)),
                  _const_spec((1, D_MODEL))],
        out_specs=pl.BlockSpec((TM_FFN, D_MODEL), lambda i: (i, 0)),
        scratch_shapes=[pltpu.VMEM((TM_FFN, D_MODEL), BF16),
                        pltpu.VMEM((TM_FFN, D_MODEL), F32),
                        pltpu.VMEM((TM_FFN, D_FFP), BF16),
                        pltpu.VMEM((2 * N_FFN_CHUNKS, SUBLANES, FFN_CHUNK), F32)],
        compiler_params=pltpu.CompilerParams(
            dimension_semantics=("arbitrary",), vmem_limit_bytes=VMEM_LIMIT),
        name="out_ffn",
    )(x2d, ym, yc, ya, wo, g2, wup, cw, cb, wdn, fg)


def _pad_last(a, width):
    return jnp.pad(a, [(0, 0)] * (a.ndim - 1) + [(0, width - a.shape[-1])])


def _row(a):
    return a[:, None, :]


def _prep_w_in(w_in):
    g_end = 4 * MLSTM_DIM + 2 * MLSTM_HEADS
    lead = [(0, 0)] * (w_in.ndim - 1)
    head = jnp.pad(w_in[..., :g_end], lead + [(0, P_W - g_end)])
    tail = jnp.pad(w_in[..., g_end:], lead + [(P_W - (w_in.shape[-1] - g_end), 0)])
    return (head + tail).astype(BF16)


def _prep_ffn_cols(a):
    halves = a.reshape(a.shape[:-1] + (2, D_FF))
    return _pad_last(halves, D_FFP).reshape(a.shape[:-1] + (2 * D_FFP,))


def _prep_w_down(w_down):
    return jnp.pad(w_down, ((0, 0), (0, D_FFP - D_FF), (0, 0))).astype(BF16)


def _prep_gate_bias(ig_b, fg_b):
    return _row(_pad_last(jnp.concatenate([ig_b, fg_b], axis=-1), GATE_PAD))


def kernel(x, norm1_g, w_in, mlstm_qk_conv_w, mlstm_qk_conv_b, mlstm_ig_b, mlstm_fg_b, mlstm_head_g, conf_dw_w, conf_dw_b, conf_ln_g, conf_ln_b, rel_bias, w_out, norm2_g, ffn_w_up, ffn_conv_w, ffn_conv_b, ffn_w_down, final_g):
    batch, seq, _ = x.shape
    bias_tiles = _bias_tiles(rel_bias)
    far_bias = rel_bias[REL_BUCKETS - 1]
    w_in_r = _prep_w_in(w_in)
    gate_bias = _prep_gate_bias(mlstm_ig_b, mlstm_fg_b)
    w_out_b = w_out.astype(BF16)
    w_up_r = _prep_ffn_cols(ffn_w_up).astype(BF16)
    f_w = _prep_ffn_cols(ffn_conv_w)
    f_b = _row(_prep_ffn_cols(ffn_conv_b))
    w_down_r = _prep_w_down(ffn_w_down)
    x2d = x.reshape(batch * seq, D_MODEL)
    for l in range(DEPTH):
        zm, za, yc = _in_proj(x2d, _row(norm1_g), w_in_r, conf_dw_w, _row(conf_dw_b),
                              _row(conf_ln_g), _row(conf_ln_b), l, seq)
        ym = _mlstm(zm, mlstm_qk_conv_w, _row(mlstm_qk_conv_b), gate_bias, _row(mlstm_head_g),
                    l, batch, seq)
        ya = _moba(za, bias_tiles, far_bias, batch, seq)
        x2d = _out_ffn(x2d, ym, yc, ya, w_out_b, _row(norm2_g), w_up_r, f_w, f_b, w_down_r,
                       final_g[None, :], l, seq, l == DEPTH - 1)
    return x2d.reshape(batch, seq, D_MODEL)
```

```python
import functools
import math

import numpy as np
import jax
import jax.numpy as jnp
from jax import lax
from jax.experimental import pallas as pl
from jax.experimental.pallas import tpu as pltpu

F32 = jnp.float32
BF16 = jnp.bfloat16
HIGHEST = lax.Precision.HIGHEST

D_MODEL = 1024
DEPTH = 2
HEAD_DIM = 64
MLSTM_DIM = 256
CONV_DIM = 256
ATTN_DIM = 512
MLSTM_HEADS = MLSTM_DIM // HEAD_DIM
ATTN_HEADS = ATTN_DIM // HEAD_DIM
QK_CONV = 4
CONF_KERNEL = 31
MOBA_BLOCK = 256
MOBA_TOPK = 3
REL_BUCKETS = 32
REL_MAX_DIST = 128
D_FF = 2752
FFN_CONV = 3
EPS = 1e-6
NEG = -1e30

LANES = 128
SUBLANES = 8
VMEM_LIMIT = 56 * 1024 * 1024

GATE_PAD = LANES
ZM_W = 4 * MLSTM_DIM + GATE_PAD
ZC_W = 2 * CONV_DIM
ZA_W = 3 * ATTN_DIM
P_W = ZM_W + ZC_W + ZA_W
D_FFP = -(-D_FF // 256) * 256
FFN_CHUNK = 256
N_FFN_CHUNKS = D_FFP // FFN_CHUNK
FFN_DOWN_ENDS = (4, 8, N_FFN_CHUNKS - 1, N_FFN_CHUNKS)

TM_IN = 512
TM_FFN = 512
PREP_ROWS = 256
MLSTM_L = 256
MLSTM_STATE_ROWS = 2 * HEAD_DIM + 16
CONF_ROWS = 64
CONF_PAD = 32
MAX_BLOCKS = SUBLANES
PEN_BIG = 2.0 ** 100

_NT = (((1,), (1,)), ((), ()))


def _t5_saturation_distance():
    n = np.arange(1, 4 * MOBA_BLOCK, dtype=np.float32)
    max_exact = REL_BUCKETS // 2
    large = max_exact + (np.log(n / max_exact) / math.log(REL_MAX_DIST / max_exact)
                         * (REL_BUCKETS - max_exact)).astype(np.int32)
    bucket = np.where(n < max_exact, n.astype(np.int32), np.minimum(large, REL_BUCKETS - 1))
    not_last = np.nonzero(bucket != REL_BUCKETS - 1)[0]
    return int(n[not_last[-1]]) + 1


assert _t5_saturation_distance() <= MOBA_BLOCK + 1


def _sigmoid(x):
    return 1.0 / (1.0 + jnp.exp(-x))


def _log_sigmoid(x):
    return jnp.minimum(x, 0.0) - jnp.log1p(jnp.exp(-jnp.abs(x)))


def _rms(xf, g):
    return xf * lax.rsqrt(jnp.mean(xf * xf, axis=-1, keepdims=True) + EPS) * g


def _dot(a, b, **kw):
    return jnp.dot(a, b, preferred_element_type=F32, **kw)


def _const_spec(shape):
    nd = len(shape)
    return pl.BlockSpec(shape, lambda *_: (0,) * nd)


def _layer_spec(shape, layer, **kw):
    nd = len(shape)
    return pl.BlockSpec((None,) + tuple(shape), lambda *_: (layer,) + (0,) * nd, **kw)


def _conformer_tile(win, w_ref, bias, ln_g, ln_b):
    win_rows = CONF_ROWS + CONF_PAD
    acc = jnp.broadcast_to(bias, (CONF_ROWS, CONV_DIM))
    for r in range(SUBLANES):
        rolled = win if r == 0 else pltpu.roll(win, win_rows - r, axis=0)
        for j in range(CONF_KERNEL):
            off = CONF_PAD - (CONF_KERNEL - 1) + j
            if off % SUBLANES == r:
                base = off - r
                acc = acc + w_ref[j:j + 1, :] * rolled[base:base + CONF_ROWS]
    mu = jnp.mean(acc, axis=-1, keepdims=True)
    d = acc - mu
    var = jnp.mean(d * d, axis=-1, keepdims=True)
    y = d * lax.rsqrt(var + EPS) * ln_g + ln_b
    return y * _sigmoid(y)


def _in_proj_kernel(x_ref, g_ref, w_ref, cw_ref, cb_ref, lg_ref, lb_ref, zm_ref, za_ref, yc_ref,
                    up_ref, *, tiles_per_seq):
    tm = TM_IN
    first = (pl.program_id(0) % tiles_per_seq) == 0
    h = _rms(x_ref[...], g_ref[...]).astype(BF16)
    zc = _dot(h, w_ref[:, ZM_W:ZM_W + ZC_W])
    up_ref[0:CONF_PAD, :] = jnp.where(first, 0.0, up_ref[tm:tm + CONF_PAD, :])
    up_ref[CONF_PAD:CONF_PAD + tm, :] = zc[:, 0:CONV_DIM] * _sigmoid(zc[:, CONV_DIM:ZC_W])
    zm_ref[...] = _dot(h, w_ref[:, 0:ZM_W])
    za_ref[...] = _dot(h, w_ref[:, ZM_W + ZC_W:P_W]).astype(BF16)
    bias, ln_g, ln_b = cb_ref[...], lg_ref[...], lb_ref[...]
    for t in range(tm // CONF_ROWS):
        r0 = t * CONF_ROWS
        win = up_ref[r0:r0 + CONF_ROWS + CONF_PAD, :]
        yc_ref[r0:r0 + CONF_ROWS, :] = _conformer_tile(win, cw_ref, bias, ln_g, ln_b).astype(yc_ref.dtype)


def _in_proj(x2d, g, w_r, conf_w, conf_b, conf_ln_g, conf_ln_b, layer, seq):
    n = x2d.shape[0]
    return pl.pallas_call(
        functools.partial(_in_proj_kernel, tiles_per_seq=seq // TM_IN),
        out_shape=(jax.ShapeDtypeStruct((n, ZM_W), F32),
                   jax.ShapeDtypeStruct((n, ZA_W), BF16),
                   jax.ShapeDtypeStruct((n, CONV_DIM), BF16)),
        grid=(n // TM_IN,),
        in_specs=[pl.BlockSpec((TM_IN, D_MODEL), lambda i: (i, 0)),
                  _layer_spec((1, D_MODEL), layer),
                  _layer_spec((D_MODEL, P_W), layer),
                  _layer_spec((CONF_KERNEL, CONV_DIM), layer),
                  _layer_spec((1, CONV_DIM), layer),
                  _layer_spec((1, CONV_DIM), layer),
                  _layer_spec((1, CONV_DIM), layer)],
        out_specs=(pl.BlockSpec((TM_IN, ZM_W), lambda i: (i, 0)),
                   pl.BlockSpec((TM_IN, ZA_W), lambda i: (i, 0)),
                   pl.BlockSpec((TM_IN, CONV_DIM), lambda i: (i, 0))),
        scratch_shapes=[pltpu.VMEM((CONF_PAD + TM_IN, CONV_DIM), F32)],
        compiler_params=pltpu.CompilerParams(
            dimension_semantics=("arbitrary",), vmem_limit_bytes=VMEM_LIMIT),
        name="in_proj",
    )(x2d, g, w_r, conf_w, conf_b, conf_ln_g, conf_ln_b)


def _mlstm_kernel(zm_ref, cw_ref, cb_ref, gb_ref, hg_ref, o_ref, g_ref, gt_ref, *, seq):
    L = MLSTM_L
    n_chunks = seq // L
    nh = MLSTM_HEADS
    scale = HEAD_DIM ** -0.5
    lane = lax.broadcasted_iota(jnp.int32, (1, LANES), 1)
    head0 = lane < HEAD_DIM

    gates = zm_ref[:, 4 * MLSTM_DIM:ZM_W] + gb_ref[...]
    is_f = (lane >= nh) & (lane < 2 * nh)
    gates = jnp.where(is_f, _log_sigmoid(gates), gates)
    g_ref[...] = gates
    gt_ref[...] = gates.T[0:SUBLANES, :]

    ri = lax.broadcasted_iota(jnp.int32, (L, L), 0)
    ci = lax.broadcasted_iota(jnp.int32, (L, L), 1)
    visible = ri <= ci
    tri = (ci <= ri).astype(F32)
    tri_t = visible.astype(F32)
    srow = lax.broadcasted_iota(jnp.int32, (MLSTM_STATE_ROWS, LANES), 0)
    scol = lax.broadcasted_iota(jnp.int32, (MLSTM_STATE_ROWS, LANES), 1)
    srow_head = jnp.where(srow < LANES, srow // HEAD_DIM, srow - LANES)
    state_mask = (scol // HEAD_DIM) == srow_head
    row_is_head0 = srow_head[:, 0:1] == 0
    vrow_is_head0 = lax.broadcasted_iota(jnp.int32, (LANES, 1), 0) < HEAD_DIM
    extra_row = lax.broadcasted_iota(jnp.int32, (MLSTM_STATE_ROWS - LANES, 1), 0)
    ones_rows = jnp.ones((MLSTM_STATE_ROWS - LANES, L), BF16)
    cb = cb_ref[...]
    hg = hg_ref[...]

    def chunk(c, carry):
        states, m_prev = carry
        r0 = pl.multiple_of(c * L, L)
        gc = g_ref[pl.ds(r0, L), :]
        gtc = gt_ref[:, pl.ds(r0, L)]
        bcols = _dot(tri, gc, precision=HIGHEST)
        brows = _dot(gtc, tri_t, precision=HIGHEST)
        e_rows = gtc[0:nh] - brows[nh:2 * nh]

        xcur = zm_ref[pl.ds(r0, L), 0:2 * MLSTM_DIM]
        pr = pl.multiple_of(jnp.maximum(r0 - SUBLANES, 0), SUBLANES)
        xprev = jnp.where(c > 0, zm_ref[pl.ds(pr, SUBLANES), 0:2 * MLSTM_DIM], 0.0)
        xcat = jnp.concatenate([xprev, xcur], axis=0)
        y = cb
        for j in range(QK_CONV):
            off = SUBLANES - (QK_CONV - 1) + j
            y = y + cw_ref[j:j + 1, :] * xcat[off:off + L]
        qk = y * _sigmoid(y)

        new_states = []
        new_m = []
        for p in range(2):
            lo = p * LANES
            q_t = (qk[:, lo:lo + LANES] * scale).T.astype(BF16)
            k_b = qk[:, MLSTM_DIM + lo:MLSTM_DIM + lo + LANES].astype(BF16)
            v_t = zm_ref[pl.ds(r0, L), 2 * MLSTM_DIM + lo:2 * MLSTM_DIM + lo + LANES].T
            v_tb = v_t.astype(BF16)
            inter = _dot(states[p].astype(BF16), q_t)
            hn_t, wks, decays = [], [], []
            for hh in range(2):
                h = 2 * p + hh
                mp = m_prev[h]
                e_col = gc[:, h:h + 1] - bcols[:, nh + h:nh + h + 1]
                e_row = e_rows[h:h + 1]
                b_row = brows[nh + h:nh + h + 1]
                em = jnp.where(visible, e_col, NEG)
                g = jnp.maximum(mp, jnp.max(em, axis=0, keepdims=True))
                w_t = jnp.exp(em - g)
                k_h = jnp.where(head0 if hh == 0 else ~head0, k_b, jnp.zeros_like(k_b))
                s_t = _dot(k_h, q_t)
                lhs = jnp.concatenate(
                    [v_tb[hh * HEAD_DIM:(hh + 1) * HEAD_DIM], ones_rows], axis=0)
                r = _dot(lhs, (s_t * w_t).astype(BF16))
                a = jnp.exp(mp - g)
                num = a * inter[hh * HEAD_DIM:(hh + 1) * HEAD_DIM] + r[0:HEAD_DIM]
                den = a * inter[LANES + hh:LANES + hh + 1] + r[HEAD_DIM:HEAD_DIM + 1]
                hv = num * (1.0 / jnp.maximum(jnp.abs(den), jnp.exp(-(b_row + g))))
                mu = jnp.mean(hv, axis=0, keepdims=True)
                d = hv - mu
                var = jnp.mean(d * d, axis=0, keepdims=True)
                hn_t.append(d * lax.rsqrt(var + EPS))
                g_last = jnp.maximum(mp, jnp.max(e_row, axis=1, keepdims=True))
                wks.append(jnp.exp(e_row - g_last))
                decays.append(jnp.exp(mp - g_last))
                new_m.append(b_row[:, L - 1:L] + g_last)
            vw = v_t * jnp.where(vrow_is_head0, wks[0], wks[1])
            extra = jnp.where(extra_row == 0, wks[0], jnp.where(extra_row == 1, wks[1], 0.0))
            upd = _dot(jnp.concatenate([vw, extra], axis=0).astype(BF16), k_b)
            decay_rows = jnp.where(row_is_head0, decays[0], decays[1])
            new_states.append(decay_rows * states[p] + jnp.where(state_mask, upd, 0.0))

            hn = jnp.concatenate(hn_t, axis=0).T
            og = zm_ref[pl.ds(r0, L), 3 * MLSTM_DIM + lo:3 * MLSTM_DIM + lo + LANES]
            o_ref[pl.ds(r0, L), lo:lo + LANES] = (hn * hg[:, lo:lo + LANES] * _sigmoid(og)).astype(o_ref.dtype)
        return tuple(new_states), tuple(new_m)

    init = (tuple(jnp.zeros((MLSTM_STATE_ROWS, LANES), F32) for _ in range(2)),
            tuple(jnp.full((1, 1), NEG, F32) for _ in range(nh)))
    lax.fori_loop(0, n_chunks, chunk, init, unroll=True)


def _mlstm(zm, cw, cb, gate_bias, head_g, layer, batch, seq):
    return pl.pallas_call(
        functools.partial(_mlstm_kernel, seq=seq),
        out_shape=jax.ShapeDtypeStruct((batch * seq, MLSTM_DIM), BF16),
        grid=(batch,),
        in_specs=[pl.BlockSpec((seq, ZM_W), lambda b: (b, 0)),
                  _layer_spec((QK_CONV, 2 * MLSTM_DIM), layer),
                  _layer_spec((1, 2 * MLSTM_DIM), layer),
                  _layer_spec((1, GATE_PAD), layer),
                  _layer_spec((1, MLSTM_DIM), layer)],
        out_specs=pl.BlockSpec((seq, MLSTM_DIM), lambda b: (b, 0)),
        scratch_shapes=[pltpu.VMEM((seq, GATE_PAD), F32),
                        pltpu.VMEM((SUBLANES, seq), F32)],
        compiler_params=pltpu.CompilerParams(
            dimension_semantics=("arbitrary",), vmem_limit_bytes=VMEM_LIMIT),
        name="mlstm",
    )(zm, cw, cb, gate_bias, head_g)


def _bias_tiles_kernel(rb_ref, o_ref):
    h = pl.program_id(0)
    i = lax.broadcasted_iota(jnp.int32, (MOBA_BLOCK, MOBA_BLOCK), 0)
    j = lax.broadcasted_iota(jnp.int32, (MOBA_BLOCK, MOBA_BLOCK), 1)
    max_exact = REL_BUCKETS // 2
    for t in range(2):
        dist = i - j + t * MOBA_BLOCK
        n = jnp.maximum(dist, 0)
        nf = jnp.maximum(n, 1).astype(F32)
        large = max_exact + (jnp.log(nf / max_exact) / math.log(REL_MAX_DIST / max_exact)
                             * (REL_BUCKETS - max_exact)).astype(jnp.int32)
        large = jnp.minimum(large, REL_BUCKETS - 1)
        bucket = jnp.where(n < max_exact, n, large)
        bias = jnp.zeros((MOBA_BLOCK, MOBA_BLOCK), F32)
        for bk in range(REL_BUCKETS):
            bias = jnp.where(bucket == bk, rb_ref[bk, h], bias)
        if t == 0:
            bias = jnp.where(dist >= 0, bias, NEG)
        o_ref[0, t] = bias


def _bias_tiles(rel_bias):
    return pl.pallas_call(
        _bias_tiles_kernel,
        out_shape=jax.ShapeDtypeStruct((ATTN_HEADS, 2, MOBA_BLOCK, MOBA_BLOCK), F32),
        grid=(ATTN_HEADS,),
        in_specs=[pl.BlockSpec(memory_space=pltpu.SMEM)],
        out_specs=pl.BlockSpec((1, 2, MOBA_BLOCK, MOBA_BLOCK), lambda h: (h, 0, 0, 0)),
        compiler_params=pltpu.CompilerParams(dimension_semantics=("arbitrary",)),
        name="bias_tiles",
    )(rel_bias)


def _moba_kernel(far_ref, q_ref, k_ref, v_ref, d_ref, o_ref, qa_ref, ka_ref, s_ref, va_ref,
                 p_ref, *, seq):
    hp = pl.program_id(0)
    blk = MOBA_BLOCK
    nb = seq // blk
    scale = HEAD_DIM ** -0.5
    lane = lax.broadcasted_iota(jnp.int32, (1, LANES), 1)
    head0 = lane < HEAD_DIM

    blk_i = lax.broadcasted_iota(jnp.int32, (MAX_BLOCKS, seq), 0)
    own_i = lax.broadcasted_iota(jnp.int32, (MAX_BLOCKS, seq), 1) // blk
    row_blk = lax.broadcasted_iota(jnp.int32, (seq, LANES), 0) // blk
    lane_full = lax.broadcasted_iota(jnp.int32, (seq, LANES), 1)

    avg = jnp.where(blk_i == own_i, 1.0 / blk, 0.0).astype(BF16)
    kmean = _dot(avg, k_ref[...])
    km_hi = kmean.astype(BF16).astype(F32)
    km_mid = (kmean - km_hi).astype(BF16).astype(F32)
    km_lo = kmean - km_hi - km_mid
    gate_lhs = jnp.concatenate(
        [jnp.where(head0 if hh == 0 else ~head0, part, 0.0)
         for hh in range(2) for part in (km_hi, km_mid, km_lo)], axis=0).astype(BF16)
    gates = lax.dot_general(gate_lhs, q_ref[...], _NT, preferred_element_type=F32)
    sel_row = lax.broadcasted_iota(jnp.int32, (2 * MAX_BLOCKS, LANES), 0)
    sel_lane = lax.broadcasted_iota(jnp.int32, (2 * MAX_BLOCKS, LANES), 1)

    for hh in range(2):
        mine = head0 if hh == 0 else ~head0
        g0 = 3 * MAX_BLOCKS * hh
        gate = (gates[g0:g0 + MAX_BLOCKS] + gates[g0 + MAX_BLOCKS:g0 + 2 * MAX_BLOCKS]
                + gates[g0 + 2 * MAX_BLOCKS:g0 + 3 * MAX_BLOCKS])
        cand = blk_i < own_i
        g = jnp.where(cand, gate, NEG)
        rank = jnp.zeros((MAX_BLOCKS, seq), jnp.int32)
        for m in range(MAX_BLOCKS):
            gm = g[m:m + 1, :]
            beats = (gm > g) | ((gm == g) & (blk_i > m))
            rank = rank + beats.astype(jnp.int32)
        keep = (cand & (rank < MOBA_TOPK)) | (blk_i >= own_i)
        pen = jnp.where(keep, 0.0, -PEN_BIG)
        pen_lane0 = (1 - hh) * HEAD_DIM
        pen16 = jnp.concatenate([pen, jnp.zeros_like(pen)], axis=0).astype(BF16)
        place = ((sel_row < MAX_BLOCKS) & (sel_lane == pen_lane0 + sel_row)).astype(BF16)
        pen_t = lax.dot_general(pen16, place, (((0,), (0,)), ((), ())),
                                preferred_element_type=F32)
        q_scaled = (q_ref[...].astype(F32) * scale).astype(BF16)
        qa_ref[hh] = jnp.where(mine, q_scaled, pen_t.astype(BF16))
        onehot = (lane_full == pen_lane0 + row_blk).astype(BF16)
        ka_ref[hh] = jnp.where(mine, k_ref[...], onehot)

    va_ref[:, 0:LANES] = v_ref[...]
    va_ref[:, LANES:2 * LANES] = jnp.ones((seq, LANES), BF16)

    half = blk // 2

    def pass1(own, hh):
        q = qa_ref[hh, own * blk:(own + 1) * blk, :]
        far_bias = far_ref[2 * hp + hh]
        mx = {}
        for n in range(own + 1):
            s = lax.dot_general(q, ka_ref[hh, n * blk:(n + 1) * blk, :], _NT,
                                preferred_element_type=F32)
            is_far = n < own - 1
            if n == own:
                s = s + d_ref[hh, 0]
            elif n == own - 1:
                s = s + d_ref[hh, 1]
            s_ref[hh, n] = s
            folded = jnp.maximum(s[:, :half], s[:, half:])
            mx[is_far] = jnp.maximum(mx[is_far], folded) if is_far in mx else folded
        mx_all = jnp.maximum(mx[False], mx[True] + far_bias) if True in mx else mx[False]
        m = jnp.max(mx_all, axis=1, keepdims=True)
        return m, m - far_bias

    def pass2(own, hh, m, m_far):
        for n in range(own + 1):
            p = jnp.exp(s_ref[hh, n] - (m_far if n < own - 1 else m))
            p_ref[hh, :, n * blk:(n + 1) * blk] = p.astype(BF16)
        keys = (own + 1) * blk
        res = _dot(p_ref[hh, :, 0:keys], va_ref[0:keys, :])
        return res[:, 0:LANES] / res[:, LANES:2 * LANES]

    items = [(own, hh) for own in range(nb) for hh in range(2)]
    stats = pass1(*items[0])
    outs = []
    for i, (own, hh) in enumerate(items):
        next_stats = pass1(*items[i + 1]) if i + 1 < len(items) else None
        outs.append(pass2(own, hh, *stats))
        stats = next_stats
        if hh == 1:
            o_ref[own * blk:(own + 1) * blk, :] = jnp.where(head0, *outs).astype(o_ref.dtype)
            outs = []


def _moba(za, bias_tiles, far_bias, batch, seq):
    n_pairs = ATTN_HEADS // 2
    assert seq % MOBA_BLOCK == 0 and seq // MOBA_BLOCK <= MAX_BLOCKS
    return pl.pallas_call(
        functools.partial(_moba_kernel, seq=seq),
        out_shape=jax.ShapeDtypeStruct((batch * seq, ATTN_DIM), BF16),
        grid=(n_pairs, batch),
        in_specs=[pl.BlockSpec(memory_space=pltpu.SMEM),
                  pl.BlockSpec((seq, LANES), lambda hp, b: (b, hp)),
                  pl.BlockSpec((seq, LANES), lambda hp, b: (b, n_pairs + hp)),
                  pl.BlockSpec((seq, LANES), lambda hp, b: (b, 2 * n_pairs + hp)),
                  pl.BlockSpec((2, 2, MOBA_BLOCK, MOBA_BLOCK), lambda hp, b: (hp, 0, 0, 0))],
        out_specs=pl.BlockSpec((seq, LANES), lambda hp, b: (b, hp)),
        scratch_shapes=[pltpu.VMEM((2, seq, LANES), BF16),
                        pltpu.VMEM((2, seq, LANES), BF16),
                        pltpu.VMEM((2, seq // MOBA_BLOCK, MOBA_BLOCK, MOBA_BLOCK), F32),
                        pltpu.VMEM((seq, 2 * LANES), BF16),
                        pltpu.VMEM((2, MOBA_BLOCK, seq), BF16)],
        compiler_params=pltpu.CompilerParams(
            dimension_semantics=("arbitrary", "arbitrary"), vmem_limit_bytes=VMEM_LIMIT),
        name="moba",
    )(far_bias, za, za, za, bias_tiles)


def _out_ffn_kernel(x_ref, ym_ref, yc_ref, ya_ref, wo_ref, g2_ref, wup_ref, cw_ref, cb_ref,
                    wdn_ref, fg_ref, o_ref, hn_ref, acc_ref, act_ref, tail_ref,
                    *, tiles_per_seq, final_norm):
    tm = TM_FFN
    first = (pl.program_id(0) % tiles_per_seq) == 0
    y = jnp.concatenate([ym_ref[...], yc_ref[...], ya_ref[...]], axis=1)
    x1 = x_ref[...] + _dot(y, wo_ref[...])
    hn_ref[...] = _rms(x1, g2_ref[...]).astype(BF16)
    acc_ref[...] = x1
    row8 = lax.broadcasted_iota(jnp.int32, (SUBLANES, FFN_CHUNK), 0)

    def up_proj(c):
        return [_dot(hn_ref[...], wup_ref[:, part * D_FFP + c * FFN_CHUNK:
                                          part * D_FFP + (c + 1) * FFN_CHUNK]) for part in range(2)]

    u_next = up_proj(0)
    for c in range(N_FFN_CHUNKS):
        u_pair = u_next
        if c + 1 < N_FFN_CHUNKS:
            u_next = up_proj(c + 1)
        conv = []
        for part in range(2):
            c0 = part * D_FFP + c * FFN_CHUNK
            u = u_pair[part]
            tail = jnp.where(first, 0.0, tail_ref[2 * c + part])
            tail_ref[2 * c + part] = u[tm - SUBLANES:tm, :]
            acc = cb_ref[:, c0:c0 + FFN_CHUNK] + cw_ref[FFN_CONV - 1:FFN_CONV, c0:c0 + FFN_CHUNK] * u
            for d in range(1, FFN_CONV):
                shifted = pltpu.roll(u, d, axis=0)
                top = jnp.where(row8 < d, pltpu.roll(tail, d, axis=0), shifted[0:SUBLANES])
                shifted = jnp.concatenate([top, shifted[SUBLANES:]], axis=0)
                j = FFN_CONV - 1 - d
                acc = acc + cw_ref[j:j + 1, c0:c0 + FFN_CHUNK] * shifted
            conv.append(acc)
        act_ref[:, c * FFN_CHUNK:(c + 1) * FFN_CHUNK] = (
            conv[0] * _sigmoid(conv[0]) * conv[1]).astype(BF16)
        if c + 1 in FFN_DOWN_ENDS:
            g = FFN_DOWN_ENDS.index(c + 1)
            k0 = (FFN_DOWN_ENDS[g - 1] if g else 0) * FFN_CHUNK
            k1 = (c + 1) * FFN_CHUNK
            acc_ref[...] += _dot(act_ref[:, k0:k1], wdn_ref[k0:k1, :])

    out = acc_ref[...]
    if final_norm:
        out = _rms(out, fg_ref[...])
    o_ref[...] = out


def _out_ffn(x2d, ym, yc, ya, wo, g2, wup, cw, cb, wdn, fg, layer, seq, final_norm):
    n = x2d.shape[0]
    tiles_per_seq = seq // TM_FFN

    def const(shape):
        return _layer_spec(shape, layer, pipeline_mode=pl.Buffered(1))

    return pl.pallas_call(
        functools.partial(_out_ffn_kernel, tiles_per_seq=tiles_per_seq, final_norm=final_norm),
        out_shape=jax.ShapeDtypeStruct((n, D_MODEL), F32),
        grid=(n // TM_FFN,),
        in_specs=[pl.BlockSpec((TM_FFN, D_MODEL), lambda i: (i, 0)),
                  pl.BlockSpec((TM_FFN, MLSTM_DIM), lambda i: (i, 0)),
                  pl.BlockSpec((TM_FFN, CONV_DIM), lambda i: (i, 0)),
                  pl.BlockSpec((TM_FFN, ATTN_DIM), lambda i: (i, 0)),
                  const((D_MODEL, D_MODEL)),
                  const((1, D_MODEL)),
                  const((D_MODEL, 2 * D_FFP)),
                  const((FFN_CONV, 2 * D_FFP)),
                  const((1, 2 * D_FFP)),
                  const((D_FFP, D_MODEL)),
                  _const_spec((1, D_MODEL))],
        out_specs=pl.BlockSpec((TM_FFN, D_MODEL), lambda i: (i, 0)),
        scratch_shapes=[pltpu.VMEM((TM_FFN, D_MODEL), BF16),
                        pltpu.VMEM((TM_FFN, D_MODEL), F32),
                        pltpu.VMEM((TM_FFN, D_FFP), BF16),
                        pltpu.VMEM((2 * N_FFN_CHUNKS, SUBLANES, FFN_CHUNK), F32)],
        compiler_params=pltpu.CompilerParams(
            dimension_semantics=("arbitrary",), vmem_limit_bytes=VMEM_LIMIT),
        name="out_ffn",
    )(x2d, ym, yc, ya, wo, g2, wup, cw, cb, wdn, fg)


def _pad_last(a, width):
    return jnp.pad(a, [(0, 0)] * (a.ndim - 1) + [(0, width - a.shape[-1])])


def _row(a):
    return a[:, None, :]


def _move_cols_kernel(x_ref, o_ref, *, moves):
    end = 0
    for src, dst, width in moves:
        if dst > end:
            o_ref[:, end:dst] = jnp.zeros((o_ref.shape[0], dst - end), o_ref.dtype)
        o_ref[:, dst:dst + width] = x_ref[:, src:src + width].astype(o_ref.dtype)
        end = dst + width
    if end < o_ref.shape[1]:
        o_ref[:, end:] = jnp.zeros((o_ref.shape[0], o_ref.shape[1] - end), o_ref.dtype)


def _move_cols(w, moves, out_cols):
    depth, rows, cols = w.shape
    out = pl.pallas_call(
        functools.partial(_move_cols_kernel, moves=tuple(moves)),
        out_shape=jax.ShapeDtypeStruct((depth * rows, out_cols), BF16),
        grid=(depth * rows // PREP_ROWS,),
        in_specs=[pl.BlockSpec((PREP_ROWS, cols), lambda i: (i, 0))],
        out_specs=pl.BlockSpec((PREP_ROWS, out_cols), lambda i: (i, 0)),
        compiler_params=pltpu.CompilerParams(
            dimension_semantics=("arbitrary",), vmem_limit_bytes=VMEM_LIMIT),
        name="move_cols",
    )(w.reshape(depth * rows, cols))
    return out.reshape(depth, rows, out_cols)


def _prep_w_in(w_in):
    g_end = 4 * MLSTM_DIM + 2 * MLSTM_HEADS
    return _move_cols(w_in, [(0, 0, g_end), (g_end, ZM_W, ZC_W + ZA_W)], P_W)


def _prep_w_up(w_up):
    return _move_cols(w_up, [(0, 0, D_FF), (D_FF, D_FFP, D_FF)], 2 * D_FFP)


def _prep_ffn_cols(a):
    return jnp.concatenate([_pad_last(a[..., :D_FF], D_FFP), _pad_last(a[..., D_FF:], D_FFP)],
                           axis=-1)


def _prep_w_down(w_down):
    return jnp.pad(w_down, ((0, 0), (0, D_FFP - D_FF), (0, 0))).astype(BF16)


def _prep_gate_bias(ig_b, fg_b):
    return _row(_pad_last(jnp.concatenate([ig_b, fg_b], axis=-1), GATE_PAD))


def kernel(x, norm1_g, w_in, mlstm_qk_conv_w, mlstm_qk_conv_b, mlstm_ig_b, mlstm_fg_b, mlstm_head_g, conf_dw_w, conf_dw_b, conf_ln_g, conf_ln_b, rel_bias, w_out, norm2_g, ffn_w_up, ffn_conv_w, ffn_conv_b, ffn_w_down, final_g):
    batch, seq, _ = x.shape
    bias_tiles = _bias_tiles(rel_bias)
    far_bias = rel_bias[REL_BUCKETS - 1]
    w_in_r = _prep_w_in(w_in)
    gate_bias = _prep_gate_bias(mlstm_ig_b, mlstm_fg_b)
    w_out_b = w_out.astype(BF16)
    w_up_r = _prep_w_up(ffn_w_up)
    f_w = _prep_ffn_cols(ffn_conv_w)
    f_b = _row(_prep_ffn_cols(ffn_conv_b))
    w_down_r = _prep_w_down(ffn_w_down)
    x2d = x.reshape(batch * seq, D_MODEL)
    for l in range(DEPTH):
        zm, za, yc = _in_proj(x2d, _row(norm1_g), w_in_r, conf_dw_w, _row(conf_dw_b),
                              _row(conf_ln_g), _row(conf_ln_b), l, seq)
        ym = _mlstm(zm, mlstm_qk_conv_w, _row(mlstm_qk_conv_b), gate_bias, _row(mlstm_head_g),
                    l, batch, seq)
        ya = _moba(za, bias_tiles, far_bias, batch, seq)
        x2d = _out_ffn(x2d, ym, yc, ya, w_out_b, _row(norm2_g), w_up_r, f_w, f_b, w_down_r,
                       final_g[None, :], l, seq, l == DEPTH - 1)
    return x2d.reshape(batch, seq, D_MODEL)
```

```python
import functools
import math

import numpy as np
import jax
import jax.numpy as jnp
from jax import lax
from jax.experimental import pallas as pl
from jax.experimental.pallas import tpu as pltpu

F32 = jnp.float32
BF16 = jnp.bfloat16
HIGHEST = lax.Precision.HIGHEST

D_MODEL = 1024
DEPTH = 2
HEAD_DIM = 64
MLSTM_DIM = 256
CONV_DIM = 256
ATTN_DIM = 512
MLSTM_HEADS = MLSTM_DIM // HEAD_DIM
ATTN_HEADS = ATTN_DIM // HEAD_DIM
QK_CONV = 4
CONF_KERNEL = 31
MOBA_BLOCK = 256
MOBA_TOPK = 3
REL_BUCKETS = 32
REL_MAX_DIST = 128
D_FF = 2752
FFN_CONV = 3
EPS = 1e-6
NEG = -1e30

LANES = 128
SUBLANES = 8
VMEM_LIMIT = 56 * 1024 * 1024

GATE_PAD = LANES
ZM_W = 4 * MLSTM_DIM + GATE_PAD
ZC_W = 2 * CONV_DIM
ZA_W = 3 * ATTN_DIM
P_W = ZM_W + ZC_W + ZA_W
D_FFP = -(-D_FF // 256) * 256
FFN_CHUNK = 256
N_FFN_CHUNKS = D_FFP // FFN_CHUNK
FFN_DOWN_ENDS = (4, 8, N_FFN_CHUNKS - 1, N_FFN_CHUNKS)

TM_IN = 512
TM_FFN = 512
PREP_ROWS = 256
MLSTM_L = 256
MLSTM_STATE_ROWS = 2 * HEAD_DIM + 16
CONF_ROWS = 64
CONF_PAD = 32
MAX_BLOCKS = SUBLANES
PEN_BIG = 2.0 ** 100

_NT = (((1,), (1,)), ((), ()))


def _t5_saturation_distance():
    n = np.arange(1, 4 * MOBA_BLOCK, dtype=np.float32)
    max_exact = REL_BUCKETS // 2
    large = max_exact + (np.log(n / max_exact) / math.log(REL_MAX_DIST / max_exact)
                         * (REL_BUCKETS - max_exact)).astype(np.int32)
    bucket = np.where(n < max_exact, n.astype(np.int32), np.minimum(large, REL_BUCKETS - 1))
    not_last = np.nonzero(bucket != REL_BUCKETS - 1)[0]
    return int(n[not_last[-1]]) + 1


assert _t5_saturation_distance() <= MOBA_BLOCK + 1


def _sigmoid(x):
    return 1.0 / (1.0 + jnp.exp(-x))


def _log_sigmoid(x):
    return jnp.minimum(x, 0.0) - jnp.log1p(jnp.exp(-jnp.abs(x)))


def _rms(xf, g):
    return xf * lax.rsqrt(jnp.mean(xf * xf, axis=-1, keepdims=True) + EPS) * g


def _dot(a, b, **kw):
    return jnp.dot(a, b, preferred_element_type=F32, **kw)


def _const_spec(shape):
    nd = len(shape)
    return pl.BlockSpec(shape, lambda *_: (0,) * nd)


def _layer_spec(shape, layer, **kw):
    nd = len(shape)
    return pl.BlockSpec((None,) + tuple(shape), lambda *_: (layer,) + (0,) * nd, **kw)


def _conformer_tile(win, w_ref, bias, ln_g, ln_b):
    win_rows = CONF_ROWS + CONF_PAD
    acc = jnp.broadcast_to(bias, (CONF_ROWS, CONV_DIM))
    for r in range(SUBLANES):
        rolled = win if r == 0 else pltpu.roll(win, win_rows - r, axis=0)
        for j in range(CONF_KERNEL):
            off = CONF_PAD - (CONF_KERNEL - 1) + j
            if off % SUBLANES == r:
                base = off - r
                acc = acc + w_ref[j:j + 1, :] * rolled[base:base + CONF_ROWS]
    mu = jnp.mean(acc, axis=-1, keepdims=True)
    d = acc - mu
    var = jnp.mean(d * d, axis=-1, keepdims=True)
    y = d * lax.rsqrt(var + EPS) * ln_g + ln_b
    return y * _sigmoid(y)


def _in_proj_kernel(x_ref, g_ref, w_ref, cw_ref, cb_ref, lg_ref, lb_ref, zm_ref, za_ref, yc_ref,
                    up_ref, *, tiles_per_seq):
    tm = TM_IN
    first = (pl.program_id(0) % tiles_per_seq) == 0
    h = _rms(x_ref[...], g_ref[...]).astype(BF16)
    zc = _dot(h, w_ref[:, ZM_W:ZM_W + ZC_W])
    up_ref[0:CONF_PAD, :] = jnp.where(first, 0.0, up_ref[tm:tm + CONF_PAD, :])
    up_ref[CONF_PAD:CONF_PAD + tm, :] = zc[:, 0:CONV_DIM] * _sigmoid(zc[:, CONV_DIM:ZC_W])
    zm_ref[...] = _dot(h, w_ref[:, 0:ZM_W])
    za_ref[...] = _dot(h, w_ref[:, ZM_W + ZC_W:P_W]).astype(BF16)
    bias, ln_g, ln_b = cb_ref[...], lg_ref[...], lb_ref[...]
    for t in range(tm // CONF_ROWS):
        r0 = t * CONF_ROWS
        win = up_ref[r0:r0 + CONF_ROWS + CONF_PAD, :]
        yc_ref[r0:r0 + CONF_ROWS, :] = _conformer_tile(win, cw_ref, bias, ln_g, ln_b).astype(yc_ref.dtype)


def _in_proj(x2d, g, w_r, conf_w, conf_b, conf_ln_g, conf_ln_b, layer, seq):
    n = x2d.shape[0]
    return pl.pallas_call(
        functools.partial(_in_proj_kernel, tiles_per_seq=seq // TM_IN),
        out_shape=(jax.ShapeDtypeStruct((n, ZM_W), F32),
                   jax.ShapeDtypeStruct((n, ZA_W), BF16),
                   jax.ShapeDtypeStruct((n, CONV_DIM), BF16)),
        grid=(n // TM_IN,),
        in_specs=[pl.BlockSpec((TM_IN, D_MODEL), lambda i: (i, 0)),
                  _layer_spec((1, D_MODEL), layer),
                  _layer_spec((D_MODEL, P_W), layer),
                  _layer_spec((CONF_KERNEL, CONV_DIM), layer),
                  _layer_spec((1, CONV_DIM), layer),
                  _layer_spec((1, CONV_DIM), layer),
                  _layer_spec((1, CONV_DIM), layer)],
        out_specs=(pl.BlockSpec((TM_IN, ZM_W), lambda i: (i, 0)),
                   pl.BlockSpec((TM_IN, ZA_W), lambda i: (i, 0)),
                   pl.BlockSpec((TM_IN, CONV_DIM), lambda i: (i, 0))),
        scratch_shapes=[pltpu.VMEM((CONF_PAD + TM_IN, CONV_DIM), F32)],
        compiler_params=pltpu.CompilerParams(
            dimension_semantics=("arbitrary",), vmem_limit_bytes=VMEM_LIMIT),
        name="in_proj",
    )(x2d, g, w_r, conf_w, conf_b, conf_ln_g, conf_ln_b)


def _mlstm_kernel(zm_ref, cw_ref, cb_ref, gb_ref, hg_ref, o_ref, g_ref, gt_ref, *, seq):
    L = MLSTM_L
    n_chunks = seq // L
    nh = MLSTM_HEADS
    scale = HEAD_DIM ** -0.5
    lane = lax.broadcasted_iota(jnp.int32, (1, LANES), 1)
    head0 = lane < HEAD_DIM

    gates = zm_ref[:, 4 * MLSTM_DIM:ZM_W] + gb_ref[...]
    is_f = (lane >= nh) & (lane < 2 * nh)
    gates = jnp.where(is_f, _log_sigmoid(gates), gates)
    g_ref[...] = gates
    gt_ref[...] = gates.T[0:SUBLANES, :]

    ri = lax.broadcasted_iota(jnp.int32, (L, L), 0)
    ci = lax.broadcasted_iota(jnp.int32, (L, L), 1)
    visible = ri <= ci
    tri = (ci <= ri).astype(F32)
    tri_t = visible.astype(F32)
    srow = lax.broadcasted_iota(jnp.int32, (MLSTM_STATE_ROWS, LANES), 0)
    scol = lax.broadcasted_iota(jnp.int32, (MLSTM_STATE_ROWS, LANES), 1)
    srow_head = jnp.where(srow < LANES, srow // HEAD_DIM, srow - LANES)
    state_mask = (scol // HEAD_DIM) == srow_head
    row_is_head0 = srow_head[:, 0:1] == 0
    vrow_is_head0 = lax.broadcasted_iota(jnp.int32, (LANES, 1), 0) < HEAD_DIM
    extra_row = lax.broadcasted_iota(jnp.int32, (MLSTM_STATE_ROWS - LANES, 1), 0)
    ones_rows = jnp.ones((MLSTM_STATE_ROWS - LANES, L), BF16)
    cb = cb_ref[...]
    hg = hg_ref[...]

    def chunk(c, carry):
        states, m_prev = carry
        r0 = pl.multiple_of(c * L, L)
        gc = g_ref[pl.ds(r0, L), :]
        gtc = gt_ref[:, pl.ds(r0, L)]
        bcols = _dot(tri, gc, precision=HIGHEST)
        brows = _dot(gtc, tri_t, precision=HIGHEST)
        e_rows = gtc[0:nh] - brows[nh:2 * nh]

        xcur = zm_ref[pl.ds(r0, L), 0:2 * MLSTM_DIM]
        pr = pl.multiple_of(jnp.maximum(r0 - SUBLANES, 0), SUBLANES)
        xprev = jnp.where(c > 0, zm_ref[pl.ds(pr, SUBLANES), 0:2 * MLSTM_DIM], 0.0)
        xcat = jnp.concatenate([xprev, xcur], axis=0)
        y = cb
        for j in range(QK_CONV):
            off = SUBLANES - (QK_CONV - 1) + j
            y = y + cw_ref[j:j + 1, :] * xcat[off:off + L]
        qk = y * _sigmoid(y)

        new_states = []
        new_m = []
        for p in range(2):
            lo = p * LANES
            q_t = (qk[:, lo:lo + LANES] * scale).T.astype(BF16)
            k_b = qk[:, MLSTM_DIM + lo:MLSTM_DIM + lo + LANES].astype(BF16)
            v_t = zm_ref[pl.ds(r0, L), 2 * MLSTM_DIM + lo:2 * MLSTM_DIM + lo + LANES].T
            v_tb = v_t.astype(BF16)
            inter = _dot(states[p].astype(BF16), q_t)
            hn_t, wks, decays = [], [], []
            for hh in range(2):
                h = 2 * p + hh
                mp = m_prev[h]
                e_col = gc[:, h:h + 1] - bcols[:, nh + h:nh + h + 1]
                e_row = e_rows[h:h + 1]
                b_row = brows[nh + h:nh + h + 1]
                em = jnp.where(visible, e_col, NEG)
                g = jnp.maximum(mp, jnp.max(em, axis=0, keepdims=True))
                w_t = jnp.exp(em - g)
                k_h = jnp.where(head0 if hh == 0 else ~head0, k_b, jnp.zeros_like(k_b))
                s_t = _dot(k_h, q_t)
                lhs = jnp.concatenate(
                    [v_tb[hh * HEAD_DIM:(hh + 1) * HEAD_DIM], ones_rows], axis=0)
                r = _dot(lhs, (s_t * w_t).astype(BF16))
                a = jnp.exp(mp - g)
                num = a * inter[hh * HEAD_DIM:(hh + 1) * HEAD_DIM] + r[0:HEAD_DIM]
                den = a * inter[LANES + hh:LANES + hh + 1] + r[HEAD_DIM:HEAD_DIM + 1]
                hv = num * (1.0 / jnp.maximum(jnp.abs(den), jnp.exp(-(b_row + g))))
                mu = jnp.mean(hv, axis=0, keepdims=True)
                d = hv - mu
                var = jnp.mean(d * d, axis=0, keepdims=True)
                hn_t.append(d * lax.rsqrt(var + EPS))
                g_last = jnp.maximum(mp, jnp.max(e_row, axis=1, keepdims=True))
                wks.append(jnp.exp(e_row - g_last))
                decays.append(jnp.exp(mp - g_last))
                new_m.append(b_row[:, L - 1:L] + g_last)
            vw = v_t * jnp.where(vrow_is_head0, wks[0], wks[1])
            extra = jnp.where(extra_row == 0, wks[0], jnp.where(extra_row == 1, wks[1], 0.0))
            upd = _dot(jnp.concatenate([vw, extra], axis=0).astype(BF16), k_b)
            decay_rows = jnp.where(row_is_head0, decays[0], decays[1])
            new_states.append(decay_rows * states[p] + jnp.where(state_mask, upd, 0.0))

            hn = jnp.concatenate(hn_t, axis=0).T
            og = zm_ref[pl.ds(r0, L), 3 * MLSTM_DIM + lo:3 * MLSTM_DIM + lo + LANES]
            o_ref[pl.ds(r0, L), lo:lo + LANES] = (hn * hg[:, lo:lo + LANES] * _sigmoid(og)).astype(o_ref.dtype)
        return tuple(new_states), tuple(new_m)

    init = (tuple(jnp.zeros((MLSTM_STATE_ROWS, LANES), F32) for _ in range(2)),
            tuple(jnp.full((1, 1), NEG, F32) for _ in range(nh)))
    lax.fori_loop(0, n_chunks, chunk, init, unroll=True)


def _mlstm(zm, cw, cb, gate_bias, head_g, layer, batch, seq):
    return pl.pallas_call(
        functools.partial(_mlstm_kernel, seq=seq),
        out_shape=jax.ShapeDtypeStruct((batch * seq, MLSTM_DIM), BF16),
        grid=(batch,),
        in_specs=[pl.BlockSpec((seq, ZM_W), lambda b: (b, 0)),
                  _layer_spec((QK_CONV, 2 * MLSTM_DIM), layer),
                  _layer_spec((1, 2 * MLSTM_DIM), layer),
                  _layer_spec((1, GATE_PAD), layer),
                  _layer_spec((1, MLSTM_DIM), layer)],
        out_specs=pl.BlockSpec((seq, MLSTM_DIM), lambda b: (b, 0)),
        scratch_shapes=[pltpu.VMEM((seq, GATE_PAD), F32),
                        pltpu.VMEM((SUBLANES, seq), F32)],
        compiler_params=pltpu.CompilerParams(
            dimension_semantics=("arbitrary",), vmem_limit_bytes=VMEM_LIMIT),
        name="mlstm",
    )(zm, cw, cb, gate_bias, head_g)


def _bias_tiles_kernel(rb_ref, o_ref):
    h = pl.program_id(0)
    i = lax.broadcasted_iota(jnp.int32, (MOBA_BLOCK, MOBA_BLOCK), 0)
    j = lax.broadcasted_iota(jnp.int32, (MOBA_BLOCK, MOBA_BLOCK), 1)
    max_exact = REL_BUCKETS // 2
    for t in range(2):
        dist = i - j + t * MOBA_BLOCK
        n = jnp.maximum(dist, 0)
        nf = jnp.maximum(n, 1).astype(F32)
        large = max_exact + (jnp.log(nf / max_exact) / math.log(REL_MAX_DIST / max_exact)
                             * (REL_BUCKETS - max_exact)).astype(jnp.int32)
        large = jnp.minimum(large, REL_BUCKETS - 1)
        bucket = jnp.where(n < max_exact, n, large)
        bias = jnp.zeros((MOBA_BLOCK, MOBA_BLOCK), F32)
        for bk in range(REL_BUCKETS):
            bias = jnp.where(bucket == bk, rb_ref[bk, h], bias)
        if t == 0:
            bias = jnp.where(dist >= 0, bias, NEG)
        o_ref[0, t] = bias


def _bias_tiles(rel_bias):
    return pl.pallas_call(
        _bias_tiles_kernel,
        out_shape=jax.ShapeDtypeStruct((ATTN_HEADS, 2, MOBA_BLOCK, MOBA_BLOCK), F32),
        grid=(ATTN_HEADS,),
        in_specs=[pl.BlockSpec(memory_space=pltpu.SMEM)],
        out_specs=pl.BlockSpec((1, 2, MOBA_BLOCK, MOBA_BLOCK), lambda h: (h, 0, 0, 0)),
        compiler_params=pltpu.CompilerParams(dimension_semantics=("arbitrary",)),
        name="bias_tiles",
    )(rel_bias)


def _moba_kernel(far_ref, q_ref, k_ref, v_ref, d_ref, o_ref, qa_ref, ka_ref, s_ref, va_ref,
                 p_ref, *, seq):
    hp = pl.program_id(0)
    blk = MOBA_BLOCK
    nb = seq // blk
    scale = HEAD_DIM ** -0.5
    lane = lax.broadcasted_iota(jnp.int32, (1, LANES), 1)
    head0 = lane < HEAD_DIM

    blk_i = lax.broadcasted_iota(jnp.int32, (MAX_BLOCKS, seq), 0)
    own_i = lax.broadcasted_iota(jnp.int32, (MAX_BLOCKS, seq), 1) // blk
    row_blk = lax.broadcasted_iota(jnp.int32, (seq, LANES), 0) // blk
    lane_full = lax.broadcasted_iota(jnp.int32, (seq, LANES), 1)

    avg = jnp.where(blk_i == own_i, 1.0 / blk, 0.0).astype(BF16)
    kmean = _dot(avg, k_ref[...])
    km_hi = kmean.astype(BF16).astype(F32)
    km_mid = (kmean - km_hi).astype(BF16).astype(F32)
    km_lo = kmean - km_hi - km_mid
    gate_lhs = jnp.concatenate(
        [jnp.where(head0 if hh == 0 else ~head0, part, 0.0)
         for hh in range(2) for part in (km_hi, km_mid, km_lo)], axis=0).astype(BF16)
    gates = lax.dot_general(gate_lhs, q_ref[...], _NT, preferred_element_type=F32)
    sel_row = lax.broadcasted_iota(jnp.int32, (2 * MAX_BLOCKS, LANES), 0)
    sel_lane = lax.broadcasted_iota(jnp.int32, (2 * MAX_BLOCKS, LANES), 1)

    for hh in range(2):
        mine = head0 if hh == 0 else ~head0
        g0 = 3 * MAX_BLOCKS * hh
        gate = (gates[g0:g0 + MAX_BLOCKS] + gates[g0 + MAX_BLOCKS:g0 + 2 * MAX_BLOCKS]
                + gates[g0 + 2 * MAX_BLOCKS:g0 + 3 * MAX_BLOCKS])
        cand = blk_i < own_i
        g = jnp.where(cand, gate, NEG)
        rank = jnp.zeros((MAX_BLOCKS, seq), jnp.int32)
        for m in range(MAX_BLOCKS):
            gm = g[m:m + 1, :]
            beats = (gm > g) | ((gm == g) & (blk_i > m))
            rank = rank + beats.astype(jnp.int32)
        keep = (cand & (rank < MOBA_TOPK)) | (blk_i >= own_i)
        pen = jnp.where(keep, 0.0, -PEN_BIG)
        pen_lane0 = (1 - hh) * HEAD_DIM
        pen16 = jnp.concatenate([pen, jnp.zeros_like(pen)], axis=0).astype(BF16)
        place = ((sel_row < MAX_BLOCKS) & (sel_lane == pen_lane0 + sel_row)).astype(BF16)
        pen_t = lax.dot_general(pen16, place, (((0,), (0,)), ((), ())),
                                preferred_element_type=F32)
        q_scaled = (q_ref[...].astype(F32) * scale).astype(BF16)
        qa_ref[hh] = jnp.where(mine, q_scaled, pen_t.astype(BF16))
        onehot = (lane_full == pen_lane0 + row_blk).astype(BF16)
        ka_ref[hh] = jnp.where(mine, k_ref[...], onehot)

    va_ref[:, 0:LANES] = v_ref[...]
    va_ref[:, LANES:2 * LANES] = jnp.ones((seq, LANES), BF16)

    half = blk // 2

    def pass1(own, hh):
        q = qa_ref[hh, own * blk:(own + 1) * blk, :]
        far_bias = far_ref[2 * hp + hh]
        mx = {}
        for n in range(own + 1):
            s = lax.dot_general(q, ka_ref[hh, n * blk:(n + 1) * blk, :], _NT,
                                preferred_element_type=F32)
            is_far = n < own - 1
            if n == own:
                s = s + d_ref[hh, 0]
            elif n == own - 1:
                s = s + d_ref[hh, 1]
            s_ref[hh, n] = s
            folded = jnp.maximum(s[:, :half], s[:, half:])
            mx[is_far] = jnp.maximum(mx[is_far], folded) if is_far in mx else folded
        mx_all = jnp.maximum(mx[False], mx[True] + far_bias) if True in mx else mx[False]
        m = jnp.max(mx_all, axis=1, keepdims=True)
        return m, m - far_bias

    def pass2(own, hh, m, m_far):
        for n in range(own + 1):
            p = jnp.exp(s_ref[hh, n] - (m_far if n < own - 1 else m))
            p_ref[hh, :, n * blk:(n + 1) * blk] = p.astype(BF16)
        keys = (own + 1) * blk
        res = _dot(p_ref[hh, :, 0:keys], va_ref[0:keys, :])
        return res[:, 0:LANES] / res[:, LANES:2 * LANES]

    items = [(own, hh) for own in range(nb) for hh in range(2)]
    stats = pass1(*items[0])
    outs = []
    for i, (own, hh) in enumerate(items):
        next_stats = pass1(*items[i + 1]) if i + 1 < len(items) else None
        outs.append(pass2(own, hh, *stats))
        stats = next_stats
        if hh == 1:
            o_ref[own * blk:(own + 1) * blk, :] = jnp.where(head0, *outs).astype(o_ref.dtype)
            outs = []


def _moba(za, bias_tiles, far_bias, batch, seq):
    n_pairs = ATTN_HEADS // 2
    assert seq % MOBA_BLOCK == 0 and seq // MOBA_BLOCK <= MAX_BLOCKS
    return pl.pallas_call(
        functools.partial(_moba_kernel, seq=seq),
        out_shape=jax.ShapeDtypeStruct((batch * seq, ATTN_DIM), BF16),
        grid=(n_pairs, batch),
        in_specs=[pl.BlockSpec(memory_space=pltpu.SMEM),
                  pl.BlockSpec((seq, LANES), lambda hp, b: (b, hp)),
                  pl.BlockSpec((seq, LANES), lambda hp, b: (b, n_pairs + hp)),
                  pl.BlockSpec((seq, LANES), lambda hp, b: (b, 2 * n_pairs + hp)),
                  pl.BlockSpec((2, 2, MOBA_BLOCK, MOBA_BLOCK), lambda hp, b: (hp, 0, 0, 0))],
        out_specs=pl.BlockSpec((seq, LANES), lambda hp, b: (b, hp)),
        scratch_shapes=[pltpu.VMEM((2, seq, LANES), BF16),
                        pltpu.VMEM((2, seq, LANES), BF16),
                        pltpu.VMEM((2, seq // MOBA_BLOCK, MOBA_BLOCK, MOBA_BLOCK), F32),
                        pltpu.VMEM((seq, 2 * LANES), BF16),
                        pltpu.VMEM((2, MOBA_BLOCK, seq), BF16)],
        compiler_params=pltpu.CompilerParams(
            dimension_semantics=("arbitrary", "arbitrary"), vmem_limit_bytes=VMEM_LIMIT),
        name="moba",
    )(far_bias, za, za, za, bias_tiles)


def _out_ffn_kernel(x_ref, ym_ref, yc_ref, ya_ref, wo_ref, g2_ref, wup_ref, cw_ref, cb_ref,
                    wdn_ref, fg_ref, o_ref, hn_ref, acc_ref, act_ref, tail_ref,
                    *, tiles_per_seq, final_norm):
    tm = TM_FFN
    first = (pl.program_id(0) % tiles_per_seq) == 0
    y = jnp.concatenate([ym_ref[...], yc_ref[...], ya_ref[...]], axis=1)
    x1 = x_ref[...] + _dot(y, wo_ref[...])
    hn_ref[...] = _rms(x1, g2_ref[...]).astype(BF16)
    acc_ref[...] = x1
    row8 = lax.broadcasted_iota(jnp.int32, (SUBLANES, FFN_CHUNK), 0)

    def up_proj(c):
        return [_dot(hn_ref[...], wup_ref[:, part * D_FFP + c * FFN_CHUNK:
                                          part * D_FFP + (c + 1) * FFN_CHUNK]) for part in range(2)]

    u_next = up_proj(0)
    for c in range(N_FFN_CHUNKS):
        u_pair = u_next
        if c + 1 < N_FFN_CHUNKS:
            u_next = up_proj(c + 1)
        conv = []
        for part in range(2):
            c0 = part * D_FFP + c * FFN_CHUNK
            u = u_pair[part]
            tail = jnp.where(first, 0.0, tail_ref[2 * c + part])
            tail_ref[2 * c + part] = u[tm - SUBLANES:tm, :]
            acc = cb_ref[:, c0:c0 + FFN_CHUNK] + cw_ref[FFN_CONV - 1:FFN_CONV, c0:c0 + FFN_CHUNK] * u
            for d in range(1, FFN_CONV):
                shifted = pltpu.roll(u, d, axis=0)
                top = jnp.where(row8 < d, pltpu.roll(tail, d, axis=0), shifted[0:SUBLANES])
                shifted = jnp.concatenate([top, shifted[SUBLANES:]], axis=0)
                j = FFN_CONV - 1 - d
                acc = acc + cw_ref[j:j + 1, c0:c0 + FFN_CHUNK] * shifted
            conv.append(acc)
        act_ref[:, c * FFN_CHUNK:(c + 1) * FFN_CHUNK] = (
            conv[0] * _sigmoid(conv[0]) * conv[1]).astype(BF16)
        if c + 1 in FFN_DOWN_ENDS:
            g = FFN_DOWN_ENDS.index(c + 1)
            k0 = (FFN_DOWN_ENDS[g - 1] if g else 0) * FFN_CHUNK
            k1 = (c + 1) * FFN_CHUNK
            acc_ref[...] += _dot(act_ref[:, k0:k1], wdn_ref[k0:k1, :])

    out = acc_ref[...]
    if final_norm:
        out = _rms(out, fg_ref[...])
    o_ref[...] = out


def _out_ffn(x2d, ym, yc, ya, wo, g2, wup, cw, cb, wdn, fg, layer, seq, final_norm):
    n = x2d.shape[0]
    tiles_per_seq = seq // TM_FFN

    def const(shape):
        return _layer_spec(shape, layer, pipeline_mode=pl.Buffered(1))

    return pl.pallas_call(
        functools.partial(_out_ffn_kernel, tiles_per_seq=tiles_per_seq, final_norm=final_norm),
        out_shape=jax.ShapeDtypeStruct((n, D_MODEL), F32),
        grid=(n // TM_FFN,),
        in_specs=[pl.BlockSpec((TM_FFN, D_MODEL), lambda i: (i, 0)),
                  pl.BlockSpec((TM_FFN, MLSTM_DIM), lambda i: (i, 0)),
                  pl.BlockSpec((TM_FFN, CONV_DIM), lambda i: (i, 0)),
                  pl.BlockSpec((TM_FFN, ATTN_DIM), lambda i: (i, 0)),
                  const((D_MODEL, D_MODEL)),
                  const((1, D_MODEL)),
                  const((D_MODEL, 2 * D_FFP)),
                  const((FFN_CONV, 2 * D_FFP)),
                  const((1, 2 * D_FFP)),
                  const((D_FFP, D_MODEL)),
                  _const_spec((1, D_MODEL))],
        out_specs=pl.BlockSpec((TM_FFN, D_MODEL), lambda i: (i, 0)),
        scratch_shapes=[pltpu.VMEM((TM_FFN, D_MODEL), BF16),
                        pltpu.VMEM((TM_FFN, D_MODEL), F32),
                        pltpu.VMEM((TM_FFN, D_FFP), BF16),
                        pltpu.VMEM((2 * N_FFN_CHUNKS, SUBLANES, FFN_CHUNK), F32)],
        compiler_params=pltpu.CompilerParams(
            dimension_semantics=("arbitrary",), vmem_limit_bytes=VMEM_LIMIT),
        name="out_ffn",
    )(x2d, ym, yc, ya, wo, g2, wup, cw, cb, wdn, fg)


def _pad_last(a, width):
    return jnp.pad(a, [(0, 0)] * (a.ndim - 1) + [(0, width - a.shape[-1])])


def _row(a):
    return a[:, None, :]


def _move_cols_kernel(x_ref, o_ref, *, moves):
    end = 0
    for src, dst, width in moves:
        if dst > end:
            o_ref[:, end:dst] = jnp.zeros((o_ref.shape[0], dst - end), o_ref.dtype)
        o_ref[:, dst:dst + width] = x_ref[:, src:src + width].astype(o_ref.dtype)
        end = dst + width
    if end < o_ref.shape[1]:
        o_ref[:, end:] = jnp.zeros((o_ref.shape[0], o_ref.shape[1] - end), o_ref.dtype)


def _move_cols(w, moves, out_cols):
    depth, rows, cols = w.shape
    return pl.pallas_call(
        functools.partial(_move_cols_kernel, moves=tuple(moves)),
        out_shape=jax.ShapeDtypeStruct((depth, rows, out_cols), BF16),
        grid=(depth, rows // PREP_ROWS),
        in_specs=[pl.BlockSpec((None, PREP_ROWS, cols), lambda l, i: (l, i, 0))],
        out_specs=pl.BlockSpec((None, PREP_ROWS, out_cols), lambda l, i: (l, i, 0)),
        compiler_params=pltpu.CompilerParams(
            dimension_semantics=("arbitrary", "arbitrary"), vmem_limit_bytes=VMEM_LIMIT),
        name="move_cols",
    )(w)


def _prep_w_in(w_in):
    g_end = 4 * MLSTM_DIM + 2 * MLSTM_HEADS
    return _move_cols(w_in, [(0, 0, g_end), (g_end, ZM_W, ZC_W + ZA_W)], P_W)


def _prep_w_up(w_up):
    return _move_cols(w_up, [(0, 0, D_FF), (D_FF, D_FFP, D_FF)], 2 * D_FFP)


def _prep_ffn_cols(a):
    return jnp.concatenate([_pad_last(a[..., :D_FF], D_FFP), _pad_last(a[..., D_FF:], D_FFP)],
                           axis=-1)


def _prep_w_down(w_down):
    return jnp.pad(w_down, ((0, 0), (0, D_FFP - D_FF), (0, 0))).astype(BF16)


def _prep_gate_bias(ig_b, fg_b):
    return _row(_pad_last(jnp.concatenate([ig_b, fg_b], axis=-1), GATE_PAD))


def kernel(x, norm1_g, w_in, mlstm_qk_conv_w, mlstm_qk_conv_b, mlstm_ig_b, mlstm_fg_b, mlstm_head_g, conf_dw_w, conf_dw_b, conf_ln_g, conf_ln_b, rel_bias, w_out, norm2_g, ffn_w_up, ffn_conv_w, ffn_conv_b, ffn_w_down, final_g):
    batch, seq, _ = x.shape
    bias_tiles = _bias_tiles(rel_bias)
    far_bias = rel_bias[REL_BUCKETS - 1]
    w_in_r = _prep_w_in(w_in)
    gate_bias = _prep_gate_bias(mlstm_ig_b, mlstm_fg_b)
    w_out_b = w_out.astype(BF16)
    w_up_r = _prep_w_up(ffn_w_up)
    f_w = _prep_ffn_cols(ffn_conv_w)
    f_b = _row(_prep_ffn_cols(ffn_conv_b))
    w_down_r = _prep_w_down(ffn_w_down)
    x2d = x.reshape(batch * seq, D_MODEL)
    for l in range(DEPTH):
        zm, za, yc = _in_proj(x2d, _row(norm1_g), w_in_r, conf_dw_w, _row(conf_dw_b),
                              _row(conf_ln_g), _row(conf_ln_b), l, seq)
        ym = _mlstm(zm, mlstm_qk_conv_w, _row(mlstm_qk_conv_b), gate_bias, _row(mlstm_head_g),
                    l, batch, seq)
        ya = _moba(za, bias_tiles, far_bias, batch, seq)
        x2d = _out_ffn(x2d, ym, yc, ya, w_out_b, _row(norm2_g), w_up_r, f_w, f_b, w_down_r,
                       final_g[None, :], l, seq, l == DEPTH - 1)
    return x2d.reshape(batch, seq, D_MODEL)
```

```python
import functools
import math

import numpy as np
import jax
import jax.numpy as jnp
from jax import lax
from jax.experimental import pallas as pl
from jax.experimental.pallas import tpu as pltpu

F32 = jnp.float32
BF16 = jnp.bfloat16
HIGHEST = lax.Precision.HIGHEST

D_MODEL = 1024
DEPTH = 2
HEAD_DIM = 64
MLSTM_DIM = 256
CONV_DIM = 256
ATTN_DIM = 512
MLSTM_HEADS = MLSTM_DIM // HEAD_DIM
ATTN_HEADS = ATTN_DIM // HEAD_DIM
QK_CONV = 4
CONF_KERNEL = 31
MOBA_BLOCK = 256
MOBA_TOPK = 3
REL_BUCKETS = 32
REL_MAX_DIST = 128
D_FF = 2752
FFN_CONV = 3
EPS = 1e-6
NEG = -1e30

LANES = 128
SUBLANES = 8
VMEM_LIMIT = 56 * 1024 * 1024

GATE_PAD = LANES
ZM_W = 4 * MLSTM_DIM + GATE_PAD
ZC_W = 2 * CONV_DIM
ZA_W = 3 * ATTN_DIM
P_W = ZM_W + ZC_W + ZA_W
D_FFP = -(-D_FF // 256) * 256
FFN_CHUNK = 256
N_FFN_CHUNKS = D_FFP // FFN_CHUNK
FFN_DOWN_ENDS = (4, 8, N_FFN_CHUNKS - 1, N_FFN_CHUNKS)

TM_IN = 512
TM_FFN = 512
PREP_ROWS = 256
MLSTM_L = 256
MLSTM_STATE_ROWS = 2 * HEAD_DIM + 16
CONF_ROWS = 64
CONF_PAD = 32
MAX_BLOCKS = SUBLANES
PEN_BIG = 2.0 ** 100

_NT = (((1,), (1,)), ((), ()))


def _t5_saturation_distance():
    n = np.arange(1, 4 * MOBA_BLOCK, dtype=np.float32)
    max_exact = REL_BUCKETS // 2
    large = max_exact + (np.log(n / max_exact) / math.log(REL_MAX_DIST / max_exact)
                         * (REL_BUCKETS - max_exact)).astype(np.int32)
    bucket = np.where(n < max_exact, n.astype(np.int32), np.minimum(large, REL_BUCKETS - 1))
    not_last = np.nonzero(bucket != REL_BUCKETS - 1)[0]
    return int(n[not_last[-1]]) + 1


assert _t5_saturation_distance() <= MOBA_BLOCK + 1


def _sigmoid(x):
    return 1.0 / (1.0 + jnp.exp(-x))


def _log_sigmoid(x):
    return jnp.minimum(x, 0.0) - jnp.log1p(jnp.exp(-jnp.abs(x)))


def _rms(xf, g):
    return xf * lax.rsqrt(jnp.mean(xf * xf, axis=-1, keepdims=True) + EPS) * g


def _dot(a, b, **kw):
    return jnp.dot(a, b, preferred_element_type=F32, **kw)


def _const_spec(shape):
    nd = len(shape)
    return pl.BlockSpec(shape, lambda *_: (0,) * nd)


def _layer_spec(shape, layer, **kw):
    nd = len(shape)
    return pl.BlockSpec((None,) + tuple(shape), lambda *_: (layer,) + (0,) * nd, **kw)


def _conformer_tile(win, w_ref, bias, ln_g, ln_b):
    win_rows = CONF_ROWS + CONF_PAD
    acc = jnp.broadcast_to(bias, (CONF_ROWS, CONV_DIM))
    for r in range(SUBLANES):
        rolled = win if r == 0 else pltpu.roll(win, win_rows - r, axis=0)
        for j in range(CONF_KERNEL):
            off = CONF_PAD - (CONF_KERNEL - 1) + j
            if off % SUBLANES == r:
                base = off - r
                acc = acc + w_ref[j:j + 1, :] * rolled[base:base + CONF_ROWS]
    mu = jnp.mean(acc, axis=-1, keepdims=True)
    d = acc - mu
    var = jnp.mean(d * d, axis=-1, keepdims=True)
    y = d * lax.rsqrt(var + EPS) * ln_g + ln_b
    return y * _sigmoid(y)


def _in_proj_kernel(x_ref, g_ref, w_ref, cw_ref, cb_ref, lg_ref, lb_ref, zm_ref, za_ref, yc_ref,
                    up_ref, *, tiles_per_seq):
    tm = TM_IN
    first = (pl.program_id(0) % tiles_per_seq) == 0
    h = _rms(x_ref[...], g_ref[...]).astype(BF16)

    def proj(c0, c1):
        return lax.dot_general(h, w_ref[c0:c1, :], _NT, preferred_element_type=F32)

    zc = proj(ZM_W, ZM_W + ZC_W)
    up_ref[0:CONF_PAD, :] = jnp.where(first, 0.0, up_ref[tm:tm + CONF_PAD, :])
    up_ref[CONF_PAD:CONF_PAD + tm, :] = zc[:, 0:CONV_DIM] * _sigmoid(zc[:, CONV_DIM:ZC_W])
    zm_ref[...] = proj(0, ZM_W)
    za_ref[...] = proj(ZM_W + ZC_W, P_W).astype(BF16)
    bias, ln_g, ln_b = cb_ref[...], lg_ref[...], lb_ref[...]
    for t in range(tm // CONF_ROWS):
        r0 = t * CONF_ROWS
        win = up_ref[r0:r0 + CONF_ROWS + CONF_PAD, :]
        yc_ref[r0:r0 + CONF_ROWS, :] = _conformer_tile(win, cw_ref, bias, ln_g, ln_b).astype(yc_ref.dtype)


def _in_proj(x2d, g, w_r, conf_w, conf_b, conf_ln_g, conf_ln_b, layer, seq):
    n = x2d.shape[0]
    return pl.pallas_call(
        functools.partial(_in_proj_kernel, tiles_per_seq=seq // TM_IN),
        out_shape=(jax.ShapeDtypeStruct((n, ZM_W), F32),
                   jax.ShapeDtypeStruct((n, ZA_W), BF16),
                   jax.ShapeDtypeStruct((n, CONV_DIM), BF16)),
        grid=(n // TM_IN,),
        in_specs=[pl.BlockSpec((TM_IN, D_MODEL), lambda i: (i, 0)),
                  _layer_spec((1, D_MODEL), layer),
                  _layer_spec((P_W, D_MODEL), layer),
                  _layer_spec((CONF_KERNEL, CONV_DIM), layer),
                  _layer_spec((1, CONV_DIM), layer),
                  _layer_spec((1, CONV_DIM), layer),
                  _layer_spec((1, CONV_DIM), layer)],
        out_specs=(pl.BlockSpec((TM_IN, ZM_W), lambda i: (i, 0)),
                   pl.BlockSpec((TM_IN, ZA_W), lambda i: (i, 0)),
                   pl.BlockSpec((TM_IN, CONV_DIM), lambda i: (i, 0))),
        scratch_shapes=[pltpu.VMEM((CONF_PAD + TM_IN, CONV_DIM), F32)],
        compiler_params=pltpu.CompilerParams(
            dimension_semantics=("arbitrary",), vmem_limit_bytes=VMEM_LIMIT),
        name="in_proj",
    )(x2d, g, w_r, conf_w, conf_b, conf_ln_g, conf_ln_b)


def _mlstm_kernel(zm_ref, cw_ref, cb_ref, gb_ref, hg_ref, o_ref, g_ref, gt_ref, *, seq):
    L = MLSTM_L
    n_chunks = seq // L
    nh = MLSTM_HEADS
    scale = HEAD_DIM ** -0.5
    lane = lax.broadcasted_iota(jnp.int32, (1, LANES), 1)
    head0 = lane < HEAD_DIM

    gates = zm_ref[:, 4 * MLSTM_DIM:ZM_W] + gb_ref[...]
    is_f = (lane >= nh) & (lane < 2 * nh)
    gates = jnp.where(is_f, _log_sigmoid(gates), gates)
    g_ref[...] = gates
    gt_ref[...] = gates.T[0:SUBLANES, :]

    ri = lax.broadcasted_iota(jnp.int32, (L, L), 0)
    ci = lax.broadcasted_iota(jnp.int32, (L, L), 1)
    visible = ri <= ci
    tri = (ci <= ri).astype(F32)
    tri_t = visible.astype(F32)
    srow = lax.broadcasted_iota(jnp.int32, (MLSTM_STATE_ROWS, LANES), 0)
    scol = lax.broadcasted_iota(jnp.int32, (MLSTM_STATE_ROWS, LANES), 1)
    srow_head = jnp.where(srow < LANES, srow // HEAD_DIM, srow - LANES)
    state_mask = (scol // HEAD_DIM) == srow_head
    row_is_head0 = srow_head[:, 0:1] == 0
    vrow_is_head0 = lax.broadcasted_iota(jnp.int32, (LANES, 1), 0) < HEAD_DIM
    extra_row = lax.broadcasted_iota(jnp.int32, (MLSTM_STATE_ROWS - LANES, 1), 0)
    ones_rows = jnp.ones((MLSTM_STATE_ROWS - LANES, L), BF16)
    cb = cb_ref[...]
    hg = hg_ref[...]

    def chunk(c, carry):
        states, m_prev = carry
        r0 = pl.multiple_of(c * L, L)
        gc = g_ref[pl.ds(r0, L), :]
        gtc = gt_ref[:, pl.ds(r0, L)]
        bcols = _dot(tri, gc, precision=HIGHEST)
        brows = _dot(gtc, tri_t, precision=HIGHEST)
        e_rows = gtc[0:nh] - brows[nh:2 * nh]

        xcur = zm_ref[pl.ds(r0, L), 0:2 * MLSTM_DIM]
        pr = pl.multiple_of(jnp.maximum(r0 - SUBLANES, 0), SUBLANES)
        xprev = jnp.where(c > 0, zm_ref[pl.ds(pr, SUBLANES), 0:2 * MLSTM_DIM], 0.0)
        xcat = jnp.concatenate([xprev, xcur], axis=0)
        y = cb
        for j in range(QK_CONV):
            off = SUBLANES - (QK_CONV - 1) + j
            y = y + cw_ref[j:j + 1, :] * xcat[off:off + L]
        qk = y * _sigmoid(y)

        new_states = []
        new_m = []
        for p in range(2):
            lo = p * LANES
            q_t = (qk[:, lo:lo + LANES] * scale).T.astype(BF16)
            k_b = qk[:, MLSTM_DIM + lo:MLSTM_DIM + lo + LANES].astype(BF16)
            v_t = zm_ref[pl.ds(r0, L), 2 * MLSTM_DIM + lo:2 * MLSTM_DIM + lo + LANES].T
            v_tb = v_t.astype(BF16)
            inter = _dot(states[p].astype(BF16), q_t)
            hn_t, wks, decays = [], [], []
            for hh in range(2):
                h = 2 * p + hh
                mp = m_prev[h]
                e_col = gc[:, h:h + 1] - bcols[:, nh + h:nh + h + 1]
                e_row = e_rows[h:h + 1]
                b_row = brows[nh + h:nh + h + 1]
                em = jnp.where(visible, e_col, NEG)
                g = jnp.maximum(mp, jnp.max(em, axis=0, keepdims=True))
                w_t = jnp.exp(em - g)
                k_h = jnp.where(head0 if hh == 0 else ~head0, k_b, jnp.zeros_like(k_b))
                s_t = _dot(k_h, q_t)
                lhs = jnp.concatenate(
                    [v_tb[hh * HEAD_DIM:(hh + 1) * HEAD_DIM], ones_rows], axis=0)
                r = _dot(lhs, (s_t * w_t).astype(BF16))
                a = jnp.exp(mp - g)
                num = a * inter[hh * HEAD_DIM:(hh + 1) * HEAD_DIM] + r[0:HEAD_DIM]
                den = a * inter[LANES + hh:LANES + hh + 1] + r[HEAD_DIM:HEAD_DIM + 1]
                hv = num * (1.0 / jnp.maximum(jnp.abs(den), jnp.exp(-(b_row + g))))
                mu = jnp.mean(hv, axis=0, keepdims=True)
                d = hv - mu
                var = jnp.mean(d * d, axis=0, keepdims=True)
                hn_t.append(d * lax.rsqrt(var + EPS))
                g_last = jnp.maximum(mp, jnp.max(e_row, axis=1, keepdims=True))
                wks.append(jnp.exp(e_row - g_last))
                decays.append(jnp.exp(mp - g_last))
                new_m.append(b_row[:, L - 1:L] + g_last)
            vw = v_t * jnp.where(vrow_is_head0, wks[0], wks[1])
            extra = jnp.where(extra_row == 0, wks[0], jnp.where(extra_row == 1, wks[1], 0.0))
            upd = _dot(jnp.concatenate([vw, extra], axis=0).astype(BF16), k_b)
            decay_rows = jnp.where(row_is_head0, decays[0], decays[1])
            new_states.append(decay_rows * states[p] + jnp.where(state_mask, upd, 0.0))

            hn = jnp.concatenate(hn_t, axis=0).T
            og = zm_ref[pl.ds(r0, L), 3 * MLSTM_DIM + lo:3 * MLSTM_DIM + lo + LANES]
            o_ref[pl.ds(r0, L), lo:lo + LANES] = (hn * hg[:, lo:lo + LANES] * _sigmoid(og)).astype(o_ref.dtype)
        return tuple(new_states), tuple(new_m)

    init = (tuple(jnp.zeros((MLSTM_STATE_ROWS, LANES), F32) for _ in range(2)),
            tuple(jnp.full((1, 1), NEG, F32) for _ in range(nh)))
    lax.fori_loop(0, n_chunks, chunk, init, unroll=True)


def _mlstm(zm, cw, cb, gate_bias, head_g, layer, batch, seq):
    return pl.pallas_call(
        functools.partial(_mlstm_kernel, seq=seq),
        out_shape=jax.ShapeDtypeStruct((batch * seq, MLSTM_DIM), BF16),
        grid=(batch,),
        in_specs=[pl.BlockSpec((seq, ZM_W), lambda b: (b, 0)),
                  _layer_spec((QK_CONV, 2 * MLSTM_DIM), layer),
                  _layer_spec((1, 2 * MLSTM_DIM), layer),
                  _layer_spec((1, GATE_PAD), layer),
                  _layer_spec((1, MLSTM_DIM), layer)],
        out_specs=pl.BlockSpec((seq, MLSTM_DIM), lambda b: (b, 0)),
        scratch_shapes=[pltpu.VMEM((seq, GATE_PAD), F32),
                        pltpu.VMEM((SUBLANES, seq), F32)],
        compiler_params=pltpu.CompilerParams(
            dimension_semantics=("arbitrary",), vmem_limit_bytes=VMEM_LIMIT),
        name="mlstm",
    )(zm, cw, cb, gate_bias, head_g)


def _bias_tiles_kernel(rb_ref, o_ref):
    h = pl.program_id(0)
    i = lax.broadcasted_iota(jnp.int32, (MOBA_BLOCK, MOBA_BLOCK), 0)
    j = lax.broadcasted_iota(jnp.int32, (MOBA_BLOCK, MOBA_BLOCK), 1)
    max_exact = REL_BUCKETS // 2
    for t in range(2):
        dist = i - j + t * MOBA_BLOCK
        n = jnp.maximum(dist, 0)
        nf = jnp.maximum(n, 1).astype(F32)
        large = max_exact + (jnp.log(nf / max_exact) / math.log(REL_MAX_DIST / max_exact)
                             * (REL_BUCKETS - max_exact)).astype(jnp.int32)
        large = jnp.minimum(large, REL_BUCKETS - 1)
        bucket = jnp.where(n < max_exact, n, large)
        bias = jnp.zeros((MOBA_BLOCK, MOBA_BLOCK), F32)
        for bk in range(REL_BUCKETS):
            bias = jnp.where(bucket == bk, rb_ref[bk, h], bias)
        if t == 0:
            bias = jnp.where(dist >= 0, bias, NEG)
        o_ref[0, t] = bias


def _bias_tiles(rel_bias):
    return pl.pallas_call(
        _bias_tiles_kernel,
        out_shape=jax.ShapeDtypeStruct((ATTN_HEADS, 2, MOBA_BLOCK, MOBA_BLOCK), F32),
        grid=(ATTN_HEADS,),
        in_specs=[pl.BlockSpec(memory_space=pltpu.SMEM)],
        out_specs=pl.BlockSpec((1, 2, MOBA_BLOCK, MOBA_BLOCK), lambda h: (h, 0, 0, 0)),
        compiler_params=pltpu.CompilerParams(dimension_semantics=("arbitrary",)),
        name="bias_tiles",
    )(rel_bias)


def _moba_kernel(far_ref, q_ref, k_ref, v_ref, d_ref, o_ref, qa_ref, ka_ref, s_ref, va_ref,
                 p_ref, *, seq):
    hp = pl.program_id(0)
    blk = MOBA_BLOCK
    nb = seq // blk
    scale = HEAD_DIM ** -0.5
    lane = lax.broadcasted_iota(jnp.int32, (1, LANES), 1)
    head0 = lane < HEAD_DIM

    blk_i = lax.broadcasted_iota(jnp.int32, (MAX_BLOCKS, seq), 0)
    own_i = lax.broadcasted_iota(jnp.int32, (MAX_BLOCKS, seq), 1) // blk
    row_blk = lax.broadcasted_iota(jnp.int32, (seq, LANES), 0) // blk
    lane_full = lax.broadcasted_iota(jnp.int32, (seq, LANES), 1)

    avg = jnp.where(blk_i == own_i, 1.0 / blk, 0.0).astype(BF16)
    kmean = _dot(avg, k_ref[...])
    km_hi = kmean.astype(BF16).astype(F32)
    km_mid = (kmean - km_hi).astype(BF16).astype(F32)
    km_lo = kmean - km_hi - km_mid
    gate_lhs = jnp.concatenate(
        [jnp.where(head0 if hh == 0 else ~head0, part, 0.0)
         for hh in range(2) for part in (km_hi, km_mid, km_lo)], axis=0).astype(BF16)
    gates = lax.dot_general(gate_lhs, q_ref[...], _NT, preferred_element_type=F32)
    sel_row = lax.broadcasted_iota(jnp.int32, (2 * MAX_BLOCKS, LANES), 0)
    sel_lane = lax.broadcasted_iota(jnp.int32, (2 * MAX_BLOCKS, LANES), 1)

    for hh in range(2):
        mine = head0 if hh == 0 else ~head0
        g0 = 3 * MAX_BLOCKS * hh
        gate = (gates[g0:g0 + MAX_BLOCKS] + gates[g0 + MAX_BLOCKS:g0 + 2 * MAX_BLOCKS]
                + gates[g0 + 2 * MAX_BLOCKS:g0 + 3 * MAX_BLOCKS])
        cand = blk_i < own_i
        g = jnp.where(cand, gate, NEG)
        rank = jnp.zeros((MAX_BLOCKS, seq), jnp.int32)
        for m in range(MAX_BLOCKS):
            gm = g[m:m + 1, :]
            beats = (gm > g) | ((gm == g) & (blk_i > m))
            rank = rank + beats.astype(jnp.int32)
        keep = (cand & (rank < MOBA_TOPK)) | (blk_i >= own_i)
        pen = jnp.where(keep, 0.0, -PEN_BIG)
        pen_lane0 = (1 - hh) * HEAD_DIM
        pen16 = jnp.concatenate([pen, jnp.zeros_like(pen)], axis=0).astype(BF16)
        place = ((sel_row < MAX_BLOCKS) & (sel_lane == pen_lane0 + sel_row)).astype(BF16)
        pen_t = lax.dot_general(pen16, place, (((0,), (0,)), ((), ())),
                                preferred_element_type=F32)
        q_scaled = (q_ref[...].astype(F32) * scale).astype(BF16)
        qa_ref[hh] = jnp.where(mine, q_scaled, pen_t.astype(BF16))
        onehot = (lane_full == pen_lane0 + row_blk).astype(BF16)
        ka_ref[hh] = jnp.where(mine, k_ref[...], onehot)

    va_ref[:, 0:LANES] = v_ref[...]
    va_ref[:, LANES:2 * LANES] = jnp.ones((seq, LANES), BF16)

    half = blk // 2

    def pass1(own, hh):
        q = qa_ref[hh, own * blk:(own + 1) * blk, :]
        far_bias = far_ref[2 * hp + hh]
        mx = {}
        for n in range(own + 1):
            s = lax.dot_general(q, ka_ref[hh, n * blk:(n + 1) * blk, :], _NT,
                                preferred_element_type=F32)
            is_far = n < own - 1
            if n == own:
                s = s + d_ref[hh, 0]
            elif n == own - 1:
                s = s + d_ref[hh, 1]
            s_ref[hh, n] = s
            folded = jnp.maximum(s[:, :half], s[:, half:])
            mx[is_far] = jnp.maximum(mx[is_far], folded) if is_far in mx else folded
        mx_all = jnp.maximum(mx[False], mx[True] + far_bias) if True in mx else mx[False]
        m = jnp.max(mx_all, axis=1, keepdims=True)
        return m, m - far_bias

    def pass2(own, hh, m, m_far):
        for n in range(own + 1):
            p = jnp.exp(s_ref[hh, n] - (m_far if n < own - 1 else m))
            p_ref[hh, :, n * blk:(n + 1) * blk] = p.astype(BF16)
        keys = (own + 1) * blk
        res = _dot(p_ref[hh, :, 0:keys], va_ref[0:keys, :])
        return res[:, 0:LANES] / res[:, LANES:2 * LANES]

    items = [(own, hh) for own in range(nb) for hh in range(2)]
    stats = pass1(*items[0])
    outs = []
    for i, (own, hh) in enumerate(items):
        next_stats = pass1(*items[i + 1]) if i + 1 < len(items) else None
        outs.append(pass2(own, hh, *stats))
        stats = next_stats
        if hh == 1:
            o_ref[own * blk:(own + 1) * blk, :] = jnp.where(head0, *outs).astype(o_ref.dtype)
            outs = []


def _moba(za, bias_tiles, far_bias, batch, seq):
    n_pairs = ATTN_HEADS // 2
    assert seq % MOBA_BLOCK == 0 and seq // MOBA_BLOCK <= MAX_BLOCKS
    return pl.pallas_call(
        functools.partial(_moba_kernel, seq=seq),
        out_shape=jax.ShapeDtypeStruct((batch * seq, ATTN_DIM), BF16),
        grid=(n_pairs, batch),
        in_specs=[pl.BlockSpec(memory_space=pltpu.SMEM),
                  pl.BlockSpec((seq, LANES), lambda hp, b: (b, hp)),
                  pl.BlockSpec((seq, LANES), lambda hp, b: (b, n_pairs + hp)),
                  pl.BlockSpec((seq, LANES), lambda hp, b: (b, 2 * n_pairs + hp)),
                  pl.BlockSpec((2, 2, MOBA_BLOCK, MOBA_BLOCK), lambda hp, b: (hp, 0, 0, 0))],
        out_specs=pl.BlockSpec((seq, LANES), lambda hp, b: (b, hp)),
        scratch_shapes=[pltpu.VMEM((2, seq, LANES), BF16),
                        pltpu.VMEM((2, seq, LANES), BF16),
                        pltpu.VMEM((2, seq // MOBA_BLOCK, MOBA_BLOCK, MOBA_BLOCK), F32),
                        pltpu.VMEM((seq, 2 * LANES), BF16),
                        pltpu.VMEM((2, MOBA_BLOCK, seq), BF16)],
        compiler_params=pltpu.CompilerParams(
            dimension_semantics=("arbitrary", "arbitrary"), vmem_limit_bytes=VMEM_LIMIT),
        name="moba",
    )(far_bias, za, za, za, bias_tiles)


def _out_ffn_kernel(x_ref, ym_ref, yc_ref, ya_ref, wo_ref, g2_ref, wup_ref, cw_ref, cb_ref,
                    wdn_ref, fg_ref, o_ref, hn_ref, acc_ref, act_ref, tail_ref,
                    *, tiles_per_seq, final_norm):
    tm = TM_FFN
    first = (pl.program_id(0) % tiles_per_seq) == 0
    y = jnp.concatenate([ym_ref[...], yc_ref[...], ya_ref[...]], axis=1)
    x1 = x_ref[...] + _dot(y, wo_ref[...])
    hn_ref[...] = _rms(x1, g2_ref[...]).astype(BF16)
    acc_ref[...] = x1
    row8 = lax.broadcasted_iota(jnp.int32, (SUBLANES, FFN_CHUNK), 0)

    def up_proj(c):
        return [_dot(hn_ref[...], wup_ref[:, part * D_FFP + c * FFN_CHUNK:
                                          part * D_FFP + (c + 1) * FFN_CHUNK]) for part in range(2)]

    u_next = up_proj(0)
    for c in range(N_FFN_CHUNKS):
        u_pair = u_next
        if c + 1 < N_FFN_CHUNKS:
            u_next = up_proj(c + 1)
        conv = []
        for part in range(2):
            c0 = part * D_FFP + c * FFN_CHUNK
            u = u_pair[part]
            tail = jnp.where(first, 0.0, tail_ref[2 * c + part])
            tail_ref[2 * c + part] = u[tm - SUBLANES:tm, :]
            acc = cb_ref[:, c0:c0 + FFN_CHUNK] + cw_ref[FFN_CONV - 1:FFN_CONV, c0:c0 + FFN_CHUNK] * u
            for d in range(1, FFN_CONV):
                shifted = pltpu.roll(u, d, axis=0)
                top = jnp.where(row8 < d, pltpu.roll(tail, d, axis=0), shifted[0:SUBLANES])
                shifted = jnp.concatenate([top, shifted[SUBLANES:]], axis=0)
                j = FFN_CONV - 1 - d
                acc = acc + cw_ref[j:j + 1, c0:c0 + FFN_CHUNK] * shifted
            conv.append(acc)
        act_ref[:, c * FFN_CHUNK:(c + 1) * FFN_CHUNK] = (
            conv[0] * _sigmoid(conv[0]) * conv[1]).astype(BF16)
        if c + 1 in FFN_DOWN_ENDS:
            g = FFN_DOWN_ENDS.index(c + 1)
            k0 = (FFN_DOWN_ENDS[g - 1] if g else 0) * FFN_CHUNK
            k1 = (c + 1) * FFN_CHUNK
            acc_ref[...] += _dot(act_ref[:, k0:k1], wdn_ref[k0:k1, :])

    out = acc_ref[...]
    if final_norm:
        out = _rms(out, fg_ref[...])
    o_ref[...] = out


def _out_ffn(x2d, ym, yc, ya, wo, g2, wup, cw, cb, wdn, fg, layer, seq, final_norm):
    n = x2d.shape[0]
    tiles_per_seq = seq // TM_FFN

    def const(shape):
        return _layer_spec(shape, layer, pipeline_mode=pl.Buffered(1))

    return pl.pallas_call(
        functools.partial(_out_ffn_kernel, tiles_per_seq=tiles_per_seq, final_norm=final_norm),
        out_shape=jax.ShapeDtypeStruct((n, D_MODEL), F32),
        grid=(n // TM_FFN,),
        in_specs=[pl.BlockSpec((TM_FFN, D_MODEL), lambda i: (i, 0)),
                  pl.BlockSpec((TM_FFN, MLSTM_DIM), lambda i: (i, 0)),
                  pl.BlockSpec((TM_FFN, CONV_DIM), lambda i: (i, 0)),
                  pl.BlockSpec((TM_FFN, ATTN_DIM), lambda i: (i, 0)),
                  const((D_MODEL, D_MODEL)),
                  const((1, D_MODEL)),
                  const((D_MODEL, 2 * D_FFP)),
                  const((FFN_CONV, 2 * D_FFP)),
                  const((1, 2 * D_FFP)),
                  const((D_FFP, D_MODEL)),
                  _const_spec((1, D_MODEL))],
        out_specs=pl.BlockSpec((TM_FFN, D_MODEL), lambda i: (i, 0)),
        scratch_shapes=[pltpu.VMEM((TM_FFN, D_MODEL), BF16),
                        pltpu.VMEM((TM_FFN, D_MODEL), F32),
                        pltpu.VMEM((TM_FFN, D_FFP), BF16),
                        pltpu.VMEM((2 * N_FFN_CHUNKS, SUBLANES, FFN_CHUNK), F32)],
        compiler_params=pltpu.CompilerParams(
            dimension_semantics=("arbitrary",), vmem_limit_bytes=VMEM_LIMIT),
        name="out_ffn",
    )(x2d, ym, yc, ya, wo, g2, wup, cw, cb, wdn, fg)


def _pad_last(a, width):
    return jnp.pad(a, [(0, 0)] * (a.ndim - 1) + [(0, width - a.shape[-1])])


def _row(a):
    return a[:, None, :]


def _move_cols_kernel(x_ref, o_ref, *, moves):
    end = 0
    for src, dst, width in moves:
        if dst > end:
            o_ref[:, end:dst] = jnp.zeros((o_ref.shape[0], dst - end), o_ref.dtype)
        o_ref[:, dst:dst + width] = x_ref[:, src:src + width].astype(o_ref.dtype)
        end = dst + width
    if end < o_ref.shape[1]:
        o_ref[:, end:] = jnp.zeros((o_ref.shape[0], o_ref.shape[1] - end), o_ref.dtype)


def _move_cols(w, moves, out_cols):
    depth, rows, cols = w.shape
    return pl.pallas_call(
        functools.partial(_move_cols_kernel, moves=tuple(moves)),
        out_shape=jax.ShapeDtypeStruct((depth, rows, out_cols), BF16),
        grid=(depth, rows // PREP_ROWS),
        in_specs=[pl.BlockSpec((None, PREP_ROWS, cols), lambda l, i: (l, i, 0))],
        out_specs=pl.BlockSpec((None, PREP_ROWS, out_cols), lambda l, i: (l, i, 0)),
        compiler_params=pltpu.CompilerParams(
            dimension_semantics=("arbitrary", "arbitrary"), vmem_limit_bytes=VMEM_LIMIT),
        name="move_cols",
    )(w)


def _w_in_rows_kernel(x_ref, o_ref):
    g0 = 4 * MLSTM_DIM
    g1 = g0 + 2 * MLSTM_HEADS
    cols = x_ref.shape[1]
    o_ref[0:g0, :] = x_ref[0:g0, :].astype(o_ref.dtype)
    gate_tile = jnp.concatenate([x_ref[g0:g1, :], jnp.zeros((GATE_PAD - (g1 - g0), cols), F32)], axis=0)
    o_ref[g0:ZM_W, :] = gate_tile.astype(o_ref.dtype)
    o_ref[ZM_W:P_W, :] = x_ref[g1:, :].astype(o_ref.dtype)


def _prep_w_in(w_in):
    depth, d_in, p_in = w_in.shape
    return pl.pallas_call(
        _w_in_rows_kernel,
        out_shape=jax.ShapeDtypeStruct((depth, P_W, d_in), BF16),
        grid=(depth, d_in // PREP_ROWS),
        in_specs=[pl.BlockSpec((None, p_in, PREP_ROWS), lambda l, i: (l, 0, i))],
        out_specs=pl.BlockSpec((None, P_W, PREP_ROWS), lambda l, i: (l, 0, i)),
        compiler_params=pltpu.CompilerParams(
            dimension_semantics=("arbitrary", "arbitrary"), vmem_limit_bytes=VMEM_LIMIT),
        name="w_in_rows",
    )(jnp.swapaxes(w_in, 1, 2))


def _prep_w_up(w_up):
    return _move_cols(w_up, [(0, 0, D_FF), (D_FF, D_FFP, D_FF)], 2 * D_FFP)


def _prep_ffn_cols(a):
    return jnp.concatenate([_pad_last(a[..., :D_FF], D_FFP), _pad_last(a[..., D_FF:], D_FFP)],
                           axis=-1)


def _prep_w_down(w_down):
    return jnp.pad(w_down, ((0, 0), (0, D_FFP - D_FF), (0, 0))).astype(BF16)


def _prep_gate_bias(ig_b, fg_b):
    return _row(_pad_last(jnp.concatenate([ig_b, fg_b], axis=-1), GATE_PAD))


def kernel(x, norm1_g, w_in, mlstm_qk_conv_w, mlstm_qk_conv_b, mlstm_ig_b, mlstm_fg_b, mlstm_head_g, conf_dw_w, conf_dw_b, conf_ln_g, conf_ln_b, rel_bias, w_out, norm2_g, ffn_w_up, ffn_conv_w, ffn_conv_b, ffn_w_down, final_g):
    batch, seq, _ = x.shape
    bias_tiles = _bias_tiles(rel_bias)
    far_bias = rel_bias[REL_BUCKETS - 1]
    w_in_r = _prep_w_in(w_in)
    gate_bias = _prep_gate_bias(mlstm_ig_b, mlstm_fg_b)
    w_out_b = w_out.astype(BF16)
    w_up_r = _prep_w_up(ffn_w_up)
    f_w = _prep_ffn_cols(ffn_conv_w)
    f_b = _row(_prep_ffn_cols(ffn_conv_b))
    w_down_r = _prep_w_down(ffn_w_down)
    x2d = x.reshape(batch * seq, D_MODEL)
    for l in range(DEPTH):
        zm, za, yc = _in_proj(x2d, _row(norm1_g), w_in_r, conf_dw_w, _row(conf_dw_b),
                              _row(conf_ln_g), _row(conf_ln_b), l, seq)
        ym = _mlstm(zm, mlstm_qk_conv_w, _row(mlstm_qk_conv_b), gate_bias, _row(mlstm_head_g),
                    l, batch, seq)
        ya = _moba(za, bias_tiles, far_bias, batch, seq)
        x2d = _out_ffn(x2d, ym, yc, ya, w_out_b, _row(norm2_g), w_up_r, f_w, f_b, w_down_r,
                       final_g[None, :], l, seq, l == DEPTH - 1)
    return x2d.reshape(batch, seq, D_MODEL)
```

```python
import functools
import math

import numpy as np
import jax
import jax.numpy as jnp
from jax import lax
from jax.experimental import pallas as pl
from jax.experimental.pallas import tpu as pltpu

F32 = jnp.float32
BF16 = jnp.bfloat16
HIGHEST = lax.Precision.HIGHEST

D_MODEL = 1024
DEPTH = 2
HEAD_DIM = 64
MLSTM_DIM = 256
CONV_DIM = 256
ATTN_DIM = 512
MLSTM_HEADS = MLSTM_DIM // HEAD_DIM
ATTN_HEADS = ATTN_DIM // HEAD_DIM
QK_CONV = 4
CONF_KERNEL = 31
MOBA_BLOCK = 256
MOBA_TOPK = 3
REL_BUCKETS = 32
REL_MAX_DIST = 128
D_FF = 2752
FFN_CONV = 3
EPS = 1e-6
NEG = -1e30

LANES = 128
SUBLANES = 8
BF16_SUBLANES = 16
VMEM_LIMIT = 56 * 1024 * 1024

GATE_PAD = LANES
ZM_W = 4 * MLSTM_DIM + GATE_PAD
ZC_W = 2 * CONV_DIM
ZA_W = 3 * ATTN_DIM
P_W = ZM_W + ZC_W + ZA_W
D_FFP = -(-D_FF // 256) * 256
FFN_CHUNK = 256
N_FFN_CHUNKS = D_FFP // FFN_CHUNK
FFN_DOWN_ENDS = (4, 8, N_FFN_CHUNKS - 1, N_FFN_CHUNKS)

TM_IN = 512
TM_FFN = 512
PREP_ROWS = 256
MLSTM_L = 256
MLSTM_STATE_ROWS = 2 * HEAD_DIM + BF16_SUBLANES
CONF_ROWS = 64
CONF_PAD = 32
MAX_BLOCKS = SUBLANES
PEN_BIG = 2.0 ** 100

_NT = (((1,), (1,)), ((), ()))


def _t5_saturation_distance():
    n = np.arange(1, 4 * MOBA_BLOCK, dtype=np.float32)
    max_exact = REL_BUCKETS // 2
    large = max_exact + (np.log(n / max_exact) / math.log(REL_MAX_DIST / max_exact)
                         * (REL_BUCKETS - max_exact)).astype(np.int32)
    bucket = np.where(n < max_exact, n.astype(np.int32), np.minimum(large, REL_BUCKETS - 1))
    not_last = np.nonzero(bucket != REL_BUCKETS - 1)[0]
    return int(n[not_last[-1]]) + 1


assert _t5_saturation_distance() <= MOBA_BLOCK + 1


def _sigmoid(x):
    return 1.0 / (1.0 + jnp.exp2(x * (-math.log2(math.e))))


def _log_sigmoid(x):
    return jnp.minimum(x, 0.0) - jnp.log1p(jnp.exp(-jnp.abs(x)))


def _rms(xf, g):
    return xf * lax.rsqrt(jnp.mean(xf * xf, axis=-1, keepdims=True) + EPS) * g


def _dot(a, b, **kw):
    return jnp.dot(a, b, preferred_element_type=F32, **kw)


def _const_spec(shape):
    nd = len(shape)
    return pl.BlockSpec(shape, lambda *_: (0,) * nd)


def _layer_spec(shape, layer, **kw):
    nd = len(shape)
    return pl.BlockSpec((None,) + tuple(shape), lambda *_: (layer,) + (0,) * nd, **kw)


def _conformer_tile(win, w_ref, bias, ln_g, ln_b):
    win_rows = CONF_ROWS + CONF_PAD
    acc = jnp.broadcast_to(bias, (CONF_ROWS, CONV_DIM))
    for r in range(SUBLANES):
        rolled = win if r == 0 else pltpu.roll(win, win_rows - r, axis=0)
        for j in range(CONF_KERNEL):
            off = CONF_PAD - (CONF_KERNEL - 1) + j
            if off % SUBLANES == r:
                base = off - r
                acc = acc + w_ref[j:j + 1, :] * rolled[base:base + CONF_ROWS]
    mu = jnp.mean(acc, axis=-1, keepdims=True)
    d = acc - mu
    var = jnp.mean(d * d, axis=-1, keepdims=True)
    y = d * lax.rsqrt(var + EPS) * ln_g + ln_b
    return y * _sigmoid(y)


def _in_proj_kernel(x_ref, g_ref, w_ref, cw_ref, cb_ref, lg_ref, lb_ref, zm_ref, za_ref, yc_ref,
                    up_ref, *, tiles_per_seq):
    tm = TM_IN
    first = (pl.program_id(0) % tiles_per_seq) == 0
    h = _rms(x_ref[...], g_ref[...]).astype(BF16)

    def proj(c0, c1):
        return lax.dot_general(h, w_ref[c0:c1, :], _NT, preferred_element_type=F32)

    zc = proj(ZM_W, ZM_W + ZC_W)
    up_ref[0:CONF_PAD, :] = jnp.where(first, 0.0, up_ref[tm:tm + CONF_PAD, :])
    up_ref[CONF_PAD:CONF_PAD + tm, :] = zc[:, 0:CONV_DIM] * _sigmoid(zc[:, CONV_DIM:ZC_W])
    zm_ref[...] = proj(0, ZM_W)
    za_ref[...] = proj(ZM_W + ZC_W, P_W).astype(BF16)
    bias, ln_g, ln_b = cb_ref[...], lg_ref[...], lb_ref[...]
    for t in range(tm // CONF_ROWS):
        r0 = t * CONF_ROWS
        win = up_ref[r0:r0 + CONF_ROWS + CONF_PAD, :]
        yc_ref[r0:r0 + CONF_ROWS, :] = _conformer_tile(win, cw_ref, bias, ln_g, ln_b).astype(yc_ref.dtype)


def _in_proj(x2d, g, w_r, conf_w, conf_b, conf_ln_g, conf_ln_b, layer, seq):
    n = x2d.shape[0]
    return pl.pallas_call(
        functools.partial(_in_proj_kernel, tiles_per_seq=seq // TM_IN),
        out_shape=(jax.ShapeDtypeStruct((n, ZM_W), F32),
                   jax.ShapeDtypeStruct((n, ZA_W), BF16),
                   jax.ShapeDtypeStruct((n, CONV_DIM), BF16)),
        grid=(n // TM_IN,),
        in_specs=[pl.BlockSpec((TM_IN, D_MODEL), lambda i: (i, 0)),
                  _layer_spec((1, D_MODEL), layer),
                  _layer_spec((P_W, D_MODEL), layer),
                  _layer_spec((CONF_KERNEL, CONV_DIM), layer),
                  _layer_spec((1, CONV_DIM), layer),
                  _layer_spec((1, CONV_DIM), layer),
                  _layer_spec((1, CONV_DIM), layer)],
        out_specs=(pl.BlockSpec((TM_IN, ZM_W), lambda i: (i, 0)),
                   pl.BlockSpec((TM_IN, ZA_W), lambda i: (i, 0)),
                   pl.BlockSpec((TM_IN, CONV_DIM), lambda i: (i, 0))),
        scratch_shapes=[pltpu.VMEM((CONF_PAD + TM_IN, CONV_DIM), F32)],
        compiler_params=pltpu.CompilerParams(
            dimension_semantics=("arbitrary",), vmem_limit_bytes=VMEM_LIMIT),
        name="in_proj",
    )(x2d, g, w_r, conf_w, conf_b, conf_ln_g, conf_ln_b)


def _mlstm_kernel(zm_ref, cw_ref, cb_ref, gb_ref, hg_ref, o_ref, g_ref, gt_ref, *, seq):
    L = MLSTM_L
    n_chunks = seq // L
    nh = MLSTM_HEADS
    scale = HEAD_DIM ** -0.5
    lane = lax.broadcasted_iota(jnp.int32, (1, LANES), 1)
    head0 = lane < HEAD_DIM

    gates = zm_ref[:, 4 * MLSTM_DIM:ZM_W] + gb_ref[...]
    is_f = (lane >= nh) & (lane < 2 * nh)
    gates = jnp.where(is_f, _log_sigmoid(gates), gates)
    g_ref[...] = gates
    gt_ref[...] = gates.T[0:SUBLANES, :]

    ri = lax.broadcasted_iota(jnp.int32, (L, L), 0)
    ci = lax.broadcasted_iota(jnp.int32, (L, L), 1)
    visible = ri <= ci
    tri = (ci <= ri).astype(F32)
    tri_t = visible.astype(F32)
    srow = lax.broadcasted_iota(jnp.int32, (MLSTM_STATE_ROWS, LANES), 0)
    scol = lax.broadcasted_iota(jnp.int32, (MLSTM_STATE_ROWS, LANES), 1)
    srow_head = jnp.where(srow < LANES, srow // HEAD_DIM, srow - LANES)
    state_mask = (scol // HEAD_DIM) == srow_head
    row_is_head0 = srow_head[:, 0:1] == 0
    vrow_is_head0 = lax.broadcasted_iota(jnp.int32, (LANES, 1), 0) < HEAD_DIM
    extra_row = lax.broadcasted_iota(jnp.int32, (MLSTM_STATE_ROWS - LANES, 1), 0)
    ones_rows = jnp.ones((MLSTM_STATE_ROWS - LANES, L), BF16)
    cb = cb_ref[...]
    hg = hg_ref[...]

    def chunk(c, carry):
        states, m_prev = carry
        r0 = pl.multiple_of(c * L, L)
        gc = g_ref[pl.ds(r0, L), :]
        gtc = gt_ref[:, pl.ds(r0, L)]
        bcols = _dot(tri, gc, precision=HIGHEST)
        brows = _dot(gtc, tri_t, precision=HIGHEST)
        e_rows = gtc[0:nh] - brows[nh:2 * nh]

        xcur = zm_ref[pl.ds(r0, L), 0:2 * MLSTM_DIM]
        pr = pl.multiple_of(jnp.maximum(r0 - SUBLANES, 0), SUBLANES)
        xprev = jnp.where(c > 0, zm_ref[pl.ds(pr, SUBLANES), 0:2 * MLSTM_DIM], 0.0)
        xcat = jnp.concatenate([xprev, xcur], axis=0)
        y = cb
        for j in range(QK_CONV):
            off = SUBLANES - (QK_CONV - 1) + j
            y = y + cw_ref[j:j + 1, :] * xcat[off:off + L]
        qk = y * _sigmoid(y)

        new_states = []
        new_m = []
        for p in range(2):
            lo = p * LANES
            q_t = (qk[:, lo:lo + LANES] * scale).T.astype(BF16)
            k_b = qk[:, MLSTM_DIM + lo:MLSTM_DIM + lo + LANES].astype(BF16)
            v_t = zm_ref[pl.ds(r0, L), 2 * MLSTM_DIM + lo:2 * MLSTM_DIM + lo + LANES].T
            v_tb = v_t.astype(BF16)
            inter = _dot(states[p].astype(BF16), q_t)
            hn_t, wks, decays = [], [], []
            for hh in range(2):
                h = 2 * p + hh
                mp = m_prev[h]
                e_col = gc[:, h:h + 1] - bcols[:, nh + h:nh + h + 1]
                e_row = e_rows[h:h + 1]
                b_row = brows[nh + h:nh + h + 1]
                em = jnp.where(visible, e_col, NEG)
                g = jnp.maximum(mp, jnp.max(em, axis=0, keepdims=True))
                w_t = jnp.exp(em - g)
                k_h = jnp.where(head0 if hh == 0 else ~head0, k_b, jnp.zeros_like(k_b))
                s_t = _dot(k_h, q_t)
                lhs = jnp.concatenate(
                    [v_tb[hh * HEAD_DIM:(hh + 1) * HEAD_DIM], ones_rows], axis=0)
                r = _dot(lhs, (s_t * w_t).astype(BF16))
                a = jnp.exp(mp - g)
                num = a * inter[hh * HEAD_DIM:(hh + 1) * HEAD_DIM] + r[0:HEAD_DIM]
                den = a * inter[LANES + hh:LANES + hh + 1] + r[HEAD_DIM:HEAD_DIM + 1]
                hv = num * (1.0 / jnp.maximum(jnp.abs(den), jnp.exp(-(b_row + g))))
                mu = jnp.mean(hv, axis=0, keepdims=True)
                d = hv - mu
                var = jnp.mean(d * d, axis=0, keepdims=True)
                hn_t.append(d * lax.rsqrt(var + EPS))
                g_last = jnp.maximum(mp, jnp.max(e_row, axis=1, keepdims=True))
                wks.append(jnp.exp(e_row - g_last))
                decays.append(jnp.exp(mp - g_last))
                new_m.append(b_row[:, L - 1:L] + g_last)
            vw = v_t * jnp.where(vrow_is_head0, wks[0], wks[1])
            extra = jnp.where(extra_row == 0, wks[0], jnp.where(extra_row == 1, wks[1], 0.0))
            upd = _dot(jnp.concatenate([vw, extra], axis=0).astype(BF16), k_b)
            decay_rows = jnp.where(row_is_head0, decays[0], decays[1])
            new_states.append(decay_rows * states[p] + jnp.where(state_mask, upd, 0.0))

            hn = jnp.concatenate(hn_t, axis=0).T
            og = zm_ref[pl.ds(r0, L), 3 * MLSTM_DIM + lo:3 * MLSTM_DIM + lo + LANES]
            o_ref[pl.ds(r0, L), lo:lo + LANES] = (hn * hg[:, lo:lo + LANES] * _sigmoid(og)).astype(o_ref.dtype)
        return tuple(new_states), tuple(new_m)

    init = (tuple(jnp.zeros((MLSTM_STATE_ROWS, LANES), F32) for _ in range(2)),
            tuple(jnp.full((1, 1), NEG, F32) for _ in range(nh)))
    lax.fori_loop(0, n_chunks, chunk, init, unroll=True)


def _mlstm(zm, cw, cb, gate_bias, head_g, layer, batch, seq):
    return pl.pallas_call(
        functools.partial(_mlstm_kernel, seq=seq),
        out_shape=jax.ShapeDtypeStruct((batch * seq, MLSTM_DIM), BF16),
        grid=(batch,),
        in_specs=[pl.BlockSpec((seq, ZM_W), lambda b: (b, 0)),
                  _layer_spec((QK_CONV, 2 * MLSTM_DIM), layer),
                  _layer_spec((1, 2 * MLSTM_DIM), layer),
                  _layer_spec((1, GATE_PAD), layer),
                  _layer_spec((1, MLSTM_DIM), layer)],
        out_specs=pl.BlockSpec((seq, MLSTM_DIM), lambda b: (b, 0)),
        scratch_shapes=[pltpu.VMEM((seq, GATE_PAD), F32),
                        pltpu.VMEM((SUBLANES, seq), F32)],
        compiler_params=pltpu.CompilerParams(
            dimension_semantics=("arbitrary",), vmem_limit_bytes=VMEM_LIMIT),
        name="mlstm",
    )(zm, cw, cb, gate_bias, head_g)


def _bias_tiles_kernel(rb_ref, o_ref):
    h = pl.program_id(0)
    i = lax.broadcasted_iota(jnp.int32, (MOBA_BLOCK, MOBA_BLOCK), 0)
    j = lax.broadcasted_iota(jnp.int32, (MOBA_BLOCK, MOBA_BLOCK), 1)
    max_exact = REL_BUCKETS // 2
    for t in range(2):
        dist = i - j + t * MOBA_BLOCK
        n = jnp.maximum(dist, 0)
        nf = jnp.maximum(n, 1).astype(F32)
        large = max_exact + (jnp.log(nf / max_exact) / math.log(REL_MAX_DIST / max_exact)
                             * (REL_BUCKETS - max_exact)).astype(jnp.int32)
        large = jnp.minimum(large, REL_BUCKETS - 1)
        bucket = jnp.where(n < max_exact, n, large)
        bias = jnp.zeros((MOBA_BLOCK, MOBA_BLOCK), F32)
        for bk in range(REL_BUCKETS):
            bias = jnp.where(bucket == bk, rb_ref[bk, h], bias)
        if t == 0:
            bias = jnp.where(dist >= 0, bias, NEG)
        o_ref[0, t] = bias


def _bias_tiles(rel_bias):
    return pl.pallas_call(
        _bias_tiles_kernel,
        out_shape=jax.ShapeDtypeStruct((ATTN_HEADS, 2, MOBA_BLOCK, MOBA_BLOCK), F32),
        grid=(ATTN_HEADS,),
        in_specs=[pl.BlockSpec(memory_space=pltpu.SMEM)],
        out_specs=pl.BlockSpec((1, 2, MOBA_BLOCK, MOBA_BLOCK), lambda h: (h, 0, 0, 0)),
        compiler_params=pltpu.CompilerParams(dimension_semantics=("arbitrary",)),
        name="bias_tiles",
    )(rel_bias)


def _moba_kernel(far_ref, q_ref, k_ref, v_ref, d_ref, o_ref, qa_ref, ka_ref, s_ref, va_ref,
                 p_ref, *, seq):
    hp = pl.program_id(0)
    blk = MOBA_BLOCK
    nb = seq // blk
    scale = HEAD_DIM ** -0.5
    lane = lax.broadcasted_iota(jnp.int32, (1, LANES), 1)
    head0 = lane < HEAD_DIM

    blk_i = lax.broadcasted_iota(jnp.int32, (MAX_BLOCKS, seq), 0)
    own_i = lax.broadcasted_iota(jnp.int32, (MAX_BLOCKS, seq), 1) // blk
    row_blk = lax.broadcasted_iota(jnp.int32, (seq, LANES), 0) // blk
    lane_full = lax.broadcasted_iota(jnp.int32, (seq, LANES), 1)

    avg = jnp.where(blk_i == own_i, 1.0 / blk, 0.0).astype(BF16)
    kmean = _dot(avg, k_ref[...])
    km_hi = kmean.astype(BF16).astype(F32)
    km_mid = (kmean - km_hi).astype(BF16).astype(F32)
    km_lo = kmean - km_hi - km_mid
    gate_lhs = jnp.concatenate(
        [jnp.where(head0 if hh == 0 else ~head0, part, 0.0)
         for hh in range(2) for part in (km_hi, km_mid, km_lo)], axis=0).astype(BF16)
    gates = lax.dot_general(gate_lhs, q_ref[...], _NT, preferred_element_type=F32)
    q_scaled = (q_ref[...].astype(F32) * scale).astype(BF16)
    sel_row = lax.broadcasted_iota(jnp.int32, (2 * MAX_BLOCKS, LANES), 0)
    sel_lane = lax.broadcasted_iota(jnp.int32, (2 * MAX_BLOCKS, LANES), 1)

    for hh in range(2):
        mine = head0 if hh == 0 else ~head0
        g0 = 3 * MAX_BLOCKS * hh
        gate = (gates[g0:g0 + MAX_BLOCKS] + gates[g0 + MAX_BLOCKS:g0 + 2 * MAX_BLOCKS]
                + gates[g0 + 2 * MAX_BLOCKS:g0 + 3 * MAX_BLOCKS])
        cand = blk_i < own_i
        g = jnp.where(cand, gate, NEG)
        rank = jnp.zeros((MAX_BLOCKS, seq), jnp.int32)
        for m in range(MAX_BLOCKS):
            gm = g[m:m + 1, :]
            beats = (gm > g) | ((gm == g) & (blk_i > m))
            rank = rank + beats.astype(jnp.int32)
        keep = (cand & (rank < MOBA_TOPK)) | (blk_i >= own_i)
        pen = jnp.where(keep, 0.0, -PEN_BIG)
        pen_lane0 = (1 - hh) * HEAD_DIM
        pen16 = jnp.concatenate([pen, jnp.zeros_like(pen)], axis=0).astype(BF16)
        place = ((sel_row < MAX_BLOCKS) & (sel_lane == pen_lane0 + sel_row)).astype(BF16)
        pen_t = lax.dot_general(pen16, place, (((0,), (0,)), ((), ())),
                                preferred_element_type=F32)
        qa_ref[hh] = jnp.where(mine, q_scaled, pen_t.astype(BF16))
        onehot = (lane_full == pen_lane0 + row_blk).astype(BF16)
        ka_ref[hh] = jnp.where(mine, k_ref[...], onehot)

    va_ref[:, 0:LANES] = v_ref[...]
    va_ref[:, LANES:2 * LANES] = jnp.ones((seq, LANES), BF16)

    half = blk // 2

    def pass1(own, hh):
        q = qa_ref[hh, own * blk:(own + 1) * blk, :]
        far_bias = far_ref[2 * hp + hh]
        mx = {}
        for n in range(own + 1):
            s = lax.dot_general(q, ka_ref[hh, n * blk:(n + 1) * blk, :], _NT,
                                preferred_element_type=F32)
            is_far = n < own - 1
            if n == own:
                s = s + d_ref[hh, 0]
            elif n == own - 1:
                s = s + d_ref[hh, 1]
            s_ref[hh, n] = s
            folded = jnp.maximum(s[:, :half], s[:, half:])
            mx[is_far] = jnp.maximum(mx[is_far], folded) if is_far in mx else folded
        mx_all = jnp.maximum(mx[False], mx[True] + far_bias) if True in mx else mx[False]
        m = jnp.max(mx_all, axis=1, keepdims=True)
        return m, m - far_bias

    def pass2(own, hh, m, m_far):
        for n in range(own + 1):
            p = jnp.exp(s_ref[hh, n] - (m_far if n < own - 1 else m))
            p_ref[hh, :, n * blk:(n + 1) * blk] = p.astype(BF16)
        keys = (own + 1) * blk
        res = _dot(p_ref[hh, :, 0:keys], va_ref[0:keys, :])
        return res[:, 0:LANES] / res[:, LANES:2 * LANES]

    items = [(own, hh) for own in range(nb) for hh in range(2)]
    stats = pass1(*items[0])
    outs = []
    for i, (own, hh) in enumerate(items):
        next_stats = pass1(*items[i + 1]) if i + 1 < len(items) else None
        outs.append(pass2(own, hh, *stats))
        stats = next_stats
        if hh == 1:
            o_ref[own * blk:(own + 1) * blk, :] = jnp.where(head0, *outs).astype(o_ref.dtype)
            outs = []


def _moba(za, bias_tiles, far_bias, batch, seq):
    n_pairs = ATTN_HEADS // 2
    assert seq % MOBA_BLOCK == 0 and seq // MOBA_BLOCK <= MAX_BLOCKS
    return pl.pallas_call(
        functools.partial(_moba_kernel, seq=seq),
        out_shape=jax.ShapeDtypeStruct((batch * seq, ATTN_DIM), BF16),
        grid=(n_pairs, batch),
        in_specs=[pl.BlockSpec(memory_space=pltpu.SMEM),
                  pl.BlockSpec((seq, LANES), lambda hp, b: (b, hp)),
                  pl.BlockSpec((seq, LANES), lambda hp, b: (b, n_pairs + hp)),
                  pl.BlockSpec((seq, LANES), lambda hp, b: (b, 2 * n_pairs + hp)),
                  pl.BlockSpec((2, 2, MOBA_BLOCK, MOBA_BLOCK), lambda hp, b: (hp, 0, 0, 0))],
        out_specs=pl.BlockSpec((seq, LANES), lambda hp, b: (b, hp)),
        scratch_shapes=[pltpu.VMEM((2, seq, LANES), BF16),
                        pltpu.VMEM((2, seq, LANES), BF16),
                        pltpu.VMEM((2, seq // MOBA_BLOCK, MOBA_BLOCK, MOBA_BLOCK), F32),
                        pltpu.VMEM((seq, 2 * LANES), BF16),
                        pltpu.VMEM((2, MOBA_BLOCK, seq), BF16)],
        compiler_params=pltpu.CompilerParams(
            dimension_semantics=("arbitrary", "arbitrary"), vmem_limit_bytes=VMEM_LIMIT),
        name="moba",
    )(far_bias, za, za, za, bias_tiles)


def _out_ffn_kernel(x_ref, ym_ref, yc_ref, ya_ref, wo_ref, g2_ref, wup_ref, cw_ref, cb_ref,
                    wdn_ref, fg_ref, o_ref, hn_ref, acc_ref, act_ref, tail_ref,
                    *, tiles_per_seq, final_norm):
    tm = TM_FFN
    first = (pl.program_id(0) % tiles_per_seq) == 0
    y = jnp.concatenate([ym_ref[...], yc_ref[...], ya_ref[...]], axis=1)
    x1 = x_ref[...] + _dot(y, wo_ref[...])
    hn_ref[...] = _rms(x1, g2_ref[...]).astype(BF16)
    acc_ref[...] = x1
    row8 = lax.broadcasted_iota(jnp.int32, (SUBLANES, FFN_CHUNK), 0)

    def up_proj(c):
        return [_dot(hn_ref[...], wup_ref[:, part * D_FFP + c * FFN_CHUNK:
                                          part * D_FFP + (c + 1) * FFN_CHUNK]) for part in range(2)]

    u_next = up_proj(0)
    for c in range(N_FFN_CHUNKS):
        u_pair = u_next
        if c + 1 < N_FFN_CHUNKS:
            u_next = up_proj(c + 1)
        conv = []
        for part in range(2):
            c0 = part * D_FFP + c * FFN_CHUNK
            u = u_pair[part]
            tail = jnp.where(first, 0.0, tail_ref[2 * c + part])
            tail_ref[2 * c + part] = u[tm - SUBLANES:tm, :]
            acc = cb_ref[:, c0:c0 + FFN_CHUNK] + cw_ref[FFN_CONV - 1:FFN_CONV, c0:c0 + FFN_CHUNK] * u
            for d in range(1, FFN_CONV):
                shifted = pltpu.roll(u, d, axis=0)
                top = jnp.where(row8 < d, pltpu.roll(tail, d, axis=0), shifted[0:SUBLANES])
                shifted = jnp.concatenate([top, shifted[SUBLANES:]], axis=0)
                j = FFN_CONV - 1 - d
                acc = acc + cw_ref[j:j + 1, c0:c0 + FFN_CHUNK] * shifted
            conv.append(acc)
        act_ref[:, c * FFN_CHUNK:(c + 1) * FFN_CHUNK] = (
            conv[0] * _sigmoid(conv[0]) * conv[1]).astype(BF16)
        if c + 1 in FFN_DOWN_ENDS:
            g = FFN_DOWN_ENDS.index(c + 1)
            k0 = (FFN_DOWN_ENDS[g - 1] if g else 0) * FFN_CHUNK
            k1 = (c + 1) * FFN_CHUNK
            acc_ref[...] += _dot(act_ref[:, k0:k1], wdn_ref[k0:k1, :])

    out = acc_ref[...]
    if final_norm:
        out = _rms(out, fg_ref[...])
    o_ref[...] = out


def _out_ffn(x2d, ym, yc, ya, wo, g2, wup, cw, cb, wdn, fg, layer, seq, final_norm):
    n = x2d.shape[0]
    tiles_per_seq = seq // TM_FFN

    def const(shape):
        return _layer_spec(shape, layer, pipeline_mode=pl.Buffered(1))

    return pl.pallas_call(
        functools.partial(_out_ffn_kernel, tiles_per_seq=tiles_per_seq, final_norm=final_norm),
        out_shape=jax.ShapeDtypeStruct((n, D_MODEL), F32),
        grid=(n // TM_FFN,),
        in_specs=[pl.BlockSpec((TM_FFN, D_MODEL), lambda i: (i, 0)),
                  pl.BlockSpec((TM_FFN, MLSTM_DIM), lambda i: (i, 0)),
                  pl.BlockSpec((TM_FFN, CONV_DIM), lambda i: (i, 0)),
                  pl.BlockSpec((TM_FFN, ATTN_DIM), lambda i: (i, 0)),
                  const((D_MODEL, D_MODEL)),
                  const((1, D_MODEL)),
                  const((D_MODEL, 2 * D_FFP)),
                  const((FFN_CONV, 2 * D_FFP)),
                  const((1, 2 * D_FFP)),
                  const((D_FFP, D_MODEL)),
                  _const_spec((1, D_MODEL))],
        out_specs=pl.BlockSpec((TM_FFN, D_MODEL), lambda i: (i, 0)),
        scratch_shapes=[pltpu.VMEM((TM_FFN, D_MODEL), BF16),
                        pltpu.VMEM((TM_FFN, D_MODEL), F32),
                        pltpu.VMEM((TM_FFN, D_FFP), BF16),
                        pltpu.VMEM((2 * N_FFN_CHUNKS, SUBLANES, FFN_CHUNK), F32)],
        compiler_params=pltpu.CompilerParams(
            dimension_semantics=("arbitrary",), vmem_limit_bytes=VMEM_LIMIT),
        name="out_ffn",
    )(x2d, ym, yc, ya, wo, g2, wup, cw, cb, wdn, fg)


def _pad_last(a, width):
    return jnp.pad(a, [(0, 0)] * (a.ndim - 1) + [(0, width - a.shape[-1])])


def _row(a):
    return a[:, None, :]


def _move_cols_kernel(x_ref, o_ref, *, moves):
    end = 0
    for src, dst, width in moves:
        if dst > end:
            o_ref[:, end:dst] = jnp.zeros((o_ref.shape[0], dst - end), o_ref.dtype)
        o_ref[:, dst:dst + width] = x_ref[:, src:src + width].astype(o_ref.dtype)
        end = dst + width
    if end < o_ref.shape[1]:
        o_ref[:, end:] = jnp.zeros((o_ref.shape[0], o_ref.shape[1] - end), o_ref.dtype)


def _move_cols(w, moves, out_cols):
    depth, rows, cols = w.shape
    return pl.pallas_call(
        functools.partial(_move_cols_kernel, moves=tuple(moves)),
        out_shape=jax.ShapeDtypeStruct((depth, rows, out_cols), BF16),
        grid=(depth, rows // PREP_ROWS),
        in_specs=[pl.BlockSpec((None, PREP_ROWS, cols), lambda l, i: (l, i, 0))],
        out_specs=pl.BlockSpec((None, PREP_ROWS, out_cols), lambda l, i: (l, i, 0)),
        compiler_params=pltpu.CompilerParams(
            dimension_semantics=("arbitrary", "arbitrary"), vmem_limit_bytes=VMEM_LIMIT),
        name="move_cols",
    )(w)


def _w_in_rows_kernel(x_ref, o_ref):
    g0 = 4 * MLSTM_DIM
    g1 = g0 + 2 * MLSTM_HEADS
    cols = x_ref.shape[1]
    o_ref[0:g0, :] = x_ref[0:g0, :].astype(o_ref.dtype)
    gate_tile = jnp.concatenate([x_ref[g0:g1, :], jnp.zeros((GATE_PAD - (g1 - g0), cols), F32)], axis=0)
    o_ref[g0:ZM_W, :] = gate_tile.astype(o_ref.dtype)
    o_ref[ZM_W:P_W, :] = x_ref[g1:, :].astype(o_ref.dtype)


def _prep_w_in(w_in):
    depth, d_in, p_in = w_in.shape
    return pl.pallas_call(
        _w_in_rows_kernel,
        out_shape=jax.ShapeDtypeStruct((depth, P_W, d_in), BF16),
        grid=(depth, d_in // PREP_ROWS),
        in_specs=[pl.BlockSpec((None, p_in, PREP_ROWS), lambda l, i: (l, 0, i))],
        out_specs=pl.BlockSpec((None, P_W, PREP_ROWS), lambda l, i: (l, 0, i)),
        compiler_params=pltpu.CompilerParams(
            dimension_semantics=("arbitrary", "arbitrary"), vmem_limit_bytes=VMEM_LIMIT),
        name="w_in_rows",
    )(jnp.swapaxes(w_in, 1, 2))


def _prep_w_up(w_up):
    return _move_cols(w_up, [(0, 0, D_FF), (D_FF, D_FFP, D_FF)], 2 * D_FFP)


def _prep_ffn_cols(a):
    return jnp.concatenate([_pad_last(a[..., :D_FF], D_FFP), _pad_last(a[..., D_FF:], D_FFP)],
                           axis=-1)


def _prep_w_down(w_down):
    return jnp.pad(w_down, ((0, 0), (0, D_FFP - D_FF), (0, 0))).astype(BF16)


def _prep_gate_bias(ig_b, fg_b):
    return _row(_pad_last(jnp.concatenate([ig_b, fg_b], axis=-1), GATE_PAD))


def kernel(x, norm1_g, w_in, mlstm_qk_conv_w, mlstm_qk_conv_b, mlstm_ig_b, mlstm_fg_b, mlstm_head_g, conf_dw_w, conf_dw_b, conf_ln_g, conf_ln_b, rel_bias, w_out, norm2_g, ffn_w_up, ffn_conv_w, ffn_conv_b, ffn_w_down, final_g):
    batch, seq, _ = x.shape
    bias_tiles = _bias_tiles(rel_bias)
    far_bias = rel_bias[REL_BUCKETS - 1]
    w_in_r = _prep_w_in(w_in)
    gate_bias = _prep_gate_bias(mlstm_ig_b, mlstm_fg_b)
    w_out_b = w_out.astype(BF16)
    w_up_r = _prep_w_up(ffn_w_up)
    f_w = _prep_ffn_cols(ffn_conv_w)
    f_b = _row(_prep_ffn_cols(ffn_conv_b))
    w_down_r = _prep_w_down(ffn_w_down)
    x2d = x.reshape(batch * seq, D_MODEL)
    for l in range(DEPTH):
        zm, za, yc = _in_proj(x2d, _row(norm1_g), w_in_r, conf_dw_w, _row(conf_dw_b),
                              _row(conf_ln_g), _row(conf_ln_b), l, seq)
        ym = _mlstm(zm, mlstm_qk_conv_w, _row(mlstm_qk_conv_b), gate_bias, _row(mlstm_head_g),
                    l, batch, seq)
        ya = _moba(za, bias_tiles, far_bias, batch, seq)
        x2d = _out_ffn(x2d, ym, yc, ya, w_out_b, _row(norm2_g), w_up_r, f_w, f_b, w_down_r,
                       final_g[None, :], l, seq, l == DEPTH - 1)
    return x2d.reshape(batch, seq, D_MODEL)
```

```python
import functools
import math

import numpy as np
import jax
import jax.numpy as jnp
from jax import lax
from jax.experimental import pallas as pl
from jax.experimental.pallas import tpu as pltpu

F32 = jnp.float32
BF16 = jnp.bfloat16
HIGHEST = lax.Precision.HIGHEST

D_MODEL = 1024
DEPTH = 2
HEAD_DIM = 64
MLSTM_DIM = 256
CONV_DIM = 256
ATTN_DIM = 512
MLSTM_HEADS = MLSTM_DIM // HEAD_DIM
ATTN_HEADS = ATTN_DIM // HEAD_DIM
QK_CONV = 4
CONF_KERNEL = 31
MOBA_BLOCK = 256
MOBA_TOPK = 3
REL_BUCKETS = 32
REL_MAX_DIST = 128
D_FF = 2752
FFN_CONV = 3
EPS = 1e-6
NEG = -1e30

LANES = 128
SUBLANES = 8
BF16_SUBLANES = 16
VMEM_LIMIT = 56 * 1024 * 1024

GATE_PAD = LANES
ZM_W = 4 * MLSTM_DIM + GATE_PAD
ZC_W = 2 * CONV_DIM
ZA_W = 3 * ATTN_DIM
P_W = ZM_W + ZC_W + ZA_W
D_FFP = -(-D_FF // 256) * 256
FFN_CHUNK = 256
N_FFN_CHUNKS = D_FFP // FFN_CHUNK
FFN_DOWN_ENDS = (4, 8, N_FFN_CHUNKS - 1, N_FFN_CHUNKS)

TM_IN = 512
TM_FFN = 512
PREP_ROWS = 256
MLSTM_L = 256
MLSTM_STATE_ROWS = 2 * HEAD_DIM + BF16_SUBLANES
CONF_ROWS = 64
CONF_PAD = 32
MAX_BLOCKS = SUBLANES
PEN_BIG = 2.0 ** 100

_NT = (((1,), (1,)), ((), ()))


def _t5_saturation_distance():
    n = np.arange(1, 4 * MOBA_BLOCK, dtype=np.float32)
    max_exact = REL_BUCKETS // 2
    large = max_exact + (np.log(n / max_exact) / math.log(REL_MAX_DIST / max_exact)
                         * (REL_BUCKETS - max_exact)).astype(np.int32)
    bucket = np.where(n < max_exact, n.astype(np.int32), np.minimum(large, REL_BUCKETS - 1))
    not_last = np.nonzero(bucket != REL_BUCKETS - 1)[0]
    return int(n[not_last[-1]]) + 1


assert _t5_saturation_distance() <= MOBA_BLOCK + 1


def _sigmoid(x):
    return 1.0 / (1.0 + jnp.exp2(x * (-math.log2(math.e))))


def _log_sigmoid(x):
    return jnp.minimum(x, 0.0) - jnp.log1p(jnp.exp(-jnp.abs(x)))


def _rms(xf, g):
    return xf * lax.rsqrt(jnp.mean(xf * xf, axis=-1, keepdims=True) + EPS) * g


def _dot(a, b, **kw):
    return jnp.dot(a, b, preferred_element_type=F32, **kw)


def _const_spec(shape):
    nd = len(shape)
    return pl.BlockSpec(shape, lambda *_: (0,) * nd)


def _layer_spec(shape, layer, **kw):
    nd = len(shape)
    return pl.BlockSpec((None,) + tuple(shape), lambda *_: (layer,) + (0,) * nd, **kw)


def _conformer_tile(win, w_ref, bias, ln_g, ln_b):
    win_rows = CONF_ROWS + CONF_PAD
    acc = jnp.broadcast_to(bias, (CONF_ROWS, CONV_DIM))
    for r in range(SUBLANES):
        rolled = win if r == 0 else pltpu.roll(win, win_rows - r, axis=0)
        for j in range(CONF_KERNEL):
            off = CONF_PAD - (CONF_KERNEL - 1) + j
            if off % SUBLANES == r:
                base = off - r
                acc = acc + w_ref[j:j + 1, :] * rolled[base:base + CONF_ROWS]
    mu = jnp.mean(acc, axis=-1, keepdims=True)
    d = acc - mu
    var = jnp.mean(d * d, axis=-1, keepdims=True)
    y = d * lax.rsqrt(var + EPS) * ln_g + ln_b
    return y * _sigmoid(y)


def _in_proj_kernel(x_ref, g_ref, w_ref, cw_ref, cb_ref, lg_ref, lb_ref, zm_ref, za_ref, yc_ref,
                    up_ref, *, layer, tiles_per_seq):
    tm = TM_IN
    row = slice(layer, layer + 1)
    first = (pl.program_id(0) % tiles_per_seq) == 0
    h = _rms(x_ref[...], g_ref[row, :]).astype(BF16)

    def proj(c0, c1):
        return lax.dot_general(h, w_ref[c0:c1, :], _NT, preferred_element_type=F32)

    zc = proj(ZM_W, ZM_W + ZC_W)
    up_ref[0:CONF_PAD, :] = jnp.where(first, 0.0, up_ref[tm:tm + CONF_PAD, :])
    up_ref[CONF_PAD:CONF_PAD + tm, :] = zc[:, 0:CONV_DIM] * _sigmoid(zc[:, CONV_DIM:ZC_W])
    zm_ref[...] = proj(0, ZM_W)
    za_ref[...] = proj(ZM_W + ZC_W, P_W).astype(BF16)
    bias, ln_g, ln_b = cb_ref[row, :], lg_ref[row, :], lb_ref[row, :]
    for t in range(tm // CONF_ROWS):
        r0 = t * CONF_ROWS
        win = up_ref[r0:r0 + CONF_ROWS + CONF_PAD, :]
        yc_ref[r0:r0 + CONF_ROWS, :] = _conformer_tile(win, cw_ref, bias, ln_g, ln_b).astype(yc_ref.dtype)


def _in_proj(x2d, g, w_r, conf_w, conf_b, conf_ln_g, conf_ln_b, layer, seq):
    n = x2d.shape[0]
    return pl.pallas_call(
        functools.partial(_in_proj_kernel, layer=layer, tiles_per_seq=seq // TM_IN),
        out_shape=(jax.ShapeDtypeStruct((n, ZM_W), F32),
                   jax.ShapeDtypeStruct((n, ZA_W), BF16),
                   jax.ShapeDtypeStruct((n, CONV_DIM), BF16)),
        grid=(n // TM_IN,),
        in_specs=[pl.BlockSpec((TM_IN, D_MODEL), lambda i: (i, 0)),
                  _const_spec((DEPTH, D_MODEL)),
                  _layer_spec((P_W, D_MODEL), layer),
                  _layer_spec((CONF_KERNEL, CONV_DIM), layer),
                  _const_spec((DEPTH, CONV_DIM)),
                  _const_spec((DEPTH, CONV_DIM)),
                  _const_spec((DEPTH, CONV_DIM))],
        out_specs=(pl.BlockSpec((TM_IN, ZM_W), lambda i: (i, 0)),
                   pl.BlockSpec((TM_IN, ZA_W), lambda i: (i, 0)),
                   pl.BlockSpec((TM_IN, CONV_DIM), lambda i: (i, 0))),
        scratch_shapes=[pltpu.VMEM((CONF_PAD + TM_IN, CONV_DIM), F32)],
        compiler_params=pltpu.CompilerParams(
            dimension_semantics=("arbitrary",), vmem_limit_bytes=VMEM_LIMIT),
        name="in_proj",
    )(x2d, g, w_r, conf_w, conf_b, conf_ln_g, conf_ln_b)


def _mlstm_kernel(zm_ref, cw_ref, cb_ref, gb_ref, hg_ref, o_ref, g_ref, gt_ref, *, layer, seq):
    L = MLSTM_L
    n_chunks = seq // L
    nh = MLSTM_HEADS
    scale = HEAD_DIM ** -0.5
    lane = lax.broadcasted_iota(jnp.int32, (1, LANES), 1)
    head0 = lane < HEAD_DIM

    row = slice(layer, layer + 1)
    gates = zm_ref[:, 4 * MLSTM_DIM:ZM_W] + gb_ref[row, :]
    is_f = (lane >= nh) & (lane < 2 * nh)
    gates = jnp.where(is_f, _log_sigmoid(gates), gates)
    g_ref[...] = gates
    gt_ref[...] = gates.T[0:SUBLANES, :]

    ri = lax.broadcasted_iota(jnp.int32, (L, L), 0)
    ci = lax.broadcasted_iota(jnp.int32, (L, L), 1)
    visible = ri <= ci
    tri = (ci <= ri).astype(F32)
    tri_t = visible.astype(F32)
    srow = lax.broadcasted_iota(jnp.int32, (MLSTM_STATE_ROWS, LANES), 0)
    scol = lax.broadcasted_iota(jnp.int32, (MLSTM_STATE_ROWS, LANES), 1)
    srow_head = jnp.where(srow < LANES, srow // HEAD_DIM, srow - LANES)
    state_mask = (scol // HEAD_DIM) == srow_head
    row_is_head0 = srow_head[:, 0:1] == 0
    vrow_is_head0 = lax.broadcasted_iota(jnp.int32, (LANES, 1), 0) < HEAD_DIM
    extra_row = lax.broadcasted_iota(jnp.int32, (MLSTM_STATE_ROWS - LANES, 1), 0)
    ones_rows = jnp.ones((MLSTM_STATE_ROWS - LANES, L), BF16)
    cb = cb_ref[row, :]
    hg = hg_ref[row, :]

    def chunk(c, carry):
        states, m_prev = carry
        r0 = pl.multiple_of(c * L, L)
        gc = g_ref[pl.ds(r0, L), :]
        gtc = gt_ref[:, pl.ds(r0, L)]
        bcols = _dot(tri, gc, precision=HIGHEST)
        brows = _dot(gtc, tri_t, precision=HIGHEST)
        e_rows = gtc[0:nh] - brows[nh:2 * nh]

        xcur = zm_ref[pl.ds(r0, L), 0:2 * MLSTM_DIM]
        pr = pl.multiple_of(jnp.maximum(r0 - SUBLANES, 0), SUBLANES)
        xprev = jnp.where(c > 0, zm_ref[pl.ds(pr, SUBLANES), 0:2 * MLSTM_DIM], 0.0)
        xcat = jnp.concatenate([xprev, xcur], axis=0)
        y = cb
        for j in range(QK_CONV):
            off = SUBLANES - (QK_CONV - 1) + j
            y = y + cw_ref[j:j + 1, :] * xcat[off:off + L]
        qk = y * _sigmoid(y)

        new_states = []
        new_m = []
        for p in range(2):
            lo = p * LANES
            q_t = (qk[:, lo:lo + LANES] * scale).T.astype(BF16)
            k_b = qk[:, MLSTM_DIM + lo:MLSTM_DIM + lo + LANES].astype(BF16)
            v_t = zm_ref[pl.ds(r0, L), 2 * MLSTM_DIM + lo:2 * MLSTM_DIM + lo + LANES].T
            v_tb = v_t.astype(BF16)
            inter = _dot(states[p].astype(BF16), q_t)
            hn_t, wks, decays = [], [], []
            for hh in range(2):
                h = 2 * p + hh
                mp = m_prev[h]
                e_col = gc[:, h:h + 1] - bcols[:, nh + h:nh + h + 1]
                e_row = e_rows[h:h + 1]
                b_row = brows[nh + h:nh + h + 1]
                em = jnp.where(visible, e_col, NEG)
                g = jnp.maximum(mp, jnp.max(em, axis=0, keepdims=True))
                w_t = jnp.exp(em - g)
                k_h = jnp.where(head0 if hh == 0 else ~head0, k_b, jnp.zeros_like(k_b))
                s_t = _dot(k_h, q_t)
                lhs = jnp.concatenate(
                    [v_tb[hh * HEAD_DIM:(hh + 1) * HEAD_DIM], ones_rows], axis=0)
                r = _dot(lhs, (s_t * w_t).astype(BF16))
                a = jnp.exp(mp - g)
                num = a * inter[hh * HEAD_DIM:(hh + 1) * HEAD_DIM] + r[0:HEAD_DIM]
                den = a * inter[LANES + hh:LANES + hh + 1] + r[HEAD_DIM:HEAD_DIM + 1]
                hv = num * (1.0 / jnp.maximum(jnp.abs(den), jnp.exp(-(b_row + g))))
                mu = jnp.mean(hv, axis=0, keepdims=True)
                d = hv - mu
                var = jnp.mean(d * d, axis=0, keepdims=True)
                hn_t.append(d * lax.rsqrt(var + EPS))
                g_last = jnp.maximum(mp, jnp.max(e_row, axis=1, keepdims=True))
                wks.append(jnp.exp(e_row - g_last))
                decays.append(jnp.exp(mp - g_last))
                new_m.append(b_row[:, L - 1:L] + g_last)
            vw = v_t * jnp.where(vrow_is_head0, wks[0], wks[1])
            extra = jnp.where(extra_row == 0, wks[0], jnp.where(extra_row == 1, wks[1], 0.0))
            upd = _dot(jnp.concatenate([vw, extra], axis=0).astype(BF16), k_b)
            decay_rows = jnp.where(row_is_head0, decays[0], decays[1])
            new_states.append(decay_rows * states[p] + jnp.where(state_mask, upd, 0.0))

            hn = jnp.concatenate(hn_t, axis=0).T
            og = zm_ref[pl.ds(r0, L), 3 * MLSTM_DIM + lo:3 * MLSTM_DIM + lo + LANES]
            o_ref[pl.ds(r0, L), lo:lo + LANES] = (hn * hg[:, lo:lo + LANES] * _sigmoid(og)).astype(o_ref.dtype)
        return tuple(new_states), tuple(new_m)

    init = (tuple(jnp.zeros((MLSTM_STATE_ROWS, LANES), F32) for _ in range(2)),
            tuple(jnp.full((1, 1), NEG, F32) for _ in range(nh)))
    lax.fori_loop(0, n_chunks, chunk, init, unroll=True)


def _mlstm(zm, cw, cb, gate_bias, head_g, layer, batch, seq):
    return pl.pallas_call(
        functools.partial(_mlstm_kernel, layer=layer, seq=seq),
        out_shape=jax.ShapeDtypeStruct((batch * seq, MLSTM_DIM), BF16),
        grid=(batch,),
        in_specs=[pl.BlockSpec((seq, ZM_W), lambda b: (b, 0)),
                  _layer_spec((QK_CONV, 2 * MLSTM_DIM), layer),
                  _const_spec((DEPTH, 2 * MLSTM_DIM)),
                  _const_spec((DEPTH, GATE_PAD)),
                  _const_spec((DEPTH, MLSTM_DIM))],
        out_specs=pl.BlockSpec((seq, MLSTM_DIM), lambda b: (b, 0)),
        scratch_shapes=[pltpu.VMEM((seq, GATE_PAD), F32),
                        pltpu.VMEM((SUBLANES, seq), F32)],
        compiler_params=pltpu.CompilerParams(
            dimension_semantics=("arbitrary",), vmem_limit_bytes=VMEM_LIMIT),
        name="mlstm",
    )(zm, cw, cb, gate_bias, head_g)


def _bias_tiles_kernel(rb_ref, o_ref):
    h = pl.program_id(0)
    i = lax.broadcasted_iota(jnp.int32, (MOBA_BLOCK, MOBA_BLOCK), 0)
    j = lax.broadcasted_iota(jnp.int32, (MOBA_BLOCK, MOBA_BLOCK), 1)
    max_exact = REL_BUCKETS // 2
    for t in range(2):
        dist = i - j + t * MOBA_BLOCK
        n = jnp.maximum(dist, 0)
        nf = jnp.maximum(n, 1).astype(F32)
        large = max_exact + (jnp.log(nf / max_exact) / math.log(REL_MAX_DIST / max_exact)
                             * (REL_BUCKETS - max_exact)).astype(jnp.int32)
        large = jnp.minimum(large, REL_BUCKETS - 1)
        bucket = jnp.where(n < max_exact, n, large)
        bias = jnp.zeros((MOBA_BLOCK, MOBA_BLOCK), F32)
        for bk in range(REL_BUCKETS):
            bias = jnp.where(bucket == bk, rb_ref[bk, h], bias)
        if t == 0:
            bias = jnp.where(dist >= 0, bias, NEG)
        o_ref[0, t] = bias


def _bias_tiles(rel_bias):
    return pl.pallas_call(
        _bias_tiles_kernel,
        out_shape=jax.ShapeDtypeStruct((ATTN_HEADS, 2, MOBA_BLOCK, MOBA_BLOCK), F32),
        grid=(ATTN_HEADS,),
        in_specs=[pl.BlockSpec(memory_space=pltpu.SMEM)],
        out_specs=pl.BlockSpec((1, 2, MOBA_BLOCK, MOBA_BLOCK), lambda h: (h, 0, 0, 0)),
        compiler_params=pltpu.CompilerParams(dimension_semantics=("arbitrary",)),
        name="bias_tiles",
    )(rel_bias)


def _moba_kernel(far_ref, q_ref, k_ref, v_ref, d_ref, o_ref, qa_ref, ka_ref, s_ref, va_ref,
                 p_ref, *, seq):
    hp = pl.program_id(0)
    blk = MOBA_BLOCK
    nb = seq // blk
    scale = HEAD_DIM ** -0.5
    lane = lax.broadcasted_iota(jnp.int32, (1, LANES), 1)
    head0 = lane < HEAD_DIM

    blk_i = lax.broadcasted_iota(jnp.int32, (MAX_BLOCKS, seq), 0)
    own_i = lax.broadcasted_iota(jnp.int32, (MAX_BLOCKS, seq), 1) // blk
    row_blk = lax.broadcasted_iota(jnp.int32, (seq, LANES), 0) // blk
    lane_full = lax.broadcasted_iota(jnp.int32, (seq, LANES), 1)

    avg = jnp.where(blk_i == own_i, 1.0 / blk, 0.0).astype(BF16)
    kmean = _dot(avg, k_ref[...])
    km_hi = kmean.astype(BF16).astype(F32)
    km_mid = (kmean - km_hi).astype(BF16).astype(F32)
    km_lo = kmean - km_hi - km_mid
    gate_lhs = jnp.concatenate(
        [jnp.where(head0 if hh == 0 else ~head0, part, 0.0)
         for hh in range(2) for part in (km_hi, km_mid, km_lo)], axis=0).astype(BF16)
    gates = lax.dot_general(gate_lhs, q_ref[...], _NT, preferred_element_type=F32)
    q_scaled = (q_ref[...].astype(F32) * scale).astype(BF16)
    sel_row = lax.broadcasted_iota(jnp.int32, (2 * MAX_BLOCKS, LANES), 0)
    sel_lane = lax.broadcasted_iota(jnp.int32, (2 * MAX_BLOCKS, LANES), 1)

    for hh in range(2):
        mine = head0 if hh == 0 else ~head0
        g0 = 3 * MAX_BLOCKS * hh
        gate = (gates[g0:g0 + MAX_BLOCKS] + gates[g0 + MAX_BLOCKS:g0 + 2 * MAX_BLOCKS]
                + gates[g0 + 2 * MAX_BLOCKS:g0 + 3 * MAX_BLOCKS])
        cand = blk_i < own_i
        g = jnp.where(cand, gate, NEG)
        rank = jnp.zeros((MAX_BLOCKS, seq), jnp.int32)
        for m in range(MAX_BLOCKS):
            gm = g[m:m + 1, :]
            beats = (gm > g) | ((gm == g) & (blk_i > m))
            rank = rank + beats.astype(jnp.int32)
        keep = (cand & (rank < MOBA_TOPK)) | (blk_i >= own_i)
        pen = jnp.where(keep, 0.0, -PEN_BIG)
        pen_lane0 = (1 - hh) * HEAD_DIM
        pen16 = jnp.concatenate([pen, jnp.zeros_like(pen)], axis=0).astype(BF16)
        place = ((sel_row < MAX_BLOCKS) & (sel_lane == pen_lane0 + sel_row)).astype(BF16)
        pen_t = lax.dot_general(pen16, place, (((0,), (0,)), ((), ())),
                                preferred_element_type=F32)
        qa_ref[hh] = jnp.where(mine, q_scaled, pen_t.astype(BF16))
        onehot = (lane_full == pen_lane0 + row_blk).astype(BF16)
        ka_ref[hh] = jnp.where(mine, k_ref[...], onehot)

    va_ref[:, 0:LANES] = v_ref[...]
    va_ref[:, LANES:2 * LANES] = jnp.ones((seq, LANES), BF16)

    half = blk // 2

    def pass1(own, hh):
        q = qa_ref[hh, own * blk:(own + 1) * blk, :]
        far_bias = far_ref[2 * hp + hh]
        mx = {}
        for n in range(own + 1):
            s = lax.dot_general(q, ka_ref[hh, n * blk:(n + 1) * blk, :], _NT,
                                preferred_element_type=F32)
            is_far = n < own - 1
            if n == own:
                s = s + d_ref[hh, 0]
            elif n == own - 1:
                s = s + d_ref[hh, 1]
            s_ref[hh, n] = s
            folded = jnp.maximum(s[:, :half], s[:, half:])
            mx[is_far] = jnp.maximum(mx[is_far], folded) if is_far in mx else folded
        mx_all = jnp.maximum(mx[False], mx[True] + far_bias) if True in mx else mx[False]
        m = jnp.max(mx_all, axis=1, keepdims=True)
        return m, m - far_bias

    def pass2(own, hh, m, m_far):
        for n in range(own + 1):
            p = jnp.exp(s_ref[hh, n] - (m_far if n < own - 1 else m))
            p_ref[hh, :, n * blk:(n + 1) * blk] = p.astype(BF16)
        keys = (own + 1) * blk
        res = _dot(p_ref[hh, :, 0:keys], va_ref[0:keys, :])
        return res[:, 0:LANES] / res[:, LANES:2 * LANES]

    items = [(own, hh) for own in range(nb) for hh in range(2)]
    stats = pass1(*items[0])
    outs = []
    for i, (own, hh) in enumerate(items):
        next_stats = pass1(*items[i + 1]) if i + 1 < len(items) else None
        outs.append(pass2(own, hh, *stats))
        stats = next_stats
        if hh == 1:
            o_ref[own * blk:(own + 1) * blk, :] = jnp.where(head0, *outs).astype(o_ref.dtype)
            outs = []


def _moba(za, bias_tiles, far_bias, batch, seq):
    n_pairs = ATTN_HEADS // 2
    assert seq % MOBA_BLOCK == 0 and seq // MOBA_BLOCK <= MAX_BLOCKS
    return pl.pallas_call(
        functools.partial(_moba_kernel, seq=seq),
        out_shape=jax.ShapeDtypeStruct((batch * seq, ATTN_DIM), BF16),
        grid=(n_pairs, batch),
        in_specs=[pl.BlockSpec(memory_space=pltpu.SMEM),
                  pl.BlockSpec((seq, LANES), lambda hp, b: (b, hp)),
                  pl.BlockSpec((seq, LANES), lambda hp, b: (b, n_pairs + hp)),
                  pl.BlockSpec((seq, LANES), lambda hp, b: (b, 2 * n_pairs + hp)),
                  pl.BlockSpec((2, 2, MOBA_BLOCK, MOBA_BLOCK), lambda hp, b: (hp, 0, 0, 0))],
        out_specs=pl.BlockSpec((seq, LANES), lambda hp, b: (b, hp)),
        scratch_shapes=[pltpu.VMEM((2, seq, LANES), BF16),
                        pltpu.VMEM((2, seq, LANES), BF16),
                        pltpu.VMEM((2, seq // MOBA_BLOCK, MOBA_BLOCK, MOBA_BLOCK), F32),
                        pltpu.VMEM((seq, 2 * LANES), BF16),
                        pltpu.VMEM((2, MOBA_BLOCK, seq), BF16)],
        compiler_params=pltpu.CompilerParams(
            dimension_semantics=("arbitrary", "arbitrary"), vmem_limit_bytes=VMEM_LIMIT),
        name="moba",
    )(far_bias, za, za, za, bias_tiles)


def _out_ffn_kernel(x_ref, ym_ref, yc_ref, ya_ref, wo_ref, g2_ref, wup_ref, cw_ref, cb_ref,
                    wdn_ref, fg_ref, o_ref, hn_ref, acc_ref, act_ref, tail_ref,
                    *, layer, tiles_per_seq, final_norm):
    tm = TM_FFN
    row = slice(layer, layer + 1)
    first = (pl.program_id(0) % tiles_per_seq) == 0
    y = jnp.concatenate([ym_ref[...], yc_ref[...], ya_ref[...]], axis=1)
    x1 = x_ref[...] + _dot(y, wo_ref[...])
    hn_ref[...] = _rms(x1, g2_ref[row, :]).astype(BF16)
    acc_ref[...] = x1
    row8 = lax.broadcasted_iota(jnp.int32, (SUBLANES, FFN_CHUNK), 0)

    def up_proj(c):
        return [_dot(hn_ref[...], wup_ref[:, part * D_FFP + c * FFN_CHUNK:
                                          part * D_FFP + (c + 1) * FFN_CHUNK]) for part in range(2)]

    u_next = up_proj(0)
    for c in range(N_FFN_CHUNKS):
        u_pair = u_next
        if c + 1 < N_FFN_CHUNKS:
            u_next = up_proj(c + 1)
        conv = []
        for part in range(2):
            c0 = part * D_FFP + c * FFN_CHUNK
            u = u_pair[part]
            tail = jnp.where(first, 0.0, tail_ref[2 * c + part])
            tail_ref[2 * c + part] = u[tm - SUBLANES:tm, :]
            acc = cb_ref[row, c0:c0 + FFN_CHUNK] + cw_ref[FFN_CONV - 1:FFN_CONV, c0:c0 + FFN_CHUNK] * u
            for d in range(1, FFN_CONV):
                shifted = pltpu.roll(u, d, axis=0)
                top = jnp.where(row8 < d, pltpu.roll(tail, d, axis=0), shifted[0:SUBLANES])
                shifted = jnp.concatenate([top, shifted[SUBLANES:]], axis=0)
                j = FFN_CONV - 1 - d
                acc = acc + cw_ref[j:j + 1, c0:c0 + FFN_CHUNK] * shifted
            conv.append(acc)
        act_ref[:, c * FFN_CHUNK:(c + 1) * FFN_CHUNK] = (
            conv[0] * _sigmoid(conv[0]) * conv[1]).astype(BF16)
        if c + 1 in FFN_DOWN_ENDS:
            g = FFN_DOWN_ENDS.index(c + 1)
            k0 = (FFN_DOWN_ENDS[g - 1] if g else 0) * FFN_CHUNK
            k1 = (c + 1) * FFN_CHUNK
            acc_ref[...] += _dot(act_ref[:, k0:k1], wdn_ref[k0:k1, :])

    out = acc_ref[...]
    if final_norm:
        out = _rms(out, fg_ref[...])
    o_ref[...] = out


def _out_ffn(x2d, ym, yc, ya, wo, g2, wup, cw, cb, wdn, fg, layer, seq, final_norm):
    n = x2d.shape[0]
    tiles_per_seq = seq // TM_FFN

    def const(shape):
        return _layer_spec(shape, layer, pipeline_mode=pl.Buffered(1))

    return pl.pallas_call(
        functools.partial(_out_ffn_kernel, layer=layer, tiles_per_seq=tiles_per_seq,
                          final_norm=final_norm),
        out_shape=jax.ShapeDtypeStruct((n, D_MODEL), F32),
        grid=(n // TM_FFN,),
        in_specs=[pl.BlockSpec((TM_FFN, D_MODEL), lambda i: (i, 0)),
                  pl.BlockSpec((TM_FFN, MLSTM_DIM), lambda i: (i, 0)),
                  pl.BlockSpec((TM_FFN, CONV_DIM), lambda i: (i, 0)),
                  pl.BlockSpec((TM_FFN, ATTN_DIM), lambda i: (i, 0)),
                  const((D_MODEL, D_MODEL)),
                  _const_spec((DEPTH, D_MODEL)),
                  const((D_MODEL, 2 * D_FFP)),
                  const((FFN_CONV, 2 * D_FFP)),
                  _const_spec((DEPTH, 2 * D_FFP)),
                  const((D_FFP, D_MODEL)),
                  _const_spec((1, D_MODEL))],
        out_specs=pl.BlockSpec((TM_FFN, D_MODEL), lambda i: (i, 0)),
        scratch_shapes=[pltpu.VMEM((TM_FFN, D_MODEL), BF16),
                        pltpu.VMEM((TM_FFN, D_MODEL), F32),
                        pltpu.VMEM((TM_FFN, D_FFP), BF16),
                        pltpu.VMEM((2 * N_FFN_CHUNKS, SUBLANES, FFN_CHUNK), F32)],
        compiler_params=pltpu.CompilerParams(
            dimension_semantics=("arbitrary",), vmem_limit_bytes=VMEM_LIMIT),
        name="out_ffn",
    )(x2d, ym, yc, ya, wo, g2, wup, cw, cb, wdn, fg)


def _pad_last(a, width):
    return jnp.pad(a, [(0, 0)] * (a.ndim - 1) + [(0, width - a.shape[-1])])


def _move_cols_kernel(x_ref, o_ref, *, moves):
    end = 0
    for src, dst, width in moves:
        if dst > end:
            o_ref[:, end:dst] = jnp.zeros((o_ref.shape[0], dst - end), o_ref.dtype)
        o_ref[:, dst:dst + width] = x_ref[:, src:src + width].astype(o_ref.dtype)
        end = dst + width
    if end < o_ref.shape[1]:
        o_ref[:, end:] = jnp.zeros((o_ref.shape[0], o_ref.shape[1] - end), o_ref.dtype)


def _move_cols(w, moves, out_cols):
    depth, rows, cols = w.shape
    return pl.pallas_call(
        functools.partial(_move_cols_kernel, moves=tuple(moves)),
        out_shape=jax.ShapeDtypeStruct((depth, rows, out_cols), BF16),
        grid=(depth, rows // PREP_ROWS),
        in_specs=[pl.BlockSpec((None, PREP_ROWS, cols), lambda l, i: (l, i, 0))],
        out_specs=pl.BlockSpec((None, PREP_ROWS, out_cols), lambda l, i: (l, i, 0)),
        compiler_params=pltpu.CompilerParams(
            dimension_semantics=("arbitrary", "arbitrary"), vmem_limit_bytes=VMEM_LIMIT),
        name="move_cols",
    )(w)


def _w_in_rows_kernel(x_ref, o_ref):
    g0 = 4 * MLSTM_DIM
    g1 = g0 + 2 * MLSTM_HEADS
    cols = x_ref.shape[1]
    o_ref[0:g0, :] = x_ref[0:g0, :].astype(o_ref.dtype)
    gate_tile = jnp.concatenate([x_ref[g0:g1, :], jnp.zeros((GATE_PAD - (g1 - g0), cols), F32)], axis=0)
    o_ref[g0:ZM_W, :] = gate_tile.astype(o_ref.dtype)
    o_ref[ZM_W:P_W, :] = x_ref[g1:, :].astype(o_ref.dtype)


def _prep_w_in(w_in):
    depth, d_in, p_in = w_in.shape
    return pl.pallas_call(
        _w_in_rows_kernel,
        out_shape=jax.ShapeDtypeStruct((depth, P_W, d_in), BF16),
        grid=(depth, d_in // PREP_ROWS),
        in_specs=[pl.BlockSpec((None, p_in, PREP_ROWS), lambda l, i: (l, 0, i))],
        out_specs=pl.BlockSpec((None, P_W, PREP_ROWS), lambda l, i: (l, 0, i)),
        compiler_params=pltpu.CompilerParams(
            dimension_semantics=("arbitrary", "arbitrary"), vmem_limit_bytes=VMEM_LIMIT),
        name="w_in_rows",
    )(jnp.swapaxes(w_in, 1, 2))


def _prep_w_up(w_up):
    return _move_cols(w_up, [(0, 0, D_FF), (D_FF, D_FFP, D_FF)], 2 * D_FFP)


def _prep_ffn_cols(a):
    return jnp.concatenate([_pad_last(a[..., :D_FF], D_FFP), _pad_last(a[..., D_FF:], D_FFP)],
                           axis=-1)


def _pad_rows_kernel(x_ref, o_ref):
    rows = x_ref.shape[0]
    o_ref[0:rows, :] = x_ref[...].astype(o_ref.dtype)
    o_ref[rows:, :] = jnp.zeros((o_ref.shape[0] - rows, o_ref.shape[1]), o_ref.dtype)


def _prep_w_down(w_down):
    depth, rows, cols = w_down.shape
    return pl.pallas_call(
        _pad_rows_kernel,
        out_shape=jax.ShapeDtypeStruct((depth, D_FFP, cols), BF16),
        grid=(depth, cols // PREP_ROWS),
        in_specs=[pl.BlockSpec((None, rows, PREP_ROWS), lambda l, i: (l, 0, i))],
        out_specs=pl.BlockSpec((None, D_FFP, PREP_ROWS), lambda l, i: (l, 0, i)),
        compiler_params=pltpu.CompilerParams(
            dimension_semantics=("arbitrary", "arbitrary"), vmem_limit_bytes=VMEM_LIMIT),
        name="pad_rows",
    )(w_down)


def _prep_gate_bias(ig_b, fg_b):
    return _pad_last(jnp.concatenate([ig_b, fg_b], axis=-1), GATE_PAD)


def kernel(x, norm1_g, w_in, mlstm_qk_conv_w, mlstm_qk_conv_b, mlstm_ig_b, mlstm_fg_b, mlstm_head_g, conf_dw_w, conf_dw_b, conf_ln_g, conf_ln_b, rel_bias, w_out, norm2_g, ffn_w_up, ffn_conv_w, ffn_conv_b, ffn_w_down, final_g):
    batch, seq, _ = x.shape
    bias_tiles = _bias_tiles(rel_bias)
    far_bias = rel_bias[REL_BUCKETS - 1]
    w_in_r = _prep_w_in(w_in)
    gate_bias = _prep_gate_bias(mlstm_ig_b, mlstm_fg_b)
    w_out_b = w_out.astype(BF16)
    w_up_r = _prep_w_up(ffn_w_up)
    f_w = _prep_ffn_cols(ffn_conv_w)
    f_b = _prep_ffn_cols(ffn_conv_b)
    w_down_r = _prep_w_down(ffn_w_down)
    x2d = x.reshape(batch * seq, D_MODEL)
    for l in range(DEPTH):
        zm, za, yc = _in_proj(x2d, norm1_g, w_in_r, conf_dw_w, conf_dw_b, conf_ln_g, conf_ln_b,
                              l, seq)
        ym = _mlstm(zm, mlstm_qk_conv_w, mlstm_qk_conv_b, gate_bias, mlstm_head_g, l, batch, seq)
        ya = _moba(za, bias_tiles, far_bias, batch, seq)
        x2d = _out_ffn(x2d, ym, yc, ya, w_out_b, norm2_g, w_up_r, f_w, f_b, w_down_r,
                       final_g[None, :], l, seq, l == DEPTH - 1)
    return x2d.reshape(batch, seq, D_MODEL)
```

```python
import functools
import math

import numpy as np
import jax
import jax.numpy as jnp
from jax import lax
from jax.experimental import pallas as pl
from jax.experimental.pallas import tpu as pltpu

F32 = jnp.float32
BF16 = jnp.bfloat16
HIGHEST = lax.Precision.HIGHEST

D_MODEL = 1024
DEPTH = 2
HEAD_DIM = 64
MLSTM_DIM = 256
CONV_DIM = 256
ATTN_DIM = 512
MLSTM_HEADS = MLSTM_DIM // HEAD_DIM
ATTN_HEADS = ATTN_DIM // HEAD_DIM
QK_CONV = 4
CONF_KERNEL = 31
MOBA_BLOCK = 256
MOBA_TOPK = 3
REL_BUCKETS = 32
REL_MAX_DIST = 128
D_FF = 2752
FFN_CONV = 3
EPS = 1e-6
NEG = -1e30

LANES = 128
SUBLANES = 8
BF16_SUBLANES = 16
VMEM_LIMIT = 56 * 1024 * 1024

GATE_PAD = LANES
ZM_W = 4 * MLSTM_DIM + GATE_PAD
ZC_W = 2 * CONV_DIM
ZA_W = 3 * ATTN_DIM
P_W = ZM_W + ZC_W + ZA_W
D_FFP = -(-D_FF // 256) * 256
FFN_CHUNK = 256
N_FFN_CHUNKS = D_FFP // FFN_CHUNK
FFN_DOWN_ENDS = (4, 8, N_FFN_CHUNKS - 1, N_FFN_CHUNKS)

TM_IN = 512
TM_FFN = 1024
PREP_ROWS = 256
MLSTM_L = 256
MLSTM_STATE_ROWS = 2 * HEAD_DIM + BF16_SUBLANES
CONF_ROWS = 64
CONF_PAD = 32
MAX_BLOCKS = SUBLANES
PEN_BIG = 2.0 ** 100

_NT = (((1,), (1,)), ((), ()))


def _t5_saturation_distance():
    n = np.arange(1, 4 * MOBA_BLOCK, dtype=np.float32)
    max_exact = REL_BUCKETS // 2
    large = max_exact + (np.log(n / max_exact) / math.log(REL_MAX_DIST / max_exact)
                         * (REL_BUCKETS - max_exact)).astype(np.int32)
    bucket = np.where(n < max_exact, n.astype(np.int32), np.minimum(large, REL_BUCKETS - 1))
    not_last = np.nonzero(bucket != REL_BUCKETS - 1)[0]
    return int(n[not_last[-1]]) + 1


assert _t5_saturation_distance() <= MOBA_BLOCK + 1


def _sigmoid(x):
    return 1.0 / (1.0 + jnp.exp2(x * (-math.log2(math.e))))


def _log_sigmoid(x):
    return jnp.minimum(x, 0.0) - jnp.log1p(jnp.exp(-jnp.abs(x)))


def _rms(xf, g):
    return xf * lax.rsqrt(jnp.mean(xf * xf, axis=-1, keepdims=True) + EPS) * g


def _dot(a, b, **kw):
    return jnp.dot(a, b, preferred_element_type=F32, **kw)


def _const_spec(shape):
    nd = len(shape)
    return pl.BlockSpec(shape, lambda *_: (0,) * nd)


def _layer_spec(shape, layer, **kw):
    nd = len(shape)
    return pl.BlockSpec((None,) + tuple(shape), lambda *_: (layer,) + (0,) * nd, **kw)


def _conformer_tile(win, w_ref, bias, ln_g, ln_b):
    win_rows = CONF_ROWS + CONF_PAD
    acc = jnp.broadcast_to(bias, (CONF_ROWS, CONV_DIM))
    for r in range(SUBLANES):
        rolled = win if r == 0 else pltpu.roll(win, win_rows - r, axis=0)
        for j in range(CONF_KERNEL):
            off = CONF_PAD - (CONF_KERNEL - 1) + j
            if off % SUBLANES == r:
                base = off - r
                acc = acc + w_ref[j:j + 1, :] * rolled[base:base + CONF_ROWS]
    mu = jnp.mean(acc, axis=-1, keepdims=True)
    d = acc - mu
    var = jnp.mean(d * d, axis=-1, keepdims=True)
    y = d * lax.rsqrt(var + EPS) * ln_g + ln_b
    return y * _sigmoid(y)


def _in_proj_kernel(x_ref, g_ref, w_ref, cw_ref, cb_ref, lg_ref, lb_ref, zm_ref, za_ref, yc_ref,
                    up_ref, *, layer, tiles_per_seq):
    tm = TM_IN
    row = slice(layer, layer + 1)
    first = (pl.program_id(0) % tiles_per_seq) == 0
    h = _rms(x_ref[...], g_ref[row, :]).astype(BF16)

    def proj(c0, c1):
        return lax.dot_general(h, w_ref[c0:c1, :], _NT, preferred_element_type=F32)

    zc = proj(ZM_W, ZM_W + ZC_W)
    up_ref[0:CONF_PAD, :] = jnp.where(first, 0.0, up_ref[tm:tm + CONF_PAD, :])
    up_ref[CONF_PAD:CONF_PAD + tm, :] = zc[:, 0:CONV_DIM] * _sigmoid(zc[:, CONV_DIM:ZC_W])
    zm_ref[...] = proj(0, ZM_W)
    za_ref[...] = proj(ZM_W + ZC_W, P_W).astype(BF16)
    bias, ln_g, ln_b = cb_ref[row, :], lg_ref[row, :], lb_ref[row, :]
    for t in range(tm // CONF_ROWS):
        r0 = t * CONF_ROWS
        win = up_ref[r0:r0 + CONF_ROWS + CONF_PAD, :]
        yc_ref[r0:r0 + CONF_ROWS, :] = _conformer_tile(win, cw_ref, bias, ln_g, ln_b).astype(yc_ref.dtype)


def _in_proj(x2d, g, w_r, conf_w, conf_b, conf_ln_g, conf_ln_b, layer, seq):
    n = x2d.shape[0]
    return pl.pallas_call(
        functools.partial(_in_proj_kernel, layer=layer, tiles_per_seq=seq // TM_IN),
        out_shape=(jax.ShapeDtypeStruct((n, ZM_W), F32),
                   jax.ShapeDtypeStruct((n, ZA_W), BF16),
                   jax.ShapeDtypeStruct((n, CONV_DIM), BF16)),
        grid=(n // TM_IN,),
        in_specs=[pl.BlockSpec((TM_IN, D_MODEL), lambda i: (i, 0)),
                  _const_spec((DEPTH, D_MODEL)),
                  _layer_spec((P_W, D_MODEL), layer),
                  _layer_spec((CONF_KERNEL, CONV_DIM), layer),
                  _const_spec((DEPTH, CONV_DIM)),
                  _const_spec((DEPTH, CONV_DIM)),
                  _const_spec((DEPTH, CONV_DIM))],
        out_specs=(pl.BlockSpec((TM_IN, ZM_W), lambda i: (i, 0)),
                   pl.BlockSpec((TM_IN, ZA_W), lambda i: (i, 0)),
                   pl.BlockSpec((TM_IN, CONV_DIM), lambda i: (i, 0))),
        scratch_shapes=[pltpu.VMEM((CONF_PAD + TM_IN, CONV_DIM), F32)],
        compiler_params=pltpu.CompilerParams(
            dimension_semantics=("arbitrary",), vmem_limit_bytes=VMEM_LIMIT),
        name="in_proj",
    )(x2d, g, w_r, conf_w, conf_b, conf_ln_g, conf_ln_b)


def _mlstm_kernel(zm_ref, cw_ref, cb_ref, gb_ref, hg_ref, o_ref, g_ref, gt_ref, *, layer, seq):
    L = MLSTM_L
    n_chunks = seq // L
    nh = MLSTM_HEADS
    scale = HEAD_DIM ** -0.5
    lane = lax.broadcasted_iota(jnp.int32, (1, LANES), 1)
    head0 = lane < HEAD_DIM

    row = slice(layer, layer + 1)
    gates = zm_ref[:, 4 * MLSTM_DIM:ZM_W] + gb_ref[row, :]
    is_f = (lane >= nh) & (lane < 2 * nh)
    gates = jnp.where(is_f, _log_sigmoid(gates), gates)
    g_ref[...] = gates
    gt_ref[...] = gates.T[0:SUBLANES, :]

    ri = lax.broadcasted_iota(jnp.int32, (L, L), 0)
    ci = lax.broadcasted_iota(jnp.int32, (L, L), 1)
    visible = ri <= ci
    tri = (ci <= ri).astype(F32)
    tri_t = visible.astype(F32)
    srow = lax.broadcasted_iota(jnp.int32, (MLSTM_STATE_ROWS, LANES), 0)
    scol = lax.broadcasted_iota(jnp.int32, (MLSTM_STATE_ROWS, LANES), 1)
    srow_head = jnp.where(srow < LANES, srow // HEAD_DIM, srow - LANES)
    state_mask = (scol // HEAD_DIM) == srow_head
    row_is_head0 = srow_head[:, 0:1] == 0
    vrow_is_head0 = lax.broadcasted_iota(jnp.int32, (LANES, 1), 0) < HEAD_DIM
    extra_row = lax.broadcasted_iota(jnp.int32, (MLSTM_STATE_ROWS - LANES, 1), 0)
    ones_rows = jnp.ones((MLSTM_STATE_ROWS - LANES, L), BF16)
    cb = cb_ref[row, :]
    hg = hg_ref[row, :]

    def chunk(c, carry):
        states, m_prev = carry
        r0 = pl.multiple_of(c * L, L)
        gc = g_ref[pl.ds(r0, L), :]
        gtc = gt_ref[:, pl.ds(r0, L)]
        bcols = _dot(tri, gc, precision=HIGHEST)
        brows = _dot(gtc, tri_t, precision=HIGHEST)
        e_rows = gtc[0:nh] - brows[nh:2 * nh]

        xcur = zm_ref[pl.ds(r0, L), 0:2 * MLSTM_DIM]
        pr = pl.multiple_of(jnp.maximum(r0 - SUBLANES, 0), SUBLANES)
        xprev = jnp.where(c > 0, zm_ref[pl.ds(pr, SUBLANES), 0:2 * MLSTM_DIM], 0.0)
        xcat = jnp.concatenate([xprev, xcur], axis=0)
        y = cb
        for j in range(QK_CONV):
            off = SUBLANES - (QK_CONV - 1) + j
            y = y + cw_ref[j:j + 1, :] * xcat[off:off + L]
        qk = y * _sigmoid(y)

        new_states = []
        new_m = []
        for p in range(2):
            lo = p * LANES
            q_t = (qk[:, lo:lo + LANES] * scale).T.astype(BF16)
            k_b = qk[:, MLSTM_DIM + lo:MLSTM_DIM + lo + LANES].astype(BF16)
            v_t = zm_ref[pl.ds(r0, L), 2 * MLSTM_DIM + lo:2 * MLSTM_DIM + lo + LANES].T
            v_tb = v_t.astype(BF16)
            inter = _dot(states[p].astype(BF16), q_t)
            hn_t, wks, decays = [], [], []
            for hh in range(2):
                h = 2 * p + hh
                mp = m_prev[h]
                e_col = gc[:, h:h + 1] - bcols[:, nh + h:nh + h + 1]
                e_row = e_rows[h:h + 1]
                b_row = brows[nh + h:nh + h + 1]
                em = jnp.where(visible, e_col, NEG)
                g = jnp.maximum(mp, jnp.max(em, axis=0, keepdims=True))
                w_t = jnp.exp(em - g)
                k_h = jnp.where(head0 if hh == 0 else ~head0, k_b, jnp.zeros_like(k_b))
                s_t = _dot(k_h, q_t)
                lhs = jnp.concatenate(
                    [v_tb[hh * HEAD_DIM:(hh + 1) * HEAD_DIM], ones_rows], axis=0)
                r = _dot(lhs, (s_t * w_t).astype(BF16))
                a = jnp.exp(mp - g)
                num = a * inter[hh * HEAD_DIM:(hh + 1) * HEAD_DIM] + r[0:HEAD_DIM]
                den = a * inter[LANES + hh:LANES + hh + 1] + r[HEAD_DIM:HEAD_DIM + 1]
                hv = num * (1.0 / jnp.maximum(jnp.abs(den), jnp.exp(-(b_row + g))))
                mu = jnp.mean(hv, axis=0, keepdims=True)
                d = hv - mu
                var = jnp.mean(d * d, axis=0, keepdims=True)
                hn_t.append(d * lax.rsqrt(var + EPS))
                g_last = jnp.maximum(mp, jnp.max(e_row, axis=1, keepdims=True))
                wks.append(jnp.exp(e_row - g_last))
                decays.append(jnp.exp(mp - g_last))
                new_m.append(b_row[:, L - 1:L] + g_last)
            vw = v_t * jnp.where(vrow_is_head0, wks[0], wks[1])
            extra = jnp.where(extra_row == 0, wks[0], jnp.where(extra_row == 1, wks[1], 0.0))
            upd = _dot(jnp.concatenate([vw, extra], axis=0).astype(BF16), k_b)
            decay_rows = jnp.where(row_is_head0, decays[0], decays[1])
            new_states.append(decay_rows * states[p] + jnp.where(state_mask, upd, 0.0))

            hn = jnp.concatenate(hn_t, axis=0).T
            og = zm_ref[pl.ds(r0, L), 3 * MLSTM_DIM + lo:3 * MLSTM_DIM + lo + LANES]
            o_ref[pl.ds(r0, L), lo:lo + LANES] = (hn * hg[:, lo:lo + LANES] * _sigmoid(og)).astype(o_ref.dtype)
        return tuple(new_states), tuple(new_m)

    init = (tuple(jnp.zeros((MLSTM_STATE_ROWS, LANES), F32) for _ in range(2)),
            tuple(jnp.full((1, 1), NEG, F32) for _ in range(nh)))
    lax.fori_loop(0, n_chunks, chunk, init, unroll=True)


def _mlstm(zm, cw, cb, gate_bias, head_g, layer, batch, seq):
    return pl.pallas_call(
        functools.partial(_mlstm_kernel, layer=layer, seq=seq),
        out_shape=jax.ShapeDtypeStruct((batch * seq, MLSTM_DIM), BF16),
        grid=(batch,),
        in_specs=[pl.BlockSpec((seq, ZM_W), lambda b: (b, 0)),
                  _layer_spec((QK_CONV, 2 * MLSTM_DIM), layer),
                  _const_spec((DEPTH, 2 * MLSTM_DIM)),
                  _const_spec((DEPTH, GATE_PAD)),
                  _const_spec((DEPTH, MLSTM_DIM))],
        out_specs=pl.BlockSpec((seq, MLSTM_DIM), lambda b: (b, 0)),
        scratch_shapes=[pltpu.VMEM((seq, GATE_PAD), F32),
                        pltpu.VMEM((SUBLANES, seq), F32)],
        compiler_params=pltpu.CompilerParams(
            dimension_semantics=("arbitrary",), vmem_limit_bytes=VMEM_LIMIT),
        name="mlstm",
    )(zm, cw, cb, gate_bias, head_g)


def _bias_tiles_kernel(rb_ref, o_ref):
    h = pl.program_id(0)
    i = lax.broadcasted_iota(jnp.int32, (MOBA_BLOCK, MOBA_BLOCK), 0)
    j = lax.broadcasted_iota(jnp.int32, (MOBA_BLOCK, MOBA_BLOCK), 1)
    max_exact = REL_BUCKETS // 2
    for t in range(2):
        dist = i - j + t * MOBA_BLOCK
        n = jnp.maximum(dist, 0)
        nf = jnp.maximum(n, 1).astype(F32)
        large = max_exact + (jnp.log(nf / max_exact) / math.log(REL_MAX_DIST / max_exact)
                             * (REL_BUCKETS - max_exact)).astype(jnp.int32)
        large = jnp.minimum(large, REL_BUCKETS - 1)
        bucket = jnp.where(n < max_exact, n, large)
        bias = jnp.zeros((MOBA_BLOCK, MOBA_BLOCK), F32)
        for bk in range(REL_BUCKETS):
            bias = jnp.where(bucket == bk, rb_ref[bk, h], bias)
        if t == 0:
            bias = jnp.where(dist >= 0, bias, NEG)
        o_ref[0, t] = bias


def _bias_tiles(rel_bias):
    return pl.pallas_call(
        _bias_tiles_kernel,
        out_shape=jax.ShapeDtypeStruct((ATTN_HEADS, 2, MOBA_BLOCK, MOBA_BLOCK), F32),
        grid=(ATTN_HEADS,),
        in_specs=[pl.BlockSpec(memory_space=pltpu.SMEM)],
        out_specs=pl.BlockSpec((1, 2, MOBA_BLOCK, MOBA_BLOCK), lambda h: (h, 0, 0, 0)),
        compiler_params=pltpu.CompilerParams(dimension_semantics=("arbitrary",)),
        name="bias_tiles",
    )(rel_bias)


def _moba_kernel(far_ref, q_ref, k_ref, v_ref, d_ref, o_ref, qa_ref, ka_ref, s_ref, va_ref,
                 p_ref, *, seq):
    hp = pl.program_id(0)
    blk = MOBA_BLOCK
    nb = seq // blk
    scale = HEAD_DIM ** -0.5
    lane = lax.broadcasted_iota(jnp.int32, (1, LANES), 1)
    head0 = lane < HEAD_DIM

    blk_i = lax.broadcasted_iota(jnp.int32, (MAX_BLOCKS, seq), 0)
    own_i = lax.broadcasted_iota(jnp.int32, (MAX_BLOCKS, seq), 1) // blk
    row_blk = lax.broadcasted_iota(jnp.int32, (seq, LANES), 0) // blk
    lane_full = lax.broadcasted_iota(jnp.int32, (seq, LANES), 1)

    avg = jnp.where(blk_i == own_i, 1.0 / blk, 0.0).astype(BF16)
    kmean = _dot(avg, k_ref[...])
    km_hi = kmean.astype(BF16).astype(F32)
    km_mid = (kmean - km_hi).astype(BF16).astype(F32)
    km_lo = kmean - km_hi - km_mid
    gate_lhs = jnp.concatenate(
        [jnp.where(head0 if hh == 0 else ~head0, part, 0.0)
         for hh in range(2) for part in (km_hi, km_mid, km_lo)], axis=0).astype(BF16)
    gates = lax.dot_general(gate_lhs, q_ref[...], _NT, preferred_element_type=F32)
    q_scaled = (q_ref[...].astype(F32) * scale).astype(BF16)
    sel_row = lax.broadcasted_iota(jnp.int32, (2 * MAX_BLOCKS, LANES), 0)
    sel_lane = lax.broadcasted_iota(jnp.int32, (2 * MAX_BLOCKS, LANES), 1)

    for hh in range(2):
        mine = head0 if hh == 0 else ~head0
        g0 = 3 * MAX_BLOCKS * hh
        gate = (gates[g0:g0 + MAX_BLOCKS] + gates[g0 + MAX_BLOCKS:g0 + 2 * MAX_BLOCKS]
                + gates[g0 + 2 * MAX_BLOCKS:g0 + 3 * MAX_BLOCKS])
        cand = blk_i < own_i
        g = jnp.where(cand, gate, NEG)
        rank = jnp.zeros((MAX_BLOCKS, seq), jnp.int32)
        for m in range(MAX_BLOCKS):
            gm = g[m:m + 1, :]
            beats = (gm > g) | ((gm == g) & (blk_i > m))
            rank = rank + beats.astype(jnp.int32)
        keep = (cand & (rank < MOBA_TOPK)) | (blk_i >= own_i)
        pen = jnp.where(keep, 0.0, -PEN_BIG)
        pen_lane0 = (1 - hh) * HEAD_DIM
        pen16 = jnp.concatenate([pen, jnp.zeros_like(pen)], axis=0).astype(BF16)
        place = ((sel_row < MAX_BLOCKS) & (sel_lane == pen_lane0 + sel_row)).astype(BF16)
        pen_t = lax.dot_general(pen16, place, (((0,), (0,)), ((), ())),
                                preferred_element_type=F32)
        qa_ref[hh] = jnp.where(mine, q_scaled, pen_t.astype(BF16))
        onehot = (lane_full == pen_lane0 + row_blk).astype(BF16)
        ka_ref[hh] = jnp.where(mine, k_ref[...], onehot)

    va_ref[:, 0:LANES] = v_ref[...]
    va_ref[:, LANES:2 * LANES] = jnp.ones((seq, LANES), BF16)

    half = blk // 2

    def pass1(own, hh):
        q = qa_ref[hh, own * blk:(own + 1) * blk, :]
        far_bias = far_ref[2 * hp + hh]
        mx = {}
        for n in range(own + 1):
            s = lax.dot_general(q, ka_ref[hh, n * blk:(n + 1) * blk, :], _NT,
                                preferred_element_type=F32)
            is_far = n < own - 1
            if n == own:
                s = s + d_ref[hh, 0]
            elif n == own - 1:
                s = s + d_ref[hh, 1]
            s_ref[hh, n] = s
            folded = jnp.maximum(s[:, :half], s[:, half:])
            mx[is_far] = jnp.maximum(mx[is_far], folded) if is_far in mx else folded
        mx_all = jnp.maximum(mx[False], mx[True] + far_bias) if True in mx else mx[False]
        m = jnp.max(mx_all, axis=1, keepdims=True)
        return m, m - far_bias

    def pass2(own, hh, m, m_far):
        for n in range(own + 1):
            p = jnp.exp(s_ref[hh, n] - (m_far if n < own - 1 else m))
            p_ref[hh, :, n * blk:(n + 1) * blk] = p.astype(BF16)
        keys = (own + 1) * blk
        res = _dot(p_ref[hh, :, 0:keys], va_ref[0:keys, :])
        return res[:, 0:LANES] / res[:, LANES:2 * LANES]

    items = [(own, hh) for own in range(nb) for hh in range(2)]
    stats = pass1(*items[0])
    outs = []
    for i, (own, hh) in enumerate(items):
        next_stats = pass1(*items[i + 1]) if i + 1 < len(items) else None
        outs.append(pass2(own, hh, *stats))
        stats = next_stats
        if hh == 1:
            o_ref[own * blk:(own + 1) * blk, :] = jnp.where(head0, *outs).astype(o_ref.dtype)
            outs = []


def _moba(za, bias_tiles, far_bias, batch, seq):
    n_pairs = ATTN_HEADS // 2
    assert seq % MOBA_BLOCK == 0 and seq // MOBA_BLOCK <= MAX_BLOCKS
    return pl.pallas_call(
        functools.partial(_moba_kernel, seq=seq),
        out_shape=jax.ShapeDtypeStruct((batch * seq, ATTN_DIM), BF16),
        grid=(n_pairs, batch),
        in_specs=[pl.BlockSpec(memory_space=pltpu.SMEM),
                  pl.BlockSpec((seq, LANES), lambda hp, b: (b, hp)),
                  pl.BlockSpec((seq, LANES), lambda hp, b: (b, n_pairs + hp)),
                  pl.BlockSpec((seq, LANES), lambda hp, b: (b, 2 * n_pairs + hp)),
                  pl.BlockSpec((2, 2, MOBA_BLOCK, MOBA_BLOCK), lambda hp, b: (hp, 0, 0, 0))],
        out_specs=pl.BlockSpec((seq, LANES), lambda hp, b: (b, hp)),
        scratch_shapes=[pltpu.VMEM((2, seq, LANES), BF16),
                        pltpu.VMEM((2, seq, LANES), BF16),
                        pltpu.VMEM((2, seq // MOBA_BLOCK, MOBA_BLOCK, MOBA_BLOCK), F32),
                        pltpu.VMEM((seq, 2 * LANES), BF16),
                        pltpu.VMEM((2, MOBA_BLOCK, seq), BF16)],
        compiler_params=pltpu.CompilerParams(
            dimension_semantics=("arbitrary", "arbitrary"), vmem_limit_bytes=VMEM_LIMIT),
        name="moba",
    )(far_bias, za, za, za, bias_tiles)


def _out_ffn_kernel(x_ref, ym_ref, yc_ref, ya_ref, wo_ref, g2_ref, wup_ref, cw_ref, cb_ref,
                    wdn_ref, fg_ref, o_ref, hn_ref, acc_ref, act_ref, tail_ref,
                    *, layer, tiles_per_seq, final_norm):
    tm = TM_FFN
    row = slice(layer, layer + 1)
    first = (pl.program_id(0) % tiles_per_seq) == 0
    y = jnp.concatenate([ym_ref[...], yc_ref[...], ya_ref[...]], axis=1)
    x1 = x_ref[...] + _dot(y, wo_ref[...])
    hn_ref[...] = _rms(x1, g2_ref[row, :]).astype(BF16)
    acc_ref[...] = x1
    row8 = lax.broadcasted_iota(jnp.int32, (SUBLANES, FFN_CHUNK), 0)

    def up_proj(c):
        return [_dot(hn_ref[...], wup_ref[:, part * D_FFP + c * FFN_CHUNK:
                                          part * D_FFP + (c + 1) * FFN_CHUNK]) for part in range(2)]

    u_next = up_proj(0)
    for c in range(N_FFN_CHUNKS):
        u_pair = u_next
        if c + 1 < N_FFN_CHUNKS:
            u_next = up_proj(c + 1)
        conv = []
        for part in range(2):
            c0 = part * D_FFP + c * FFN_CHUNK
            u = u_pair[part]
            tail = jnp.where(first, 0.0, tail_ref[2 * c + part])
            tail_ref[2 * c + part] = u[tm - SUBLANES:tm, :]
            acc = cb_ref[row, c0:c0 + FFN_CHUNK] + cw_ref[FFN_CONV - 1:FFN_CONV, c0:c0 + FFN_CHUNK] * u
            for d in range(1, FFN_CONV):
                shifted = pltpu.roll(u, d, axis=0)
                top = jnp.where(row8 < d, pltpu.roll(tail, d, axis=0), shifted[0:SUBLANES])
                shifted = jnp.concatenate([top, shifted[SUBLANES:]], axis=0)
                j = FFN_CONV - 1 - d
                acc = acc + cw_ref[j:j + 1, c0:c0 + FFN_CHUNK] * shifted
            conv.append(acc)
        act_ref[:, c * FFN_CHUNK:(c + 1) * FFN_CHUNK] = (
            conv[0] * _sigmoid(conv[0]) * conv[1]).astype(BF16)
        if c + 1 in FFN_DOWN_ENDS:
            g = FFN_DOWN_ENDS.index(c + 1)
            k0 = (FFN_DOWN_ENDS[g - 1] if g else 0) * FFN_CHUNK
            k1 = (c + 1) * FFN_CHUNK
            acc_ref[...] += _dot(act_ref[:, k0:k1], wdn_ref[k0:k1, :])

    out = acc_ref[...]
    if final_norm:
        out = _rms(out, fg_ref[...])
    o_ref[...] = out


def _out_ffn(x2d, ym, yc, ya, wo, g2, wup, cw, cb, wdn, fg, layer, seq, final_norm):
    n = x2d.shape[0]
    tiles_per_seq = seq // TM_FFN

    def const(shape):
        return _layer_spec(shape, layer, pipeline_mode=pl.Buffered(1))

    return pl.pallas_call(
        functools.partial(_out_ffn_kernel, layer=layer, tiles_per_seq=tiles_per_seq,
                          final_norm=final_norm),
        out_shape=jax.ShapeDtypeStruct((n, D_MODEL), F32),
        grid=(n // TM_FFN,),
        in_specs=[pl.BlockSpec((TM_FFN, D_MODEL), lambda i: (i, 0)),
                  pl.BlockSpec((TM_FFN, MLSTM_DIM), lambda i: (i, 0)),
                  pl.BlockSpec((TM_FFN, CONV_DIM), lambda i: (i, 0)),
                  pl.BlockSpec((TM_FFN, ATTN_DIM), lambda i: (i, 0)),
                  const((D_MODEL, D_MODEL)),
                  _const_spec((DEPTH, D_MODEL)),
                  const((D_MODEL, 2 * D_FFP)),
                  const((FFN_CONV, 2 * D_FFP)),
                  _const_spec((DEPTH, 2 * D_FFP)),
                  const((D_FFP, D_MODEL)),
                  _const_spec((1, D_MODEL))],
        out_specs=pl.BlockSpec((TM_FFN, D_MODEL), lambda i: (i, 0)),
        scratch_shapes=[pltpu.VMEM((TM_FFN, D_MODEL), BF16),
                        pltpu.VMEM((TM_FFN, D_MODEL), F32),
                        pltpu.VMEM((TM_FFN, D_FFP), BF16),
                        pltpu.VMEM((2 * N_FFN_CHUNKS, SUBLANES, FFN_CHUNK), F32)],
        compiler_params=pltpu.CompilerParams(
            dimension_semantics=("arbitrary",), vmem_limit_bytes=VMEM_LIMIT),
        name="out_ffn",
    )(x2d, ym, yc, ya, wo, g2, wup, cw, cb, wdn, fg)


def _pad_last(a, width):
    return jnp.pad(a, [(0, 0)] * (a.ndim - 1) + [(0, width - a.shape[-1])])


def _move_cols_kernel(x_ref, o_ref, *, moves):
    end = 0
    for src, dst, width in moves:
        if dst > end:
            o_ref[:, end:dst] = jnp.zeros((o_ref.shape[0], dst - end), o_ref.dtype)
        o_ref[:, dst:dst + width] = x_ref[:, src:src + width].astype(o_ref.dtype)
        end = dst + width
    if end < o_ref.shape[1]:
        o_ref[:, end:] = jnp.zeros((o_ref.shape[0], o_ref.shape[1] - end), o_ref.dtype)


def _move_cols(w, moves, out_cols):
    depth, rows, cols = w.shape
    return pl.pallas_call(
        functools.partial(_move_cols_kernel, moves=tuple(moves)),
        out_shape=jax.ShapeDtypeStruct((depth, rows, out_cols), BF16),
        grid=(depth, rows // PREP_ROWS),
        in_specs=[pl.BlockSpec((None, PREP_ROWS, cols), lambda l, i: (l, i, 0))],
        out_specs=pl.BlockSpec((None, PREP_ROWS, out_cols), lambda l, i: (l, i, 0)),
        compiler_params=pltpu.CompilerParams(
            dimension_semantics=("arbitrary", "arbitrary"), vmem_limit_bytes=VMEM_LIMIT),
        name="move_cols",
    )(w)


def _w_in_rows_kernel(x_ref, o_ref):
    g0 = 4 * MLSTM_DIM
    g1 = g0 + 2 * MLSTM_HEADS
    cols = x_ref.shape[1]
    o_ref[0:g0, :] = x_ref[0:g0, :].astype(o_ref.dtype)
    gate_tile = jnp.concatenate([x_ref[g0:g1, :], jnp.zeros((GATE_PAD - (g1 - g0), cols), F32)], axis=0)
    o_ref[g0:ZM_W, :] = gate_tile.astype(o_ref.dtype)
    o_ref[ZM_W:P_W, :] = x_ref[g1:, :].astype(o_ref.dtype)


def _prep_w_in(w_in):
    depth, d_in, p_in = w_in.shape
    return pl.pallas_call(
        _w_in_rows_kernel,
        out_shape=jax.ShapeDtypeStruct((depth, P_W, d_in), BF16),
        grid=(depth, d_in // PREP_ROWS),
        in_specs=[pl.BlockSpec((None, p_in, PREP_ROWS), lambda l, i: (l, 0, i))],
        out_specs=pl.BlockSpec((None, P_W, PREP_ROWS), lambda l, i: (l, 0, i)),
        compiler_params=pltpu.CompilerParams(
            dimension_semantics=("arbitrary", "arbitrary"), vmem_limit_bytes=VMEM_LIMIT),
        name="w_in_rows",
    )(jnp.swapaxes(w_in, 1, 2))


def _prep_w_up(w_up):
    return _move_cols(w_up, [(0, 0, D_FF), (D_FF, D_FFP, D_FF)], 2 * D_FFP)


def _prep_ffn_cols(a):
    return jnp.concatenate([_pad_last(a[..., :D_FF], D_FFP), _pad_last(a[..., D_FF:], D_FFP)],
                           axis=-1)


def _pad_rows_kernel(x_ref, o_ref):
    rows = x_ref.shape[0]
    o_ref[0:rows, :] = x_ref[...].astype(o_ref.dtype)
    o_ref[rows:, :] = jnp.zeros((o_ref.shape[0] - rows, o_ref.shape[1]), o_ref.dtype)


def _prep_w_down(w_down):
    depth, rows, cols = w_down.shape
    return pl.pallas_call(
        _pad_rows_kernel,
        out_shape=jax.ShapeDtypeStruct((depth, D_FFP, cols), BF16),
        grid=(depth, cols // PREP_ROWS),
        in_specs=[pl.BlockSpec((None, rows, PREP_ROWS), lambda l, i: (l, 0, i))],
        out_specs=pl.BlockSpec((None, D_FFP, PREP_ROWS), lambda l, i: (l, 0, i)),
        compiler_params=pltpu.CompilerParams(
            dimension_semantics=("arbitrary", "arbitrary"), vmem_limit_bytes=VMEM_LIMIT),
        name="pad_rows",
    )(w_down)


def _prep_gate_bias(ig_b, fg_b):
    return _pad_last(jnp.concatenate([ig_b, fg_b], axis=-1), GATE_PAD)


def kernel(x, norm1_g, w_in, mlstm_qk_conv_w, mlstm_qk_conv_b, mlstm_ig_b, mlstm_fg_b, mlstm_head_g, conf_dw_w, conf_dw_b, conf_ln_g, conf_ln_b, rel_bias, w_out, norm2_g, ffn_w_up, ffn_conv_w, ffn_conv_b, ffn_w_down, final_g):
    batch, seq, _ = x.shape
    bias_tiles = _bias_tiles(rel_bias)
    far_bias = rel_bias[REL_BUCKETS - 1]
    w_in_r = _prep_w_in(w_in)
    gate_bias = _prep_gate_bias(mlstm_ig_b, mlstm_fg_b)
    w_out_b = w_out.astype(BF16)
    w_up_r = _prep_w_up(ffn_w_up)
    f_w = _prep_ffn_cols(ffn_conv_w)
    f_b = _prep_ffn_cols(ffn_conv_b)
    w_down_r = _prep_w_down(ffn_w_down)
    x2d = x.reshape(batch * seq, D_MODEL)
    for l in range(DEPTH):
        zm, za, yc = _in_proj(x2d, norm1_g, w_in_r, conf_dw_w, conf_dw_b, conf_ln_g, conf_ln_b,
                              l, seq)
        ym = _mlstm(zm, mlstm_qk_conv_w, mlstm_qk_conv_b, gate_bias, mlstm_head_g, l, batch, seq)
        ya = _moba(za, bias_tiles, far_bias, batch, seq)
        x2d = _out_ffn(x2d, ym, yc, ya, w_out_b, norm2_g, w_up_r, f_w, f_b, w_down_r,
                       final_g[None, :], l, seq, l == DEPTH - 1)
    return x2d.reshape(batch, seq, D_MODEL)
```

```python
import functools
import math

import numpy as np
import jax
import jax.numpy as jnp
from jax import lax
from jax.experimental import pallas as pl
from jax.experimental.pallas import tpu as pltpu

F32 = jnp.float32
BF16 = jnp.bfloat16
HIGHEST = lax.Precision.HIGHEST

D_MODEL = 1024
DEPTH = 2
HEAD_DIM = 64
MLSTM_DIM = 256
CONV_DIM = 256
ATTN_DIM = 512
MLSTM_HEADS = MLSTM_DIM // HEAD_DIM
ATTN_HEADS = ATTN_DIM // HEAD_DIM
QK_CONV = 4
CONF_KERNEL = 31
MOBA_BLOCK = 256
MOBA_TOPK = 3
REL_BUCKETS = 32
REL_MAX_DIST = 128
D_FF = 2752
FFN_CONV = 3
EPS = 1e-6
NEG = -1e30

LANES = 128
SUBLANES = 8
BF16_SUBLANES = 16
VMEM_LIMIT = 56 * 1024 * 1024

GATE_PAD = LANES
ZM_W = 4 * MLSTM_DIM + GATE_PAD
ZC_W = 2 * CONV_DIM
ZA_W = 3 * ATTN_DIM
P_W = ZM_W + ZC_W + ZA_W
D_FFP = -(-D_FF // 256) * 256
FFN_CHUNK = 256
N_FFN_CHUNKS = D_FFP // FFN_CHUNK
FFN_DOWN_ENDS = (4, 8, N_FFN_CHUNKS - 1, N_FFN_CHUNKS)

TM_IN = 512
TM_FFN = 1024
PREP_ROWS = 256
MLSTM_L = 256
MLSTM_STATE_ROWS = 2 * HEAD_DIM + BF16_SUBLANES
CONF_ROWS = 64
CONF_CHAINS = 1
CONF_PAD = 32
MAX_BLOCKS = SUBLANES
PEN_BIG = 2.0 ** 100

_NT = (((1,), (1,)), ((), ()))


def _t5_saturation_distance():
    n = np.arange(1, 4 * MOBA_BLOCK, dtype=np.float32)
    max_exact = REL_BUCKETS // 2
    large = max_exact + (np.log(n / max_exact) / math.log(REL_MAX_DIST / max_exact)
                         * (REL_BUCKETS - max_exact)).astype(np.int32)
    bucket = np.where(n < max_exact, n.astype(np.int32), np.minimum(large, REL_BUCKETS - 1))
    not_last = np.nonzero(bucket != REL_BUCKETS - 1)[0]
    return int(n[not_last[-1]]) + 1


assert _t5_saturation_distance() <= MOBA_BLOCK + 1


def _sigmoid(x):
    return 1.0 / (1.0 + jnp.exp2(x * (-math.log2(math.e))))


def _log_sigmoid(x):
    return jnp.minimum(x, 0.0) - jnp.log1p(jnp.exp(-jnp.abs(x)))


def _rms(xf, g):
    return xf * lax.rsqrt(jnp.mean(xf * xf, axis=-1, keepdims=True) + EPS) * g


def _dot(a, b, **kw):
    return jnp.dot(a, b, preferred_element_type=F32, **kw)


def _const_spec(shape):
    nd = len(shape)
    return pl.BlockSpec(shape, lambda *_: (0,) * nd)


def _layer_spec(shape, layer, **kw):
    nd = len(shape)
    return pl.BlockSpec((None,) + tuple(shape), lambda *_: (layer,) + (0,) * nd, **kw)


def _conformer_tile(up_ref, r0, w_ref, start, ln_g, ln_b):
    win_rows = CONF_ROWS + CONF_PAD
    halves = []
    for c0 in range(0, CONV_DIM, LANES):
        win = up_ref[r0:r0 + win_rows, c0:c0 + LANES]
        acc = jnp.broadcast_to(start[:, c0:c0 + LANES], (CONF_ROWS, LANES))
        for r in range(SUBLANES):
            rolled = win if r == 0 else pltpu.roll(win, win_rows - r, axis=0)
            for j in range(CONF_KERNEL):
                off = CONF_PAD - (CONF_KERNEL - 1) + j
                if off % SUBLANES == r:
                    base = off - r
                    acc = acc + w_ref[j:j + 1, c0:c0 + LANES] * rolled[base:base + CONF_ROWS]
        halves.append(acc)
    acc = jnp.concatenate(halves, axis=1)
    mu = jnp.mean(acc, axis=-1, keepdims=True)
    d = acc - mu
    var = jnp.mean(d * d, axis=-1, keepdims=True)
    y = d * lax.rsqrt(var + EPS) * ln_g + ln_b
    return y * _sigmoid(y)


def _in_proj_kernel(x_ref, g_ref, w_ref, cw_ref, cb_ref, lg_ref, lb_ref, zm_ref, za_ref, yc_ref,
                    up_ref, *, layer, tiles_per_seq):
    tm = TM_IN
    row = slice(layer, layer + 1)
    first = (pl.program_id(0) % tiles_per_seq) == 0
    h = _rms(x_ref[...], g_ref[row, :]).astype(BF16)

    def proj(c0, c1):
        return lax.dot_general(h, w_ref[c0:c1, :], _NT, preferred_element_type=F32)

    zc = proj(ZM_W, ZM_W + ZC_W)
    up_ref[0:CONF_PAD, :] = jnp.where(first, 0.0, up_ref[tm:tm + CONF_PAD, :])
    up_ref[CONF_PAD:CONF_PAD + tm, :] = zc[:, 0:CONV_DIM] * _sigmoid(zc[:, CONV_DIM:ZC_W])
    zm_ref[...] = proj(0, ZM_W)
    za_ref[...] = proj(ZM_W + ZC_W, P_W).astype(BF16)
    bias, ln_g, ln_b = cb_ref[row, :], lg_ref[row, :], lb_ref[row, :]
    never = pl.program_id(0) < 0
    last = [None] * CONF_CHAINS
    for t in range(tm // CONF_ROWS):
        r0 = t * CONF_ROWS
        prev = last[t % CONF_CHAINS]
        start = bias if prev is None else jnp.where(never, prev[0:1, :], bias)
        y = _conformer_tile(up_ref, r0, cw_ref, start, ln_g, ln_b)
        last[t % CONF_CHAINS] = y
        yc_ref[r0:r0 + CONF_ROWS, :] = y.astype(yc_ref.dtype)


def _in_proj(x2d, g, w_r, conf_w, conf_b, conf_ln_g, conf_ln_b, layer, seq):
    n = x2d.shape[0]
    return pl.pallas_call(
        functools.partial(_in_proj_kernel, layer=layer, tiles_per_seq=seq // TM_IN),
        out_shape=(jax.ShapeDtypeStruct((n, ZM_W), F32),
                   jax.ShapeDtypeStruct((n, ZA_W), BF16),
                   jax.ShapeDtypeStruct((n, CONV_DIM), BF16)),
        grid=(n // TM_IN,),
        in_specs=[pl.BlockSpec((TM_IN, D_MODEL), lambda i: (i, 0)),
                  _const_spec((DEPTH, D_MODEL)),
                  _layer_spec((P_W, D_MODEL), layer),
                  _layer_spec((CONF_KERNEL, CONV_DIM), layer),
                  _const_spec((DEPTH, CONV_DIM)),
                  _const_spec((DEPTH, CONV_DIM)),
                  _const_spec((DEPTH, CONV_DIM))],
        out_specs=(pl.BlockSpec((TM_IN, ZM_W), lambda i: (i, 0)),
                   pl.BlockSpec((TM_IN, ZA_W), lambda i: (i, 0)),
                   pl.BlockSpec((TM_IN, CONV_DIM), lambda i: (i, 0))),
        scratch_shapes=[pltpu.VMEM((CONF_PAD + TM_IN, CONV_DIM), F32)],
        compiler_params=pltpu.CompilerParams(
            dimension_semantics=("arbitrary",), vmem_limit_bytes=VMEM_LIMIT),
        name="in_proj",
    )(x2d, g, w_r, conf_w, conf_b, conf_ln_g, conf_ln_b)


def _mlstm_kernel(zm_ref, cw_ref, cb_ref, gb_ref, hg_ref, o_ref, g_ref, gt_ref, *, layer, seq):
    L = MLSTM_L
    n_chunks = seq // L
    nh = MLSTM_HEADS
    scale = HEAD_DIM ** -0.5
    lane = lax.broadcasted_iota(jnp.int32, (1, LANES), 1)
    head0 = lane < HEAD_DIM

    row = slice(layer, layer + 1)
    gates = zm_ref[:, 4 * MLSTM_DIM:ZM_W] + gb_ref[row, :]
    is_f = (lane >= nh) & (lane < 2 * nh)
    gates = jnp.where(is_f, _log_sigmoid(gates), gates)
    g_ref[...] = gates
    gt_ref[...] = gates.T[0:SUBLANES, :]

    ri = lax.broadcasted_iota(jnp.int32, (L, L), 0)
    ci = lax.broadcasted_iota(jnp.int32, (L, L), 1)
    visible = ri <= ci
    tri = (ci <= ri).astype(F32)
    tri_t = visible.astype(F32)
    srow = lax.broadcasted_iota(jnp.int32, (MLSTM_STATE_ROWS, LANES), 0)
    scol = lax.broadcasted_iota(jnp.int32, (MLSTM_STATE_ROWS, LANES), 1)
    srow_head = jnp.where(srow < LANES, srow // HEAD_DIM, srow - LANES)
    state_mask = (scol // HEAD_DIM) == srow_head
    row_is_head0 = srow_head[:, 0:1] == 0
    vrow_is_head0 = lax.broadcasted_iota(jnp.int32, (LANES, 1), 0) < HEAD_DIM
    extra_row = lax.broadcasted_iota(jnp.int32, (MLSTM_STATE_ROWS - LANES, 1), 0)
    ones_rows = jnp.ones((MLSTM_STATE_ROWS - LANES, L), BF16)
    cb = cb_ref[row, :]
    hg = hg_ref[row, :]

    def chunk(c, carry):
        states, m_prev = carry
        r0 = pl.multiple_of(c * L, L)
        gc = g_ref[pl.ds(r0, L), :]
        gtc = gt_ref[:, pl.ds(r0, L)]
        bcols = _dot(tri, gc, precision=HIGHEST)
        brows = _dot(gtc, tri_t, precision=HIGHEST)
        e_rows = gtc[0:nh] - brows[nh:2 * nh]

        xcur = zm_ref[pl.ds(r0, L), 0:2 * MLSTM_DIM]
        pr = pl.multiple_of(jnp.maximum(r0 - SUBLANES, 0), SUBLANES)
        xprev = jnp.where(c > 0, zm_ref[pl.ds(pr, SUBLANES), 0:2 * MLSTM_DIM], 0.0)
        xcat = jnp.concatenate([xprev, xcur], axis=0)
        y = cb
        for j in range(QK_CONV):
            off = SUBLANES - (QK_CONV - 1) + j
            y = y + cw_ref[j:j + 1, :] * xcat[off:off + L]
        qk = y * _sigmoid(y)

        new_states = []
        new_m = []
        for p in range(2):
            lo = p * LANES
            q_t = (qk[:, lo:lo + LANES] * scale).T.astype(BF16)
            k_b = qk[:, MLSTM_DIM + lo:MLSTM_DIM + lo + LANES].astype(BF16)
            v_t = zm_ref[pl.ds(r0, L), 2 * MLSTM_DIM + lo:2 * MLSTM_DIM + lo + LANES].T
            v_tb = v_t.astype(BF16)
            inter = _dot(states[p].astype(BF16), q_t)
            hn_t, wks, decays = [], [], []
            for hh in range(2):
                h = 2 * p + hh
                mp = m_prev[h]
                e_col = gc[:, h:h + 1] - bcols[:, nh + h:nh + h + 1]
                e_row = e_rows[h:h + 1]
                b_row = brows[nh + h:nh + h + 1]
                em = jnp.where(visible, e_col, NEG)
                g = jnp.maximum(mp, jnp.max(em, axis=0, keepdims=True))
                w_t = jnp.exp(em - g)
                k_h = jnp.where(head0 if hh == 0 else ~head0, k_b, jnp.zeros_like(k_b))
                s_t = _dot(k_h, q_t)
                lhs = jnp.concatenate(
                    [v_tb[hh * HEAD_DIM:(hh + 1) * HEAD_DIM], ones_rows], axis=0)
                r = _dot(lhs, (s_t * w_t).astype(BF16))
                a = jnp.exp(mp - g)
                num = a * inter[hh * HEAD_DIM:(hh + 1) * HEAD_DIM] + r[0:HEAD_DIM]
                den = a * inter[LANES + hh:LANES + hh + 1] + r[HEAD_DIM:HEAD_DIM + 1]
                hv = num * (1.0 / jnp.maximum(jnp.abs(den), jnp.exp(-(b_row + g))))
                mu = jnp.mean(hv, axis=0, keepdims=True)
                d = hv - mu
                var = jnp.mean(d * d, axis=0, keepdims=True)
                hn_t.append(d * lax.rsqrt(var + EPS))
                g_last = jnp.maximum(mp, jnp.max(e_row, axis=1, keepdims=True))
                wks.append(jnp.exp(e_row - g_last))
                decays.append(jnp.exp(mp - g_last))
                new_m.append(b_row[:, L - 1:L] + g_last)
            vw = v_t * jnp.where(vrow_is_head0, wks[0], wks[1])
            extra = jnp.where(extra_row == 0, wks[0], jnp.where(extra_row == 1, wks[1], 0.0))
            upd = _dot(jnp.concatenate([vw, extra], axis=0).astype(BF16), k_b)
            decay_rows = jnp.where(row_is_head0, decays[0], decays[1])
            new_states.append(decay_rows * states[p] + jnp.where(state_mask, upd, 0.0))

            hn = jnp.concatenate(hn_t, axis=0).T
            og = zm_ref[pl.ds(r0, L), 3 * MLSTM_DIM + lo:3 * MLSTM_DIM + lo + LANES]
            o_ref[pl.ds(r0, L), lo:lo + LANES] = (hn * hg[:, lo:lo + LANES] * _sigmoid(og)).astype(o_ref.dtype)
        return tuple(new_states), tuple(new_m)

    init = (tuple(jnp.zeros((MLSTM_STATE_ROWS, LANES), F32) for _ in range(2)),
            tuple(jnp.full((1, 1), NEG, F32) for _ in range(nh)))
    lax.fori_loop(0, n_chunks, chunk, init, unroll=True)


def _mlstm(zm, cw, cb, gate_bias, head_g, layer, batch, seq):
    return pl.pallas_call(
        functools.partial(_mlstm_kernel, layer=layer, seq=seq),
        out_shape=jax.ShapeDtypeStruct((batch * seq, MLSTM_DIM), BF16),
        grid=(batch,),
        in_specs=[pl.BlockSpec((seq, ZM_W), lambda b: (b, 0)),
                  _layer_spec((QK_CONV, 2 * MLSTM_DIM), layer),
                  _const_spec((DEPTH, 2 * MLSTM_DIM)),
                  _const_spec((DEPTH, GATE_PAD)),
                  _const_spec((DEPTH, MLSTM_DIM))],
        out_specs=pl.BlockSpec((seq, MLSTM_DIM), lambda b: (b, 0)),
        scratch_shapes=[pltpu.VMEM((seq, GATE_PAD), F32),
                        pltpu.VMEM((SUBLANES, seq), F32)],
        compiler_params=pltpu.CompilerParams(
            dimension_semantics=("arbitrary",), vmem_limit_bytes=VMEM_LIMIT),
        name="mlstm",
    )(zm, cw, cb, gate_bias, head_g)


def _bias_tiles_kernel(rb_ref, o_ref):
    h = pl.program_id(0)
    i = lax.broadcasted_iota(jnp.int32, (MOBA_BLOCK, MOBA_BLOCK), 0)
    j = lax.broadcasted_iota(jnp.int32, (MOBA_BLOCK, MOBA_BLOCK), 1)
    max_exact = REL_BUCKETS // 2
    for t in range(2):
        dist = i - j + t * MOBA_BLOCK
        n = jnp.maximum(dist, 0)
        nf = jnp.maximum(n, 1).astype(F32)
        large = max_exact + (jnp.log(nf / max_exact) / math.log(REL_MAX_DIST / max_exact)
                             * (REL_BUCKETS - max_exact)).astype(jnp.int32)
        large = jnp.minimum(large, REL_BUCKETS - 1)
        bucket = jnp.where(n < max_exact, n, large)
        bias = jnp.zeros((MOBA_BLOCK, MOBA_BLOCK), F32)
        for bk in range(REL_BUCKETS):
            bias = jnp.where(bucket == bk, rb_ref[bk, h], bias)
        if t == 0:
            bias = jnp.where(dist >= 0, bias, NEG)
        o_ref[0, t] = bias


def _bias_tiles(rel_bias):
    return pl.pallas_call(
        _bias_tiles_kernel,
        out_shape=jax.ShapeDtypeStruct((ATTN_HEADS, 2, MOBA_BLOCK, MOBA_BLOCK), F32),
        grid=(ATTN_HEADS,),
        in_specs=[pl.BlockSpec(memory_space=pltpu.SMEM)],
        out_specs=pl.BlockSpec((1, 2, MOBA_BLOCK, MOBA_BLOCK), lambda h: (h, 0, 0, 0)),
        compiler_params=pltpu.CompilerParams(dimension_semantics=("arbitrary",)),
        name="bias_tiles",
    )(rel_bias)


def _moba_kernel(far_ref, q_ref, k_ref, v_ref, d_ref, o_ref, qa_ref, ka_ref, s_ref, va_ref,
                 p_ref, *, seq):
    hp = pl.program_id(0)
    blk = MOBA_BLOCK
    nb = seq // blk
    scale = HEAD_DIM ** -0.5
    lane = lax.broadcasted_iota(jnp.int32, (1, LANES), 1)
    head0 = lane < HEAD_DIM

    blk_i = lax.broadcasted_iota(jnp.int32, (MAX_BLOCKS, seq), 0)
    own_i = lax.broadcasted_iota(jnp.int32, (MAX_BLOCKS, seq), 1) // blk
    row_blk = lax.broadcasted_iota(jnp.int32, (seq, LANES), 0) // blk
    lane_full = lax.broadcasted_iota(jnp.int32, (seq, LANES), 1)

    avg = jnp.where(blk_i == own_i, 1.0 / blk, 0.0).astype(BF16)
    kmean = _dot(avg, k_ref[...])
    km_hi = kmean.astype(BF16).astype(F32)
    km_mid = (kmean - km_hi).astype(BF16).astype(F32)
    km_lo = kmean - km_hi - km_mid
    gate_lhs = jnp.concatenate(
        [jnp.where(head0 if hh == 0 else ~head0, part, 0.0)
         for hh in range(2) for part in (km_hi, km_mid, km_lo)], axis=0).astype(BF16)
    gates = lax.dot_general(gate_lhs, q_ref[...], _NT, preferred_element_type=F32)
    q_scaled = (q_ref[...].astype(F32) * scale).astype(BF16)
    sel_row = lax.broadcasted_iota(jnp.int32, (2 * MAX_BLOCKS, LANES), 0)
    sel_lane = lax.broadcasted_iota(jnp.int32, (2 * MAX_BLOCKS, LANES), 1)

    for hh in range(2):
        mine = head0 if hh == 0 else ~head0
        g0 = 3 * MAX_BLOCKS * hh
        gate = (gates[g0:g0 + MAX_BLOCKS] + gates[g0 + MAX_BLOCKS:g0 + 2 * MAX_BLOCKS]
                + gates[g0 + 2 * MAX_BLOCKS:g0 + 3 * MAX_BLOCKS])
        cand = blk_i < own_i
        g = jnp.where(cand, gate, NEG)
        rank = jnp.zeros((MAX_BLOCKS, seq), jnp.int32)
        for m in range(MAX_BLOCKS):
            gm = g[m:m + 1, :]
            beats = (gm > g) | ((gm == g) & (blk_i > m))
            rank = rank + beats.astype(jnp.int32)
        keep = (cand & (rank < MOBA_TOPK)) | (blk_i >= own_i)
        pen = jnp.where(keep, 0.0, -PEN_BIG)
        pen_lane0 = (1 - hh) * HEAD_DIM
        pen16 = jnp.concatenate([pen, jnp.zeros_like(pen)], axis=0).astype(BF16)
        place = ((sel_row < MAX_BLOCKS) & (sel_lane == pen_lane0 + sel_row)).astype(BF16)
        pen_t = lax.dot_general(pen16, place, (((0,), (0,)), ((), ())),
                                preferred_element_type=F32)
        qa_ref[hh] = jnp.where(mine, q_scaled, pen_t.astype(BF16))
        onehot = (lane_full == pen_lane0 + row_blk).astype(BF16)
        ka_ref[hh] = jnp.where(mine, k_ref[...], onehot)

    va_ref[:, 0:LANES] = v_ref[...]
    va_ref[:, LANES:2 * LANES] = jnp.ones((seq, LANES), BF16)

    half = blk // 2

    def pass1(own, hh):
        q = qa_ref[hh, own * blk:(own + 1) * blk, :]
        far_bias = far_ref[2 * hp + hh]
        mx = {}
        for n in range(own + 1):
            s = lax.dot_general(q, ka_ref[hh, n * blk:(n + 1) * blk, :], _NT,
                                preferred_element_type=F32)
            is_far = n < own - 1
            if n == own:
                s = s + d_ref[hh, 0]
            elif n == own - 1:
                s = s + d_ref[hh, 1]
            s_ref[hh, n] = s
            folded = jnp.maximum(s[:, :half], s[:, half:])
            mx[is_far] = jnp.maximum(mx[is_far], folded) if is_far in mx else folded
        mx_all = jnp.maximum(mx[False], mx[True] + far_bias) if True in mx else mx[False]
        m = jnp.max(mx_all, axis=1, keepdims=True)
        return m, m - far_bias

    def pass2(own, hh, m, m_far):
        for n in range(own + 1):
            p = jnp.exp(s_ref[hh, n] - (m_far if n < own - 1 else m))
            p_ref[hh, :, n * blk:(n + 1) * blk] = p.astype(BF16)
        keys = (own + 1) * blk
        res = _dot(p_ref[hh, :, 0:keys], va_ref[0:keys, :])
        return res[:, 0:LANES] / res[:, LANES:2 * LANES]

    items = [(own, hh) for own in range(nb) for hh in range(2)]
    stats = pass1(*items[0])
    outs = []
    for i, (own, hh) in enumerate(items):
        next_stats = pass1(*items[i + 1]) if i + 1 < len(items) else None
        outs.append(pass2(own, hh, *stats))
        stats = next_stats
        if hh == 1:
            o_ref[own * blk:(own + 1) * blk, :] = jnp.where(head0, *outs).astype(o_ref.dtype)
            outs = []


def _moba(za, bias_tiles, far_bias, batch, seq):
    n_pairs = ATTN_HEADS // 2
    assert seq % MOBA_BLOCK == 0 and seq // MOBA_BLOCK <= MAX_BLOCKS
    return pl.pallas_call(
        functools.partial(_moba_kernel, seq=seq),
        out_shape=jax.ShapeDtypeStruct((batch * seq, ATTN_DIM), BF16),
        grid=(n_pairs, batch),
        in_specs=[pl.BlockSpec(memory_space=pltpu.SMEM),
                  pl.BlockSpec((seq, LANES), lambda hp, b: (b, hp)),
                  pl.BlockSpec((seq, LANES), lambda hp, b: (b, n_pairs + hp)),
                  pl.BlockSpec((seq, LANES), lambda hp, b: (b, 2 * n_pairs + hp)),
                  pl.BlockSpec((2, 2, MOBA_BLOCK, MOBA_BLOCK), lambda hp, b: (hp, 0, 0, 0))],
        out_specs=pl.BlockSpec((seq, LANES), lambda hp, b: (b, hp)),
        scratch_shapes=[pltpu.VMEM((2, seq, LANES), BF16),
                        pltpu.VMEM((2, seq, LANES), BF16),
                        pltpu.VMEM((2, seq // MOBA_BLOCK, MOBA_BLOCK, MOBA_BLOCK), F32),
                        pltpu.VMEM((seq, 2 * LANES), BF16),
                        pltpu.VMEM((2, MOBA_BLOCK, seq), BF16)],
        compiler_params=pltpu.CompilerParams(
            dimension_semantics=("arbitrary", "arbitrary"), vmem_limit_bytes=VMEM_LIMIT),
        name="moba",
    )(far_bias, za, za, za, bias_tiles)


def _out_ffn_kernel(x_ref, ym_ref, yc_ref, ya_ref, wo_ref, g2_ref, wup_ref, cw_ref, cb_ref,
                    wdn_ref, fg_ref, o_ref, hn_ref, acc_ref, act_ref, tail_ref,
                    *, layer, tiles_per_seq, final_norm):
    tm = TM_FFN
    row = slice(layer, layer + 1)
    first = (pl.program_id(0) % tiles_per_seq) == 0
    y = jnp.concatenate([ym_ref[...], yc_ref[...], ya_ref[...]], axis=1)
    x1 = x_ref[...] + _dot(y, wo_ref[...])
    hn_ref[...] = _rms(x1, g2_ref[row, :]).astype(BF16)
    acc_ref[...] = x1
    row8 = lax.broadcasted_iota(jnp.int32, (SUBLANES, FFN_CHUNK), 0)

    def up_proj(c):
        return [_dot(hn_ref[...], wup_ref[:, part * D_FFP + c * FFN_CHUNK:
                                          part * D_FFP + (c + 1) * FFN_CHUNK]) for part in range(2)]

    u_next = up_proj(0)
    for c in range(N_FFN_CHUNKS):
        u_pair = u_next
        if c + 1 < N_FFN_CHUNKS:
            u_next = up_proj(c + 1)
        conv = []
        for part in range(2):
            c0 = part * D_FFP + c * FFN_CHUNK
            u = u_pair[part]
            tail = jnp.where(first, 0.0, tail_ref[2 * c + part])
            tail_ref[2 * c + part] = u[tm - SUBLANES:tm, :]
            acc = cb_ref[row, c0:c0 + FFN_CHUNK] + cw_ref[FFN_CONV - 1:FFN_CONV, c0:c0 + FFN_CHUNK] * u
            for d in range(1, FFN_CONV):
                shifted = pltpu.roll(u, d, axis=0)
                top = jnp.where(row8 < d, pltpu.roll(tail, d, axis=0), shifted[0:SUBLANES])
                shifted = jnp.concatenate([top, shifted[SUBLANES:]], axis=0)
                j = FFN_CONV - 1 - d
                acc = acc + cw_ref[j:j + 1, c0:c0 + FFN_CHUNK] * shifted
            conv.append(acc)
        act_ref[:, c * FFN_CHUNK:(c + 1) * FFN_CHUNK] = (
            conv[0] * _sigmoid(conv[0]) * conv[1]).astype(BF16)
        if c + 1 in FFN_DOWN_ENDS:
            g = FFN_DOWN_ENDS.index(c + 1)
            k0 = (FFN_DOWN_ENDS[g - 1] if g else 0) * FFN_CHUNK
            k1 = (c + 1) * FFN_CHUNK
            acc_ref[...] += _dot(act_ref[:, k0:k1], wdn_ref[k0:k1, :])

    out = acc_ref[...]
    if final_norm:
        out = _rms(out, fg_ref[...])
    o_ref[...] = out


def _out_ffn(x2d, ym, yc, ya, wo, g2, wup, cw, cb, wdn, fg, layer, seq, final_norm):
    n = x2d.shape[0]
    tiles_per_seq = seq // TM_FFN

    def const(shape):
        return _layer_spec(shape, layer, pipeline_mode=pl.Buffered(1))

    return pl.pallas_call(
        functools.partial(_out_ffn_kernel, layer=layer, tiles_per_seq=tiles_per_seq,
                          final_norm=final_norm),
        out_shape=jax.ShapeDtypeStruct((n, D_MODEL), F32),
        grid=(n // TM_FFN,),
        in_specs=[pl.BlockSpec((TM_FFN, D_MODEL), lambda i: (i, 0)),
                  pl.BlockSpec((TM_FFN, MLSTM_DIM), lambda i: (i, 0)),
                  pl.BlockSpec((TM_FFN, CONV_DIM), lambda i: (i, 0)),
                  pl.BlockSpec((TM_FFN, ATTN_DIM), lambda i: (i, 0)),
                  const((D_MODEL, D_MODEL)),
                  _const_spec((DEPTH, D_MODEL)),
                  const((D_MODEL, 2 * D_FFP)),
                  const((FFN_CONV, 2 * D_FFP)),
                  _const_spec((DEPTH, 2 * D_FFP)),
                  const((D_FFP, D_MODEL)),
                  _const_spec((1, D_MODEL))],
        out_specs=pl.BlockSpec((TM_FFN, D_MODEL), lambda i: (i, 0)),
        scratch_shapes=[pltpu.VMEM((TM_FFN, D_MODEL), BF16),
                        pltpu.VMEM((TM_FFN, D_MODEL), F32),
                        pltpu.VMEM((TM_FFN, D_FFP), BF16),
                        pltpu.VMEM((2 * N_FFN_CHUNKS, SUBLANES, FFN_CHUNK), F32)],
        compiler_params=pltpu.CompilerParams(
            dimension_semantics=("arbitrary",), vmem_limit_bytes=VMEM_LIMIT),
        name="out_ffn",
    )(x2d, ym, yc, ya, wo, g2, wup, cw, cb, wdn, fg)


def _pad_last(a, width):
    return jnp.pad(a, [(0, 0)] * (a.ndim - 1) + [(0, width - a.shape[-1])])


def _move_cols_kernel(x_ref, o_ref, *, moves):
    end = 0
    for src, dst, width in moves:
        if dst > end:
            o_ref[:, end:dst] = jnp.zeros((o_ref.shape[0], dst - end), o_ref.dtype)
        o_ref[:, dst:dst + width] = x_ref[:, src:src + width].astype(o_ref.dtype)
        end = dst + width
    if end < o_ref.shape[1]:
        o_ref[:, end:] = jnp.zeros((o_ref.shape[0], o_ref.shape[1] - end), o_ref.dtype)


def _move_cols(w, moves, out_cols):
    depth, rows, cols = w.shape
    return pl.pallas_call(
        functools.partial(_move_cols_kernel, moves=tuple(moves)),
        out_shape=jax.ShapeDtypeStruct((depth, rows, out_cols), BF16),
        grid=(depth, rows // PREP_ROWS),
        in_specs=[pl.BlockSpec((None, PREP_ROWS, cols), lambda l, i: (l, i, 0))],
        out_specs=pl.BlockSpec((None, PREP_ROWS, out_cols), lambda l, i: (l, i, 0)),
        compiler_params=pltpu.CompilerParams(
            dimension_semantics=("arbitrary", "arbitrary"), vmem_limit_bytes=VMEM_LIMIT),
        name="move_cols",
    )(w)


def _w_in_rows_kernel(x_ref, o_ref):
    g0 = 4 * MLSTM_DIM
    g1 = g0 + 2 * MLSTM_HEADS
    cols = x_ref.shape[1]
    o_ref[0:g0, :] = x_ref[0:g0, :].astype(o_ref.dtype)
    gate_tile = jnp.concatenate([x_ref[g0:g1, :], jnp.zeros((GATE_PAD - (g1 - g0), cols), F32)], axis=0)
    o_ref[g0:ZM_W, :] = gate_tile.astype(o_ref.dtype)
    o_ref[ZM_W:P_W, :] = x_ref[g1:, :].astype(o_ref.dtype)


def _prep_w_in(w_in):
    depth, d_in, p_in = w_in.shape
    return pl.pallas_call(
        _w_in_rows_kernel,
        out_shape=jax.ShapeDtypeStruct((depth, P_W, d_in), BF16),
        grid=(depth, d_in // PREP_ROWS),
        in_specs=[pl.BlockSpec((None, p_in, PREP_ROWS), lambda l, i: (l, 0, i))],
        out_specs=pl.BlockSpec((None, P_W, PREP_ROWS), lambda l, i: (l, 0, i)),
        compiler_params=pltpu.CompilerParams(
            dimension_semantics=("arbitrary", "arbitrary"), vmem_limit_bytes=VMEM_LIMIT),
        name="w_in_rows",
    )(jnp.swapaxes(w_in, 1, 2))


def _prep_w_up(w_up):
    return _move_cols(w_up, [(0, 0, D_FF), (D_FF, D_FFP, D_FF)], 2 * D_FFP)


def _prep_ffn_cols(a):
    return jnp.concatenate([_pad_last(a[..., :D_FF], D_FFP), _pad_last(a[..., D_FF:], D_FFP)],
                           axis=-1)


def _pad_rows_kernel(x_ref, o_ref):
    rows = x_ref.shape[0]
    o_ref[0:rows, :] = x_ref[...].astype(o_ref.dtype)
    o_ref[rows:, :] = jnp.zeros((o_ref.shape[0] - rows, o_ref.shape[1]), o_ref.dtype)


def _prep_w_down(w_down):
    depth, rows, cols = w_down.shape
    return pl.pallas_call(
        _pad_rows_kernel,
        out_shape=jax.ShapeDtypeStruct((depth, D_FFP, cols), BF16),
        grid=(depth, cols // PREP_ROWS),
        in_specs=[pl.BlockSpec((None, rows, PREP_ROWS), lambda l, i: (l, 0, i))],
        out_specs=pl.BlockSpec((None, D_FFP, PREP_ROWS), lambda l, i: (l, 0, i)),
        compiler_params=pltpu.CompilerParams(
            dimension_semantics=("arbitrary", "arbitrary"), vmem_limit_bytes=VMEM_LIMIT),
        name="pad_rows",
    )(w_down)


def _prep_gate_bias(ig_b, fg_b):
    return _pad_last(jnp.concatenate([ig_b, fg_b], axis=-1), GATE_PAD)


def kernel(x, norm1_g, w_in, mlstm_qk_conv_w, mlstm_qk_conv_b, mlstm_ig_b, mlstm_fg_b, mlstm_head_g, conf_dw_w, conf_dw_b, conf_ln_g, conf_ln_b, rel_bias, w_out, norm2_g, ffn_w_up, ffn_conv_w, ffn_conv_b, ffn_w_down, final_g):
    batch, seq, _ = x.shape
    bias_tiles = _bias_tiles(rel_bias)
    far_bias = rel_bias[REL_BUCKETS - 1]
    w_in_r = _prep_w_in(w_in)
    gate_bias = _prep_gate_bias(mlstm_ig_b, mlstm_fg_b)
    w_out_b = w_out.astype(BF16)
    w_up_r = _prep_w_up(ffn_w_up)
    f_w = _prep_ffn_cols(ffn_conv_w)
    f_b = _prep_ffn_cols(ffn_conv_b)
    w_down_r = _prep_w_down(ffn_w_down)
    x2d = x.reshape(batch * seq, D_MODEL)
    for l in range(DEPTH):
        zm, za, yc = _in_proj(x2d, norm1_g, w_in_r, conf_dw_w, conf_dw_b, conf_ln_g, conf_ln_b,
                              l, seq)
        ym = _mlstm(zm, mlstm_qk_conv_w, mlstm_qk_conv_b, gate_bias, mlstm_head_g, l, batch, seq)
        ya = _moba(za, bias_tiles, far_bias, batch, seq)
        x2d = _out_ffn(x2d, ym, yc, ya, w_out_b, norm2_g, w_up_r, f_w, f_b, w_down_r,
                       final_g[None, :], l, seq, l == DEPTH - 1)
    return x2d.reshape(batch, seq, D_MODEL)
```

```python
import functools
import math

import numpy as np
import jax
import jax.numpy as jnp
from jax import lax
from jax.experimental import pallas as pl
from jax.experimental.pallas import tpu as pltpu

F32 = jnp.float32
BF16 = jnp.bfloat16
HIGHEST = lax.Precision.HIGHEST

D_MODEL = 1024
DEPTH = 2
HEAD_DIM = 64
MLSTM_DIM = 256
CONV_DIM = 256
ATTN_DIM = 512
MLSTM_HEADS = MLSTM_DIM // HEAD_DIM
ATTN_HEADS = ATTN_DIM // HEAD_DIM
QK_CONV = 4
CONF_KERNEL = 31
MOBA_BLOCK = 256
MOBA_TOPK = 3
REL_BUCKETS = 32
REL_MAX_DIST = 128
D_FF = 2752
FFN_CONV = 3
EPS = 1e-6
NEG = -1e30

LANES = 128
SUBLANES = 8
BF16_SUBLANES = 16
VMEM_LIMIT = 56 * 1024 * 1024

GATE_PAD = LANES
ZM_W = 4 * MLSTM_DIM + GATE_PAD
ZC_W = 2 * CONV_DIM
ZA_W = 3 * ATTN_DIM
P_W = ZM_W + ZC_W + ZA_W
D_FFP = -(-D_FF // 256) * 256
FFN_CHUNK = 256
N_FFN_CHUNKS = D_FFP // FFN_CHUNK
FFN_DOWN_ENDS = (4, 8, N_FFN_CHUNKS - 1, N_FFN_CHUNKS)

TM_IN = 512
TM_FFN = 1024
PREP_ROWS = 256
MLSTM_L = 256
MLSTM_STATE_ROWS = 2 * HEAD_DIM + BF16_SUBLANES
CONF_ROWS = 64
CONF_CHAINS = 1
CONF_PAD = 32
MAX_BLOCKS = SUBLANES
PEN_BIG = 2.0 ** 100

_NT = (((1,), (1,)), ((), ()))


def _t5_saturation_distance():
    n = np.arange(1, 4 * MOBA_BLOCK, dtype=np.float32)
    max_exact = REL_BUCKETS // 2
    large = max_exact + (np.log(n / max_exact) / math.log(REL_MAX_DIST / max_exact)
                         * (REL_BUCKETS - max_exact)).astype(np.int32)
    bucket = np.where(n < max_exact, n.astype(np.int32), np.minimum(large, REL_BUCKETS - 1))
    not_last = np.nonzero(bucket != REL_BUCKETS - 1)[0]
    return int(n[not_last[-1]]) + 1


assert _t5_saturation_distance() <= MOBA_BLOCK + 1


def _sigmoid(x):
    return 1.0 / (1.0 + jnp.exp2(x * (-math.log2(math.e))))


def _log_sigmoid(x):
    return jnp.minimum(x, 0.0) - jnp.log1p(jnp.exp(-jnp.abs(x)))


def _rms(xf, g):
    return xf * lax.rsqrt(jnp.mean(xf * xf, axis=-1, keepdims=True) + EPS) * g


def _dot(a, b, **kw):
    return jnp.dot(a, b, preferred_element_type=F32, **kw)


def _const_spec(shape):
    nd = len(shape)
    return pl.BlockSpec(shape, lambda *_: (0,) * nd)


def _layer_spec(shape, layer, **kw):
    nd = len(shape)
    return pl.BlockSpec((None,) + tuple(shape), lambda *_: (layer,) + (0,) * nd, **kw)


def _conformer_tile(up_ref, r0, w_ref, start, ln_g, ln_b, never):
    win_rows = CONF_ROWS + CONF_PAD
    halves = []
    for c0 in range(0, CONV_DIM, LANES):
        win = up_ref[r0:r0 + win_rows, c0:c0 + LANES]
        seed = start[:, c0:c0 + LANES]
        if halves:
            seed = jnp.where(never, halves[-1][0:1, :], seed)
        acc = jnp.broadcast_to(seed, (CONF_ROWS, LANES))
        for r in range(SUBLANES):
            rolled = win if r == 0 else pltpu.roll(win, win_rows - r, axis=0)
            for j in range(CONF_KERNEL):
                off = CONF_PAD - (CONF_KERNEL - 1) + j
                if off % SUBLANES == r:
                    base = off - r
                    acc = acc + w_ref[j:j + 1, c0:c0 + LANES] * rolled[base:base + CONF_ROWS]
        halves.append(acc)
    acc = jnp.concatenate(halves, axis=1)
    mu = jnp.mean(acc, axis=-1, keepdims=True)
    d = acc - mu
    var = jnp.mean(d * d, axis=-1, keepdims=True)
    y = d * lax.rsqrt(var + EPS) * ln_g + ln_b
    return y * _sigmoid(y)


def _in_proj_kernel(x_ref, g_ref, w_ref, cw_ref, cb_ref, lg_ref, lb_ref, zm_ref, za_ref, yc_ref,
                    up_ref, *, layer, tiles_per_seq):
    tm = TM_IN
    row = slice(layer, layer + 1)
    first = (pl.program_id(0) % tiles_per_seq) == 0
    h = _rms(x_ref[...], g_ref[row, :]).astype(BF16)

    def proj(c0, c1):
        return lax.dot_general(h, w_ref[c0:c1, :], _NT, preferred_element_type=F32)

    zc = proj(ZM_W, ZM_W + ZC_W)
    up_ref[0:CONF_PAD, :] = jnp.where(first, 0.0, up_ref[tm:tm + CONF_PAD, :])
    up_ref[CONF_PAD:CONF_PAD + tm, :] = zc[:, 0:CONV_DIM] * _sigmoid(zc[:, CONV_DIM:ZC_W])
    zm_ref[...] = proj(0, ZM_W)
    za_ref[...] = proj(ZM_W + ZC_W, P_W).astype(BF16)
    bias, ln_g, ln_b = cb_ref[row, :], lg_ref[row, :], lb_ref[row, :]
    never = pl.program_id(0) < 0
    last = [None] * CONF_CHAINS
    for t in range(tm // CONF_ROWS):
        r0 = t * CONF_ROWS
        prev = last[t % CONF_CHAINS]
        start = bias if prev is None else jnp.where(never, prev[0:1, :], bias)
        y = _conformer_tile(up_ref, r0, cw_ref, start, ln_g, ln_b, never)
        last[t % CONF_CHAINS] = y
        yc_ref[r0:r0 + CONF_ROWS, :] = y.astype(yc_ref.dtype)


def _in_proj(x2d, g, w_r, conf_w, conf_b, conf_ln_g, conf_ln_b, layer, seq):
    n = x2d.shape[0]
    return pl.pallas_call(
        functools.partial(_in_proj_kernel, layer=layer, tiles_per_seq=seq // TM_IN),
        out_shape=(jax.ShapeDtypeStruct((n, ZM_W), F32),
                   jax.ShapeDtypeStruct((n, ZA_W), BF16),
                   jax.ShapeDtypeStruct((n, CONV_DIM), BF16)),
        grid=(n // TM_IN,),
        in_specs=[pl.BlockSpec((TM_IN, D_MODEL), lambda i: (i, 0)),
                  _const_spec((DEPTH, D_MODEL)),
                  _layer_spec((P_W, D_MODEL), layer),
                  _layer_spec((CONF_KERNEL, CONV_DIM), layer),
                  _const_spec((DEPTH, CONV_DIM)),
                  _const_spec((DEPTH, CONV_DIM)),
                  _const_spec((DEPTH, CONV_DIM))],
        out_specs=(pl.BlockSpec((TM_IN, ZM_W), lambda i: (i, 0)),
                   pl.BlockSpec((TM_IN, ZA_W), lambda i: (i, 0)),
                   pl.BlockSpec((TM_IN, CONV_DIM), lambda i: (i, 0))),
        scratch_shapes=[pltpu.VMEM((CONF_PAD + TM_IN, CONV_DIM), F32)],
        compiler_params=pltpu.CompilerParams(
            dimension_semantics=("arbitrary",), vmem_limit_bytes=VMEM_LIMIT),
        name="in_proj",
    )(x2d, g, w_r, conf_w, conf_b, conf_ln_g, conf_ln_b)


def _mlstm_kernel(zm_ref, cw_ref, cb_ref, gb_ref, hg_ref, o_ref, g_ref, gt_ref, *, layer, seq):
    L = MLSTM_L
    n_chunks = seq // L
    nh = MLSTM_HEADS
    scale = HEAD_DIM ** -0.5
    lane = lax.broadcasted_iota(jnp.int32, (1, LANES), 1)
    head0 = lane < HEAD_DIM

    row = slice(layer, layer + 1)
    gates = zm_ref[:, 4 * MLSTM_DIM:ZM_W] + gb_ref[row, :]
    is_f = (lane >= nh) & (lane < 2 * nh)
    gates = jnp.where(is_f, _log_sigmoid(gates), gates)
    g_ref[...] = gates
    gt_ref[...] = gates.T[0:SUBLANES, :]

    ri = lax.broadcasted_iota(jnp.int32, (L, L), 0)
    ci = lax.broadcasted_iota(jnp.int32, (L, L), 1)
    visible = ri <= ci
    tri = (ci <= ri).astype(F32)
    tri_t = visible.astype(F32)
    srow = lax.broadcasted_iota(jnp.int32, (MLSTM_STATE_ROWS, LANES), 0)
    scol = lax.broadcasted_iota(jnp.int32, (MLSTM_STATE_ROWS, LANES), 1)
    srow_head = jnp.where(srow < LANES, srow // HEAD_DIM, srow - LANES)
    state_mask = (scol // HEAD_DIM) == srow_head
    row_is_head0 = srow_head[:, 0:1] == 0
    vrow_is_head0 = lax.broadcasted_iota(jnp.int32, (LANES, 1), 0) < HEAD_DIM
    extra_row = lax.broadcasted_iota(jnp.int32, (MLSTM_STATE_ROWS - LANES, 1), 0)
    ones_rows = jnp.ones((MLSTM_STATE_ROWS - LANES, L), BF16)
    cb = cb_ref[row, :]
    hg = hg_ref[row, :]

    def chunk(c, carry):
        states, m_prev = carry
        r0 = pl.multiple_of(c * L, L)
        gc = g_ref[pl.ds(r0, L), :]
        gtc = gt_ref[:, pl.ds(r0, L)]
        bcols = _dot(tri, gc, precision=HIGHEST)
        brows = _dot(gtc, tri_t, precision=HIGHEST)
        e_rows = gtc[0:nh] - brows[nh:2 * nh]

        xcur = zm_ref[pl.ds(r0, L), 0:2 * MLSTM_DIM]
        pr = pl.multiple_of(jnp.maximum(r0 - SUBLANES, 0), SUBLANES)
        xprev = jnp.where(c > 0, zm_ref[pl.ds(pr, SUBLANES), 0:2 * MLSTM_DIM], 0.0)
        xcat = jnp.concatenate([xprev, xcur], axis=0)
        y = cb
        for j in range(QK_CONV):
            off = SUBLANES - (QK_CONV - 1) + j
            y = y + cw_ref[j:j + 1, :] * xcat[off:off + L]
        qk = y * _sigmoid(y)

        new_states = []
        new_m = []
        for p in range(2):
            lo = p * LANES
            q_t = (qk[:, lo:lo + LANES] * scale).T.astype(BF16)
            k_b = qk[:, MLSTM_DIM + lo:MLSTM_DIM + lo + LANES].astype(BF16)
            v_t = zm_ref[pl.ds(r0, L), 2 * MLSTM_DIM + lo:2 * MLSTM_DIM + lo + LANES].T
            v_tb = v_t.astype(BF16)
            inter = _dot(states[p].astype(BF16), q_t)
            hn_t, wks, decays = [], [], []
            for hh in range(2):
                h = 2 * p + hh
                mp = m_prev[h]
                e_col = gc[:, h:h + 1] - bcols[:, nh + h:nh + h + 1]
                e_row = e_rows[h:h + 1]
                b_row = brows[nh + h:nh + h + 1]
                em = jnp.where(visible, e_col, NEG)
                g = jnp.maximum(mp, jnp.max(em, axis=0, keepdims=True))
                w_t = jnp.exp(em - g)
                k_h = jnp.where(head0 if hh == 0 else ~head0, k_b, jnp.zeros_like(k_b))
                s_t = _dot(k_h, q_t)
                lhs = jnp.concatenate(
                    [v_tb[hh * HEAD_DIM:(hh + 1) * HEAD_DIM], ones_rows], axis=0)
                r = _dot(lhs, (s_t * w_t).astype(BF16))
                a = jnp.exp(mp - g)
                num = a * inter[hh * HEAD_DIM:(hh + 1) * HEAD_DIM] + r[0:HEAD_DIM]
                den = a * inter[LANES + hh:LANES + hh + 1] + r[HEAD_DIM:HEAD_DIM + 1]
                hv = num * (1.0 / jnp.maximum(jnp.abs(den), jnp.exp(-(b_row + g))))
                mu = jnp.mean(hv, axis=0, keepdims=True)
                d = hv - mu
                var = jnp.mean(d * d, axis=0, keepdims=True)
                hn_t.append(d * lax.rsqrt(var + EPS))
                g_last = jnp.maximum(mp, jnp.max(e_row, axis=1, keepdims=True))
                wks.append(jnp.exp(e_row - g_last))
                decays.append(jnp.exp(mp - g_last))
                new_m.append(b_row[:, L - 1:L] + g_last)
            vw = v_t * jnp.where(vrow_is_head0, wks[0], wks[1])
            extra = jnp.where(extra_row == 0, wks[0], jnp.where(extra_row == 1, wks[1], 0.0))
            upd = _dot(jnp.concatenate([vw, extra], axis=0).astype(BF16), k_b)
            decay_rows = jnp.where(row_is_head0, decays[0], decays[1])
            new_states.append(decay_rows * states[p] + jnp.where(state_mask, upd, 0.0))

            hn = jnp.concatenate(hn_t, axis=0).T
            og = zm_ref[pl.ds(r0, L), 3 * MLSTM_DIM + lo:3 * MLSTM_DIM + lo + LANES]
            o_ref[pl.ds(r0, L), lo:lo + LANES] = (hn * hg[:, lo:lo + LANES] * _sigmoid(og)).astype(o_ref.dtype)
        return tuple(new_states), tuple(new_m)

    init = (tuple(jnp.zeros((MLSTM_STATE_ROWS, LANES), F32) for _ in range(2)),
            tuple(jnp.full((1, 1), NEG, F32) for _ in range(nh)))
    lax.fori_loop(0, n_chunks, chunk, init, unroll=True)


def _mlstm(zm, cw, cb, gate_bias, head_g, layer, batch, seq):
    return pl.pallas_call(
        functools.partial(_mlstm_kernel, layer=layer, seq=seq),
        out_shape=jax.ShapeDtypeStruct((batch * seq, MLSTM_DIM), BF16),
        grid=(batch,),
        in_specs=[pl.BlockSpec((seq, ZM_W), lambda b: (b, 0)),
                  _layer_spec((QK_CONV, 2 * MLSTM_DIM), layer),
                  _const_spec((DEPTH, 2 * MLSTM_DIM)),
                  _const_spec((DEPTH, GATE_PAD)),
                  _const_spec((DEPTH, MLSTM_DIM))],
        out_specs=pl.BlockSpec((seq, MLSTM_DIM), lambda b: (b, 0)),
        scratch_shapes=[pltpu.VMEM((seq, GATE_PAD), F32),
                        pltpu.VMEM((SUBLANES, seq), F32)],
        compiler_params=pltpu.CompilerParams(
            dimension_semantics=("arbitrary",), vmem_limit_bytes=VMEM_LIMIT),
        name="mlstm",
    )(zm, cw, cb, gate_bias, head_g)


def _bias_tiles_kernel(rb_ref, o_ref):
    h = pl.program_id(0)
    i = lax.broadcasted_iota(jnp.int32, (MOBA_BLOCK, MOBA_BLOCK), 0)
    j = lax.broadcasted_iota(jnp.int32, (MOBA_BLOCK, MOBA_BLOCK), 1)
    max_exact = REL_BUCKETS // 2
    for t in range(2):
        dist = i - j + t * MOBA_BLOCK
        n = jnp.maximum(dist, 0)
        nf = jnp.maximum(n, 1).astype(F32)
        large = max_exact + (jnp.log(nf / max_exact) / math.log(REL_MAX_DIST / max_exact)
                             * (REL_BUCKETS - max_exact)).astype(jnp.int32)
        large = jnp.minimum(large, REL_BUCKETS - 1)
        bucket = jnp.where(n < max_exact, n, large)
        bias = jnp.zeros((MOBA_BLOCK, MOBA_BLOCK), F32)
        for bk in range(REL_BUCKETS):
            bias = jnp.where(bucket == bk, rb_ref[bk, h], bias)
        if t == 0:
            bias = jnp.where(dist >= 0, bias, NEG)
        o_ref[0, t] = bias


def _bias_tiles(rel_bias):
    return pl.pallas_call(
        _bias_tiles_kernel,
        out_shape=jax.ShapeDtypeStruct((ATTN_HEADS, 2, MOBA_BLOCK, MOBA_BLOCK), F32),
        grid=(ATTN_HEADS,),
        in_specs=[pl.BlockSpec(memory_space=pltpu.SMEM)],
        out_specs=pl.BlockSpec((1, 2, MOBA_BLOCK, MOBA_BLOCK), lambda h: (h, 0, 0, 0)),
        compiler_params=pltpu.CompilerParams(dimension_semantics=("arbitrary",)),
        name="bias_tiles",
    )(rel_bias)


def _moba_kernel(far_ref, q_ref, k_ref, v_ref, d_ref, o_ref, qa_ref, ka_ref, s_ref, va_ref,
                 p_ref, *, seq):
    hp = pl.program_id(0)
    blk = MOBA_BLOCK
    nb = seq // blk
    scale = HEAD_DIM ** -0.5
    lane = lax.broadcasted_iota(jnp.int32, (1, LANES), 1)
    head0 = lane < HEAD_DIM

    blk_i = lax.broadcasted_iota(jnp.int32, (MAX_BLOCKS, seq), 0)
    own_i = lax.broadcasted_iota(jnp.int32, (MAX_BLOCKS, seq), 1) // blk
    row_blk = lax.broadcasted_iota(jnp.int32, (seq, LANES), 0) // blk
    lane_full = lax.broadcasted_iota(jnp.int32, (seq, LANES), 1)

    avg = jnp.where(blk_i == own_i, 1.0 / blk, 0.0).astype(BF16)
    kmean = _dot(avg, k_ref[...])
    km_hi = kmean.astype(BF16).astype(F32)
    km_mid = (kmean - km_hi).astype(BF16).astype(F32)
    km_lo = kmean - km_hi - km_mid
    gate_lhs = jnp.concatenate(
        [jnp.where(head0 if hh == 0 else ~head0, part, 0.0)
         for hh in range(2) for part in (km_hi, km_mid, km_lo)], axis=0).astype(BF16)
    gates = lax.dot_general(gate_lhs, q_ref[...], _NT, preferred_element_type=F32)
    q_scaled = (q_ref[...].astype(F32) * scale).astype(BF16)
    sel_row = lax.broadcasted_iota(jnp.int32, (2 * MAX_BLOCKS, LANES), 0)
    sel_lane = lax.broadcasted_iota(jnp.int32, (2 * MAX_BLOCKS, LANES), 1)

    for hh in range(2):
        mine = head0 if hh == 0 else ~head0
        g0 = 3 * MAX_BLOCKS * hh
        gate = (gates[g0:g0 + MAX_BLOCKS] + gates[g0 + MAX_BLOCKS:g0 + 2 * MAX_BLOCKS]
                + gates[g0 + 2 * MAX_BLOCKS:g0 + 3 * MAX_BLOCKS])
        cand = blk_i < own_i
        g = jnp.where(cand, gate, NEG)
        rank = jnp.zeros((MAX_BLOCKS, seq), jnp.int32)
        for m in range(MAX_BLOCKS):
            gm = g[m:m + 1, :]
            beats = (gm > g) | ((gm == g) & (blk_i > m))
            rank = rank + beats.astype(jnp.int32)
        keep = (cand & (rank < MOBA_TOPK)) | (blk_i >= own_i)
        pen = jnp.where(keep, 0.0, -PEN_BIG)
        pen_lane0 = (1 - hh) * HEAD_DIM
        pen16 = jnp.concatenate([pen, jnp.zeros_like(pen)], axis=0).astype(BF16)
        place = ((sel_row < MAX_BLOCKS) & (sel_lane == pen_lane0 + sel_row)).astype(BF16)
        pen_t = lax.dot_general(pen16, place, (((0,), (0,)), ((), ())),
                                preferred_element_type=F32)
        qa_ref[hh] = jnp.where(mine, q_scaled, pen_t.astype(BF16))
        onehot = (lane_full == pen_lane0 + row_blk).astype(BF16)
        ka_ref[hh] = jnp.where(mine, k_ref[...], onehot)

    va_ref[:, 0:LANES] = v_ref[...]
    va_ref[:, LANES:2 * LANES] = jnp.ones((seq, LANES), BF16)

    half = blk // 2

    def pass1(own, hh):
        q = qa_ref[hh, own * blk:(own + 1) * blk, :]
        far_bias = far_ref[2 * hp + hh]
        mx = {}
        for n in range(own + 1):
            s = lax.dot_general(q, ka_ref[hh, n * blk:(n + 1) * blk, :], _NT,
                                preferred_element_type=F32)
            is_far = n < own - 1
            if n == own:
                s = s + d_ref[hh, 0]
            elif n == own - 1:
                s = s + d_ref[hh, 1]
            s_ref[hh, n] = s
            folded = jnp.maximum(s[:, :half], s[:, half:])
            mx[is_far] = jnp.maximum(mx[is_far], folded) if is_far in mx else folded
        mx_all = jnp.maximum(mx[False], mx[True] + far_bias) if True in mx else mx[False]
        m = jnp.max(mx_all, axis=1, keepdims=True)
        return m, m - far_bias

    def pass2(own, hh, m, m_far):
        for n in range(own + 1):
            p = jnp.exp(s_ref[hh, n] - (m_far if n < own - 1 else m))
            p_ref[hh, :, n * blk:(n + 1) * blk] = p.astype(BF16)
        keys = (own + 1) * blk
        res = _dot(p_ref[hh, :, 0:keys], va_ref[0:keys, :])
        return res[:, 0:LANES] / res[:, LANES:2 * LANES]

    items = [(own, hh) for own in range(nb) for hh in range(2)]
    stats = pass1(*items[0])
    outs = []
    for i, (own, hh) in enumerate(items):
        next_stats = pass1(*items[i + 1]) if i + 1 < len(items) else None
        outs.append(pass2(own, hh, *stats))
        stats = next_stats
        if hh == 1:
            o_ref[own * blk:(own + 1) * blk, :] = jnp.where(head0, *outs).astype(o_ref.dtype)
            outs = []


def _moba(za, bias_tiles, far_bias, batch, seq):
    n_pairs = ATTN_HEADS // 2
    assert seq % MOBA_BLOCK == 0 and seq // MOBA_BLOCK <= MAX_BLOCKS
    return pl.pallas_call(
        functools.partial(_moba_kernel, seq=seq),
        out_shape=jax.ShapeDtypeStruct((batch * seq, ATTN_DIM), BF16),
        grid=(n_pairs, batch),
        in_specs=[pl.BlockSpec(memory_space=pltpu.SMEM),
                  pl.BlockSpec((seq, LANES), lambda hp, b: (b, hp)),
                  pl.BlockSpec((seq, LANES), lambda hp, b: (b, n_pairs + hp)),
                  pl.BlockSpec((seq, LANES), lambda hp, b: (b, 2 * n_pairs + hp)),
                  pl.BlockSpec((2, 2, MOBA_BLOCK, MOBA_BLOCK), lambda hp, b: (hp, 0, 0, 0))],
        out_specs=pl.BlockSpec((seq, LANES), lambda hp, b: (b, hp)),
        scratch_shapes=[pltpu.VMEM((2, seq, LANES), BF16),
                        pltpu.VMEM((2, seq, LANES), BF16),
                        pltpu.VMEM((2, seq // MOBA_BLOCK, MOBA_BLOCK, MOBA_BLOCK), F32),
                        pltpu.VMEM((seq, 2 * LANES), BF16),
                        pltpu.VMEM((2, MOBA_BLOCK, seq), BF16)],
        compiler_params=pltpu.CompilerParams(
            dimension_semantics=("arbitrary", "arbitrary"), vmem_limit_bytes=VMEM_LIMIT),
        name="moba",
    )(far_bias, za, za, za, bias_tiles)


def _out_ffn_kernel(x_ref, ym_ref, yc_ref, ya_ref, wo_ref, g2_ref, wup_ref, cw_ref, cb_ref,
                    wdn_ref, fg_ref, o_ref, hn_ref, acc_ref, act_ref, tail_ref,
                    *, layer, tiles_per_seq, final_norm):
    tm = TM_FFN
    row = slice(layer, layer + 1)
    first = (pl.program_id(0) % tiles_per_seq) == 0
    y = jnp.concatenate([ym_ref[...], yc_ref[...], ya_ref[...]], axis=1)
    x1 = x_ref[...] + _dot(y, wo_ref[...])
    hn_ref[...] = _rms(x1, g2_ref[row, :]).astype(BF16)
    acc_ref[...] = x1
    row8 = lax.broadcasted_iota(jnp.int32, (SUBLANES, FFN_CHUNK), 0)

    def up_proj(c):
        return [_dot(hn_ref[...], wup_ref[:, part * D_FFP + c * FFN_CHUNK:
                                          part * D_FFP + (c + 1) * FFN_CHUNK]) for part in range(2)]

    u_next = up_proj(0)
    for c in range(N_FFN_CHUNKS):
        u_pair = u_next
        if c + 1 < N_FFN_CHUNKS:
            u_next = up_proj(c + 1)
        conv = []
        for part in range(2):
            c0 = part * D_FFP + c * FFN_CHUNK
            u = u_pair[part]
            tail = jnp.where(first, 0.0, tail_ref[2 * c + part])
            tail_ref[2 * c + part] = u[tm - SUBLANES:tm, :]
            acc = cb_ref[row, c0:c0 + FFN_CHUNK] + cw_ref[FFN_CONV - 1:FFN_CONV, c0:c0 + FFN_CHUNK] * u
            for d in range(1, FFN_CONV):
                shifted = pltpu.roll(u, d, axis=0)
                top = jnp.where(row8 < d, pltpu.roll(tail, d, axis=0), shifted[0:SUBLANES])
                shifted = jnp.concatenate([top, shifted[SUBLANES:]], axis=0)
                j = FFN_CONV - 1 - d
                acc = acc + cw_ref[j:j + 1, c0:c0 + FFN_CHUNK] * shifted
            conv.append(acc)
        act_ref[:, c * FFN_CHUNK:(c + 1) * FFN_CHUNK] = (
            conv[0] * _sigmoid(conv[0]) * conv[1]).astype(BF16)
        if c + 1 in FFN_DOWN_ENDS:
            g = FFN_DOWN_ENDS.index(c + 1)
            k0 = (FFN_DOWN_ENDS[g - 1] if g else 0) * FFN_CHUNK
            k1 = (c + 1) * FFN_CHUNK
            acc_ref[...] += _dot(act_ref[:, k0:k1], wdn_ref[k0:k1, :])

    out = acc_ref[...]
    if final_norm:
        out = _rms(out, fg_ref[...])
    o_ref[...] = out


def _out_ffn(x2d, ym, yc, ya, wo, g2, wup, cw, cb, wdn, fg, layer, seq, final_norm):
    n = x2d.shape[0]
    tiles_per_seq = seq // TM_FFN

    def const(shape):
        return _layer_spec(shape, layer, pipeline_mode=pl.Buffered(1))

    return pl.pallas_call(
        functools.partial(_out_ffn_kernel, layer=layer, tiles_per_seq=tiles_per_seq,
                          final_norm=final_norm),
        out_shape=jax.ShapeDtypeStruct((n, D_MODEL), F32),
        grid=(n // TM_FFN,),
        in_specs=[pl.BlockSpec((TM_FFN, D_MODEL), lambda i: (i, 0)),
                  pl.BlockSpec((TM_FFN, MLSTM_DIM), lambda i: (i, 0)),
                  pl.BlockSpec((TM_FFN, CONV_DIM), lambda i: (i, 0)),
                  pl.BlockSpec((TM_FFN, ATTN_DIM), lambda i: (i, 0)),
                  const((D_MODEL, D_MODEL)),
                  _const_spec((DEPTH, D_MODEL)),
                  const((D_MODEL, 2 * D_FFP)),
                  const((FFN_CONV, 2 * D_FFP)),
                  _const_spec((DEPTH, 2 * D_FFP)),
                  const((D_FFP, D_MODEL)),
                  _const_spec((1, D_MODEL))],
        out_specs=pl.BlockSpec((TM_FFN, D_MODEL), lambda i: (i, 0)),
        scratch_shapes=[pltpu.VMEM((TM_FFN, D_MODEL), BF16),
                        pltpu.VMEM((TM_FFN, D_MODEL), F32),
                        pltpu.VMEM((TM_FFN, D_FFP), BF16),
                        pltpu.VMEM((2 * N_FFN_CHUNKS, SUBLANES, FFN_CHUNK), F32)],
        compiler_params=pltpu.CompilerParams(
            dimension_semantics=("arbitrary",), vmem_limit_bytes=VMEM_LIMIT),
        name="out_ffn",
    )(x2d, ym, yc, ya, wo, g2, wup, cw, cb, wdn, fg)


def _pad_last(a, width):
    return jnp.pad(a, [(0, 0)] * (a.ndim - 1) + [(0, width - a.shape[-1])])


def _move_cols_kernel(x_ref, o_ref, *, moves):
    end = 0
    for src, dst, width in moves:
        if dst > end:
            o_ref[:, end:dst] = jnp.zeros((o_ref.shape[0], dst - end), o_ref.dtype)
        o_ref[:, dst:dst + width] = x_ref[:, src:src + width].astype(o_ref.dtype)
        end = dst + width
    if end < o_ref.shape[1]:
        o_ref[:, end:] = jnp.zeros((o_ref.shape[0], o_ref.shape[1] - end), o_ref.dtype)


def _move_cols(w, moves, out_cols):
    depth, rows, cols = w.shape
    return pl.pallas_call(
        functools.partial(_move_cols_kernel, moves=tuple(moves)),
        out_shape=jax.ShapeDtypeStruct((depth, rows, out_cols), BF16),
        grid=(depth, rows // PREP_ROWS),
        in_specs=[pl.BlockSpec((None, PREP_ROWS, cols), lambda l, i: (l, i, 0))],
        out_specs=pl.BlockSpec((None, PREP_ROWS, out_cols), lambda l, i: (l, i, 0)),
        compiler_params=pltpu.CompilerParams(
            dimension_semantics=("arbitrary", "arbitrary"), vmem_limit_bytes=VMEM_LIMIT),
        name="move_cols",
    )(w)


def _w_in_rows_kernel(x_ref, o_ref):
    g0 = 4 * MLSTM_DIM
    g1 = g0 + 2 * MLSTM_HEADS
    cols = x_ref.shape[1]
    o_ref[0:g0, :] = x_ref[0:g0, :].astype(o_ref.dtype)
    gate_tile = jnp.concatenate([x_ref[g0:g1, :], jnp.zeros((GATE_PAD - (g1 - g0), cols), F32)], axis=0)
    o_ref[g0:ZM_W, :] = gate_tile.astype(o_ref.dtype)
    o_ref[ZM_W:P_W, :] = x_ref[g1:, :].astype(o_ref.dtype)


def _prep_w_in(w_in):
    depth, d_in, p_in = w_in.shape
    return pl.pallas_call(
        _w_in_rows_kernel,
        out_shape=jax.ShapeDtypeStruct((depth, P_W, d_in), BF16),
        grid=(depth, d_in // PREP_ROWS),
        in_specs=[pl.BlockSpec((None, p_in, PREP_ROWS), lambda l, i: (l, 0, i))],
        out_specs=pl.BlockSpec((None, P_W, PREP_ROWS), lambda l, i: (l, 0, i)),
        compiler_params=pltpu.CompilerParams(
            dimension_semantics=("arbitrary", "arbitrary"), vmem_limit_bytes=VMEM_LIMIT),
        name="w_in_rows",
    )(jnp.swapaxes(w_in, 1, 2))


def _prep_w_up(w_up):
    return _move_cols(w_up, [(0, 0, D_FF), (D_FF, D_FFP, D_FF)], 2 * D_FFP)


def _prep_ffn_cols(a):
    return jnp.concatenate([_pad_last(a[..., :D_FF], D_FFP), _pad_last(a[..., D_FF:], D_FFP)],
                           axis=-1)


def _pad_rows_kernel(x_ref, o_ref):
    rows = x_ref.shape[0]
    o_ref[0:rows, :] = x_ref[...].astype(o_ref.dtype)
    o_ref[rows:, :] = jnp.zeros((o_ref.shape[0] - rows, o_ref.shape[1]), o_ref.dtype)


def _prep_w_down(w_down):
    depth, rows, cols = w_down.shape
    return pl.pallas_call(
        _pad_rows_kernel,
        out_shape=jax.ShapeDtypeStruct((depth, D_FFP, cols), BF16),
        grid=(depth, cols // PREP_ROWS),
        in_specs=[pl.BlockSpec((None, rows, PREP_ROWS), lambda l, i: (l, 0, i))],
        out_specs=pl.BlockSpec((None, D_FFP, PREP_ROWS), lambda l, i: (l, 0, i)),
        compiler_params=pltpu.CompilerParams(
            dimension_semantics=("arbitrary", "arbitrary"), vmem_limit_bytes=VMEM_LIMIT),
        name="pad_rows",
    )(w_down)


def _prep_gate_bias(ig_b, fg_b):
    return _pad_last(jnp.concatenate([ig_b, fg_b], axis=-1), GATE_PAD)


def kernel(x, norm1_g, w_in, mlstm_qk_conv_w, mlstm_qk_conv_b, mlstm_ig_b, mlstm_fg_b, mlstm_head_g, conf_dw_w, conf_dw_b, conf_ln_g, conf_ln_b, rel_bias, w_out, norm2_g, ffn_w_up, ffn_conv_w, ffn_conv_b, ffn_w_down, final_g):
    batch, seq, _ = x.shape
    bias_tiles = _bias_tiles(rel_bias)
    far_bias = rel_bias[REL_BUCKETS - 1]
    w_in_r = _prep_w_in(w_in)
    gate_bias = _prep_gate_bias(mlstm_ig_b, mlstm_fg_b)
    w_out_b = w_out.astype(BF16)
    w_up_r = _prep_w_up(ffn_w_up)
    f_w = _prep_ffn_cols(ffn_conv_w)
    f_b = _prep_ffn_cols(ffn_conv_b)
    w_down_r = _prep_w_down(ffn_w_down)
    x2d = x.reshape(batch * seq, D_MODEL)
    for l in range(DEPTH):
        zm, za, yc = _in_proj(x2d, norm1_g, w_in_r, conf_dw_w, conf_dw_b, conf_ln_g, conf_ln_b,
                              l, seq)
        ym = _mlstm(zm, mlstm_qk_conv_w, mlstm_qk_conv_b, gate_bias, mlstm_head_g, l, batch, seq)
        ya = _moba(za, bias_tiles, far_bias, batch, seq)
        x2d = _out_ffn(x2d, ym, yc, ya, w_out_b, norm2_g, w_up_r, f_w, f_b, w_down_r,
                       final_g[None, :], l, seq, l == DEPTH - 1)
    return x2d.reshape(batch, seq, D_MODEL)
```

```python
import functools
import math

import numpy as np
import jax
import jax.numpy as jnp
from jax import lax
from jax.experimental import pallas as pl
from jax.experimental.pallas import tpu as pltpu

F32 = jnp.float32
BF16 = jnp.bfloat16
HIGHEST = lax.Precision.HIGHEST

D_MODEL = 1024
DEPTH = 2
HEAD_DIM = 64
MLSTM_DIM = 256
CONV_DIM = 256
ATTN_DIM = 512
MLSTM_HEADS = MLSTM_DIM // HEAD_DIM
ATTN_HEADS = ATTN_DIM // HEAD_DIM
QK_CONV = 4
CONF_KERNEL = 31
MOBA_BLOCK = 256
MOBA_TOPK = 3
REL_BUCKETS = 32
REL_MAX_DIST = 128
D_FF = 2752
FFN_CONV = 3
EPS = 1e-6
NEG = -1e30

LANES = 128
SUBLANES = 8
BF16_SUBLANES = 16
VMEM_LIMIT = 56 * 1024 * 1024

GATE_PAD = LANES
ZM_W = 4 * MLSTM_DIM + GATE_PAD
ZC_W = 2 * CONV_DIM
ZA_W = 3 * ATTN_DIM
P_W = ZM_W + ZC_W + ZA_W
D_FFP = -(-D_FF // 256) * 256
FFN_CHUNK = 256
N_FFN_CHUNKS = D_FFP // FFN_CHUNK
FFN_DOWN_ENDS = (4, 8, N_FFN_CHUNKS - 1, N_FFN_CHUNKS)

TM_IN = 512
TM_FFN = 1024
PREP_ROWS = 256
MLSTM_L = 256
MLSTM_STATE_ROWS = 2 * HEAD_DIM + BF16_SUBLANES
CONF_ROWS = 64
CONF_CHAINS = 1
CONF_PAD = 32
MAX_BLOCKS = SUBLANES
PEN_BIG = 2.0 ** 100

_NT = (((1,), (1,)), ((), ()))


def _t5_saturation_distance():
    n = np.arange(1, 4 * MOBA_BLOCK, dtype=np.float32)
    max_exact = REL_BUCKETS // 2
    large = max_exact + (np.log(n / max_exact) / math.log(REL_MAX_DIST / max_exact)
                         * (REL_BUCKETS - max_exact)).astype(np.int32)
    bucket = np.where(n < max_exact, n.astype(np.int32), np.minimum(large, REL_BUCKETS - 1))
    not_last = np.nonzero(bucket != REL_BUCKETS - 1)[0]
    return int(n[not_last[-1]]) + 1


assert _t5_saturation_distance() <= MOBA_BLOCK + 1


def _sigmoid(x):
    return 1.0 / (1.0 + jnp.exp2(x * (-math.log2(math.e))))


def _log_sigmoid(x):
    return jnp.minimum(x, 0.0) - jnp.log1p(jnp.exp(-jnp.abs(x)))


def _rms(xf, g):
    return xf * lax.rsqrt(jnp.mean(xf * xf, axis=-1, keepdims=True) + EPS) * g


def _dot(a, b, **kw):
    return jnp.dot(a, b, preferred_element_type=F32, **kw)


def _const_spec(shape):
    nd = len(shape)
    return pl.BlockSpec(shape, lambda *_: (0,) * nd)


def _layer_spec(shape, layer, **kw):
    nd = len(shape)
    return pl.BlockSpec((None,) + tuple(shape), lambda *_: (layer,) + (0,) * nd, **kw)


def _conformer_tile(up_ref, r0, w_ref, start, ln_g, ln_b, never):
    win_rows = CONF_ROWS + CONF_PAD
    halves = []
    for c0 in range(0, CONV_DIM, LANES):
        win = up_ref[r0:r0 + win_rows, c0:c0 + LANES]
        seed = start[:, c0:c0 + LANES]
        if halves:
            seed = jnp.where(never, halves[-1][0:1, :], seed)
        acc = jnp.broadcast_to(seed, (CONF_ROWS, LANES))
        for r in range(SUBLANES):
            rolled = win if r == 0 else pltpu.roll(win, win_rows - r, axis=0)
            for j in range(CONF_KERNEL):
                off = CONF_PAD - (CONF_KERNEL - 1) + j
                if off % SUBLANES == r:
                    base = off - r
                    acc = acc + w_ref[j:j + 1, c0:c0 + LANES] * rolled[base:base + CONF_ROWS]
        halves.append(acc)
    acc = jnp.concatenate(halves, axis=1)
    mu = jnp.mean(acc, axis=-1, keepdims=True)
    d = acc - mu
    var = jnp.mean(d * d, axis=-1, keepdims=True)
    y = d * lax.rsqrt(var + EPS) * ln_g + ln_b
    return y * _sigmoid(y)


def _in_proj_kernel(x_ref, g_ref, w_ref, cw_ref, cb_ref, lg_ref, lb_ref, zm_ref, za_ref, yc_ref,
                    up_ref, *, layer, tiles_per_seq):
    tm = TM_IN
    row = slice(layer, layer + 1)
    first = (pl.program_id(0) % tiles_per_seq) == 0
    h = _rms(x_ref[...], g_ref[row, :]).astype(BF16)

    def proj(c0, c1):
        return lax.dot_general(h, w_ref[c0:c1, :], _NT, preferred_element_type=F32)

    zc = proj(ZM_W, ZM_W + ZC_W)
    up_ref[0:CONF_PAD, :] = jnp.where(first, 0.0, up_ref[tm:tm + CONF_PAD, :])
    up_ref[CONF_PAD:CONF_PAD + tm, :] = zc[:, 0:CONV_DIM] * _sigmoid(zc[:, CONV_DIM:ZC_W])
    zm_ref[...] = proj(0, ZM_W)
    za_ref[...] = proj(ZM_W + ZC_W, P_W).astype(BF16)
    bias, ln_g, ln_b = cb_ref[row, :], lg_ref[row, :], lb_ref[row, :]
    never = pl.program_id(0) < 0
    last = [None] * CONF_CHAINS
    for t in range(tm // CONF_ROWS):
        r0 = t * CONF_ROWS
        prev = last[t % CONF_CHAINS]
        start = bias if prev is None else jnp.where(never, prev[0:1, :], bias)
        y = _conformer_tile(up_ref, r0, cw_ref, start, ln_g, ln_b, never)
        last[t % CONF_CHAINS] = y
        yc_ref[r0:r0 + CONF_ROWS, :] = y.astype(yc_ref.dtype)


def _in_proj(x2d, g, w_r, conf_w, conf_b, conf_ln_g, conf_ln_b, layer, seq):
    n = x2d.shape[0]
    return pl.pallas_call(
        functools.partial(_in_proj_kernel, layer=layer, tiles_per_seq=seq // TM_IN),
        out_shape=(jax.ShapeDtypeStruct((n, ZM_W), F32),
                   jax.ShapeDtypeStruct((n, ZA_W), BF16),
                   jax.ShapeDtypeStruct((n, CONV_DIM), BF16)),
        grid=(n // TM_IN,),
        in_specs=[pl.BlockSpec((TM_IN, D_MODEL), lambda i: (i, 0)),
                  _const_spec((DEPTH, D_MODEL)),
                  _layer_spec((P_W, D_MODEL), layer),
                  _layer_spec((CONF_KERNEL, CONV_DIM), layer),
                  _const_spec((DEPTH, CONV_DIM)),
                  _const_spec((DEPTH, CONV_DIM)),
                  _const_spec((DEPTH, CONV_DIM))],
        out_specs=(pl.BlockSpec((TM_IN, ZM_W), lambda i: (i, 0)),
                   pl.BlockSpec((TM_IN, ZA_W), lambda i: (i, 0)),
                   pl.BlockSpec((TM_IN, CONV_DIM), lambda i: (i, 0))),
        scratch_shapes=[pltpu.VMEM((CONF_PAD + TM_IN, CONV_DIM), F32)],
        compiler_params=pltpu.CompilerParams(
            dimension_semantics=("arbitrary",), vmem_limit_bytes=VMEM_LIMIT),
        name="in_proj",
    )(x2d, g, w_r, conf_w, conf_b, conf_ln_g, conf_ln_b)


def _mlstm_kernel(zm_ref, cw_ref, cb_ref, gb_ref, hg_ref, o_ref, gt_ref, *, layer, seq):
    L = MLSTM_L
    n_chunks = seq // L
    nh = MLSTM_HEADS
    scale = HEAD_DIM ** -0.5
    lane = lax.broadcasted_iota(jnp.int32, (1, LANES), 1)
    head0 = lane < HEAD_DIM

    row = slice(layer, layer + 1)
    gates = zm_ref[:, 4 * MLSTM_DIM:ZM_W] + gb_ref[row, :]
    is_f = (lane >= nh) & (lane < 2 * nh)
    gates = jnp.where(is_f, _log_sigmoid(gates), gates)
    gt_ref[...] = gates.T[0:SUBLANES, :]

    ri = lax.broadcasted_iota(jnp.int32, (L, L), 0)
    ci = lax.broadcasted_iota(jnp.int32, (L, L), 1)
    visible = ri <= ci
    tri_t = visible.astype(F32)
    prow = lax.broadcasted_iota(jnp.int32, (4 * nh, LANES), 0)
    plane = lax.broadcasted_iota(jnp.int32, (4 * nh, LANES), 1)
    place = ((prow < 3 * nh) & (plane == prow % nh)).astype(BF16)
    srow = lax.broadcasted_iota(jnp.int32, (MLSTM_STATE_ROWS, LANES), 0)
    scol = lax.broadcasted_iota(jnp.int32, (MLSTM_STATE_ROWS, LANES), 1)
    srow_head = jnp.where(srow < LANES, srow // HEAD_DIM, srow - LANES)
    state_mask = (scol // HEAD_DIM) == srow_head
    row_is_head0 = srow_head[:, 0:1] == 0
    vrow_is_head0 = lax.broadcasted_iota(jnp.int32, (LANES, 1), 0) < HEAD_DIM
    extra_row = lax.broadcasted_iota(jnp.int32, (MLSTM_STATE_ROWS - LANES, 1), 0)
    ones_rows = jnp.ones((MLSTM_STATE_ROWS - LANES, L), BF16)
    cb = cb_ref[row, :]
    hg = hg_ref[row, :]

    def chunk(c, carry):
        states, m_prev = carry
        r0 = pl.multiple_of(c * L, L)
        gtc = gt_ref[:, pl.ds(r0, L)]
        brows = _dot(gtc, tri_t, precision=HIGHEST)
        e_rows = gtc[0:nh] - brows[nh:2 * nh]
        e_hi = e_rows.astype(BF16).astype(F32)
        e_mid = (e_rows - e_hi).astype(BF16).astype(F32)
        e_lo = e_rows - e_hi - e_mid
        e_terms = jnp.concatenate([e_hi, e_mid, e_lo, jnp.zeros_like(e_lo)], axis=0).astype(BF16)
        e_cols = lax.dot_general(e_terms, place, (((0,), (0,)), ((), ())),
                                 preferred_element_type=F32)

        xcur = zm_ref[pl.ds(r0, L), 0:2 * MLSTM_DIM]
        pr = pl.multiple_of(jnp.maximum(r0 - SUBLANES, 0), SUBLANES)
        xprev = jnp.where(c > 0, zm_ref[pl.ds(pr, SUBLANES), 0:2 * MLSTM_DIM], 0.0)
        xcat = jnp.concatenate([xprev, xcur], axis=0)
        y = cb
        for j in range(QK_CONV):
            off = SUBLANES - (QK_CONV - 1) + j
            y = y + cw_ref[j:j + 1, :] * xcat[off:off + L]
        qk = y * _sigmoid(y)

        new_states = []
        new_m = []
        for p in range(2):
            lo = p * LANES
            q_t = (qk[:, lo:lo + LANES] * scale).T.astype(BF16)
            k_b = qk[:, MLSTM_DIM + lo:MLSTM_DIM + lo + LANES].astype(BF16)
            v_t = zm_ref[pl.ds(r0, L), 2 * MLSTM_DIM + lo:2 * MLSTM_DIM + lo + LANES].T
            v_tb = v_t.astype(BF16)
            inter = _dot(states[p].astype(BF16), q_t)
            hn_t, wks, decays = [], [], []
            for hh in range(2):
                h = 2 * p + hh
                mp = m_prev[h]
                e_col = e_cols[:, h:h + 1]
                e_row = e_rows[h:h + 1]
                b_row = brows[nh + h:nh + h + 1]
                em = jnp.where(visible, e_col, NEG)
                g = jnp.maximum(mp, jnp.max(em, axis=0, keepdims=True))
                w_t = jnp.exp(em - g)
                k_h = jnp.where(head0 if hh == 0 else ~head0, k_b, jnp.zeros_like(k_b))
                s_t = _dot(k_h, q_t)
                lhs = jnp.concatenate(
                    [v_tb[hh * HEAD_DIM:(hh + 1) * HEAD_DIM], ones_rows], axis=0)
                r = _dot(lhs, (s_t * w_t).astype(BF16))
                a = jnp.exp(mp - g)
                num = a * inter[hh * HEAD_DIM:(hh + 1) * HEAD_DIM] + r[0:HEAD_DIM]
                den = a * inter[LANES + hh:LANES + hh + 1] + r[HEAD_DIM:HEAD_DIM + 1]
                hv = num * (1.0 / jnp.maximum(jnp.abs(den), jnp.exp(-(b_row + g))))
                mu = jnp.mean(hv, axis=0, keepdims=True)
                d = hv - mu
                var = jnp.mean(d * d, axis=0, keepdims=True)
                hn_t.append(d * lax.rsqrt(var + EPS))
                g_last = jnp.maximum(mp, jnp.max(e_row, axis=1, keepdims=True))
                wks.append(jnp.exp(e_row - g_last))
                decays.append(jnp.exp(mp - g_last))
                new_m.append(b_row[:, L - 1:L] + g_last)
            vw = v_t * jnp.where(vrow_is_head0, wks[0], wks[1])
            extra = jnp.where(extra_row == 0, wks[0], jnp.where(extra_row == 1, wks[1], 0.0))
            upd = _dot(jnp.concatenate([vw, extra], axis=0).astype(BF16), k_b)
            decay_rows = jnp.where(row_is_head0, decays[0], decays[1])
            new_states.append(decay_rows * states[p] + jnp.where(state_mask, upd, 0.0))

            hn = jnp.concatenate(hn_t, axis=0).T
            og = zm_ref[pl.ds(r0, L), 3 * MLSTM_DIM + lo:3 * MLSTM_DIM + lo + LANES]
            o_ref[pl.ds(r0, L), lo:lo + LANES] = (hn * hg[:, lo:lo + LANES] * _sigmoid(og)).astype(o_ref.dtype)
        return tuple(new_states), tuple(new_m)

    init = (tuple(jnp.zeros((MLSTM_STATE_ROWS, LANES), F32) for _ in range(2)),
            tuple(jnp.full((1, 1), NEG, F32) for _ in range(nh)))
    lax.fori_loop(0, n_chunks, chunk, init, unroll=True)


def _mlstm(zm, cw, cb, gate_bias, head_g, layer, batch, seq):
    return pl.pallas_call(
        functools.partial(_mlstm_kernel, layer=layer, seq=seq),
        out_shape=jax.ShapeDtypeStruct((batch * seq, MLSTM_DIM), BF16),
        grid=(batch,),
        in_specs=[pl.BlockSpec((seq, ZM_W), lambda b: (b, 0)),
                  _layer_spec((QK_CONV, 2 * MLSTM_DIM), layer),
                  _const_spec((DEPTH, 2 * MLSTM_DIM)),
                  _const_spec((DEPTH, GATE_PAD)),
                  _const_spec((DEPTH, MLSTM_DIM))],
        out_specs=pl.BlockSpec((seq, MLSTM_DIM), lambda b: (b, 0)),
        scratch_shapes=[pltpu.VMEM((SUBLANES, seq), F32)],
        compiler_params=pltpu.CompilerParams(
            dimension_semantics=("arbitrary",), vmem_limit_bytes=VMEM_LIMIT),
        name="mlstm",
    )(zm, cw, cb, gate_bias, head_g)


def _bias_tiles_kernel(rb_ref, o_ref):
    h = pl.program_id(0)
    blk = MOBA_BLOCK
    span = 2 * blk
    max_exact = REL_BUCKETS // 2
    k = lax.broadcasted_iota(jnp.int32, (1, span), 1)
    n = (span - k) % span
    nf = jnp.maximum(n, 1).astype(F32)
    large = max_exact + (jnp.log(nf / max_exact) / math.log(REL_MAX_DIST / max_exact)
                         * (REL_BUCKETS - max_exact)).astype(jnp.int32)
    large = jnp.minimum(large, REL_BUCKETS - 1)
    bucket = jnp.where(n < max_exact, n, large)
    by_dist = jnp.zeros((1, span), F32)
    for bk in range(REL_BUCKETS):
        by_dist = jnp.where(bucket == bk, rb_ref[bk, h], by_dist)
    toeplitz = pltpu.roll(jnp.broadcast_to(by_dist, (blk, span)), 0, axis=1, stride=1,
                          stride_axis=0)
    i = lax.broadcasted_iota(jnp.int32, (blk, blk), 0)
    j = lax.broadcasted_iota(jnp.int32, (blk, blk), 1)
    o_ref[0, 0] = jnp.where(i >= j, toeplitz[:, 0:blk], NEG)
    o_ref[0, 1] = toeplitz[:, blk:span]


def _bias_tiles(rel_bias):
    return pl.pallas_call(
        _bias_tiles_kernel,
        out_shape=jax.ShapeDtypeStruct((ATTN_HEADS, 2, MOBA_BLOCK, MOBA_BLOCK), F32),
        grid=(ATTN_HEADS,),
        in_specs=[pl.BlockSpec(memory_space=pltpu.SMEM)],
        out_specs=pl.BlockSpec((1, 2, MOBA_BLOCK, MOBA_BLOCK), lambda h: (h, 0, 0, 0)),
        compiler_params=pltpu.CompilerParams(dimension_semantics=("arbitrary",)),
        name="bias_tiles",
    )(rel_bias)


def _moba_kernel(far_ref, q_ref, k_ref, v_ref, d_ref, o_ref, qa_ref, ka_ref, s_ref, va_ref,
                 p_ref, *, seq):
    hp = pl.program_id(0)
    blk = MOBA_BLOCK
    nb = seq // blk
    scale = HEAD_DIM ** -0.5
    lane = lax.broadcasted_iota(jnp.int32, (1, LANES), 1)
    head0 = lane < HEAD_DIM

    blk_i = lax.broadcasted_iota(jnp.int32, (MAX_BLOCKS, seq), 0)
    own_i = lax.broadcasted_iota(jnp.int32, (MAX_BLOCKS, seq), 1) // blk
    row_blk = lax.broadcasted_iota(jnp.int32, (seq, LANES), 0) // blk
    lane_full = lax.broadcasted_iota(jnp.int32, (seq, LANES), 1)

    avg = jnp.where(blk_i == own_i, 1.0 / blk, 0.0).astype(BF16)
    kmean = _dot(avg, k_ref[...])
    km_hi = kmean.astype(BF16).astype(F32)
    km_mid = (kmean - km_hi).astype(BF16).astype(F32)
    km_lo = kmean - km_hi - km_mid
    gate_lhs = jnp.concatenate(
        [jnp.where(head0 if hh == 0 else ~head0, part, 0.0)
         for hh in range(2) for part in (km_hi, km_mid, km_lo)], axis=0).astype(BF16)
    gates = lax.dot_general(gate_lhs, q_ref[...], _NT, preferred_element_type=F32)
    q_scaled = (q_ref[...].astype(F32) * scale).astype(BF16)
    sel_row = lax.broadcasted_iota(jnp.int32, (2 * MAX_BLOCKS, LANES), 0)
    sel_lane = lax.broadcasted_iota(jnp.int32, (2 * MAX_BLOCKS, LANES), 1)

    for hh in range(2):
        mine = head0 if hh == 0 else ~head0
        g0 = 3 * MAX_BLOCKS * hh
        gate = (gates[g0:g0 + MAX_BLOCKS] + gates[g0 + MAX_BLOCKS:g0 + 2 * MAX_BLOCKS]
                + gates[g0 + 2 * MAX_BLOCKS:g0 + 3 * MAX_BLOCKS])
        cand = blk_i < own_i
        g = jnp.where(cand, gate, NEG)
        rank = jnp.zeros((MAX_BLOCKS, seq), jnp.int32)
        for m in range(MAX_BLOCKS):
            gm = g[m:m + 1, :]
            beats = (gm > g) | ((gm == g) & (blk_i > m))
            rank = rank + beats.astype(jnp.int32)
        keep = (cand & (rank < MOBA_TOPK)) | (blk_i >= own_i)
        pen = jnp.where(keep, 0.0, -PEN_BIG)
        pen_lane0 = (1 - hh) * HEAD_DIM
        pen16 = jnp.concatenate([pen, jnp.zeros_like(pen)], axis=0).astype(BF16)
        place = ((sel_row < MAX_BLOCKS) & (sel_lane == pen_lane0 + sel_row)).astype(BF16)
        pen_t = lax.dot_general(pen16, place, (((0,), (0,)), ((), ())),
                                preferred_element_type=F32)
        qa_ref[hh] = jnp.where(mine, q_scaled, pen_t.astype(BF16))
        onehot = (lane_full == pen_lane0 + row_blk).astype(BF16)
        ka_ref[hh] = jnp.where(mine, k_ref[...], onehot)

    va_ref[:, 0:LANES] = v_ref[...]
    va_ref[:, LANES:2 * LANES] = jnp.ones((seq, LANES), BF16)

    half = blk // 2

    def pass1(own, hh):
        q = qa_ref[hh, own * blk:(own + 1) * blk, :]
        far_bias = far_ref[2 * hp + hh]
        mx = {}
        for n in range(own + 1):
            s = lax.dot_general(q, ka_ref[hh, n * blk:(n + 1) * blk, :], _NT,
                                preferred_element_type=F32)
            is_far = n < own - 1
            if n == own:
                s = s + d_ref[hh, 0]
            elif n == own - 1:
                s = s + d_ref[hh, 1]
            s_ref[hh, n] = s
            folded = jnp.maximum(s[:, :half], s[:, half:])
            mx[is_far] = jnp.maximum(mx[is_far], folded) if is_far in mx else folded
        mx_all = jnp.maximum(mx[False], mx[True] + far_bias) if True in mx else mx[False]
        m = jnp.max(mx_all, axis=1, keepdims=True)
        return m, m - far_bias

    def pass2(own, hh, m, m_far):
        for n in range(own + 1):
            p = jnp.exp(s_ref[hh, n] - (m_far if n < own - 1 else m))
            p_ref[hh, :, n * blk:(n + 1) * blk] = p.astype(BF16)
        keys = (own + 1) * blk
        res = _dot(p_ref[hh, :, 0:keys], va_ref[0:keys, :])
        return res[:, 0:LANES] / res[:, LANES:2 * LANES]

    items = [(own, hh) for own in range(nb) for hh in range(2)]
    stats = pass1(*items[0])
    outs = []
    for i, (own, hh) in enumerate(items):
        next_stats = pass1(*items[i + 1]) if i + 1 < len(items) else None
        outs.append(pass2(own, hh, *stats))
        stats = next_stats
        if hh == 1:
            o_ref[own * blk:(own + 1) * blk, :] = jnp.where(head0, *outs).astype(o_ref.dtype)
            outs = []


def _moba(za, bias_tiles, far_bias, batch, seq):
    n_pairs = ATTN_HEADS // 2
    assert seq % MOBA_BLOCK == 0 and seq // MOBA_BLOCK <= MAX_BLOCKS
    return pl.pallas_call(
        functools.partial(_moba_kernel, seq=seq),
        out_shape=jax.ShapeDtypeStruct((batch * seq, ATTN_DIM), BF16),
        grid=(n_pairs, batch),
        in_specs=[pl.BlockSpec(memory_space=pltpu.SMEM),
                  pl.BlockSpec((seq, LANES), lambda hp, b: (b, hp)),
                  pl.BlockSpec((seq, LANES), lambda hp, b: (b, n_pairs + hp)),
                  pl.BlockSpec((seq, LANES), lambda hp, b: (b, 2 * n_pairs + hp)),
                  pl.BlockSpec((2, 2, MOBA_BLOCK, MOBA_BLOCK), lambda hp, b: (hp, 0, 0, 0))],
        out_specs=pl.BlockSpec((seq, LANES), lambda hp, b: (b, hp)),
        scratch_shapes=[pltpu.VMEM((2, seq, LANES), BF16),
                        pltpu.VMEM((2, seq, LANES), BF16),
                        pltpu.VMEM((2, seq // MOBA_BLOCK, MOBA_BLOCK, MOBA_BLOCK), F32),
                        pltpu.VMEM((seq, 2 * LANES), BF16),
                        pltpu.VMEM((2, MOBA_BLOCK, seq), BF16)],
        compiler_params=pltpu.CompilerParams(
            dimension_semantics=("arbitrary", "arbitrary"), vmem_limit_bytes=VMEM_LIMIT),
        name="moba",
    )(far_bias, za, za, za, bias_tiles)


def _out_ffn_kernel(x_ref, ym_ref, yc_ref, ya_ref, wo_ref, g2_ref, wup_ref, cw_ref, cb_ref,
                    wdn_ref, fg_ref, o_ref, hn_ref, acc_ref, act_ref, tail_ref,
                    *, layer, tiles_per_seq, final_norm):
    tm = TM_FFN
    row = slice(layer, layer + 1)
    first = (pl.program_id(0) % tiles_per_seq) == 0
    y = jnp.concatenate([ym_ref[...], yc_ref[...], ya_ref[...]], axis=1)
    x1 = x_ref[...] + _dot(y, wo_ref[...])
    hn_ref[...] = _rms(x1, g2_ref[row, :]).astype(BF16)
    acc_ref[...] = x1
    row8 = lax.broadcasted_iota(jnp.int32, (SUBLANES, FFN_CHUNK), 0)

    def up_proj(c):
        return [_dot(hn_ref[...], wup_ref[:, part * D_FFP + c * FFN_CHUNK:
                                          part * D_FFP + (c + 1) * FFN_CHUNK]) for part in range(2)]

    u_next = up_proj(0)
    for c in range(N_FFN_CHUNKS):
        u_pair = u_next
        if c + 1 < N_FFN_CHUNKS:
            u_next = up_proj(c + 1)
        conv = []
        for part in range(2):
            c0 = part * D_FFP + c * FFN_CHUNK
            u = u_pair[part]
            tail = jnp.where(first, 0.0, tail_ref[2 * c + part])
            tail_ref[2 * c + part] = u[tm - SUBLANES:tm, :]
            acc = cb_ref[row, c0:c0 + FFN_CHUNK] + cw_ref[FFN_CONV - 1:FFN_CONV, c0:c0 + FFN_CHUNK] * u
            for d in range(1, FFN_CONV):
                shifted = pltpu.roll(u, d, axis=0)
                top = jnp.where(row8 < d, pltpu.roll(tail, d, axis=0), shifted[0:SUBLANES])
                shifted = jnp.concatenate([top, shifted[SUBLANES:]], axis=0)
                j = FFN_CONV - 1 - d
                acc = acc + cw_ref[j:j + 1, c0:c0 + FFN_CHUNK] * shifted
            conv.append(acc)
        act_ref[:, c * FFN_CHUNK:(c + 1) * FFN_CHUNK] = (
            conv[0] * _sigmoid(conv[0]) * conv[1]).astype(BF16)
        if c + 1 in FFN_DOWN_ENDS:
            g = FFN_DOWN_ENDS.index(c + 1)
            k0 = (FFN_DOWN_ENDS[g - 1] if g else 0) * FFN_CHUNK
            k1 = (c + 1) * FFN_CHUNK
            acc_ref[...] += _dot(act_ref[:, k0:k1], wdn_ref[k0:k1, :])

    out = acc_ref[...]
    if final_norm:
        out = _rms(out, fg_ref[...])
    o_ref[...] = out


def _out_ffn(x2d, ym, yc, ya, wo, g2, wup, cw, cb, wdn, fg, layer, seq, final_norm):
    n = x2d.shape[0]
    tiles_per_seq = seq // TM_FFN

    def const(shape):
        return _layer_spec(shape, layer, pipeline_mode=pl.Buffered(1))

    return pl.pallas_call(
        functools.partial(_out_ffn_kernel, layer=layer, tiles_per_seq=tiles_per_seq,
                          final_norm=final_norm),
        out_shape=jax.ShapeDtypeStruct((n, D_MODEL), F32),
        grid=(n // TM_FFN,),
        in_specs=[pl.BlockSpec((TM_FFN, D_MODEL), lambda i: (i, 0)),
                  pl.BlockSpec((TM_FFN, MLSTM_DIM), lambda i: (i, 0)),
                  pl.BlockSpec((TM_FFN, CONV_DIM), lambda i: (i, 0)),
                  pl.BlockSpec((TM_FFN, ATTN_DIM), lambda i: (i, 0)),
                  const((D_MODEL, D_MODEL)),
                  _const_spec((DEPTH, D_MODEL)),
                  const((D_MODEL, 2 * D_FFP)),
                  const((FFN_CONV, 2 * D_FFP)),
                  _const_spec((DEPTH, 2 * D_FFP)),
                  const((D_FFP, D_MODEL)),
                  _const_spec((1, D_MODEL))],
        out_specs=pl.BlockSpec((TM_FFN, D_MODEL), lambda i: (i, 0)),
        scratch_shapes=[pltpu.VMEM((TM_FFN, D_MODEL), BF16),
                        pltpu.VMEM((TM_FFN, D_MODEL), F32),
                        pltpu.VMEM((TM_FFN, D_FFP), BF16),
                        pltpu.VMEM((2 * N_FFN_CHUNKS, SUBLANES, FFN_CHUNK), F32)],
        compiler_params=pltpu.CompilerParams(
            dimension_semantics=("arbitrary",), vmem_limit_bytes=VMEM_LIMIT),
        name="out_ffn",
    )(x2d, ym, yc, ya, wo, g2, wup, cw, cb, wdn, fg)


def _pad_last(a, width):
    return jnp.pad(a, [(0, 0)] * (a.ndim - 1) + [(0, width - a.shape[-1])])


def _move_cols_kernel(x_ref, o_ref, *, moves):
    end = 0
    for src, dst, width in moves:
        if dst > end:
            o_ref[:, end:dst] = jnp.zeros((o_ref.shape[0], dst - end), o_ref.dtype)
        o_ref[:, dst:dst + width] = x_ref[:, src:src + width].astype(o_ref.dtype)
        end = dst + width
    if end < o_ref.shape[1]:
        o_ref[:, end:] = jnp.zeros((o_ref.shape[0], o_ref.shape[1] - end), o_ref.dtype)


def _move_cols(w, moves, out_cols):
    depth, rows, cols = w.shape
    return pl.pallas_call(
        functools.partial(_move_cols_kernel, moves=tuple(moves)),
        out_shape=jax.ShapeDtypeStruct((depth, rows, out_cols), BF16),
        grid=(depth, rows // PREP_ROWS),
        in_specs=[pl.BlockSpec((None, PREP_ROWS, cols), lambda l, i: (l, i, 0))],
        out_specs=pl.BlockSpec((None, PREP_ROWS, out_cols), lambda l, i: (l, i, 0)),
        compiler_params=pltpu.CompilerParams(
            dimension_semantics=("arbitrary", "arbitrary"), vmem_limit_bytes=VMEM_LIMIT),
        name="move_cols",
    )(w)


def _w_in_rows_kernel(x_ref, o_ref):
    g0 = 4 * MLSTM_DIM
    g1 = g0 + 2 * MLSTM_HEADS
    cols = x_ref.shape[1]
    o_ref[0:g0, :] = x_ref[0:g0, :].astype(o_ref.dtype)
    gate_tile = jnp.concatenate([x_ref[g0:g1, :], jnp.zeros((GATE_PAD - (g1 - g0), cols), F32)], axis=0)
    o_ref[g0:ZM_W, :] = gate_tile.astype(o_ref.dtype)
    o_ref[ZM_W:P_W, :] = x_ref[g1:, :].astype(o_ref.dtype)


def _prep_w_in(w_in):
    depth, d_in, p_in = w_in.shape
    return pl.pallas_call(
        _w_in_rows_kernel,
        out_shape=jax.ShapeDtypeStruct((depth, P_W, d_in), BF16),
        grid=(depth, d_in // PREP_ROWS),
        in_specs=[pl.BlockSpec((None, p_in, PREP_ROWS), lambda l, i: (l, 0, i))],
        out_specs=pl.BlockSpec((None, P_W, PREP_ROWS), lambda l, i: (l, 0, i)),
        compiler_params=pltpu.CompilerParams(
            dimension_semantics=("arbitrary", "arbitrary"), vmem_limit_bytes=VMEM_LIMIT),
        name="w_in_rows",
    )(jnp.swapaxes(w_in, 1, 2))


def _prep_w_up(w_up):
    return _move_cols(w_up, [(0, 0, D_FF), (D_FF, D_FFP, D_FF)], 2 * D_FFP)


def _prep_ffn_cols(a):
    return jnp.concatenate([_pad_last(a[..., :D_FF], D_FFP), _pad_last(a[..., D_FF:], D_FFP)],
                           axis=-1)


def _pad_rows_kernel(x_ref, o_ref):
    rows = x_ref.shape[0]
    o_ref[0:rows, :] = x_ref[...].astype(o_ref.dtype)
    o_ref[rows:, :] = jnp.zeros((o_ref.shape[0] - rows, o_ref.shape[1]), o_ref.dtype)


def _prep_w_down(w_down):
    depth, rows, cols = w_down.shape
    return pl.pallas_call(
        _pad_rows_kernel,
        out_shape=jax.ShapeDtypeStruct((depth, D_FFP, cols), BF16),
        grid=(depth, cols // PREP_ROWS),
        in_specs=[pl.BlockSpec((None, rows, PREP_ROWS), lambda l, i: (l, 0, i))],
        out_specs=pl.BlockSpec((None, D_FFP, PREP_ROWS), lambda l, i: (l, 0, i)),
        compiler_params=pltpu.CompilerParams(
            dimension_semantics=("arbitrary", "arbitrary"), vmem_limit_bytes=VMEM_LIMIT),
        name="pad_rows",
    )(w_down)


def _prep_gate_bias(ig_b, fg_b):
    return _pad_last(jnp.concatenate([ig_b, fg_b], axis=-1), GATE_PAD)


def kernel(x, norm1_g, w_in, mlstm_qk_conv_w, mlstm_qk_conv_b, mlstm_ig_b, mlstm_fg_b, mlstm_head_g, conf_dw_w, conf_dw_b, conf_ln_g, conf_ln_b, rel_bias, w_out, norm2_g, ffn_w_up, ffn_conv_w, ffn_conv_b, ffn_w_down, final_g):
    batch, seq, _ = x.shape
    bias_tiles = _bias_tiles(rel_bias)
    far_bias = rel_bias[REL_BUCKETS - 1]
    w_in_r = _prep_w_in(w_in)
    gate_bias = _prep_gate_bias(mlstm_ig_b, mlstm_fg_b)
    w_out_b = w_out.astype(BF16)
    w_up_r = _prep_w_up(ffn_w_up)
    f_w = _prep_ffn_cols(ffn_conv_w)
    f_b = _prep_ffn_cols(ffn_conv_b)
    w_down_r = _prep_w_down(ffn_w_down)
    x2d = x.reshape(batch * seq, D_MODEL)
    for l in range(DEPTH):
        zm, za, yc = _in_proj(x2d, norm1_g, w_in_r, conf_dw_w, conf_dw_b, conf_ln_g, conf_ln_b,
                              l, seq)
        ym = _mlstm(zm, mlstm_qk_conv_w, mlstm_qk_conv_b, gate_bias, mlstm_head_g, l, batch, seq)
        ya = _moba(za, bias_tiles, far_bias, batch, seq)
        x2d = _out_ffn(x2d, ym, yc, ya, w_out_b, norm2_g, w_up_r, f_w, f_b, w_down_r,
                       final_g[None, :], l, seq, l == DEPTH - 1)
    return x2d.reshape(batch, seq, D_MODEL)
```

```python
import functools
import math

import numpy as np
import jax
import jax.numpy as jnp
from jax import lax
from jax.experimental import pallas as pl
from jax.experimental.pallas import tpu as pltpu

F32 = jnp.float32
BF16 = jnp.bfloat16
HIGHEST = lax.Precision.HIGHEST

D_MODEL = 1024
DEPTH = 2
HEAD_DIM = 64
MLSTM_DIM = 256
CONV_DIM = 256
ATTN_DIM = 512
MLSTM_HEADS = MLSTM_DIM // HEAD_DIM
ATTN_HEADS = ATTN_DIM // HEAD_DIM
QK_CONV = 4
CONF_KERNEL = 31
MOBA_BLOCK = 256
MOBA_TOPK = 3
REL_BUCKETS = 32
REL_MAX_DIST = 128
D_FF = 2752
FFN_CONV = 3
EPS = 1e-6
NEG = -1e30

LANES = 128
SUBLANES = 8
BF16_SUBLANES = 16
VMEM_LIMIT = 56 * 1024 * 1024

GATE_PAD = LANES
ZM_W = 4 * MLSTM_DIM + GATE_PAD
ZC_W = 2 * CONV_DIM
ZA_W = 3 * ATTN_DIM
P_W = ZM_W + ZC_W + ZA_W
D_FFP = -(-D_FF // 256) * 256
FFN_CHUNK = 256
N_FFN_CHUNKS = D_FFP // FFN_CHUNK
FFN_DOWN_ENDS = (4, 8, N_FFN_CHUNKS - 1, N_FFN_CHUNKS)

TM_IN = 512
TM_FFN = 1024
PREP_ROWS = 256
MLSTM_L = 256
MLSTM_STATE_ROWS = 2 * HEAD_DIM + BF16_SUBLANES
CONF_ROWS = 32
CONF_CHAINS = 2
CONF_PAD = 32
MAX_BLOCKS = SUBLANES
PEN_BIG = 2.0 ** 100

_NT = (((1,), (1,)), ((), ()))


def _t5_saturation_distance():
    n = np.arange(1, 4 * MOBA_BLOCK, dtype=np.float32)
    max_exact = REL_BUCKETS // 2
    large = max_exact + (np.log(n / max_exact) / math.log(REL_MAX_DIST / max_exact)
                         * (REL_BUCKETS - max_exact)).astype(np.int32)
    bucket = np.where(n < max_exact, n.astype(np.int32), np.minimum(large, REL_BUCKETS - 1))
    not_last = np.nonzero(bucket != REL_BUCKETS - 1)[0]
    return int(n[not_last[-1]]) + 1


assert _t5_saturation_distance() <= MOBA_BLOCK + 1


def _sigmoid(x):
    return 1.0 / (1.0 + jnp.exp2(x * (-math.log2(math.e))))


def _log_sigmoid(x):
    return jnp.minimum(x, 0.0) - jnp.log1p(jnp.exp(-jnp.abs(x)))


def _rms(xf, g):
    return xf * lax.rsqrt(jnp.mean(xf * xf, axis=-1, keepdims=True) + EPS) * g


def _dot(a, b, **kw):
    return jnp.dot(a, b, preferred_element_type=F32, **kw)


def _const_spec(shape):
    nd = len(shape)
    return pl.BlockSpec(shape, lambda *_: (0,) * nd)


def _layer_spec(shape, layer, **kw):
    nd = len(shape)
    return pl.BlockSpec((None,) + tuple(shape), lambda *_: (layer,) + (0,) * nd, **kw)


def _conformer_tile(up_ref, r0, w_ref, start, ln_g, ln_b, never):
    win_rows = CONF_ROWS + CONF_PAD
    halves = []
    for c0 in range(0, CONV_DIM, LANES):
        win = up_ref[r0:r0 + win_rows, c0:c0 + LANES]
        seed = start[:, c0:c0 + LANES]
        if halves:
            seed = jnp.where(never, halves[-1][0:1, :], seed)
        acc = jnp.broadcast_to(seed, (CONF_ROWS, LANES))
        for r in range(SUBLANES):
            rolled = win if r == 0 else pltpu.roll(win, win_rows - r, axis=0)
            for j in range(CONF_KERNEL):
                off = CONF_PAD - (CONF_KERNEL - 1) + j
                if off % SUBLANES == r:
                    base = off - r
                    acc = acc + w_ref[j:j + 1, c0:c0 + LANES] * rolled[base:base + CONF_ROWS]
        halves.append(acc)
    acc = jnp.concatenate(halves, axis=1)
    mu = jnp.mean(acc, axis=-1, keepdims=True)
    d = acc - mu
    var = jnp.mean(d * d, axis=-1, keepdims=True)
    y = d * lax.rsqrt(var + EPS) * ln_g + ln_b
    return y * _sigmoid(y)


def _in_proj_kernel(x_ref, g_ref, w_ref, cw_ref, cb_ref, lg_ref, lb_ref, zm_ref, za_ref, yc_ref,
                    up_ref, *, layer, tiles_per_seq):
    tm = TM_IN
    row = slice(layer, layer + 1)
    first = (pl.program_id(0) % tiles_per_seq) == 0
    h = _rms(x_ref[...], g_ref[row, :]).astype(BF16)

    def proj(c0, c1):
        return lax.dot_general(h, w_ref[c0:c1, :], _NT, preferred_element_type=F32)

    zc = proj(ZM_W, ZM_W + ZC_W)
    up_ref[0:CONF_PAD, :] = jnp.where(first, 0.0, up_ref[tm:tm + CONF_PAD, :])
    up_ref[CONF_PAD:CONF_PAD + tm, :] = zc[:, 0:CONV_DIM] * _sigmoid(zc[:, CONV_DIM:ZC_W])
    zm_ref[...] = proj(0, ZM_W)
    za_ref[...] = proj(ZM_W + ZC_W, P_W).astype(BF16)
    bias, ln_g, ln_b = cb_ref[row, :], lg_ref[row, :], lb_ref[row, :]
    never = pl.program_id(0) < 0
    last = [None] * CONF_CHAINS
    for t in range(tm // CONF_ROWS):
        r0 = t * CONF_ROWS
        prev = last[t % CONF_CHAINS]
        start = bias if prev is None else jnp.where(never, prev[0:1, :], bias)
        y = _conformer_tile(up_ref, r0, cw_ref, start, ln_g, ln_b, never)
        last[t % CONF_CHAINS] = y
        yc_ref[r0:r0 + CONF_ROWS, :] = y.astype(yc_ref.dtype)


def _in_proj(x2d, g, w_r, conf_w, conf_b, conf_ln_g, conf_ln_b, layer, seq):
    n = x2d.shape[0]
    return pl.pallas_call(
        functools.partial(_in_proj_kernel, layer=layer, tiles_per_seq=seq // TM_IN),
        out_shape=(jax.ShapeDtypeStruct((n, ZM_W), F32),
                   jax.ShapeDtypeStruct((n, ZA_W), BF16),
                   jax.ShapeDtypeStruct((n, CONV_DIM), BF16)),
        grid=(n // TM_IN,),
        in_specs=[pl.BlockSpec((TM_IN, D_MODEL), lambda i: (i, 0)),
                  _const_spec((DEPTH, D_MODEL)),
                  _layer_spec((P_W, D_MODEL), layer),
                  _layer_spec((CONF_KERNEL, CONV_DIM), layer),
                  _const_spec((DEPTH, CONV_DIM)),
                  _const_spec((DEPTH, CONV_DIM)),
                  _const_spec((DEPTH, CONV_DIM))],
        out_specs=(pl.BlockSpec((TM_IN, ZM_W), lambda i: (i, 0)),
                   pl.BlockSpec((TM_IN, ZA_W), lambda i: (i, 0)),
                   pl.BlockSpec((TM_IN, CONV_DIM), lambda i: (i, 0))),
        scratch_shapes=[pltpu.VMEM((CONF_PAD + TM_IN, CONV_DIM), F32)],
        compiler_params=pltpu.CompilerParams(
            dimension_semantics=("arbitrary",), vmem_limit_bytes=VMEM_LIMIT),
        name="in_proj",
    )(x2d, g, w_r, conf_w, conf_b, conf_ln_g, conf_ln_b)


def _mlstm_kernel(zm_ref, cw_ref, cb_ref, gb_ref, hg_ref, o_ref, gt_ref, *, layer, seq):
    L = MLSTM_L
    n_chunks = seq // L
    nh = MLSTM_HEADS
    scale = HEAD_DIM ** -0.5
    lane = lax.broadcasted_iota(jnp.int32, (1, LANES), 1)
    head0 = lane < HEAD_DIM

    row = slice(layer, layer + 1)
    gates = zm_ref[:, 4 * MLSTM_DIM:ZM_W] + gb_ref[row, :]
    is_f = (lane >= nh) & (lane < 2 * nh)
    gates = jnp.where(is_f, _log_sigmoid(gates), gates)
    gt_ref[...] = gates.T[0:SUBLANES, :]

    ri = lax.broadcasted_iota(jnp.int32, (L, L), 0)
    ci = lax.broadcasted_iota(jnp.int32, (L, L), 1)
    visible = ri <= ci
    tri_t = visible.astype(F32)
    prow = lax.broadcasted_iota(jnp.int32, (4 * nh, LANES), 0)
    plane = lax.broadcasted_iota(jnp.int32, (4 * nh, LANES), 1)
    place = ((prow < 3 * nh) & (plane == prow % nh)).astype(BF16)
    srow = lax.broadcasted_iota(jnp.int32, (MLSTM_STATE_ROWS, LANES), 0)
    scol = lax.broadcasted_iota(jnp.int32, (MLSTM_STATE_ROWS, LANES), 1)
    srow_head = jnp.where(srow < LANES, srow // HEAD_DIM, srow - LANES)
    state_mask = (scol // HEAD_DIM) == srow_head
    row_is_head0 = srow_head[:, 0:1] == 0
    vrow_is_head0 = lax.broadcasted_iota(jnp.int32, (LANES, 1), 0) < HEAD_DIM
    extra_row = lax.broadcasted_iota(jnp.int32, (MLSTM_STATE_ROWS - LANES, 1), 0)
    ones_rows = jnp.ones((MLSTM_STATE_ROWS - LANES, L), BF16)
    cb = cb_ref[row, :]
    hg = hg_ref[row, :]

    def chunk(c, carry):
        states, m_prev = carry
        r0 = pl.multiple_of(c * L, L)
        gtc = gt_ref[:, pl.ds(r0, L)]
        brows = _dot(gtc, tri_t, precision=HIGHEST)
        e_rows = gtc[0:nh] - brows[nh:2 * nh]
        e_hi = e_rows.astype(BF16).astype(F32)
        e_mid = (e_rows - e_hi).astype(BF16).astype(F32)
        e_lo = e_rows - e_hi - e_mid
        e_terms = jnp.concatenate([e_hi, e_mid, e_lo, jnp.zeros_like(e_lo)], axis=0).astype(BF16)
        e_cols = lax.dot_general(e_terms, place, (((0,), (0,)), ((), ())),
                                 preferred_element_type=F32)

        xcur = zm_ref[pl.ds(r0, L), 0:2 * MLSTM_DIM]
        pr = pl.multiple_of(jnp.maximum(r0 - SUBLANES, 0), SUBLANES)
        xprev = jnp.where(c > 0, zm_ref[pl.ds(pr, SUBLANES), 0:2 * MLSTM_DIM], 0.0)
        xcat = jnp.concatenate([xprev, xcur], axis=0)
        y = cb
        for j in range(QK_CONV):
            off = SUBLANES - (QK_CONV - 1) + j
            y = y + cw_ref[j:j + 1, :] * xcat[off:off + L]
        qk = y * _sigmoid(y)

        new_states = []
        new_m = []
        for p in range(2):
            lo = p * LANES
            q_t = (qk[:, lo:lo + LANES] * scale).T.astype(BF16)
            k_b = qk[:, MLSTM_DIM + lo:MLSTM_DIM + lo + LANES].astype(BF16)
            v_t = zm_ref[pl.ds(r0, L), 2 * MLSTM_DIM + lo:2 * MLSTM_DIM + lo + LANES].T
            v_tb = v_t.astype(BF16)
            inter = _dot(states[p].astype(BF16), q_t)
            hn_t, wks, decays = [], [], []
            for hh in range(2):
                h = 2 * p + hh
                mp = m_prev[h]
                e_col = e_cols[:, h:h + 1]
                e_row = e_rows[h:h + 1]
                b_row = brows[nh + h:nh + h + 1]
                em = jnp.where(visible, e_col, NEG)
                g = jnp.maximum(mp, jnp.max(em, axis=0, keepdims=True))
                w_t = jnp.exp(em - g)
                k_h = jnp.where(head0 if hh == 0 else ~head0, k_b, jnp.zeros_like(k_b))
                s_t = _dot(k_h, q_t)
                lhs = jnp.concatenate(
                    [v_tb[hh * HEAD_DIM:(hh + 1) * HEAD_DIM], ones_rows], axis=0)
                r = _dot(lhs, (s_t * w_t).astype(BF16))
                a = jnp.exp(mp - g)
                num = a * inter[hh * HEAD_DIM:(hh + 1) * HEAD_DIM] + r[0:HEAD_DIM]
                den = a * inter[LANES + hh:LANES + hh + 1] + r[HEAD_DIM:HEAD_DIM + 1]
                hv = num * (1.0 / jnp.maximum(jnp.abs(den), jnp.exp(-(b_row + g))))
                mu = jnp.mean(hv, axis=0, keepdims=True)
                d = hv - mu
                var = jnp.mean(d * d, axis=0, keepdims=True)
                hn_t.append(d * lax.rsqrt(var + EPS))
                g_last = jnp.maximum(mp, jnp.max(e_row, axis=1, keepdims=True))
                wks.append(jnp.exp(e_row - g_last))
                decays.append(jnp.exp(mp - g_last))
                new_m.append(b_row[:, L - 1:L] + g_last)
            vw = v_t * jnp.where(vrow_is_head0, wks[0], wks[1])
            extra = jnp.where(extra_row == 0, wks[0], jnp.where(extra_row == 1, wks[1], 0.0))
            upd = _dot(jnp.concatenate([vw, extra], axis=0).astype(BF16), k_b)
            decay_rows = jnp.where(row_is_head0, decays[0], decays[1])
            new_states.append(decay_rows * states[p] + jnp.where(state_mask, upd, 0.0))

            hn = jnp.concatenate(hn_t, axis=0).T
            og = zm_ref[pl.ds(r0, L), 3 * MLSTM_DIM + lo:3 * MLSTM_DIM + lo + LANES]
            o_ref[pl.ds(r0, L), lo:lo + LANES] = (hn * hg[:, lo:lo + LANES] * _sigmoid(og)).astype(o_ref.dtype)
        return tuple(new_states), tuple(new_m)

    init = (tuple(jnp.zeros((MLSTM_STATE_ROWS, LANES), F32) for _ in range(2)),
            tuple(jnp.full((1, 1), NEG, F32) for _ in range(nh)))
    lax.fori_loop(0, n_chunks, chunk, init, unroll=True)


def _mlstm(zm, cw, cb, gate_bias, head_g, layer, batch, seq):
    return pl.pallas_call(
        functools.partial(_mlstm_kernel, layer=layer, seq=seq),
        out_shape=jax.ShapeDtypeStruct((batch * seq, MLSTM_DIM), BF16),
        grid=(batch,),
        in_specs=[pl.BlockSpec((seq, ZM_W), lambda b: (b, 0)),
                  _layer_spec((QK_CONV, 2 * MLSTM_DIM), layer),
                  _const_spec((DEPTH, 2 * MLSTM_DIM)),
                  _const_spec((DEPTH, GATE_PAD)),
                  _const_spec((DEPTH, MLSTM_DIM))],
        out_specs=pl.BlockSpec((seq, MLSTM_DIM), lambda b: (b, 0)),
        scratch_shapes=[pltpu.VMEM((SUBLANES, seq), F32)],
        compiler_params=pltpu.CompilerParams(
            dimension_semantics=("arbitrary",), vmem_limit_bytes=VMEM_LIMIT),
        name="mlstm",
    )(zm, cw, cb, gate_bias, head_g)


def _bias_tiles_kernel(rb_ref, o_ref):
    h = pl.program_id(0)
    blk = MOBA_BLOCK
    span = 2 * blk
    max_exact = REL_BUCKETS // 2
    k = lax.broadcasted_iota(jnp.int32, (1, span), 1)
    n = (span - k) % span
    nf = jnp.maximum(n, 1).astype(F32)
    large = max_exact + (jnp.log(nf / max_exact) / math.log(REL_MAX_DIST / max_exact)
                         * (REL_BUCKETS - max_exact)).astype(jnp.int32)
    large = jnp.minimum(large, REL_BUCKETS - 1)
    bucket = jnp.where(n < max_exact, n, large)
    by_dist = jnp.zeros((1, span), F32)
    for bk in range(REL_BUCKETS):
        by_dist = jnp.where(bucket == bk, rb_ref[bk, h], by_dist)
    toeplitz = pltpu.roll(jnp.broadcast_to(by_dist, (blk, span)), 0, axis=1, stride=1,
                          stride_axis=0)
    i = lax.broadcasted_iota(jnp.int32, (blk, blk), 0)
    j = lax.broadcasted_iota(jnp.int32, (blk, blk), 1)
    o_ref[0, 0] = jnp.where(i >= j, toeplitz[:, 0:blk], NEG)
    o_ref[0, 1] = toeplitz[:, blk:span]


def _bias_tiles(rel_bias):
    return pl.pallas_call(
        _bias_tiles_kernel,
        out_shape=jax.ShapeDtypeStruct((ATTN_HEADS, 2, MOBA_BLOCK, MOBA_BLOCK), F32),
        grid=(ATTN_HEADS,),
        in_specs=[pl.BlockSpec(memory_space=pltpu.SMEM)],
        out_specs=pl.BlockSpec((1, 2, MOBA_BLOCK, MOBA_BLOCK), lambda h: (h, 0, 0, 0)),
        compiler_params=pltpu.CompilerParams(dimension_semantics=("arbitrary",)),
        name="bias_tiles",
    )(rel_bias)


def _moba_kernel(far_ref, q_ref, k_ref, v_ref, d_ref, o_ref, qa_ref, ka_ref, s_ref, va_ref,
                 p_ref, *, seq):
    hp = pl.program_id(0)
    blk = MOBA_BLOCK
    nb = seq // blk
    scale = HEAD_DIM ** -0.5
    lane = lax.broadcasted_iota(jnp.int32, (1, LANES), 1)
    head0 = lane < HEAD_DIM

    blk_i = lax.broadcasted_iota(jnp.int32, (MAX_BLOCKS, seq), 0)
    own_i = lax.broadcasted_iota(jnp.int32, (MAX_BLOCKS, seq), 1) // blk
    row_blk = lax.broadcasted_iota(jnp.int32, (seq, LANES), 0) // blk
    lane_full = lax.broadcasted_iota(jnp.int32, (seq, LANES), 1)

    avg = jnp.where(blk_i == own_i, 1.0 / blk, 0.0).astype(BF16)
    kmean = _dot(avg, k_ref[...])
    km_hi = kmean.astype(BF16).astype(F32)
    km_mid = (kmean - km_hi).astype(BF16).astype(F32)
    km_lo = kmean - km_hi - km_mid
    gate_lhs = jnp.concatenate(
        [jnp.where(head0 if hh == 0 else ~head0, part, 0.0)
         for hh in range(2) for part in (km_hi, km_mid, km_lo)], axis=0).astype(BF16)
    gates = lax.dot_general(gate_lhs, q_ref[...], _NT, preferred_element_type=F32)
    q_scaled = (q_ref[...].astype(F32) * scale).astype(BF16)
    sel_row = lax.broadcasted_iota(jnp.int32, (2 * MAX_BLOCKS, LANES), 0)
    sel_lane = lax.broadcasted_iota(jnp.int32, (2 * MAX_BLOCKS, LANES), 1)

    for hh in range(2):
        mine = head0 if hh == 0 else ~head0
        g0 = 3 * MAX_BLOCKS * hh
        gate = (gates[g0:g0 + MAX_BLOCKS] + gates[g0 + MAX_BLOCKS:g0 + 2 * MAX_BLOCKS]
                + gates[g0 + 2 * MAX_BLOCKS:g0 + 3 * MAX_BLOCKS])
        cand = blk_i < own_i
        g = jnp.where(cand, gate, NEG)
        rank = jnp.zeros((MAX_BLOCKS, seq), jnp.int32)
        for m in range(MAX_BLOCKS):
            gm = g[m:m + 1, :]
            beats = (gm > g) | ((gm == g) & (blk_i > m))
            rank = rank + beats.astype(jnp.int32)
        keep = (cand & (rank < MOBA_TOPK)) | (blk_i >= own_i)
        pen = jnp.where(keep, 0.0, -PEN_BIG)
        pen_lane0 = (1 - hh) * HEAD_DIM
        pen16 = jnp.concatenate([pen, jnp.zeros_like(pen)], axis=0).astype(BF16)
        place = ((sel_row < MAX_BLOCKS) & (sel_lane == pen_lane0 + sel_row)).astype(BF16)
        pen_t = lax.dot_general(pen16, place, (((0,), (0,)), ((), ())),
                                preferred_element_type=F32)
        qa_ref[hh] = jnp.where(mine, q_scaled, pen_t.astype(BF16))
        onehot = (lane_full == pen_lane0 + row_blk).astype(BF16)
        ka_ref[hh] = jnp.where(mine, k_ref[...], onehot)

    va_ref[:, 0:LANES] = v_ref[...]
    va_ref[:, LANES:2 * LANES] = jnp.ones((seq, LANES), BF16)

    half = blk // 2

    def pass1(own, hh):
        q = qa_ref[hh, own * blk:(own + 1) * blk, :]
        far_bias = far_ref[2 * hp + hh]
        mx = {}
        for n in range(own + 1):
            s = lax.dot_general(q, ka_ref[hh, n * blk:(n + 1) * blk, :], _NT,
                                preferred_element_type=F32)
            is_far = n < own - 1
            if n == own:
                s = s + d_ref[hh, 0]
            elif n == own - 1:
                s = s + d_ref[hh, 1]
            s_ref[hh, n] = s
            folded = jnp.maximum(s[:, :half], s[:, half:])
            mx[is_far] = jnp.maximum(mx[is_far], folded) if is_far in mx else folded
        mx_all = jnp.maximum(mx[False], mx[True] + far_bias) if True in mx else mx[False]
        m = jnp.max(mx_all, axis=1, keepdims=True)
        return m, m - far_bias

    def pass2(own, hh, m, m_far):
        for n in range(own + 1):
            p = jnp.exp(s_ref[hh, n] - (m_far if n < own - 1 else m))
            p_ref[hh, :, n * blk:(n + 1) * blk] = p.astype(BF16)
        keys = (own + 1) * blk
        res = _dot(p_ref[hh, :, 0:keys], va_ref[0:keys, :])
        return res[:, 0:LANES] / res[:, LANES:2 * LANES]

    items = [(own, hh) for own in range(nb) for hh in range(2)]
    stats = pass1(*items[0])
    outs = []
    for i, (own, hh) in enumerate(items):
        next_stats = pass1(*items[i + 1]) if i + 1 < len(items) else None
        outs.append(pass2(own, hh, *stats))
        stats = next_stats
        if hh == 1:
            o_ref[own * blk:(own + 1) * blk, :] = jnp.where(head0, *outs).astype(o_ref.dtype)
            outs = []


def _moba(za, bias_tiles, far_bias, batch, seq):
    n_pairs = ATTN_HEADS // 2
    assert seq % MOBA_BLOCK == 0 and seq // MOBA_BLOCK <= MAX_BLOCKS
    return pl.pallas_call(
        functools.partial(_moba_kernel, seq=seq),
        out_shape=jax.ShapeDtypeStruct((batch * seq, ATTN_DIM), BF16),
        grid=(n_pairs, batch),
        in_specs=[pl.BlockSpec(memory_space=pltpu.SMEM),
                  pl.BlockSpec((seq, LANES), lambda hp, b: (b, hp)),
                  pl.BlockSpec((seq, LANES), lambda hp, b: (b, n_pairs + hp)),
                  pl.BlockSpec((seq, LANES), lambda hp, b: (b, 2 * n_pairs + hp)),
                  pl.BlockSpec((2, 2, MOBA_BLOCK, MOBA_BLOCK), lambda hp, b: (hp, 0, 0, 0))],
        out_specs=pl.BlockSpec((seq, LANES), lambda hp, b: (b, hp)),
        scratch_shapes=[pltpu.VMEM((2, seq, LANES), BF16),
                        pltpu.VMEM((2, seq, LANES), BF16),
                        pltpu.VMEM((2, seq // MOBA_BLOCK, MOBA_BLOCK, MOBA_BLOCK), F32),
                        pltpu.VMEM((seq, 2 * LANES), BF16),
                        pltpu.VMEM((2, MOBA_BLOCK, seq), BF16)],
        compiler_params=pltpu.CompilerParams(
            dimension_semantics=("arbitrary", "arbitrary"), vmem_limit_bytes=VMEM_LIMIT),
        name="moba",
    )(far_bias, za, za, za, bias_tiles)


def _out_ffn_kernel(x_ref, ym_ref, yc_ref, ya_ref, wo_ref, g2_ref, wup_ref, cw_ref, cb_ref,
                    wdn_ref, fg_ref, o_ref, hn_ref, acc_ref, act_ref, tail_ref,
                    *, layer, tiles_per_seq, final_norm):
    tm = TM_FFN
    row = slice(layer, layer + 1)
    first = (pl.program_id(0) % tiles_per_seq) == 0
    y = jnp.concatenate([ym_ref[...], yc_ref[...], ya_ref[...]], axis=1)
    x1 = x_ref[...] + _dot(y, wo_ref[...])
    hn_ref[...] = _rms(x1, g2_ref[row, :]).astype(BF16)
    acc_ref[...] = x1
    row8 = lax.broadcasted_iota(jnp.int32, (SUBLANES, FFN_CHUNK), 0)

    def up_proj(c):
        return [_dot(hn_ref[...], wup_ref[:, part * D_FFP + c * FFN_CHUNK:
                                          part * D_FFP + (c + 1) * FFN_CHUNK]) for part in range(2)]

    u_next = up_proj(0)
    for c in range(N_FFN_CHUNKS):
        u_pair = u_next
        if c + 1 < N_FFN_CHUNKS:
            u_next = up_proj(c + 1)
        conv = []
        for part in range(2):
            c0 = part * D_FFP + c * FFN_CHUNK
            u = u_pair[part]
            tail = jnp.where(first, 0.0, tail_ref[2 * c + part])
            tail_ref[2 * c + part] = u[tm - SUBLANES:tm, :]
            acc = cb_ref[row, c0:c0 + FFN_CHUNK] + cw_ref[FFN_CONV - 1:FFN_CONV, c0:c0 + FFN_CHUNK] * u
            for d in range(1, FFN_CONV):
                shifted = pltpu.roll(u, d, axis=0)
                top = jnp.where(row8 < d, pltpu.roll(tail, d, axis=0), shifted[0:SUBLANES])
                shifted = jnp.concatenate([top, shifted[SUBLANES:]], axis=0)
                j = FFN_CONV - 1 - d
                acc = acc + cw_ref[j:j + 1, c0:c0 + FFN_CHUNK] * shifted
            conv.append(acc)
        act_ref[:, c * FFN_CHUNK:(c + 1) * FFN_CHUNK] = (
            conv[0] * _sigmoid(conv[0]) * conv[1]).astype(BF16)
        if c + 1 in FFN_DOWN_ENDS:
            g = FFN_DOWN_ENDS.index(c + 1)
            k0 = (FFN_DOWN_ENDS[g - 1] if g else 0) * FFN_CHUNK
            k1 = (c + 1) * FFN_CHUNK
            acc_ref[...] += _dot(act_ref[:, k0:k1], wdn_ref[k0:k1, :])

    out = acc_ref[...]
    if final_norm:
        out = _rms(out, fg_ref[...])
    o_ref[...] = out


def _out_ffn(x2d, ym, yc, ya, wo, g2, wup, cw, cb, wdn, fg, layer, seq, final_norm):
    n = x2d.shape[0]
    tiles_per_seq = seq // TM_FFN

    def const(shape):
        return _layer_spec(shape, layer, pipeline_mode=pl.Buffered(1))

    return pl.pallas_call(
        functools.partial(_out_ffn_kernel, layer=layer, tiles_per_seq=tiles_per_seq,
                          final_norm=final_norm),
        out_shape=jax.ShapeDtypeStruct((n, D_MODEL), F32),
        grid=(n // TM_FFN,),
        in_specs=[pl.BlockSpec((TM_FFN, D_MODEL), lambda i: (i, 0)),
                  pl.BlockSpec((TM_FFN, MLSTM_DIM), lambda i: (i, 0)),
                  pl.BlockSpec((TM_FFN, CONV_DIM), lambda i: (i, 0)),
                  pl.BlockSpec((TM_FFN, ATTN_DIM), lambda i: (i, 0)),
                  const((D_MODEL, D_MODEL)),
                  _const_spec((DEPTH, D_MODEL)),
                  const((D_MODEL, 2 * D_FFP)),
                  const((FFN_CONV, 2 * D_FFP)),
                  _const_spec((DEPTH, 2 * D_FFP)),
                  const((D_FFP, D_MODEL)),
                  _const_spec((1, D_MODEL))],
        out_specs=pl.BlockSpec((TM_FFN, D_MODEL), lambda i: (i, 0)),
        scratch_shapes=[pltpu.VMEM((TM_FFN, D_MODEL), BF16),
                        pltpu.VMEM((TM_FFN, D_MODEL), F32),
                        pltpu.VMEM((TM_FFN, D_FFP), BF16),
                        pltpu.VMEM((2 * N_FFN_CHUNKS, SUBLANES, FFN_CHUNK), F32)],
        compiler_params=pltpu.CompilerParams(
            dimension_semantics=("arbitrary",), vmem_limit_bytes=VMEM_LIMIT),
        name="out_ffn",
    )(x2d, ym, yc, ya, wo, g2, wup, cw, cb, wdn, fg)


def _pad_last(a, width):
    return jnp.pad(a, [(0, 0)] * (a.ndim - 1) + [(0, width - a.shape[-1])])


def _move_cols_kernel(x_ref, o_ref, *, moves):
    end = 0
    for src, dst, width in moves:
        if dst > end:
            o_ref[:, end:dst] = jnp.zeros((o_ref.shape[0], dst - end), o_ref.dtype)
        o_ref[:, dst:dst + width] = x_ref[:, src:src + width].astype(o_ref.dtype)
        end = dst + width
    if end < o_ref.shape[1]:
        o_ref[:, end:] = jnp.zeros((o_ref.shape[0], o_ref.shape[1] - end), o_ref.dtype)


def _move_cols(w, moves, out_cols):
    depth, rows, cols = w.shape
    return pl.pallas_call(
        functools.partial(_move_cols_kernel, moves=tuple(moves)),
        out_shape=jax.ShapeDtypeStruct((depth, rows, out_cols), BF16),
        grid=(depth, rows // PREP_ROWS),
        in_specs=[pl.BlockSpec((None, PREP_ROWS, cols), lambda l, i: (l, i, 0))],
        out_specs=pl.BlockSpec((None, PREP_ROWS, out_cols), lambda l, i: (l, i, 0)),
        compiler_params=pltpu.CompilerParams(
            dimension_semantics=("arbitrary", "arbitrary"), vmem_limit_bytes=VMEM_LIMIT),
        name="move_cols",
    )(w)


def _w_in_rows_kernel(x_ref, o_ref):
    g0 = 4 * MLSTM_DIM
    g1 = g0 + 2 * MLSTM_HEADS
    cols = x_ref.shape[1]
    o_ref[0:g0, :] = x_ref[0:g0, :].astype(o_ref.dtype)
    gate_tile = jnp.concatenate([x_ref[g0:g1, :], jnp.zeros((GATE_PAD - (g1 - g0), cols), F32)], axis=0)
    o_ref[g0:ZM_W, :] = gate_tile.astype(o_ref.dtype)
    o_ref[ZM_W:P_W, :] = x_ref[g1:, :].astype(o_ref.dtype)


def _prep_w_in(w_in):
    depth, d_in, p_in = w_in.shape
    return pl.pallas_call(
        _w_in_rows_kernel,
        out_shape=jax.ShapeDtypeStruct((depth, P_W, d_in), BF16),
        grid=(depth, d_in // PREP_ROWS),
        in_specs=[pl.BlockSpec((None, p_in, PREP_ROWS), lambda l, i: (l, 0, i))],
        out_specs=pl.BlockSpec((None, P_W, PREP_ROWS), lambda l, i: (l, 0, i)),
        compiler_params=pltpu.CompilerParams(
            dimension_semantics=("arbitrary", "arbitrary"), vmem_limit_bytes=VMEM_LIMIT),
        name="w_in_rows",
    )(jnp.swapaxes(w_in, 1, 2))


def _prep_w_up(w_up):
    return _move_cols(w_up, [(0, 0, D_FF), (D_FF, D_FFP, D_FF)], 2 * D_FFP)


def _prep_ffn_cols(a):
    return jnp.concatenate([_pad_last(a[..., :D_FF], D_FFP), _pad_last(a[..., D_FF:], D_FFP)],
                           axis=-1)


def _pad_rows_kernel(x_ref, o_ref):
    rows = x_ref.shape[0]
    o_ref[0:rows, :] = x_ref[...].astype(o_ref.dtype)
    o_ref[rows:, :] = jnp.zeros((o_ref.shape[0] - rows, o_ref.shape[1]), o_ref.dtype)


def _prep_w_down(w_down):
    depth, rows, cols = w_down.shape
    return pl.pallas_call(
        _pad_rows_kernel,
        out_shape=jax.ShapeDtypeStruct((depth, D_FFP, cols), BF16),
        grid=(depth, cols // PREP_ROWS),
        in_specs=[pl.BlockSpec((None, rows, PREP_ROWS), lambda l, i: (l, 0, i))],
        out_specs=pl.BlockSpec((None, D_FFP, PREP_ROWS), lambda l, i: (l, 0, i)),
        compiler_params=pltpu.CompilerParams(
            dimension_semantics=("arbitrary", "arbitrary"), vmem_limit_bytes=VMEM_LIMIT),
        name="pad_rows",
    )(w_down)


def _prep_gate_bias(ig_b, fg_b):
    return _pad_last(jnp.concatenate([ig_b, fg_b], axis=-1), GATE_PAD)


def kernel(x, norm1_g, w_in, mlstm_qk_conv_w, mlstm_qk_conv_b, mlstm_ig_b, mlstm_fg_b, mlstm_head_g, conf_dw_w, conf_dw_b, conf_ln_g, conf_ln_b, rel_bias, w_out, norm2_g, ffn_w_up, ffn_conv_w, ffn_conv_b, ffn_w_down, final_g):
    batch, seq, _ = x.shape
    bias_tiles = _bias_tiles(rel_bias)
    far_bias = rel_bias[REL_BUCKETS - 1]
    w_in_r = _prep_w_in(w_in)
    gate_bias = _prep_gate_bias(mlstm_ig_b, mlstm_fg_b)
    w_out_b = w_out.astype(BF16)
    w_up_r = _prep_w_up(ffn_w_up)
    f_w = _prep_ffn_cols(ffn_conv_w)
    f_b = _prep_ffn_cols(ffn_conv_b)
    w_down_r = _prep_w_down(ffn_w_down)
    x2d = x.reshape(batch * seq, D_MODEL)
    for l in range(DEPTH):
        zm, za, yc = _in_proj(x2d, norm1_g, w_in_r, conf_dw_w, conf_dw_b, conf_ln_g, conf_ln_b,
                              l, seq)
        ym = _mlstm(zm, mlstm_qk_conv_w, mlstm_qk_conv_b, gate_bias, mlstm_head_g, l, batch, seq)
        ya = _moba(za, bias_tiles, far_bias, batch, seq)
        x2d = _out_ffn(x2d, ym, yc, ya, w_out_b, norm2_g, w_up_r, f_w, f_b, w_down_r,
                       final_g[None, :], l, seq, l == DEPTH - 1)
    return x2d.reshape(batch, seq, D_MODEL)
```

```python
import functools
import math

import numpy as np
import jax
import jax.numpy as jnp
from jax import lax
from jax.experimental import pallas as pl
from jax.experimental.pallas import tpu as pltpu

F32 = jnp.float32
BF16 = jnp.bfloat16
HIGHEST = lax.Precision.HIGHEST

D_MODEL = 1024
DEPTH = 2
HEAD_DIM = 64
MLSTM_DIM = 256
CONV_DIM = 256
ATTN_DIM = 512
MLSTM_HEADS = MLSTM_DIM // HEAD_DIM
ATTN_HEADS = ATTN_DIM // HEAD_DIM
QK_CONV = 4
CONF_KERNEL = 31
MOBA_BLOCK = 256
MOBA_TOPK = 3
REL_BUCKETS = 32
REL_MAX_DIST = 128
D_FF = 2752
FFN_CONV = 3
EPS = 1e-6
NEG = -1e30

LANES = 128
SUBLANES = 8
BF16_SUBLANES = 16
VMEM_LIMIT = 56 * 1024 * 1024

GATE_PAD = LANES
ZM_W = 4 * MLSTM_DIM + GATE_PAD
ZC_W = 2 * CONV_DIM
ZA_W = 3 * ATTN_DIM
P_W = ZM_W + ZC_W + ZA_W
D_FFP = -(-D_FF // 256) * 256
FFN_CHUNK = 256
N_FFN_CHUNKS = D_FFP // FFN_CHUNK
FFN_DOWN_ENDS = (4, 8, N_FFN_CHUNKS - 1, N_FFN_CHUNKS)

TM_IN = 1024
TM_FFN = 1024
PREP_ROWS = 256
MLSTM_L = 256
MLSTM_STATE_ROWS = 2 * HEAD_DIM + BF16_SUBLANES
CONF_ROWS = 32
CONF_CHAINS = 2
CONF_PAD = 32
MAX_BLOCKS = SUBLANES
PEN_BIG = 2.0 ** 100

_NT = (((1,), (1,)), ((), ()))


def _t5_saturation_distance():
    n = np.arange(1, 4 * MOBA_BLOCK, dtype=np.float32)
    max_exact = REL_BUCKETS // 2
    large = max_exact + (np.log(n / max_exact) / math.log(REL_MAX_DIST / max_exact)
                         * (REL_BUCKETS - max_exact)).astype(np.int32)
    bucket = np.where(n < max_exact, n.astype(np.int32), np.minimum(large, REL_BUCKETS - 1))
    not_last = np.nonzero(bucket != REL_BUCKETS - 1)[0]
    return int(n[not_last[-1]]) + 1


assert _t5_saturation_distance() <= MOBA_BLOCK + 1


def _sigmoid(x):
    return 1.0 / (1.0 + jnp.exp2(x * (-math.log2(math.e))))


def _log_sigmoid(x):
    return jnp.minimum(x, 0.0) - jnp.log1p(jnp.exp(-jnp.abs(x)))


def _rms(xf, g):
    return xf * lax.rsqrt(jnp.mean(xf * xf, axis=-1, keepdims=True) + EPS) * g


def _dot(a, b, **kw):
    return jnp.dot(a, b, preferred_element_type=F32, **kw)


def _const_spec(shape):
    nd = len(shape)
    return pl.BlockSpec(shape, lambda *_: (0,) * nd)


def _layer_spec(shape, layer, **kw):
    nd = len(shape)
    return pl.BlockSpec((None,) + tuple(shape), lambda *_: (layer,) + (0,) * nd, **kw)


def _conformer_tile(up_ref, r0, w_ref, start, ln_g, ln_b, never):
    win_rows = CONF_ROWS + CONF_PAD
    halves = []
    for c0 in range(0, CONV_DIM, LANES):
        win = up_ref[r0:r0 + win_rows, c0:c0 + LANES]
        seed = start[:, c0:c0 + LANES]
        if halves:
            seed = jnp.where(never, halves[-1][0:1, :], seed)
        acc = jnp.broadcast_to(seed, (CONF_ROWS, LANES))
        for r in range(SUBLANES):
            rolled = win if r == 0 else pltpu.roll(win, win_rows - r, axis=0)
            for j in range(CONF_KERNEL):
                off = CONF_PAD - (CONF_KERNEL - 1) + j
                if off % SUBLANES == r:
                    base = off - r
                    acc = acc + w_ref[j:j + 1, c0:c0 + LANES] * rolled[base:base + CONF_ROWS]
        halves.append(acc)
    acc = jnp.concatenate(halves, axis=1)
    mu = jnp.mean(acc, axis=-1, keepdims=True)
    d = acc - mu
    var = jnp.mean(d * d, axis=-1, keepdims=True)
    y = d * lax.rsqrt(var + EPS) * ln_g + ln_b
    return y * _sigmoid(y)


def _in_proj_kernel(x_ref, g_ref, w_ref, cw_ref, cb_ref, lg_ref, lb_ref, zm_ref, za_ref, yc_ref,
                    up_ref, *, layer, tiles_per_seq):
    tm = TM_IN
    row = slice(layer, layer + 1)
    first = (pl.program_id(0) % tiles_per_seq) == 0
    h = _rms(x_ref[...], g_ref[row, :]).astype(BF16)

    def proj(c0, c1):
        return lax.dot_general(h, w_ref[c0:c1, :], _NT, preferred_element_type=F32)

    zc = proj(ZM_W, ZM_W + ZC_W)
    up_ref[0:CONF_PAD, :] = jnp.where(first, 0.0, up_ref[tm:tm + CONF_PAD, :])
    up_ref[CONF_PAD:CONF_PAD + tm, :] = zc[:, 0:CONV_DIM] * _sigmoid(zc[:, CONV_DIM:ZC_W])
    zm_ref[...] = proj(0, ZM_W)
    za_ref[...] = proj(ZM_W + ZC_W, P_W).astype(BF16)
    bias, ln_g, ln_b = cb_ref[row, :], lg_ref[row, :], lb_ref[row, :]
    never = pl.program_id(0) < 0
    last = [None] * CONF_CHAINS
    for t in range(tm // CONF_ROWS):
        r0 = t * CONF_ROWS
        prev = last[t % CONF_CHAINS]
        start = bias if prev is None else jnp.where(never, prev[0:1, :], bias)
        y = _conformer_tile(up_ref, r0, cw_ref, start, ln_g, ln_b, never)
        last[t % CONF_CHAINS] = y
        yc_ref[r0:r0 + CONF_ROWS, :] = y.astype(yc_ref.dtype)


def _in_proj(x2d, g, w_r, conf_w, conf_b, conf_ln_g, conf_ln_b, layer, seq):
    n = x2d.shape[0]
    return pl.pallas_call(
        functools.partial(_in_proj_kernel, layer=layer, tiles_per_seq=seq // TM_IN),
        out_shape=(jax.ShapeDtypeStruct((n, ZM_W), F32),
                   jax.ShapeDtypeStruct((n, ZA_W), BF16),
                   jax.ShapeDtypeStruct((n, CONV_DIM), BF16)),
        grid=(n // TM_IN,),
        in_specs=[pl.BlockSpec((TM_IN, D_MODEL), lambda i: (i, 0)),
                  _const_spec((DEPTH, D_MODEL)),
                  _layer_spec((P_W, D_MODEL), layer),
                  _layer_spec((CONF_KERNEL, CONV_DIM), layer),
                  _const_spec((DEPTH, CONV_DIM)),
                  _const_spec((DEPTH, CONV_DIM)),
                  _const_spec((DEPTH, CONV_DIM))],
        out_specs=(pl.BlockSpec((TM_IN, ZM_W), lambda i: (i, 0)),
                   pl.BlockSpec((TM_IN, ZA_W), lambda i: (i, 0)),
                   pl.BlockSpec((TM_IN, CONV_DIM), lambda i: (i, 0))),
        scratch_shapes=[pltpu.VMEM((CONF_PAD + TM_IN, CONV_DIM), F32)],
        compiler_params=pltpu.CompilerParams(
            dimension_semantics=("arbitrary",), vmem_limit_bytes=VMEM_LIMIT),
        name="in_proj",
    )(x2d, g, w_r, conf_w, conf_b, conf_ln_g, conf_ln_b)


def _mlstm_kernel(zm_ref, cw_ref, cb_ref, gb_ref, hg_ref, o_ref, gt_ref, *, layer, seq):
    L = MLSTM_L
    n_chunks = seq // L
    nh = MLSTM_HEADS
    scale = HEAD_DIM ** -0.5
    lane = lax.broadcasted_iota(jnp.int32, (1, LANES), 1)
    head0 = lane < HEAD_DIM

    row = slice(layer, layer + 1)
    gates = zm_ref[:, 4 * MLSTM_DIM:ZM_W] + gb_ref[row, :]
    is_f = (lane >= nh) & (lane < 2 * nh)
    gates = jnp.where(is_f, _log_sigmoid(gates), gates)
    gt_ref[...] = gates.T[0:SUBLANES, :]

    ri = lax.broadcasted_iota(jnp.int32, (L, L), 0)
    ci = lax.broadcasted_iota(jnp.int32, (L, L), 1)
    visible = ri <= ci
    tri_t = visible.astype(F32)
    prow = lax.broadcasted_iota(jnp.int32, (4 * nh, LANES), 0)
    plane = lax.broadcasted_iota(jnp.int32, (4 * nh, LANES), 1)
    place = ((prow < 3 * nh) & (plane == prow % nh)).astype(BF16)
    srow = lax.broadcasted_iota(jnp.int32, (MLSTM_STATE_ROWS, LANES), 0)
    scol = lax.broadcasted_iota(jnp.int32, (MLSTM_STATE_ROWS, LANES), 1)
    srow_head = jnp.where(srow < LANES, srow // HEAD_DIM, srow - LANES)
    state_mask = (scol // HEAD_DIM) == srow_head
    row_is_head0 = srow_head[:, 0:1] == 0
    vrow_is_head0 = lax.broadcasted_iota(jnp.int32, (LANES, 1), 0) < HEAD_DIM
    extra_row = lax.broadcasted_iota(jnp.int32, (MLSTM_STATE_ROWS - LANES, 1), 0)
    ones_rows = jnp.ones((MLSTM_STATE_ROWS - LANES, L), BF16)
    cb = cb_ref[row, :]
    hg = hg_ref[row, :]

    def chunk(c, carry):
        states, m_prev = carry
        r0 = pl.multiple_of(c * L, L)
        gtc = gt_ref[:, pl.ds(r0, L)]
        brows = _dot(gtc, tri_t, precision=HIGHEST)
        e_rows = gtc[0:nh] - brows[nh:2 * nh]
        e_hi = e_rows.astype(BF16).astype(F32)
        e_mid = (e_rows - e_hi).astype(BF16).astype(F32)
        e_lo = e_rows - e_hi - e_mid
        e_terms = jnp.concatenate([e_hi, e_mid, e_lo, jnp.zeros_like(e_lo)], axis=0).astype(BF16)
        e_cols = lax.dot_general(e_terms, place, (((0,), (0,)), ((), ())),
                                 preferred_element_type=F32)

        xcur = zm_ref[pl.ds(r0, L), 0:2 * MLSTM_DIM]
        pr = pl.multiple_of(jnp.maximum(r0 - SUBLANES, 0), SUBLANES)
        xprev = jnp.where(c > 0, zm_ref[pl.ds(pr, SUBLANES), 0:2 * MLSTM_DIM], 0.0)
        xcat = jnp.concatenate([xprev, xcur], axis=0)
        y = cb
        for j in range(QK_CONV):
            off = SUBLANES - (QK_CONV - 1) + j
            y = y + cw_ref[j:j + 1, :] * xcat[off:off + L]
        qk = y * _sigmoid(y)

        new_states = []
        new_m = []
        for p in range(2):
            lo = p * LANES
            q_t = (qk[:, lo:lo + LANES] * scale).T.astype(BF16)
            k_b = qk[:, MLSTM_DIM + lo:MLSTM_DIM + lo + LANES].astype(BF16)
            v_t = zm_ref[pl.ds(r0, L), 2 * MLSTM_DIM + lo:2 * MLSTM_DIM + lo + LANES].T
            v_tb = v_t.astype(BF16)
            inter = _dot(states[p].astype(BF16), q_t)
            hn_t, wks, decays = [], [], []
            for hh in range(2):
                h = 2 * p + hh
                mp = m_prev[h]
                e_col = e_cols[:, h:h + 1]
                e_row = e_rows[h:h + 1]
                b_row = brows[nh + h:nh + h + 1]
                em = jnp.where(visible, e_col, NEG)
                g = jnp.maximum(mp, jnp.max(em, axis=0, keepdims=True))
                w_t = jnp.exp(em - g)
                k_h = jnp.where(head0 if hh == 0 else ~head0, k_b, jnp.zeros_like(k_b))
                s_t = _dot(k_h, q_t)
                lhs = jnp.concatenate(
                    [v_tb[hh * HEAD_DIM:(hh + 1) * HEAD_DIM], ones_rows], axis=0)
                r = _dot(lhs, (s_t * w_t).astype(BF16))
                a = jnp.exp(mp - g)
                num = a * inter[hh * HEAD_DIM:(hh + 1) * HEAD_DIM] + r[0:HEAD_DIM]
                den = a * inter[LANES + hh:LANES + hh + 1] + r[HEAD_DIM:HEAD_DIM + 1]
                hv = num * (1.0 / jnp.maximum(jnp.abs(den), jnp.exp(-(b_row + g))))
                mu = jnp.mean(hv, axis=0, keepdims=True)
                d = hv - mu
                var = jnp.mean(d * d, axis=0, keepdims=True)
                hn_t.append(d * lax.rsqrt(var + EPS))
                g_last = jnp.maximum(mp, jnp.max(e_row, axis=1, keepdims=True))
                wks.append(jnp.exp(e_row - g_last))
                decays.append(jnp.exp(mp - g_last))
                new_m.append(b_row[:, L - 1:L] + g_last)
            vw = v_t * jnp.where(vrow_is_head0, wks[0], wks[1])
            extra = jnp.where(extra_row == 0, wks[0], jnp.where(extra_row == 1, wks[1], 0.0))
            upd = _dot(jnp.concatenate([vw, extra], axis=0).astype(BF16), k_b)
            decay_rows = jnp.where(row_is_head0, decays[0], decays[1])
            new_states.append(decay_rows * states[p] + jnp.where(state_mask, upd, 0.0))

            hn = jnp.concatenate(hn_t, axis=0).T
            og = zm_ref[pl.ds(r0, L), 3 * MLSTM_DIM + lo:3 * MLSTM_DIM + lo + LANES]
            o_ref[pl.ds(r0, L), lo:lo + LANES] = (hn * hg[:, lo:lo + LANES] * _sigmoid(og)).astype(o_ref.dtype)
        return tuple(new_states), tuple(new_m)

    init = (tuple(jnp.zeros((MLSTM_STATE_ROWS, LANES), F32) for _ in range(2)),
            tuple(jnp.full((1, 1), NEG, F32) for _ in range(nh)))
    lax.fori_loop(0, n_chunks, chunk, init, unroll=True)


def _mlstm(zm, cw, cb, gate_bias, head_g, layer, batch, seq):
    return pl.pallas_call(
        functools.partial(_mlstm_kernel, layer=layer, seq=seq),
        out_shape=jax.ShapeDtypeStruct((batch * seq, MLSTM_DIM), BF16),
        grid=(batch,),
        in_specs=[pl.BlockSpec((seq, ZM_W), lambda b: (b, 0)),
                  _layer_spec((QK_CONV, 2 * MLSTM_DIM), layer),
                  _const_spec((DEPTH, 2 * MLSTM_DIM)),
                  _const_spec((DEPTH, GATE_PAD)),
                  _const_spec((DEPTH, MLSTM_DIM))],
        out_specs=pl.BlockSpec((seq, MLSTM_DIM), lambda b: (b, 0)),
        scratch_shapes=[pltpu.VMEM((SUBLANES, seq), F32)],
        compiler_params=pltpu.CompilerParams(
            dimension_semantics=("arbitrary",), vmem_limit_bytes=VMEM_LIMIT),
        name="mlstm",
    )(zm, cw, cb, gate_bias, head_g)


def _bias_tiles_kernel(rb_ref, o_ref):
    h = pl.program_id(0)
    blk = MOBA_BLOCK
    span = 2 * blk
    max_exact = REL_BUCKETS // 2
    k = lax.broadcasted_iota(jnp.int32, (1, span), 1)
    n = (span - k) % span
    nf = jnp.maximum(n, 1).astype(F32)
    large = max_exact + (jnp.log(nf / max_exact) / math.log(REL_MAX_DIST / max_exact)
                         * (REL_BUCKETS - max_exact)).astype(jnp.int32)
    large = jnp.minimum(large, REL_BUCKETS - 1)
    bucket = jnp.where(n < max_exact, n, large)
    by_dist = jnp.zeros((1, span), F32)
    for bk in range(REL_BUCKETS):
        by_dist = jnp.where(bucket == bk, rb_ref[bk, h], by_dist)
    toeplitz = pltpu.roll(jnp.broadcast_to(by_dist, (blk, span)), 0, axis=1, stride=1,
                          stride_axis=0)
    i = lax.broadcasted_iota(jnp.int32, (blk, blk), 0)
    j = lax.broadcasted_iota(jnp.int32, (blk, blk), 1)
    o_ref[0, 0] = jnp.where(i >= j, toeplitz[:, 0:blk], NEG)
    o_ref[0, 1] = toeplitz[:, blk:span]


def _bias_tiles(rel_bias):
    return pl.pallas_call(
        _bias_tiles_kernel,
        out_shape=jax.ShapeDtypeStruct((ATTN_HEADS, 2, MOBA_BLOCK, MOBA_BLOCK), F32),
        grid=(ATTN_HEADS,),
        in_specs=[pl.BlockSpec(memory_space=pltpu.SMEM)],
        out_specs=pl.BlockSpec((1, 2, MOBA_BLOCK, MOBA_BLOCK), lambda h: (h, 0, 0, 0)),
        compiler_params=pltpu.CompilerParams(dimension_semantics=("arbitrary",)),
        name="bias_tiles",
    )(rel_bias)


def _moba_kernel(far_ref, q_ref, k_ref, v_ref, d_ref, o_ref, qa_ref, ka_ref, s_ref, va_ref,
                 p_ref, *, seq):
    hp = pl.program_id(0)
    blk = MOBA_BLOCK
    nb = seq // blk
    scale = HEAD_DIM ** -0.5
    lane = lax.broadcasted_iota(jnp.int32, (1, LANES), 1)
    head0 = lane < HEAD_DIM

    blk_i = lax.broadcasted_iota(jnp.int32, (MAX_BLOCKS, seq), 0)
    own_i = lax.broadcasted_iota(jnp.int32, (MAX_BLOCKS, seq), 1) // blk
    row_blk = lax.broadcasted_iota(jnp.int32, (seq, LANES), 0) // blk
    lane_full = lax.broadcasted_iota(jnp.int32, (seq, LANES), 1)

    avg = jnp.where(blk_i == own_i, 1.0 / blk, 0.0).astype(BF16)
    kmean = _dot(avg, k_ref[...])
    km_hi = kmean.astype(BF16).astype(F32)
    km_mid = (kmean - km_hi).astype(BF16).astype(F32)
    km_lo = kmean - km_hi - km_mid
    gate_lhs = jnp.concatenate(
        [jnp.where(head0 if hh == 0 else ~head0, part, 0.0)
         for hh in range(2) for part in (km_hi, km_mid, km_lo)], axis=0).astype(BF16)
    gates = lax.dot_general(gate_lhs, q_ref[...], _NT, preferred_element_type=F32)
    q_scaled = (q_ref[...].astype(F32) * scale).astype(BF16)
    sel_row = lax.broadcasted_iota(jnp.int32, (2 * MAX_BLOCKS, LANES), 0)
    sel_lane = lax.broadcasted_iota(jnp.int32, (2 * MAX_BLOCKS, LANES), 1)

    for hh in range(2):
        mine = head0 if hh == 0 else ~head0
        g0 = 3 * MAX_BLOCKS * hh
        gate = (gates[g0:g0 + MAX_BLOCKS] + gates[g0 + MAX_BLOCKS:g0 + 2 * MAX_BLOCKS]
                + gates[g0 + 2 * MAX_BLOCKS:g0 + 3 * MAX_BLOCKS])
        cand = blk_i < own_i
        g = jnp.where(cand, gate, NEG)
        rank = jnp.zeros((MAX_BLOCKS, seq), jnp.int32)
        for m in range(MAX_BLOCKS):
            gm = g[m:m + 1, :]
            beats = (gm > g) | ((gm == g) & (blk_i > m))
            rank = rank + beats.astype(jnp.int32)
        keep = (cand & (rank < MOBA_TOPK)) | (blk_i >= own_i)
        pen = jnp.where(keep, 0.0, -PEN_BIG)
        pen_lane0 = (1 - hh) * HEAD_DIM
        pen16 = jnp.concatenate([pen, jnp.zeros_like(pen)], axis=0).astype(BF16)
        place = ((sel_row < MAX_BLOCKS) & (sel_lane == pen_lane0 + sel_row)).astype(BF16)
        pen_t = lax.dot_general(pen16, place, (((0,), (0,)), ((), ())),
                                preferred_element_type=F32)
        qa_ref[hh] = jnp.where(mine, q_scaled, pen_t.astype(BF16))
        onehot = (lane_full == pen_lane0 + row_blk).astype(BF16)
        ka_ref[hh] = jnp.where(mine, k_ref[...], onehot)

    va_ref[:, 0:LANES] = v_ref[...]
    va_ref[:, LANES:2 * LANES] = jnp.ones((seq, LANES), BF16)

    half = blk // 2

    def pass1(own, hh):
        q = qa_ref[hh, own * blk:(own + 1) * blk, :]
        far_bias = far_ref[2 * hp + hh]
        mx = {}
        for n in range(own + 1):
            s = lax.dot_general(q, ka_ref[hh, n * blk:(n + 1) * blk, :], _NT,
                                preferred_element_type=F32)
            is_far = n < own - 1
            if n == own:
                s = s + d_ref[hh, 0]
            elif n == own - 1:
                s = s + d_ref[hh, 1]
            s_ref[hh, n] = s
            folded = jnp.maximum(s[:, :half], s[:, half:])
            mx[is_far] = jnp.maximum(mx[is_far], folded) if is_far in mx else folded
        mx_all = jnp.maximum(mx[False], mx[True] + far_bias) if True in mx else mx[False]
        m = jnp.max(mx_all, axis=1, keepdims=True)
        return m, m - far_bias

    def pass2(own, hh, m, m_far):
        for n in range(own + 1):
            p = jnp.exp(s_ref[hh, n] - (m_far if n < own - 1 else m))
            p_ref[hh, :, n * blk:(n + 1) * blk] = p.astype(BF16)
        keys = (own + 1) * blk
        res = _dot(p_ref[hh, :, 0:keys], va_ref[0:keys, :])
        return res[:, 0:LANES] / res[:, LANES:2 * LANES]

    items = [(own, hh) for own in range(nb) for hh in range(2)]
    stats = pass1(*items[0])
    outs = []
    for i, (own, hh) in enumerate(items):
        next_stats = pass1(*items[i + 1]) if i + 1 < len(items) else None
        outs.append(pass2(own, hh, *stats))
        stats = next_stats
        if hh == 1:
            o_ref[own * blk:(own + 1) * blk, :] = jnp.where(head0, *outs).astype(o_ref.dtype)
            outs = []


def _moba(za, bias_tiles, far_bias, batch, seq):
    n_pairs = ATTN_HEADS // 2
    assert seq % MOBA_BLOCK == 0 and seq // MOBA_BLOCK <= MAX_BLOCKS
    return pl.pallas_call(
        functools.partial(_moba_kernel, seq=seq),
        out_shape=jax.ShapeDtypeStruct((batch * seq, ATTN_DIM), BF16),
        grid=(n_pairs, batch),
        in_specs=[pl.BlockSpec(memory_space=pltpu.SMEM),
                  pl.BlockSpec((seq, LANES), lambda hp, b: (b, hp)),
                  pl.BlockSpec((seq, LANES), lambda hp, b: (b, n_pairs + hp)),
                  pl.BlockSpec((seq, LANES), lambda hp, b: (b, 2 * n_pairs + hp)),
                  pl.BlockSpec((2, 2, MOBA_BLOCK, MOBA_BLOCK), lambda hp, b: (hp, 0, 0, 0))],
        out_specs=pl.BlockSpec((seq, LANES), lambda hp, b: (b, hp)),
        scratch_shapes=[pltpu.VMEM((2, seq, LANES), BF16),
                        pltpu.VMEM((2, seq, LANES), BF16),
                        pltpu.VMEM((2, seq // MOBA_BLOCK, MOBA_BLOCK, MOBA_BLOCK), F32),
                        pltpu.VMEM((seq, 2 * LANES), BF16),
                        pltpu.VMEM((2, MOBA_BLOCK, seq), BF16)],
        compiler_params=pltpu.CompilerParams(
            dimension_semantics=("arbitrary", "arbitrary"), vmem_limit_bytes=VMEM_LIMIT),
        name="moba",
    )(far_bias, za, za, za, bias_tiles)


def _out_ffn_kernel(x_ref, ym_ref, yc_ref, ya_ref, wo_ref, g2_ref, wup_ref, cw_ref, cb_ref,
                    wdn_ref, fg_ref, o_ref, hn_ref, acc_ref, act_ref, tail_ref,
                    *, layer, tiles_per_seq, final_norm):
    tm = TM_FFN
    row = slice(layer, layer + 1)
    first = (pl.program_id(0) % tiles_per_seq) == 0
    y = jnp.concatenate([ym_ref[...], yc_ref[...], ya_ref[...]], axis=1)
    x1 = x_ref[...] + _dot(y, wo_ref[...])
    hn_ref[...] = _rms(x1, g2_ref[row, :]).astype(BF16)
    acc_ref[...] = x1
    row8 = lax.broadcasted_iota(jnp.int32, (SUBLANES, FFN_CHUNK), 0)

    def up_proj(c):
        return [_dot(hn_ref[...], wup_ref[:, part * D_FFP + c * FFN_CHUNK:
                                          part * D_FFP + (c + 1) * FFN_CHUNK]) for part in range(2)]

    u_next = up_proj(0)
    for c in range(N_FFN_CHUNKS):
        u_pair = u_next
        if c + 1 < N_FFN_CHUNKS:
            u_next = up_proj(c + 1)
        conv = []
        for part in range(2):
            c0 = part * D_FFP + c * FFN_CHUNK
            u = u_pair[part]
            tail = jnp.where(first, 0.0, tail_ref[2 * c + part])
            tail_ref[2 * c + part] = u[tm - SUBLANES:tm, :]
            acc = cb_ref[row, c0:c0 + FFN_CHUNK] + cw_ref[FFN_CONV - 1:FFN_CONV, c0:c0 + FFN_CHUNK] * u
            for d in range(1, FFN_CONV):
                shifted = pltpu.roll(u, d, axis=0)
                top = jnp.where(row8 < d, pltpu.roll(tail, d, axis=0), shifted[0:SUBLANES])
                shifted = jnp.concatenate([top, shifted[SUBLANES:]], axis=0)
                j = FFN_CONV - 1 - d
                acc = acc + cw_ref[j:j + 1, c0:c0 + FFN_CHUNK] * shifted
            conv.append(acc)
        act_ref[:, c * FFN_CHUNK:(c + 1) * FFN_CHUNK] = (
            conv[0] * _sigmoid(conv[0]) * conv[1]).astype(BF16)
        if c + 1 in FFN_DOWN_ENDS:
            g = FFN_DOWN_ENDS.index(c + 1)
            k0 = (FFN_DOWN_ENDS[g - 1] if g else 0) * FFN_CHUNK
            k1 = (c + 1) * FFN_CHUNK
            acc_ref[...] += _dot(act_ref[:, k0:k1], wdn_ref[k0:k1, :])

    out = acc_ref[...]
    if final_norm:
        out = _rms(out, fg_ref[...])
    o_ref[...] = out


def _out_ffn(x2d, ym, yc, ya, wo, g2, wup, cw, cb, wdn, fg, layer, seq, final_norm):
    n = x2d.shape[0]
    tiles_per_seq = seq // TM_FFN

    def const(shape):
        return _layer_spec(shape, layer, pipeline_mode=pl.Buffered(1))

    return pl.pallas_call(
        functools.partial(_out_ffn_kernel, layer=layer, tiles_per_seq=tiles_per_seq,
                          final_norm=final_norm),
        out_shape=jax.ShapeDtypeStruct((n, D_MODEL), F32),
        grid=(n // TM_FFN,),
        in_specs=[pl.BlockSpec((TM_FFN, D_MODEL), lambda i: (i, 0)),
                  pl.BlockSpec((TM_FFN, MLSTM_DIM), lambda i: (i, 0)),
                  pl.BlockSpec((TM_FFN, CONV_DIM), lambda i: (i, 0)),
                  pl.BlockSpec((TM_FFN, ATTN_DIM), lambda i: (i, 0)),
                  const((D_MODEL, D_MODEL)),
                  _const_spec((DEPTH, D_MODEL)),
                  const((D_MODEL, 2 * D_FFP)),
                  const((FFN_CONV, 2 * D_FFP)),
                  _const_spec((DEPTH, 2 * D_FFP)),
                  const((D_FFP, D_MODEL)),
                  _const_spec((1, D_MODEL))],
        out_specs=pl.BlockSpec((TM_FFN, D_MODEL), lambda i: (i, 0)),
        scratch_shapes=[pltpu.VMEM((TM_FFN, D_MODEL), BF16),
                        pltpu.VMEM((TM_FFN, D_MODEL), F32),
                        pltpu.VMEM((TM_FFN, D_FFP), BF16),
                        pltpu.VMEM((2 * N_FFN_CHUNKS, SUBLANES, FFN_CHUNK), F32)],
        compiler_params=pltpu.CompilerParams(
            dimension_semantics=("arbitrary",), vmem_limit_bytes=VMEM_LIMIT),
        name="out_ffn",
    )(x2d, ym, yc, ya, wo, g2, wup, cw, cb, wdn, fg)


def _pad_last(a, width):
    return jnp.pad(a, [(0, 0)] * (a.ndim - 1) + [(0, width - a.shape[-1])])


def _move_cols_kernel(x_ref, o_ref, *, moves):
    end = 0
    for src, dst, width in moves:
        if dst > end:
            o_ref[:, end:dst] = jnp.zeros((o_ref.shape[0], dst - end), o_ref.dtype)
        o_ref[:, dst:dst + width] = x_ref[:, src:src + width].astype(o_ref.dtype)
        end = dst + width
    if end < o_ref.shape[1]:
        o_ref[:, end:] = jnp.zeros((o_ref.shape[0], o_ref.shape[1] - end), o_ref.dtype)


def _move_cols(w, moves, out_cols):
    depth, rows, cols = w.shape
    return pl.pallas_call(
        functools.partial(_move_cols_kernel, moves=tuple(moves)),
        out_shape=jax.ShapeDtypeStruct((depth, rows, out_cols), BF16),
        grid=(depth, rows // PREP_ROWS),
        in_specs=[pl.BlockSpec((None, PREP_ROWS, cols), lambda l, i: (l, i, 0))],
        out_specs=pl.BlockSpec((None, PREP_ROWS, out_cols), lambda l, i: (l, i, 0)),
        compiler_params=pltpu.CompilerParams(
            dimension_semantics=("arbitrary", "arbitrary"), vmem_limit_bytes=VMEM_LIMIT),
        name="move_cols",
    )(w)


def _w_in_rows_kernel(x_ref, o_ref):
    g0 = 4 * MLSTM_DIM
    g1 = g0 + 2 * MLSTM_HEADS
    cols = x_ref.shape[1]
    o_ref[0:g0, :] = x_ref[0:g0, :].astype(o_ref.dtype)
    gate_tile = jnp.concatenate([x_ref[g0:g1, :], jnp.zeros((GATE_PAD - (g1 - g0), cols), F32)], axis=0)
    o_ref[g0:ZM_W, :] = gate_tile.astype(o_ref.dtype)
    o_ref[ZM_W:P_W, :] = x_ref[g1:, :].astype(o_ref.dtype)


def _prep_w_in(w_in):
    depth, d_in, p_in = w_in.shape
    return pl.pallas_call(
        _w_in_rows_kernel,
        out_shape=jax.ShapeDtypeStruct((depth, P_W, d_in), BF16),
        grid=(depth, d_in // PREP_ROWS),
        in_specs=[pl.BlockSpec((None, p_in, PREP_ROWS), lambda l, i: (l, 0, i))],
        out_specs=pl.BlockSpec((None, P_W, PREP_ROWS), lambda l, i: (l, 0, i)),
        compiler_params=pltpu.CompilerParams(
            dimension_semantics=("arbitrary", "arbitrary"), vmem_limit_bytes=VMEM_LIMIT),
        name="w_in_rows",
    )(jnp.swapaxes(w_in, 1, 2))


def _prep_w_up(w_up):
    return _move_cols(w_up, [(0, 0, D_FF), (D_FF, D_FFP, D_FF)], 2 * D_FFP)


def _prep_ffn_cols(a):
    return jnp.concatenate([_pad_last(a[..., :D_FF], D_FFP), _pad_last(a[..., D_FF:], D_FFP)],
                           axis=-1)


def _pad_rows_kernel(x_ref, o_ref):
    rows = x_ref.shape[0]
    o_ref[0:rows, :] = x_ref[...].astype(o_ref.dtype)
    o_ref[rows:, :] = jnp.zeros((o_ref.shape[0] - rows, o_ref.shape[1]), o_ref.dtype)


def _prep_w_down(w_down):
    depth, rows, cols = w_down.shape
    return pl.pallas_call(
        _pad_rows_kernel,
        out_shape=jax.ShapeDtypeStruct((depth, D_FFP, cols), BF16),
        grid=(depth, cols // PREP_ROWS),
        in_specs=[pl.BlockSpec((None, rows, PREP_ROWS), lambda l, i: (l, 0, i))],
        out_specs=pl.BlockSpec((None, D_FFP, PREP_ROWS), lambda l, i: (l, 0, i)),
        compiler_params=pltpu.CompilerParams(
            dimension_semantics=("arbitrary", "arbitrary"), vmem_limit_bytes=VMEM_LIMIT),
        name="pad_rows",
    )(w_down)


def _prep_gate_bias(ig_b, fg_b):
    return _pad_last(jnp.concatenate([ig_b, fg_b], axis=-1), GATE_PAD)


def kernel(x, norm1_g, w_in, mlstm_qk_conv_w, mlstm_qk_conv_b, mlstm_ig_b, mlstm_fg_b, mlstm_head_g, conf_dw_w, conf_dw_b, conf_ln_g, conf_ln_b, rel_bias, w_out, norm2_g, ffn_w_up, ffn_conv_w, ffn_conv_b, ffn_w_down, final_g):
    batch, seq, _ = x.shape
    bias_tiles = _bias_tiles(rel_bias)
    far_bias = rel_bias[REL_BUCKETS - 1]
    w_in_r = _prep_w_in(w_in)
    gate_bias = _prep_gate_bias(mlstm_ig_b, mlstm_fg_b)
    w_out_b = w_out.astype(BF16)
    w_up_r = _prep_w_up(ffn_w_up)
    f_w = _prep_ffn_cols(ffn_conv_w)
    f_b = _prep_ffn_cols(ffn_conv_b)
    w_down_r = _prep_w_down(ffn_w_down)
    x2d = x.reshape(batch * seq, D_MODEL)
    for l in range(DEPTH):
        zm, za, yc = _in_proj(x2d, norm1_g, w_in_r, conf_dw_w, conf_dw_b, conf_ln_g, conf_ln_b,
                              l, seq)
        ym = _mlstm(zm, mlstm_qk_conv_w, mlstm_qk_conv_b, gate_bias, mlstm_head_g, l, batch, seq)
        ya = _moba(za, bias_tiles, far_bias, batch, seq)
        x2d = _out_ffn(x2d, ym, yc, ya, w_out_b, norm2_g, w_up_r, f_w, f_b, w_down_r,
                       final_g[None, :], l, seq, l == DEPTH - 1)
    return x2d.reshape(batch, seq, D_MODEL)
```
